```python
import math
import jax, jax.numpy as jnp
from jax import lax
import numpy as np

D_MODEL = 2048
BATCH = 4
SEQ = 2048
DEPTH = 2
DEC_BATCH = 128
DEC_SEQ = 4
PAST_LEN = 16384
PAGE_SIZE = 128

D_S5 = D_MODEL // 2
S5_GROUP = 16
N_S5_GROUPS = D_S5 // S5_GROUP
S5_STATE = 64
DT_MIN = 1e-3
DT_MAX = 1e-1
D_RG = D_MODEL // 2
N_RG_HEADS = 16
RG_HEAD_DIM = D_RG // N_RG_HEADS
CONV_W = 4
RG_C = 8.0
D_IN = D_S5 + 2 * D_RG
D_FF = 3 * D_MODEL
N_EXPERTS = 8
TOP_K = 2
D_FF_EXPERT = D_FF // TOP_K
N_DENSE = (DEPTH + 1) // 2
N_MOE = DEPTH // 2
EPS = 1e-6

kernel_name = 'hybrid_s5_rglru_adaln_moe_step'


def rmsnorm(x, g):
    xf = x.astype(jnp.float32)
    y = xf * lax.rsqrt(jnp.mean(xf * xf, axis=-1, keepdims=True) + EPS)
    return (y * g.astype(jnp.float32)).astype(x.dtype)


def linear_scan(a, b, h0):
    b = b.at[:, 0].add(a[:, 0] * h0)
    def combine(left, right):
        a_l, b_l = left
        a_r, b_r = right
        return a_l * a_r, a_r * b_l + b_r
    _, h = lax.associative_scan(combine, (a, b), axis=1)
    return h


def s5_mixer(u, h0_re, h0_im, lam_re, lam_im, log_dt, b_re, b_im, c_re, c_im, d_skip, w_glu, b_glu):
    f32 = jnp.float32
    Bn, S, _ = u.shape
    uf = u.astype(f32).reshape(Bn, S, N_S5_GROUPS, S5_GROUP)
    lam = lax.complex(lam_re.astype(f32), lam_im.astype(f32))
    dt = jnp.exp(log_dt.astype(f32))[:, None]
    lam_bar = jnp.exp(lam * dt)
    b_bar = ((lam_bar - 1.0) / lam)[:, :, None] * lax.complex(b_re.astype(f32), b_im.astype(f32))
    bu = jnp.einsum('bsgh,gph->bsgp', uf, b_bar)
    h0 = lax.complex(h0_re.astype(f32), h0_im.astype(f32))
    h = linear_scan(jnp.broadcast_to(lam_bar, bu.shape), bu, h0)
    c_mat = lax.complex(c_re.astype(f32), c_im.astype(f32))
    y = jnp.einsum('bsgp,ghp->bsgh', h, c_mat).real + d_skip.astype(f32) * uf
    y = jax.nn.gelu(y.reshape(Bn, S, D_S5))
    y = y * jax.nn.sigmoid(y @ w_glu.astype(f32) + b_glu.astype(f32))
    h_last = h[:, -1]
    return y.astype(u.dtype), h_last.real, h_last.imag


def rglru_mixer(xr, gy, conv_buf, h0, conv_w, conv_b, w_a, b_a, w_i, b_i, lam):
    f32 = jnp.float32
    Bn, S, _ = xr.shape
    xpad = jnp.concatenate([conv_buf.astype(xr.dtype), xr], axis=1)
    xc = conv_b + sum(xpad[:, k:k + S] * conv_w[k] for k in range(CONV_W))
    new_buf = xpad[:, S:]
    xh = xc.astype(f32).reshape(Bn, S, N_RG_HEADS, RG_HEAD_DIM)
    r = jax.nn.sigmoid(jnp.einsum('bshi,hij->bshj', xh, w_a.astype(f32)) + b_a.astype(f32))
    i = jax.nn.sigmoid(jnp.einsum('bshi,hij->bshj', xh, w_i.astype(f32)) + b_i.astype(f32))
    log_a = -RG_C * r * jax.nn.softplus(-lam.astype(f32)).reshape(N_RG_HEADS, RG_HEAD_DIM)
    a = jnp.exp(log_a)
    b = jnp.sqrt(-jnp.expm1(2.0 * log_a)) * (i * xh)
    h = linear_scan(a.reshape(Bn, S, D_RG), b.reshape(Bn, S, D_RG), h0.astype(f32))
    y = h.astype(xr.dtype) * gy
    return y, h[:, -1], new_buf


def swiglu(h, w1, w3, w2):
    return (jax.nn.silu(h @ w1) * (h @ w3)) @ w2


def moe_swiglu(h, w_router, w1, w3, w2):
    Bn, S, D = h.shape
    t = h.reshape(Bn * S, D)
    logits = t.astype(jnp.float32) @ w_router.astype(jnp.float32)
    top_v, top_i = lax.top_k(logits, TOP_K)
    top_w = jax.nn.softmax(top_v, axis=-1)
    comb = jnp.sum(jax.nn.one_hot(top_i, N_EXPERTS, dtype=jnp.float32) * top_w[..., None], axis=1)
    def expert(acc, xs):
        w1e, w3e, w2e, ce = xs
        y = swiglu(t, w1e, w3e, w2e)
        return acc + ce[:, None] * y.astype(jnp.float32), None
    acc, _ = lax.scan(expert, jnp.zeros(t.shape, jnp.float32), (w1, w3, w2, comb.T))
    return acc.astype(h.dtype).reshape(Bn, S, D)


def setup_inputs(seed: int = 0) -> dict:
    key = jax.random.key(seed)
    ks = iter(jax.random.split(key, 64))
    f32 = jnp.float32
    def nrm(shape, scale):
        return jax.random.normal(next(ks), shape, f32) * scale
    D = D_MODEL
    G, P, H = N_S5_GROUPS, S5_STATE, S5_GROUP
    n_idx = jnp.arange(P, dtype=f32)
    u_a = jax.random.uniform(next(ks), (DEPTH, D_RG), f32, 0.81, 0.998)
    a_init = u_a ** (1.0 / RG_C)
    inp = {
        'x_prompt': nrm((BATCH, SEQ, D), 1.0),
        'x_sample': nrm((DEC_BATCH, DEC_SEQ, D), 1.0),
        'c_prompt': nrm((BATCH, D), 1.0),
        'c_sample': nrm((DEC_BATCH, D), 1.0),
        'state_s5_re': nrm((DEPTH, DEC_BATCH, G, P), 0.3),
        'state_s5_im': nrm((DEPTH, DEC_BATCH, G, P), 0.3),
        'state_rglru': nrm((DEPTH, DEC_BATCH, D_RG), 0.5),
        'state_conv': nrm((DEPTH, DEC_BATCH, CONV_W - 1, D_RG), 1.0),
        'norm_mix': 1.0 + nrm((DEPTH, D), 0.02),
        'norm_ffn': 1.0 + nrm((DEPTH, D), 0.02),
        'norm_f': 1.0 + nrm((D,), 0.02),
        'w_ada': nrm((DEPTH, D, 6 * D), 0.5 * D ** -0.5),
        'b_ada': nrm((DEPTH, 6 * D), 0.02),
        'w_in': nrm((DEPTH, D, D_IN), D ** -0.5),
        's5_lam_re': -0.5 + nrm((DEPTH, G, P), 0.01),
        's5_lam_im': math.pi * n_idx + nrm((DEPTH, G, P), 0.01),
        's5_log_dt': jax.random.uniform(next(ks), (DEPTH, G), f32, math.log(DT_MIN), math.log(DT_MAX)),
        's5_b_re': nrm((DEPTH, G, P, H), (2.0 * H) ** -0.5),
        's5_b_im': nrm((DEPTH, G, P, H), (2.0 * H) ** -0.5),
        's5_c_re': nrm((DEPTH, G, H, P), (2.0 * P) ** -0.5),
        's5_c_im': nrm((DEPTH, G, H, P), (2.0 * P) ** -0.5),
        's5_d': nrm((DEPTH, G, H), 1.0),
        's5_w_glu': nrm((DEPTH, D_S5, D_S5), D_S5 ** -0.5),
        's5_b_glu': nrm((DEPTH, D_S5), 0.02),
        'rg_conv_w': nrm((DEPTH, CONV_W, D_RG), CONV_W ** -0.5),
        'rg_conv_b': nrm((DEPTH, D_RG), 0.02),
        'rg_w_a': nrm((DEPTH, N_RG_HEADS, RG_HEAD_DIM, RG_HEAD_DIM), RG_HEAD_DIM ** -0.5),
        'rg_b_a': nrm((DEPTH, N_RG_HEADS, RG_HEAD_DIM), 0.02),
        'rg_w_i': nrm((DEPTH, N_RG_HEADS, RG_HEAD_DIM, RG_HEAD_DIM), RG_HEAD_DIM ** -0.5),
        'rg_b_i': nrm((DEPTH, N_RG_HEADS, RG_HEAD_DIM), 0.02),
        'rg_lam': jnp.log(a_init) - jnp.log1p(-a_init),
        'w_gate': nrm((DEPTH, D, 2 * D), D ** -0.5),
        'b_gate': nrm((DEPTH, 2 * D), 0.02),
        'w_br_s5': nrm((DEPTH, D_S5, D), D_S5 ** -0.5),
        'w_br_rg': nrm((DEPTH, D_RG, D), D_RG ** -0.5),
        'w_out': nrm((DEPTH, D, D), D ** -0.5),
        'ffn_w1': nrm((N_DENSE, D, D_FF), D ** -0.5),
        'ffn_w3': nrm((N_DENSE, D, D_FF), D ** -0.5),
        'ffn_w2': nrm((N_DENSE, D_FF, D), D_FF ** -0.5),
        'moe_router': nrm((N_MOE, D, N_EXPERTS), D ** -0.5),
        'moe_w1': nrm((N_MOE, N_EXPERTS, D, D_FF_EXPERT), D ** -0.5),
        'moe_w3': nrm((N_MOE, N_EXPERTS, D, D_FF_EXPERT), D ** -0.5),
        'moe_w2': nrm((N_MOE, N_EXPERTS, D_FF_EXPERT, D), D_FF_EXPERT ** -0.5),
    }
    return inp


def reference(x_prompt, x_sample, c_prompt, c_sample, state_s5_re, state_s5_im, state_rglru, state_conv,
              norm_mix, norm_ffn, norm_f, w_ada, b_ada, w_in,
              s5_lam_re, s5_lam_im, s5_log_dt, s5_b_re, s5_b_im, s5_c_re, s5_c_im, s5_d, s5_w_glu, s5_b_glu,
              rg_conv_w, rg_conv_b, rg_w_a, rg_b_a, rg_w_i, rg_b_i, rg_lam,
              w_gate, b_gate, w_br_s5, w_br_rg, w_out,
              ffn_w1, ffn_w3, ffn_w2, moe_router, moe_w1, moe_w3, moe_w2):
    D = D_MODEL

    def run(x, c, s5_re0, s5_im0, rg_h0, conv0):
        s5_re_out, s5_im_out, rg_out, conv_out = [], [], [], []
        cs = jax.nn.silu(c)
        for l in range(DEPTH):
            mod = (cs @ w_ada[l] + b_ada[l])[:, None, :]
            sh1, sc1, g1, sh2, sc2, g2 = jnp.split(mod, 6, axis=-1)
            h = rmsnorm(x, norm_mix[l]) * (1.0 + sc1) + sh1
            proj = h @ w_in[l]
            u_s5 = proj[..., :D_S5]
            xr = proj[..., D_S5:D_S5 + D_RG]
            gy = jax.nn.gelu(proj[..., D_S5 + D_RG:])
            y_s5, s5r, s5i = s5_mixer(u_s5, s5_re0[l], s5_im0[l], s5_lam_re[l], s5_lam_im[l], s5_log_dt[l],
                                      s5_b_re[l], s5_b_im[l], s5_c_re[l], s5_c_im[l], s5_d[l],
                                      s5_w_glu[l], s5_b_glu[l])
            y_rg, rgh, cbuf = rglru_mixer(xr, gy, conv0[l], rg_h0[l], rg_conv_w[l], rg_conv_b[l],
                                          rg_w_a[l], rg_b_a[l], rg_w_i[l], rg_b_i[l], rg_lam[l])
            gates = jax.nn.sigmoid(h @ w_gate[l] + b_gate[l])
            merged = gates[..., :D] * (y_s5 @ w_br_s5[l]) + gates[..., D:] * (y_rg @ w_br_rg[l])
            x = x + g1 * (merged @ w_out[l])
            h = rmsnorm(x, norm_ffn[l]) * (1.0 + sc2) + sh2
            if l % 2 == 0:
                y = swiglu(h, ffn_w1[l // 2], ffn_w3[l // 2], ffn_w2[l // 2])
            else:
                y = moe_swiglu(h, moe_router[l // 2], moe_w1[l // 2], moe_w3[l // 2], moe_w2[l // 2])
            x = x + g2 * y
            s5_re_out.append(s5r)
            s5_im_out.append(s5i)
            rg_out.append(rgh)
            conv_out.append(cbuf)
        y_final = rmsnorm(x, norm_f)
        return (y_final, jnp.stack(s5_re_out).astype(state_s5_re.dtype), jnp.stack(s5_im_out).astype(state_s5_im.dtype),
                jnp.stack(rg_out).astype(state_rglru.dtype), jnp.stack(conv_out).astype(state_conv.dtype))

    zeros_s5 = jnp.zeros((DEPTH, BATCH, N_S5_GROUPS, S5_STATE), jnp.float32)
    zeros_rg = jnp.zeros((DEPTH, BATCH, D_RG), jnp.float32)
    zeros_conv = jnp.zeros((DEPTH, BATCH, CONV_W - 1, D_RG), x_prompt.dtype)
    y_prompt, p_s5_re, p_s5_im, p_rg, p_conv = run(x_prompt, c_prompt, zeros_s5, zeros_s5, zeros_rg, zeros_conv)
    y_sample, s_s5_re, s_s5_im, s_rg, s_conv = run(x_sample, c_sample, state_s5_re, state_s5_im, state_rglru, state_conv)
    return (y_prompt, y_sample, p_s5_re, p_s5_im, p_rg, p_conv, s_s5_re, s_s5_im, s_rg, s_conv)
```

```python
import functools
import math

import jax
import jax.numpy as jnp
from jax import lax
from jax.experimental import pallas as pl
from jax.experimental.pallas import tpu as pltpu

F32 = jnp.float32
EPS = 1e-6
RG_C = 8.0
CONV_W = 4
S5_GROUP = 16
S5_STATE = 64
RG_HEAD_DIM = 64
TOP_K = 2

V7X_LANES = 128
V7X_SUBLANES = 8
V7X_MXU_DIM = 256
VMEM_LIMIT_BYTES = 56 * 1024 * 1024

GELU_C = math.sqrt(2.0 / math.pi)


def _gelu(x):
    return 0.5 * x * (1.0 + jnp.tanh(GELU_C * (x + 0.044715 * (x * x * x))))


def _sigmoid(x):
    return 1.0 / (1.0 + jnp.exp(-x))


def _silu(x):
    return x * _sigmoid(x)


def _dot(a, b):
    return jnp.dot(a, b, preferred_element_type=F32)


def _params(*sem):
    return pltpu.CompilerParams(dimension_semantics=sem, vmem_limit_bytes=VMEM_LIMIT_BYTES)


def _mod_spec(j, rows, width, tiles_per_block, ncol_axis=None):
    if ncol_axis is None:
        return pl.BlockSpec((None, None, rows, width), lambda m, *_: (j, m // tiles_per_block, 0, 0))
    return pl.BlockSpec((None, None, rows, width),
                        lambda m, *idx: (j, m // tiles_per_block, 0, idx[ncol_axis]))


ROW_CHUNK = 128
PROMPT_TILE = 1024


def _rmsnorm(x, g):
    ms = jnp.mean(x * x, axis=-1, keepdims=True)
    return (x * lax.rsqrt(ms + EPS)) * g


def _norm_mod_rows(x_ref, g_ref, sc_ref, sh_ref, o_ref):
    tm = x_ref.shape[0]
    chunk = min(ROW_CHUNK, tm)
    per_row = sc_ref.shape[0] != 1

    def body(i, carry):
        rows = pl.ds(pl.multiple_of(i * chunk, chunk), chunk)
        sc = sc_ref[rows, :] if per_row else sc_ref[...]
        sh = sh_ref[rows, :] if per_row else sh_ref[...]
        o_ref[rows, :] = _rmsnorm(x_ref[rows, :], g_ref[...]) * (1.0 + sc) + sh
        return carry

    lax.fori_loop(0, tm // chunk, body, 0)


def _ada_kernel(c_ref, w_ref, b_ref, o_ref):
    o_ref[...] = _dot(_silu(c_ref[...]), w_ref[...]) + b_ref[...]


def _ada_mod(c_all, w_ada, b_ada, *, tn=1024):
    depth, d, n6 = w_ada.shape
    rows = c_all.shape[0]
    return pl.pallas_call(
        _ada_kernel,
        grid=(depth, n6 // tn),
        in_specs=[pl.BlockSpec((rows, d), lambda l, n: (0, 0)),
                  pl.BlockSpec((None, d, tn), lambda l, n: (l, 0, n)),
                  pl.BlockSpec((None, 1, tn), lambda l, n: (l, 0, n))],
        out_specs=pl.BlockSpec((None, rows, tn), lambda l, n: (l, 0, n)),
        out_shape=jax.ShapeDtypeStruct((depth, rows, n6), F32),
        compiler_params=_params("arbitrary", "arbitrary"),
        name="ada_mod",
    )(c_all, w_ada, b_ada.reshape(depth, 1, n6))


def _normed_linear_kernel(*refs, mode, n_plain):
    if mode == "glu":
        x_ref, sh_ref, sc_ref, g_ref, w1_ref, w3_ref, o_ref, hn_ref = refs
    elif mode == "gate":
        x_ref, sh_ref, sc_ref, g_ref, w_ref, b_ref, o_ref, hn_ref = refs
    else:
        x_ref, sh_ref, sc_ref, g_ref, w_ref, o_ref, hn_ref = refs
    n = pl.program_id(1)

    @pl.when(n == 0)
    def _():
        _norm_mod_rows(x_ref, g_ref, sc_ref, sh_ref, hn_ref)

    h = hn_ref[...]
    if mode == "glu":
        o_ref[...] = _silu(_dot(h, w1_ref[...])) * _dot(h, w3_ref[...])
    elif mode == "gate":
        o_ref[...] = _sigmoid(_dot(h, w_ref[...]) + b_ref[...])
    else:
        z = _dot(h, w_ref[...])

        @pl.when(n < n_plain)
        def _():
            o_ref[...] = z

        @pl.when(n >= n_plain)
        def _():
            o_ref[...] = _gelu(z)


def _normed_linear(x, mod, j_sh, g, l_g, ws, l, *, mode, tm, tn, tiles_per_block, bias=None, plain_cols=0):
    t, d = x.shape
    n_out = ws[0].shape[-1]
    rows = mod.shape[2]
    w_spec = pl.BlockSpec((None, d, tn), lambda m, n: (l, 0, n))
    in_specs = [pl.BlockSpec((tm, d), lambda m, n: (m, 0)),
                _mod_spec(j_sh, rows, d, tiles_per_block),
                _mod_spec(j_sh + 1, rows, d, tiles_per_block),
                pl.BlockSpec((None, 1, d), lambda m, n: (l_g, 0, 0))]
    args = [x, mod, mod, g.reshape(g.shape[0], 1, d)]
    for w in ws:
        in_specs.append(w_spec)
        args.append(w)
    if bias is not None:
        in_specs.append(pl.BlockSpec((None, 1, tn), lambda m, n: (l, 0, n)))
        args.append(bias.reshape(bias.shape[0], 1, n_out))
    return pl.pallas_call(
        functools.partial(_normed_linear_kernel, mode=mode, n_plain=plain_cols // tn),
        grid=(t // tm, n_out // tn),
        in_specs=in_specs,
        out_specs=pl.BlockSpec((tm, tn), lambda m, n: (m, n)),
        out_shape=jax.ShapeDtypeStruct((t, n_out), F32),
        scratch_shapes=[pltpu.VMEM((tm, d), F32)],
        compiler_params=_params("arbitrary", "arbitrary"),
        name="normed_linear_" + mode,
    )(*args)


def _s5_tables(lam_re, lam_im, log_dt, b_re, b_im, c_re, c_im):
    g, p = lam_re.shape
    h = b_re.shape[-1]
    gpc = V7X_MXU_DIM // h
    nck = g // gpc
    lam = lax.complex(lam_re, lam_im)
    dt = jnp.exp(log_dt)[:, None]
    lam_bar = jnp.exp(lam * dt)
    b_bar = ((lam_bar - 1.0) / lam)[:, :, None] * lax.complex(b_re, b_im)
    eye = jnp.eye(gpc, dtype=F32)

    def expand_b(m):
        m = m.reshape(nck, gpc, p, h).transpose(0, 1, 3, 2)
        return jnp.einsum("kghp,gj->kghjp", m, eye).reshape(nck, gpc * h, gpc * p)

    def expand_c(m):
        m = m.reshape(nck, gpc, h, p).transpose(0, 1, 3, 2)
        return jnp.einsum("kgph,gj->kgpjh", m, eye).reshape(nck, gpc * p, gpc * h)

    lb = lam_bar.reshape(1, g * p)
    pows = jnp.cumprod(jnp.broadcast_to(lb, (V7X_SUBLANES, g * p)), axis=0)
    row = jnp.arange(V7X_SUBLANES)[:, None]
    tabs = []
    for k in (1, 2, 4):
        lk = jnp.where(row >= k, pows[k - 1][None, :], 0.0)
        tabs += [lk.real, lk.imag]
    tabs += [pows.real, pows.imag]
    return dict(wb_re=expand_b(b_bar.real), wb_im=expand_b(b_bar.imag),
                wc_re=expand_c(c_re), wc_imneg=expand_c(-c_im),
                tabs=jnp.stack(tabs).astype(F32),
                lb_re=lb.real, lb_im=lb.imag)


def _s5_input_proj(u, wbre_ref, wbim_ref, bre_ref, bim_ref):
    nck, kc, nc = wbre_ref.shape
    for k in range(nck):
        uk = u[:, k * kc:(k + 1) * kc]
        bre_ref[:, k * nc:(k + 1) * nc] = _dot(uk, wbre_ref[k])
        bim_ref[:, k * nc:(k + 1) * nc] = _dot(uk, wbim_ref[k])


def _s5_output(u, bre_ref, bim_ref, wcre_ref, wcim_ref, d_ref, wglu_ref, bglu_ref, y_ref):
    nck, nc, kc = wcre_ref.shape
    for k in range(nck):
        yk = _dot(bre_ref[:, k * nc:(k + 1) * nc], wcre_ref[k]) + _dot(bim_ref[:, k * nc:(k + 1) * nc], wcim_ref[k])
        yk = yk + d_ref[:, k * kc:(k + 1) * kc] * u[:, k * kc:(k + 1) * kc]
        y_ref[:, k * kc:(k + 1) * kc] = _gelu(yk)
    yg = y_ref[...]
    y_ref[...] = yg * _sigmoid(_dot(yg, wglu_ref[...]) + bglu_ref[...])


def _s5_seq_kernel(u_ref, h0re_ref, h0im_ref, wbre_ref, wbim_ref, wcre_ref, wcim_ref, tabs_ref, d_ref,
                   wglu_ref, bglu_ref, y_ref, hre_ref, him_ref, bre_ref, bim_ref, cre_ref, cim_ref,
                   *, tm, lane_chunk):
    s = pl.program_id(1)
    ns = cre_ref.shape[1]

    @pl.when(s == 0)
    def _():
        cre_ref[...] = jnp.broadcast_to(h0re_ref[...], cre_ref.shape)
        cim_ref[...] = jnp.broadcast_to(h0im_ref[...], cim_ref.shape)

    u = u_ref[...]
    _s5_input_proj(u, wbre_ref, wbim_ref, bre_ref, bim_ref)

    for c in range(ns // lane_chunk):
        sl = slice(c * lane_chunk, (c + 1) * lane_chunk)
        tab = [tabs_ref[i, :, sl] for i in range(8)]

        def body(r, carry, sl=sl, tab=tab):
            cr, ci = carry
            rows = pl.ds(pl.multiple_of(r * V7X_SUBLANES, V7X_SUBLANES), V7X_SUBLANES)
            xr = bre_ref[rows, sl]
            xi = bim_ref[rows, sl]
            for i, k in enumerate((1, 2, 4)):
                lr, li = tab[2 * i], tab[2 * i + 1]
                sr = pltpu.roll(xr, k, 0)
                si = pltpu.roll(xi, k, 0)
                xr, xi = xr + (lr * sr - li * si), xi + (lr * si + li * sr)
            pr, pi = tab[6], tab[7]
            hr = xr + (pr * cr - pi * ci)
            hi = xi + (pr * ci + pi * cr)
            bre_ref[rows, sl] = hr
            bim_ref[rows, sl] = hi
            last = V7X_SUBLANES - 1
            return (jnp.broadcast_to(hr[last:last + 1], hr.shape), jnp.broadcast_to(hi[last:last + 1], hi.shape))

        cr, ci = lax.fori_loop(0, tm // V7X_SUBLANES, body, (cre_ref[:, sl], cim_ref[:, sl]))
        cre_ref[:, sl] = cr
        cim_ref[:, sl] = ci

    _s5_output(u, bre_ref, bim_ref, wcre_ref, wcim_ref, d_ref, wglu_ref, bglu_ref, y_ref)

    @pl.when(s == pl.num_programs(1) - 1)
    def _():
        hre_ref[...] = cre_ref[0:1, :]
        him_ref[...] = cim_ref[0:1, :]


def _const_spec(a):
    nd = a.ndim
    return pl.BlockSpec(a.shape, lambda *_: (0,) * nd, pipeline_mode=pl.Buffered(1))


def _s5_seq(proj, h0re, h0im, tb, d_skip, w_glu, b_glu, l, *, bn, seq, tm=256, lane_chunk=256):
    c = w_glu.shape[-1]
    ns = h0re.shape[-1]
    n_s = seq // tm
    consts = [tb["wb_re"], tb["wb_im"], tb["wc_re"], tb["wc_imneg"], tb["tabs"]]
    in_specs = ([pl.BlockSpec((tm, c), lambda b, s: (b * n_s + s, 0)),
                 pl.BlockSpec((None, 1, ns), lambda b, s: (b, 0, 0)),
                 pl.BlockSpec((None, 1, ns), lambda b, s: (b, 0, 0))]
                + [_const_spec(a) for a in consts]
                + [pl.BlockSpec((None, 1, c), lambda b, s: (l, 0, 0)),
                   pl.BlockSpec((None, c, c), lambda b, s: (l, 0, 0), pipeline_mode=pl.Buffered(1)),
                   pl.BlockSpec((None, 1, c), lambda b, s: (l, 0, 0))])
    y, hre, him = pl.pallas_call(
        functools.partial(_s5_seq_kernel, tm=tm, lane_chunk=lane_chunk),
        grid=(bn, n_s),
        in_specs=in_specs,
        out_specs=[pl.BlockSpec((tm, c), lambda b, s: (b * n_s + s, 0)),
                   pl.BlockSpec((None, 1, ns), lambda b, s: (b, 0, 0)),
                   pl.BlockSpec((None, 1, ns), lambda b, s: (b, 0, 0))],
        out_shape=[jax.ShapeDtypeStruct((bn * seq, c), F32),
                   jax.ShapeDtypeStruct((bn, 1, ns), F32),
                   jax.ShapeDtypeStruct((bn, 1, ns), F32)],
        scratch_shapes=[pltpu.VMEM((tm, ns), F32), pltpu.VMEM((tm, ns), F32),
                        pltpu.VMEM((V7X_SUBLANES, ns), F32), pltpu.VMEM((V7X_SUBLANES, ns), F32)],
        compiler_params=_params("arbitrary", "arbitrary"),
        name="s5_seq",
    )(proj, h0re.reshape(bn, 1, ns), h0im.reshape(bn, 1, ns), *consts,
      d_skip.reshape(d_skip.shape[0], 1, c), w_glu, b_glu.reshape(b_glu.shape[0], 1, c))
    return y, hre.reshape(bn, ns), him.reshape(bn, ns)


def _s5_step_kernel(u_ref, h0re_ref, h0im_ref, wbre_ref, wbim_ref, wcre_ref, wcim_ref, lbre_ref, lbim_ref, d_ref,
                    wglu_ref, bglu_ref, y_ref, hre_ref, him_ref, bre_ref, bim_ref, *, seq, bn, lane_chunk):
    u = u_ref[...]
    _s5_input_proj(u, wbre_ref, wbim_ref, bre_ref, bim_ref)
    ns = bre_ref.shape[1]
    for c in range(ns // lane_chunk):
        sl = slice(c * lane_chunk, (c + 1) * lane_chunk)
        lr = lbre_ref[:, sl]
        li = lbim_ref[:, sl]
        hr = h0re_ref[:, sl]
        hi = h0im_ref[:, sl]
        for t in range(seq):
            rows = slice(t * bn, (t + 1) * bn)
            hr, hi = (lr * hr - li * hi) + bre_ref[rows, sl], (lr * hi + li * hr) + bim_ref[rows, sl]
            bre_ref[rows, sl] = hr
            bim_ref[rows, sl] = hi
        hre_ref[:, sl] = hr
        him_ref[:, sl] = hi
    _s5_output(u, bre_ref, bim_ref, wcre_ref, wcim_ref, d_ref, wglu_ref, bglu_ref, y_ref)


def _s5_step(proj, h0re, h0im, tb, d_skip, w_glu, b_glu, l, *, bn, seq, lane_chunk=128):
    c = w_glu.shape[-1]
    ns = h0re.shape[-1]
    t = bn * seq
    consts = [tb["wb_re"], tb["wb_im"], tb["wc_re"], tb["wc_imneg"], tb["lb_re"], tb["lb_im"]]
    in_specs = ([pl.BlockSpec((t, c), lambda i: (0, 0)),
                 pl.BlockSpec((bn, ns), lambda i: (0, 0)),
                 pl.BlockSpec((bn, ns), lambda i: (0, 0))]
                + [_const_spec(a) for a in consts]
                + [pl.BlockSpec((None, 1, c), lambda i: (l, 0, 0)),
                   pl.BlockSpec((None, c, c), lambda i: (l, 0, 0), pipeline_mode=pl.Buffered(1)),
                   pl.BlockSpec((None, 1, c), lambda i: (l, 0, 0))])
    return pl.pallas_call(
        functools.partial(_s5_step_kernel, seq=seq, bn=bn, lane_chunk=lane_chunk),
        grid=(1,),
        in_specs=in_specs,
        out_specs=[pl.BlockSpec((t, c), lambda i: (0, 0)),
                   pl.BlockSpec((bn, ns), lambda i: (0, 0)),
                   pl.BlockSpec((bn, ns), lambda i: (0, 0))],
        out_shape=[jax.ShapeDtypeStruct((t, c), F32),
                   jax.ShapeDtypeStruct((bn, ns), F32),
                   jax.ShapeDtypeStruct((bn, ns), F32)],
        scratch_shapes=[pltpu.VMEM((t, ns), F32), pltpu.VMEM((t, ns), F32)],
        compiler_params=_params("arbitrary"),
        name="s5_step",
    )(proj, h0re, h0im, *consts,
      d_skip.reshape(d_skip.shape[0], 1, c), w_glu, b_glu.reshape(b_glu.shape[0], 1, c))


def _rg_tables(w_a, w_i):
    nh, hd, _ = w_a.shape
    hpc = V7X_MXU_DIM // hd
    eye = jnp.eye(hpc, dtype=F32)

    def expand(w):
        w = w.reshape(nh // hpc, hpc, hd, hd)
        return jnp.einsum("khij,hg->khigj", w, eye).reshape(nh // hpc, hpc * hd, hpc * hd)

    return expand(w_a), expand(w_i)


def _rg_gates(xc, wa_ref, wi_ref, ba_ref, bi_ref, lam_ref, a_ref, b_ref, rows):
    nck, kc, _ = wa_ref.shape
    lam = lam_ref[...]
    neg = -lam
    softplus = jnp.maximum(neg, 0.0) + jnp.log(1.0 + jnp.exp(-jnp.abs(neg)))
    for k in range(nck):
        cs = slice(k * kc, (k + 1) * kc)
        xk = xc[:, cs]
        r = _sigmoid(_dot(xk, wa_ref[k]) + ba_ref[:, cs])
        i = _sigmoid(_dot(xk, wi_ref[k]) + bi_ref[:, cs])
        log_a = (-RG_C * r) * softplus[:, cs]
        a_ref[rows, cs] = jnp.exp(log_a)
        th = jnp.tanh(log_a)
        b_ref[rows, cs] = jnp.sqrt(-2.0 * th / (1.0 - th)) * (i * xk)


def _rg_seq_kernel(x_ref, gy_ref, conv0_ref, h0_ref, cw_ref, cb_ref, wa_ref, wi_ref, ba_ref, bi_ref, lam_ref,
                   y_ref, hout_ref, xe_ref, a_ref, b_ref, c_ref, *, tm):
    s = pl.program_id(1)
    pad = V7X_SUBLANES
    nconv = conv0_ref.shape[0]

    @pl.when(s == 0)
    def _():
        xe_ref[pad - nconv:pad, :] = conv0_ref[...]
        c_ref[...] = jnp.broadcast_to(h0_ref[...], c_ref.shape)

    @pl.when(s > 0)
    def _():
        xe_ref[0:pad, :] = xe_ref[tm:tm + pad, :]

    xe_ref[pad:pad + tm, :] = x_ref[...]
    acc = cw_ref[0:1, :] * xe_ref[pad - nconv:pad - nconv + tm, :]
    for k in range(1, nconv + 1):
        acc = acc + cw_ref[k:k + 1, :] * xe_ref[pad - nconv + k:pad - nconv + k + tm, :]
    xc = cb_ref[...] + acc
    _rg_gates(xc, wa_ref, wi_ref, ba_ref, bi_ref, lam_ref, a_ref, b_ref, slice(None))

    row = lax.broadcasted_iota(jnp.int32, (V7X_SUBLANES, a_ref.shape[1]), 0)

    def body(r, carry):
        rows = pl.ds(pl.multiple_of(r * V7X_SUBLANES, V7X_SUBLANES), V7X_SUBLANES)
        a = a_ref[rows, :]
        b = b_ref[rows, :]
        for k in (1, 2, 4):
            keep = row >= k
            b = jnp.where(keep, b + a * pltpu.roll(b, k, 0), b)
            a = jnp.where(keep, a * pltpu.roll(a, k, 0), a)
        h = b + a * carry
        b_ref[rows, :] = h
        last = V7X_SUBLANES - 1
        return jnp.broadcast_to(h[last:last + 1], h.shape)

    carry = lax.fori_loop(0, tm // V7X_SUBLANES, body, c_ref[...])
    c_ref[...] = carry
    y_ref[...] = b_ref[...] * gy_ref[...]

    @pl.when(s == pl.num_programs(1) - 1)
    def _():
        hout_ref[...] = c_ref[0:1, :]


def _rg_seq(proj, conv0, h0, conv_w, conv_b, wa, wi, b_a, b_i, lam, l, *, bn, seq, tm=256):
    c = h0.shape[-1]
    n_s = seq // tm
    depth = conv_w.shape[0]
    vec = lambda a: a.reshape(depth, 1, c)
    y, hout = pl.pallas_call(
        functools.partial(_rg_seq_kernel, tm=tm),
        grid=(bn, n_s),
        in_specs=[pl.BlockSpec((tm, c), lambda b, s: (b * n_s + s, 1)),
                  pl.BlockSpec((tm, c), lambda b, s: (b * n_s + s, 2)),
                  pl.BlockSpec((None, CONV_W - 1, c), lambda b, s: (b, 0, 0)),
                  pl.BlockSpec((None, 1, c), lambda b, s: (b, 0, 0)),
                  pl.BlockSpec((None, CONV_W, c), lambda b, s: (l, 0, 0)),
                  pl.BlockSpec((None, 1, c), lambda b, s: (l, 0, 0)),
                  _const_spec(wa), _const_spec(wi),
                  pl.BlockSpec((None, 1, c), lambda b, s: (l, 0, 0)),
                  pl.BlockSpec((None, 1, c), lambda b, s: (l, 0, 0)),
                  pl.BlockSpec((None, 1, c), lambda b, s: (l, 0, 0))],
        out_specs=[pl.BlockSpec((tm, c), lambda b, s: (b * n_s + s, 0)),
                   pl.BlockSpec((None, 1, c), lambda b, s: (b, 0, 0))],
        out_shape=[jax.ShapeDtypeStruct((bn * seq, c), F32), jax.ShapeDtypeStruct((bn, 1, c), F32)],
        scratch_shapes=[pltpu.VMEM((tm + 2 * V7X_SUBLANES, c), F32), pltpu.VMEM((tm, c), F32),
                        pltpu.VMEM((tm, c), F32), pltpu.VMEM((V7X_SUBLANES, c), F32)],
        compiler_params=_params("arbitrary", "arbitrary"),
        name="rg_seq",
    )(proj, proj, conv0, h0.reshape(bn, 1, c), conv_w, vec(conv_b), wa, wi, vec(b_a), vec(b_i), vec(lam))
    return y, hout.reshape(bn, c)


def _rg_step_kernel(x_ref, gy_ref, conv0_ref, h0_ref, cw_ref, cb_ref, wa_ref, wi_ref, ba_ref, bi_ref, lam_ref,
                    y_ref, hout_ref, a_ref, b_ref, *, seq, bn):
    nconv = conv0_ref.shape[0]

    def xpad(i):
        if i < nconv:
            return conv0_ref[i]
        return x_ref[(i - nconv) * bn:(i - nconv + 1) * bn, :]

    for t in range(seq):
        acc = cw_ref[0:1, :] * xpad(t)
        for k in range(1, nconv + 1):
            acc = acc + cw_ref[k:k + 1, :] * xpad(t + k)
        xc = cb_ref[...] + acc
        _rg_gates(xc, wa_ref, wi_ref, ba_ref, bi_ref, lam_ref, a_ref, b_ref, slice(t * bn, (t + 1) * bn))
    h = h0_ref[...]
    for t in range(seq):
        rows = slice(t * bn, (t + 1) * bn)
        h = a_ref[rows, :] * h + b_ref[rows, :]
        y_ref[rows, :] = h * gy_ref[rows, :]
    hout_ref[...] = h


def _rg_step(proj, conv0_tm, h0, conv_w, conv_b, wa, wi, b_a, b_i, lam, l, *, bn, seq):
    c = h0.shape[-1]
    t = bn * seq
    depth = conv_w.shape[0]
    vec = lambda a: a.reshape(depth, 1, c)
    lspec = pl.BlockSpec((None, 1, c), lambda i: (l, 0, 0))
    return pl.pallas_call(
        functools.partial(_rg_step_kernel, seq=seq, bn=bn),
        grid=(1,),
        in_specs=[pl.BlockSpec((t, c), lambda i: (0, 1)),
                  pl.BlockSpec((t, c), lambda i: (0, 2)),
                  pl.BlockSpec(conv0_tm.shape, lambda i: (0, 0, 0)),
                  pl.BlockSpec((bn, c), lambda i: (0, 0)),
                  pl.BlockSpec((None, CONV_W, c), lambda i: (l, 0, 0)),
                  lspec, _const_spec(wa), _const_spec(wi), lspec, lspec, lspec],
        out_specs=[pl.BlockSpec((t, c), lambda i: (0, 0)), pl.BlockSpec((bn, c), lambda i: (0, 0))],
        out_shape=[jax.ShapeDtypeStruct((t, c), F32), jax.ShapeDtypeStruct((bn, c), F32)],
        scratch_shapes=[pltpu.VMEM((t, c), F32), pltpu.VMEM((t, c), F32)],
        compiler_params=_params("arbitrary"),
        name="rg_step",
    )(proj, proj, conv0_tm, h0, conv_w, vec(conv_b), wa, wi, vec(b_a), vec(b_i), vec(lam))


def _merge_kernel(ys_ref, yr_ref, ws_ref, wr_ref, gs_ref, gr_ref, o_ref):
    o_ref[...] = gs_ref[...] * _dot(ys_ref[...], ws_ref[...]) + gr_ref[...] * _dot(yr_ref[...], wr_ref[...])


def _branch_merge(ys, yr, w_s, w_r, gates, l, *, tm, tn):
    t, c = ys.shape
    d = w_s.shape[-1]
    off = d // tn
    return pl.pallas_call(
        _merge_kernel,
        grid=(t // tm, d // tn),
        in_specs=[pl.BlockSpec((tm, c), lambda m, n: (m, 0)),
                  pl.BlockSpec((tm, c), lambda m, n: (m, 0)),
                  pl.BlockSpec((None, c, tn), lambda m, n: (l, 0, n)),
                  pl.BlockSpec((None, c, tn), lambda m, n: (l, 0, n)),
                  pl.BlockSpec((tm, tn), lambda m, n: (m, n)),
                  pl.BlockSpec((tm, tn), lambda m, n: (m, n + off))],
        out_specs=pl.BlockSpec((tm, tn), lambda m, n: (m, n)),
        out_shape=jax.ShapeDtypeStruct((t, d), F32),
        compiler_params=_params("arbitrary", "arbitrary"),
        name="branch_merge",
    )(ys, yr, w_s, w_r, gates, gates)


def _linear_residual_kernel(a_ref, w_ref, x_ref, g_ref, o_ref, acc_ref):
    k = pl.program_id(2)

    @pl.when(k == 0)
    def _():
        acc_ref[...] = jnp.zeros_like(acc_ref)

    acc_ref[...] += _dot(a_ref[...], w_ref[...])

    @pl.when(k == pl.num_programs(2) - 1)
    def _():
        o_ref[...] = x_ref[...] + g_ref[...] * acc_ref[...]


def _linear_residual(a, w, l, x, mod, j_g, *, tm, tn, tk, tiles_per_block):
    t, kdim = a.shape
    d = w.shape[-1]
    rows = mod.shape[2]
    return pl.pallas_call(
        _linear_residual_kernel,
        grid=(t // tm, d // tn, kdim // tk),
        in_specs=[pl.BlockSpec((tm, tk), lambda m, n, k: (m, k)),
                  pl.BlockSpec((None, tk, tn), lambda m, n, k: (l, k, n)),
                  pl.BlockSpec((tm, tn), lambda m, n, k: (m, n)),
                  _mod_spec(j_g, rows, tn, tiles_per_block, ncol_axis=0)],
        out_specs=pl.BlockSpec((tm, tn), lambda m, n, k: (m, n)),
        out_shape=jax.ShapeDtypeStruct((t, d), F32),
        scratch_shapes=[pltpu.VMEM((tm, tn), F32)],
        compiler_params=_params("arbitrary", "arbitrary", "arbitrary"),
        name="linear_residual",
    )(a, w, x, mod)


def _moe_down_kernel(a_ref, w_ref, ce_ref, x_ref, g_ref, o_ref, acc_ref, y_ref):
    e = pl.program_id(2)
    k = pl.program_id(3)
    last_k = k == pl.num_programs(3) - 1

    @pl.when((e == 0) & (k == 0))
    def _():
        acc_ref[...] = jnp.zeros_like(acc_ref)

    @pl.when(k == 0)
    def _():
        y_ref[...] = jnp.zeros_like(y_ref)

    y_ref[...] += _dot(a_ref[...], w_ref[...])

    @pl.when(last_k)
    def _():
        acc_ref[...] += ce_ref[...] * y_ref[...]

    @pl.when(last_k & (e == pl.num_programs(2) - 1))
    def _():
        o_ref[...] = x_ref[...] + g_ref[...] * acc_ref[...]


def _moe_down(act, w2, lm, comb_t, x, mod, j_g, *, tm, tn, tk, tiles_per_block):
    ne, t, f = act.shape
    d = w2.shape[-1]
    rows = mod.shape[2]
    return pl.pallas_call(
        _moe_down_kernel,
        grid=(t // tm, d // tn, ne, f // tk),
        in_specs=[pl.BlockSpec((None, tm, tk), lambda m, n, e, k: (e, m, k)),
                  pl.BlockSpec((None, None, tk, tn), lambda m, n, e, k: (lm, e, k, n)),
                  pl.BlockSpec((None, tm, 1), lambda m, n, e, k: (e, m, 0)),
                  pl.BlockSpec((tm, tn), lambda m, n, e, k: (m, n)),
                  _mod_spec(j_g, rows, tn, tiles_per_block, ncol_axis=0)],
        out_specs=pl.BlockSpec((tm, tn), lambda m, n, e, k: (m, n)),
        out_shape=jax.ShapeDtypeStruct((t, d), F32),
        scratch_shapes=[pltpu.VMEM((tm, tn), F32), pltpu.VMEM((tm, tn), F32)],
        compiler_params=_params("arbitrary", "arbitrary", "arbitrary", "arbitrary"),
        name="moe_down",
    )(act, w2, comb_t, x, mod)


def _router_kernel(x_ref, sh_ref, sc_ref, g_ref, wr_ref, hn_ref, comb_ref, *, n_experts):
    _norm_mod_rows(x_ref, g_ref, sc_ref, sh_ref, hn_ref)
    logits = jnp.dot(hn_ref[...], wr_ref[...], preferred_element_type=F32, precision=lax.Precision.HIGHEST)
    lane = lax.broadcasted_iota(jnp.int32, logits.shape, 1).astype(F32)
    width = float(logits.shape[-1])
    lg = jnp.where(lane < n_experts, logits, -jnp.inf)
    v1 = jnp.max(lg, axis=-1, keepdims=True)
    i1 = jnp.min(jnp.where(lg == v1, lane, width), axis=-1, keepdims=True)
    lg2 = jnp.where(lane == i1, -jnp.inf, lg)
    v2 = jnp.max(lg2, axis=-1, keepdims=True)
    i2 = jnp.min(jnp.where(lg2 == v2, lane, width), axis=-1, keepdims=True)
    e2 = jnp.exp(v2 - v1)
    den = 1.0 + e2
    comb_ref[...] = jnp.where(lane == i1, 1.0 / den, 0.0) + jnp.where(lane == i2, e2 / den, 0.0)


def _router(x, mod, j_sh, g, w_router_pad, l, *, tm, tiles_per_block, n_experts):
    t, d = x.shape
    rows = mod.shape[2]
    width = w_router_pad.shape[-1]
    return pl.pallas_call(
        functools.partial(_router_kernel, n_experts=n_experts),
        grid=(t // tm,),
        in_specs=[pl.BlockSpec((tm, d), lambda m: (m, 0)),
                  _mod_spec(j_sh, rows, d, tiles_per_block),
                  _mod_spec(j_sh + 1, rows, d, tiles_per_block),
                  pl.BlockSpec((None, 1, d), lambda m: (l, 0, 0)),
                  pl.BlockSpec((d, width), lambda m: (0, 0))],
        out_specs=[pl.BlockSpec((tm, d), lambda m: (m, 0)), pl.BlockSpec((tm, width), lambda m: (m, 0))],
        out_shape=[jax.ShapeDtypeStruct((t, d), F32), jax.ShapeDtypeStruct((t, width), F32)],
        compiler_params=_params("arbitrary"),
        name="moe_router",
    )(x, mod, mod, g.reshape(g.shape[0], 1, d), w_router_pad)


def _moe_up_kernel(h_ref, w1_ref, w3_ref, o_ref):
    h = h_ref[...]
    o_ref[...] = _silu(_dot(h, w1_ref[...])) * _dot(h, w3_ref[...])


def _moe_up(hn, w1, w3, lm, *, tm, tn):
    t, d = hn.shape
    ne, f = w1.shape[1], w1.shape[-1]
    wspec = pl.BlockSpec((None, None, d, tn), lambda e, m, n: (lm, e, 0, n))
    return pl.pallas_call(
        _moe_up_kernel,
        grid=(ne, t // tm, f // tn),
        in_specs=[pl.BlockSpec((tm, d), lambda e, m, n: (m, 0)), wspec, wspec],
        out_specs=pl.BlockSpec((None, tm, tn), lambda e, m, n: (e, m, n)),
        out_shape=jax.ShapeDtypeStruct((ne, t, f), F32),
        compiler_params=_params("arbitrary", "arbitrary", "arbitrary"),
        name="moe_up",
    )(hn, w1, w3)


def _final_norm_kernel(x_ref, g_ref, o_ref):
    o_ref[...] = _rmsnorm(x_ref[...], g_ref[...])


def _final_norm(x, g, *, tm):
    t, d = x.shape
    return pl.pallas_call(
        _final_norm_kernel,
        grid=(t // tm,),
        in_specs=[pl.BlockSpec((tm, d), lambda m: (m, 0)), pl.BlockSpec((1, d), lambda m: (0, 0))],
        out_specs=pl.BlockSpec((tm, d), lambda m: (m, 0)),
        out_shape=jax.ShapeDtypeStruct((t, d), F32),
        compiler_params=_params("arbitrary"),
        name="final_norm",
    )(x, g.reshape(1, d))


def _run(x, mods, state, p, *, bn, seq, time_major, tm):
    t, d = x.shape
    depth = p["w_in"].shape[0]
    c = p["s5_w_glu"].shape[-1]
    def tiled(tile):
        return dict(tm=tile, tiles_per_block=1 if time_major else seq // tile)

    kw = tiled(tm)
    nconv = CONV_W - 1
    assert seq >= nconv
    s5_re, s5_im, rg_h, conv = [], [], [], []
    for l in range(depth):
        mod = mods[l]
        tb = _s5_tables(p["s5_lam_re"][l], p["s5_lam_im"][l], p["s5_log_dt"][l], p["s5_b_re"][l], p["s5_b_im"][l],
                        p["s5_c_re"][l], p["s5_c_im"][l])
        wa, wi = _rg_tables(p["rg_w_a"][l], p["rg_w_i"][l])
        proj = _normed_linear(x, mod, 0, p["norm_mix"], l, [p["w_in"]], l, mode="in", tn=512,
                              plain_cols=2 * c, **kw)
        gates = _normed_linear(x, mod, 0, p["norm_mix"], l, [p["w_gate"]], l, mode="gate", tn=512,
                               bias=p["b_gate"], **kw)
        h0re = state["s5_re"][l].reshape(bn, -1)
        h0im = state["s5_im"][l].reshape(bn, -1)
        rg_args = (p["rg_conv_w"], p["rg_conv_b"], wa, wi, p["rg_b_a"].reshape(depth, c), p["rg_b_i"].reshape(depth, c),
                   p["rg_lam"], l)
        if time_major:
            y_s5, hre, him = _s5_step(proj, h0re, h0im, tb, p["s5_d"].reshape(depth, c), p["s5_w_glu"], p["s5_b_glu"], l,
                                      bn=bn, seq=seq)
            y_rg, hrg = _rg_step(proj, state["conv"][l].transpose(1, 0, 2), state["rg"][l], *rg_args, bn=bn, seq=seq)
            conv.append(proj[:, c:2 * c].reshape(seq, bn, c)[seq - nconv:].transpose(1, 0, 2))
        else:
            y_s5, hre, him = _s5_seq(proj, h0re, h0im, tb, p["s5_d"].reshape(depth, c), p["s5_w_glu"], p["s5_b_glu"], l,
                                     bn=bn, seq=seq)
            y_rg, hrg = _rg_seq(proj, state["conv"][l], state["rg"][l], *rg_args, bn=bn, seq=seq)
            conv.append(proj.reshape(bn, seq, 3 * c)[:, seq - nconv:, c:2 * c])
        s5_re.append(hre.reshape(state["s5_re"][l].shape))
        s5_im.append(him.reshape(state["s5_im"][l].shape))
        rg_h.append(hrg)
        merged = _branch_merge(y_s5, y_rg, p["w_br_s5"], p["w_br_rg"], gates, l, tm=tm, tn=512)
        x = _linear_residual(merged, p["w_out"], l, x, mod, 2, tn=1024, tk=1024, **kw)
        if l % 2 == 0:
            act = _normed_linear(x, mod, 3, p["norm_ffn"], l, [p["ffn_w1"], p["ffn_w3"]], l // 2, mode="glu",
                                 tn=512, **kw)
            x = _linear_residual(act, p["ffn_w2"], l // 2, x, mod, 5, tn=1024, tk=1024, **kw)
        else:
            lm = l // 2
            ne = p["moe_router"].shape[-1]
            w_r = jnp.pad(p["moe_router"][lm], ((0, 0), (0, V7X_LANES - ne)))
            hn, comb = _router(x, mod, 3, p["norm_ffn"], w_r, l, n_experts=ne, **tiled(min(tm, 512)))
            act = _moe_up(hn, p["moe_w1"], p["moe_w3"], lm, tm=tm, tn=512)
            comb_t = comb[:, :ne].T.reshape(ne, t, 1)
            x = _moe_down(act, p["moe_w2"], lm, comb_t, x, mod, 5, tn=1024, tk=1024, **kw)
    y = _final_norm(x, p["norm_f"], tm=min(tm, 256))
    return y, jnp.stack(s5_re), jnp.stack(s5_im), jnp.stack(rg_h), jnp.stack(conv)


def kernel(x_prompt, x_sample, c_prompt, c_sample, state_s5_re, state_s5_im, state_rglru, state_conv, norm_mix, norm_ffn, norm_f, w_ada, b_ada, w_in, s5_lam_re, s5_lam_im, s5_log_dt, s5_b_re, s5_b_im, s5_c_re, s5_c_im, s5_d, s5_w_glu, s5_b_glu, rg_conv_w, rg_conv_b, rg_w_a, rg_b_a, rg_w_i, rg_b_i, rg_lam, w_gate, b_gate, w_br_s5, w_br_rg, w_out, ffn_w1, ffn_w3, ffn_w2, moe_router, moe_w1, moe_w3, moe_w2):
    p = dict(norm_mix=norm_mix, norm_ffn=norm_ffn, norm_f=norm_f, w_in=w_in,
             s5_lam_re=s5_lam_re, s5_lam_im=s5_lam_im, s5_log_dt=s5_log_dt, s5_b_re=s5_b_re, s5_b_im=s5_b_im,
             s5_c_re=s5_c_re, s5_c_im=s5_c_im, s5_d=s5_d, s5_w_glu=s5_w_glu, s5_b_glu=s5_b_glu,
             rg_conv_w=rg_conv_w, rg_conv_b=rg_conv_b, rg_w_a=rg_w_a, rg_b_a=rg_b_a, rg_w_i=rg_w_i, rg_b_i=rg_b_i,
             rg_lam=rg_lam, w_gate=w_gate, b_gate=b_gate, w_br_s5=w_br_s5, w_br_rg=w_br_rg, w_out=w_out,
             ffn_w1=ffn_w1, ffn_w3=ffn_w3, ffn_w2=ffn_w2, moe_router=moe_router, moe_w1=moe_w1, moe_w3=moe_w3,
             moe_w2=moe_w2)
    depth = w_in.shape[0]
    bp, sp, d = x_prompt.shape
    bs, ss, _ = x_sample.shape
    g, pst = state_s5_re.shape[2:]
    c = state_rglru.shape[-1]

    n_c = bp + bs
    rows_c = -(-n_c // V7X_SUBLANES) * V7X_SUBLANES
    c_all = jnp.pad(jnp.concatenate([c_prompt, c_sample], axis=0), ((0, rows_c - n_c), (0, 0)))
    mod_all = _ada_mod(c_all, w_ada, b_ada)
    mods_p, mods_s = [], []
    for l in range(depth):
        mp = mod_all[l, :bp].reshape(bp, 6, 1, d).transpose(1, 0, 2, 3)
        ms = mod_all[l, bp:n_c].reshape(bs, 6, d).transpose(1, 0, 2)
        ms = jnp.tile(ms, (1, ss, 1)).reshape(6, 1, ss * bs, d)
        mods_p.append(mp)
        mods_s.append(ms)

    zero_state = dict(s5_re=jnp.zeros((depth, bp, g, pst), F32), s5_im=jnp.zeros((depth, bp, g, pst), F32),
                      rg=jnp.zeros((depth, bp, c), F32), conv=jnp.zeros((depth, bp, CONV_W - 1, c), x_prompt.dtype))
    y_p, p_re, p_im, p_rg, p_conv = _run(x_prompt.reshape(bp * sp, d), mods_p, zero_state, p,
                                         bn=bp, seq=sp, time_major=False, tm=min(PROMPT_TILE, sp))
    sample_state = dict(s5_re=state_s5_re, s5_im=state_s5_im, rg=state_rglru, conv=state_conv)
    x_s = x_sample.transpose(1, 0, 2).reshape(ss * bs, d)
    y_s, s_re, s_im, s_rg, s_conv = _run(x_s, mods_s, sample_state, p, bn=bs, seq=ss, time_major=True, tm=ss * bs)
    y_s = y_s.reshape(ss, bs, d).transpose(1, 0, 2)
    return (y_p.reshape(bp, sp, d), y_s, p_re, p_im, p_rg, p_conv, s_re, s_im, s_rg, s_conv)
```

```python
import functools
import math

import jax
import jax.numpy as jnp
from jax import lax
from jax.experimental import pallas as pl
from jax.experimental.pallas import tpu as pltpu

F32 = jnp.float32
EPS = 1e-6
RG_C = 8.0
CONV_W = 4
S5_GROUP = 16
S5_STATE = 64
RG_HEAD_DIM = 64
TOP_K = 2

V7X_LANES = 128
V7X_SUBLANES = 8
V7X_MXU_DIM = 256
VMEM_LIMIT_BYTES = 56 * 1024 * 1024

GELU_C = math.sqrt(2.0 / math.pi)


def _gelu(x):
    return 0.5 * x * (1.0 + jnp.tanh(GELU_C * (x + 0.044715 * (x * x * x))))


def _sigmoid(x):
    return 1.0 / (1.0 + jnp.exp(-x))


def _silu(x):
    return x * _sigmoid(x)


def _dot(a, b):
    return jnp.dot(a, b, preferred_element_type=F32)


def _params(*sem):
    return pltpu.CompilerParams(dimension_semantics=sem, vmem_limit_bytes=VMEM_LIMIT_BYTES)


def _mod_spec(j, rows, width, tiles_per_block, ncol_axis=None):
    if ncol_axis is None:
        return pl.BlockSpec((None, None, rows, width), lambda m, *_: (j, m // tiles_per_block, 0, 0))
    return pl.BlockSpec((None, None, rows, width),
                        lambda m, *idx: (j, m // tiles_per_block, 0, idx[ncol_axis]))


ROW_CHUNK = 128
PROMPT_TILE = 1024


def _rmsnorm(x, g):
    ms = jnp.mean(x * x, axis=-1, keepdims=True)
    return (x * lax.rsqrt(ms + EPS)) * g


def _norm_mod_rows(x_ref, g_ref, sc_ref, sh_ref, o_ref):
    tm = x_ref.shape[0]
    chunk = min(ROW_CHUNK, tm)
    per_row = sc_ref.shape[0] != 1

    def body(i, carry):
        rows = pl.ds(pl.multiple_of(i * chunk, chunk), chunk)
        sc = sc_ref[rows, :] if per_row else sc_ref[...]
        sh = sh_ref[rows, :] if per_row else sh_ref[...]
        o_ref[rows, :] = _rmsnorm(x_ref[rows, :], g_ref[...]) * (1.0 + sc) + sh
        return carry

    lax.fori_loop(0, tm // chunk, body, 0)


def _ada_kernel(c_ref, w_ref, b_ref, o_ref):
    o_ref[...] = _dot(_silu(c_ref[...]), w_ref[...]) + b_ref[...]


def _ada_mod(c_all, w_ada, b_ada, *, tn=1024):
    depth, d, n6 = w_ada.shape
    rows = c_all.shape[0]
    return pl.pallas_call(
        _ada_kernel,
        grid=(depth, n6 // tn),
        in_specs=[pl.BlockSpec((rows, d), lambda l, n: (0, 0)),
                  pl.BlockSpec((None, d, tn), lambda l, n: (l, 0, n)),
                  pl.BlockSpec((None, 1, tn), lambda l, n: (l, 0, n))],
        out_specs=pl.BlockSpec((None, rows, tn), lambda l, n: (l, 0, n)),
        out_shape=jax.ShapeDtypeStruct((depth, rows, n6), F32),
        compiler_params=_params("arbitrary", "arbitrary"),
        name="ada_mod",
    )(c_all, w_ada, b_ada.reshape(depth, 1, n6))


def _normed_linear_kernel(*refs, mode, n_plain):
    if mode == "glu":
        x_ref, sh_ref, sc_ref, g_ref, w1_ref, w3_ref, o_ref, hn_ref = refs
    elif mode == "gate":
        x_ref, sh_ref, sc_ref, g_ref, w_ref, b_ref, o_ref, hn_ref = refs
    else:
        x_ref, sh_ref, sc_ref, g_ref, w_ref, o_ref, hn_ref = refs
    n = pl.program_id(1)

    @pl.when(n == 0)
    def _():
        _norm_mod_rows(x_ref, g_ref, sc_ref, sh_ref, hn_ref)

    h = hn_ref[...]
    if mode == "glu":
        o_ref[...] = _silu(_dot(h, w1_ref[...])) * _dot(h, w3_ref[...])
    elif mode == "gate":
        o_ref[...] = _sigmoid(_dot(h, w_ref[...]) + b_ref[...])
    else:
        z = _dot(h, w_ref[...])

        @pl.when(n < n_plain)
        def _():
            o_ref[...] = z

        @pl.when(n >= n_plain)
        def _():
            o_ref[...] = _gelu(z)


def _normed_linear(x, mod, j_sh, g, l_g, ws, l, *, mode, tm, tn, tiles_per_block, bias=None, plain_cols=0):
    t, d = x.shape
    n_out = ws[0].shape[-1]
    rows = mod.shape[2]
    w_spec = pl.BlockSpec((None, d, tn), lambda m, n: (l, 0, n))
    in_specs = [pl.BlockSpec((tm, d), lambda m, n: (m, 0)),
                _mod_spec(j_sh, rows, d, tiles_per_block),
                _mod_spec(j_sh + 1, rows, d, tiles_per_block),
                pl.BlockSpec((None, 1, d), lambda m, n: (l_g, 0, 0))]
    args = [x, mod, mod, g.reshape(g.shape[0], 1, d)]
    for w in ws:
        in_specs.append(w_spec)
        args.append(w)
    if bias is not None:
        in_specs.append(pl.BlockSpec((None, 1, tn), lambda m, n: (l, 0, n)))
        args.append(bias.reshape(bias.shape[0], 1, n_out))
    return pl.pallas_call(
        functools.partial(_normed_linear_kernel, mode=mode, n_plain=plain_cols // tn),
        grid=(t // tm, n_out // tn),
        in_specs=in_specs,
        out_specs=pl.BlockSpec((tm, tn), lambda m, n: (m, n)),
        out_shape=jax.ShapeDtypeStruct((t, n_out), F32),
        scratch_shapes=[pltpu.VMEM((tm, d), F32)],
        compiler_params=_params("arbitrary", "arbitrary"),
        name="normed_linear_" + mode,
    )(*args)


def _s5_tables(lam_re, lam_im, log_dt, b_re, b_im, c_re, c_im):
    g, p = lam_re.shape
    h = b_re.shape[-1]
    gpc = V7X_MXU_DIM // h
    nck = g // gpc
    dt = jnp.exp(log_dt)[:, None]
    mag = jnp.exp(lam_re * dt)
    lb_re = mag * jnp.cos(lam_im * dt)
    lb_im = mag * jnp.sin(lam_im * dt)
    den = lam_re * lam_re + lam_im * lam_im
    cf_re = ((lb_re - 1.0) * lam_re + lb_im * lam_im) / den
    cf_im = (lb_im * lam_re - (lb_re - 1.0) * lam_im) / den
    bb_re = cf_re[:, :, None] * b_re - cf_im[:, :, None] * b_im
    bb_im = cf_re[:, :, None] * b_im + cf_im[:, :, None] * b_re
    eye = jnp.eye(gpc, dtype=F32)

    def expand_b(m):
        m = m.reshape(nck, gpc, p, h).transpose(0, 1, 3, 2)
        return jnp.einsum("kghp,gj->kghjp", m, eye).reshape(nck, gpc * h, gpc * p)

    def expand_c(m):
        m = m.reshape(nck, gpc, h, p).transpose(0, 1, 3, 2)
        return jnp.einsum("kgph,gj->kgpjh", m, eye).reshape(nck, gpc * p, gpc * h)

    lb_re = lb_re.reshape(1, g * p)
    lb_im = lb_im.reshape(1, g * p)
    pw_re, pw_im = [lb_re], [lb_im]
    for _ in range(V7X_SUBLANES - 1):
        pw_re, pw_im = (pw_re + [pw_re[-1] * lb_re - pw_im[-1] * lb_im],
                        pw_im + [pw_re[-1] * lb_im + pw_im[-1] * lb_re])
    row = jnp.arange(V7X_SUBLANES)[:, None]
    tabs = []
    for k in (1, 2, 4):
        tabs += [jnp.where(row >= k, pw_re[k - 1], 0.0), jnp.where(row >= k, pw_im[k - 1], 0.0)]
    tabs += [jnp.concatenate(pw_re, axis=0), jnp.concatenate(pw_im, axis=0)]
    return dict(wb_re=expand_b(bb_re), wb_im=expand_b(bb_im),
                wc_re=expand_c(c_re), wc_imneg=expand_c(-c_im),
                tabs=jnp.stack(tabs).astype(F32),
                lb_re=lb_re, lb_im=lb_im)


def _s5_input_proj(u, wbre_ref, wbim_ref, bre_ref, bim_ref):
    nck, kc, nc = wbre_ref.shape
    for k in range(nck):
        uk = u[:, k * kc:(k + 1) * kc]
        bre_ref[:, k * nc:(k + 1) * nc] = _dot(uk, wbre_ref[k])
        bim_ref[:, k * nc:(k + 1) * nc] = _dot(uk, wbim_ref[k])


def _s5_output(u, bre_ref, bim_ref, wcre_ref, wcim_ref, d_ref, wglu_ref, bglu_ref, y_ref):
    nck, nc, kc = wcre_ref.shape
    for k in range(nck):
        yk = _dot(bre_ref[:, k * nc:(k + 1) * nc], wcre_ref[k]) + _dot(bim_ref[:, k * nc:(k + 1) * nc], wcim_ref[k])
        yk = yk + d_ref[:, k * kc:(k + 1) * kc] * u[:, k * kc:(k + 1) * kc]
        y_ref[:, k * kc:(k + 1) * kc] = _gelu(yk)
    yg = y_ref[...]
    y_ref[...] = yg * _sigmoid(_dot(yg, wglu_ref[...]) + bglu_ref[...])


def _s5_seq_kernel(u_ref, h0re_ref, h0im_ref, wbre_ref, wbim_ref, wcre_ref, wcim_ref, tabs_ref, d_ref,
                   wglu_ref, bglu_ref, y_ref, hre_ref, him_ref, bre_ref, bim_ref, cre_ref, cim_ref,
                   *, tm, lane_chunk):
    s = pl.program_id(1)
    ns = cre_ref.shape[1]

    @pl.when(s == 0)
    def _():
        cre_ref[...] = jnp.broadcast_to(h0re_ref[...], cre_ref.shape)
        cim_ref[...] = jnp.broadcast_to(h0im_ref[...], cim_ref.shape)

    u = u_ref[...]
    _s5_input_proj(u, wbre_ref, wbim_ref, bre_ref, bim_ref)

    for c in range(ns // lane_chunk):
        sl = slice(c * lane_chunk, (c + 1) * lane_chunk)
        tab = [tabs_ref[i, :, sl] for i in range(8)]

        def body(r, carry, sl=sl, tab=tab):
            cr, ci = carry
            rows = pl.ds(pl.multiple_of(r * V7X_SUBLANES, V7X_SUBLANES), V7X_SUBLANES)
            xr = bre_ref[rows, sl]
            xi = bim_ref[rows, sl]
            for i, k in enumerate((1, 2, 4)):
                lr, li = tab[2 * i], tab[2 * i + 1]
                sr = pltpu.roll(xr, k, 0)
                si = pltpu.roll(xi, k, 0)
                xr, xi = xr + (lr * sr - li * si), xi + (lr * si + li * sr)
            pr, pi = tab[6], tab[7]
            hr = xr + (pr * cr - pi * ci)
            hi = xi + (pr * ci + pi * cr)
            bre_ref[rows, sl] = hr
            bim_ref[rows, sl] = hi
            last = V7X_SUBLANES - 1
            return (jnp.broadcast_to(hr[last:last + 1], hr.shape), jnp.broadcast_to(hi[last:last + 1], hi.shape))

        cr, ci = lax.fori_loop(0, tm // V7X_SUBLANES, body, (cre_ref[:, sl], cim_ref[:, sl]))
        cre_ref[:, sl] = cr
        cim_ref[:, sl] = ci

    _s5_output(u, bre_ref, bim_ref, wcre_ref, wcim_ref, d_ref, wglu_ref, bglu_ref, y_ref)

    @pl.when(s == pl.num_programs(1) - 1)
    def _():
        hre_ref[...] = cre_ref[0:1, :]
        him_ref[...] = cim_ref[0:1, :]


def _const_spec(a):
    nd = a.ndim
    return pl.BlockSpec(a.shape, lambda *_: (0,) * nd, pipeline_mode=pl.Buffered(1))


def _s5_seq(proj, h0re, h0im, tb, d_skip, w_glu, b_glu, l, *, bn, seq, tm=256, lane_chunk=256):
    c = w_glu.shape[-1]
    ns = h0re.shape[-1]
    n_s = seq // tm
    consts = [tb["wb_re"], tb["wb_im"], tb["wc_re"], tb["wc_imneg"], tb["tabs"]]
    in_specs = ([pl.BlockSpec((tm, c), lambda b, s: (b * n_s + s, 0)),
                 pl.BlockSpec((None, 1, ns), lambda b, s: (b, 0, 0)),
                 pl.BlockSpec((None, 1, ns), lambda b, s: (b, 0, 0))]
                + [_const_spec(a) for a in consts]
                + [pl.BlockSpec((None, 1, c), lambda b, s: (l, 0, 0)),
                   pl.BlockSpec((None, c, c), lambda b, s: (l, 0, 0), pipeline_mode=pl.Buffered(1)),
                   pl.BlockSpec((None, 1, c), lambda b, s: (l, 0, 0))])
    y, hre, him = pl.pallas_call(
        functools.partial(_s5_seq_kernel, tm=tm, lane_chunk=lane_chunk),
        grid=(bn, n_s),
        in_specs=in_specs,
        out_specs=[pl.BlockSpec((tm, c), lambda b, s: (b * n_s + s, 0)),
                   pl.BlockSpec((None, 1, ns), lambda b, s: (b, 0, 0)),
                   pl.BlockSpec((None, 1, ns), lambda b, s: (b, 0, 0))],
        out_shape=[jax.ShapeDtypeStruct((bn * seq, c), F32),
                   jax.ShapeDtypeStruct((bn, 1, ns), F32),
                   jax.ShapeDtypeStruct((bn, 1, ns), F32)],
        scratch_shapes=[pltpu.VMEM((tm, ns), F32), pltpu.VMEM((tm, ns), F32),
                        pltpu.VMEM((V7X_SUBLANES, ns), F32), pltpu.VMEM((V7X_SUBLANES, ns), F32)],
        compiler_params=_params("arbitrary", "arbitrary"),
        name="s5_seq",
    )(proj, h0re.reshape(bn, 1, ns), h0im.reshape(bn, 1, ns), *consts,
      d_skip.reshape(d_skip.shape[0], 1, c), w_glu, b_glu.reshape(b_glu.shape[0], 1, c))
    return y, hre.reshape(bn, ns), him.reshape(bn, ns)


def _s5_step_kernel(u_ref, h0re_ref, h0im_ref, wbre_ref, wbim_ref, wcre_ref, wcim_ref, lbre_ref, lbim_ref, d_ref,
                    wglu_ref, bglu_ref, y_ref, hre_ref, him_ref, bre_ref, bim_ref, *, seq, bn, lane_chunk):
    u = u_ref[...]
    _s5_input_proj(u, wbre_ref, wbim_ref, bre_ref, bim_ref)
    ns = bre_ref.shape[1]
    for c in range(ns // lane_chunk):
        sl = slice(c * lane_chunk, (c + 1) * lane_chunk)
        lr = lbre_ref[:, sl]
        li = lbim_ref[:, sl]
        hr = h0re_ref[:, sl]
        hi = h0im_ref[:, sl]
        for t in range(seq):
            rows = slice(t * bn, (t + 1) * bn)
            hr, hi = (lr * hr - li * hi) + bre_ref[rows, sl], (lr * hi + li * hr) + bim_ref[rows, sl]
            bre_ref[rows, sl] = hr
            bim_ref[rows, sl] = hi
        hre_ref[:, sl] = hr
        him_ref[:, sl] = hi
    _s5_output(u, bre_ref, bim_ref, wcre_ref, wcim_ref, d_ref, wglu_ref, bglu_ref, y_ref)


def _s5_step(proj, h0re, h0im, tb, d_skip, w_glu, b_glu, l, *, bn, seq, lane_chunk=128):
    c = w_glu.shape[-1]
    ns = h0re.shape[-1]
    t = bn * seq
    consts = [tb["wb_re"], tb["wb_im"], tb["wc_re"], tb["wc_imneg"], tb["lb_re"], tb["lb_im"]]
    in_specs = ([pl.BlockSpec((t, c), lambda i: (0, 0)),
                 pl.BlockSpec((bn, ns), lambda i: (0, 0)),
                 pl.BlockSpec((bn, ns), lambda i: (0, 0))]
                + [_const_spec(a) for a in consts]
                + [pl.BlockSpec((None, 1, c), lambda i: (l, 0, 0)),
                   pl.BlockSpec((None, c, c), lambda i: (l, 0, 0), pipeline_mode=pl.Buffered(1)),
                   pl.BlockSpec((None, 1, c), lambda i: (l, 0, 0))])
    return pl.pallas_call(
        functools.partial(_s5_step_kernel, seq=seq, bn=bn, lane_chunk=lane_chunk),
        grid=(1,),
        in_specs=in_specs,
        out_specs=[pl.BlockSpec((t, c), lambda i: (0, 0)),
                   pl.BlockSpec((bn, ns), lambda i: (0, 0)),
                   pl.BlockSpec((bn, ns), lambda i: (0, 0))],
        out_shape=[jax.ShapeDtypeStruct((t, c), F32),
                   jax.ShapeDtypeStruct((bn, ns), F32),
                   jax.ShapeDtypeStruct((bn, ns), F32)],
        scratch_shapes=[pltpu.VMEM((t, ns), F32), pltpu.VMEM((t, ns), F32)],
        compiler_params=_params("arbitrary"),
        name="s5_step",
    )(proj, h0re, h0im, *consts,
      d_skip.reshape(d_skip.shape[0], 1, c), w_glu, b_glu.reshape(b_glu.shape[0], 1, c))


def _rg_tables(w_a, w_i):
    nh, hd, _ = w_a.shape
    hpc = V7X_MXU_DIM // hd
    eye = jnp.eye(hpc, dtype=F32)

    def expand(w):
        w = w.reshape(nh // hpc, hpc, hd, hd)
        return jnp.einsum("khij,hg->khigj", w, eye).reshape(nh // hpc, hpc * hd, hpc * hd)

    return expand(w_a), expand(w_i)


def _rg_gates(xc, wa_ref, wi_ref, ba_ref, bi_ref, lam_ref, a_ref, b_ref, rows):
    nck, kc, _ = wa_ref.shape
    lam = lam_ref[...]
    neg = -lam
    softplus = jnp.maximum(neg, 0.0) + jnp.log(1.0 + jnp.exp(-jnp.abs(neg)))
    for k in range(nck):
        cs = slice(k * kc, (k + 1) * kc)
        xk = xc[:, cs]
        r = _sigmoid(_dot(xk, wa_ref[k]) + ba_ref[:, cs])
        i = _sigmoid(_dot(xk, wi_ref[k]) + bi_ref[:, cs])
        log_a = (-RG_C * r) * softplus[:, cs]
        a_ref[rows, cs] = jnp.exp(log_a)
        th = jnp.tanh(log_a)
        b_ref[rows, cs] = jnp.sqrt(-2.0 * th / (1.0 - th)) * (i * xk)


def _rg_seq_kernel(x_ref, gy_ref, conv0_ref, h0_ref, cw_ref, cb_ref, wa_ref, wi_ref, ba_ref, bi_ref, lam_ref,
                   y_ref, hout_ref, xe_ref, a_ref, b_ref, c_ref, *, tm):
    s = pl.program_id(1)
    pad = V7X_SUBLANES
    nconv = conv0_ref.shape[0]

    @pl.when(s == 0)
    def _():
        xe_ref[pad - nconv:pad, :] = conv0_ref[...]
        c_ref[...] = jnp.broadcast_to(h0_ref[...], c_ref.shape)

    @pl.when(s > 0)
    def _():
        xe_ref[0:pad, :] = xe_ref[tm:tm + pad, :]

    xe_ref[pad:pad + tm, :] = x_ref[...]
    acc = cw_ref[0:1, :] * xe_ref[pad - nconv:pad - nconv + tm, :]
    for k in range(1, nconv + 1):
        acc = acc + cw_ref[k:k + 1, :] * xe_ref[pad - nconv + k:pad - nconv + k + tm, :]
    xc = cb_ref[...] + acc
    _rg_gates(xc, wa_ref, wi_ref, ba_ref, bi_ref, lam_ref, a_ref, b_ref, slice(None))

    row = lax.broadcasted_iota(jnp.int32, (V7X_SUBLANES, a_ref.shape[1]), 0)

    def body(r, carry):
        rows = pl.ds(pl.multiple_of(r * V7X_SUBLANES, V7X_SUBLANES), V7X_SUBLANES)
        a = a_ref[rows, :]
        b = b_ref[rows, :]
        for k in (1, 2, 4):
            keep = row >= k
            b = jnp.where(keep, b + a * pltpu.roll(b, k, 0), b)
            a = jnp.where(keep, a * pltpu.roll(a, k, 0), a)
        h = b + a * carry
        b_ref[rows, :] = h
        last = V7X_SUBLANES - 1
        return jnp.broadcast_to(h[last:last + 1], h.shape)

    carry = lax.fori_loop(0, tm // V7X_SUBLANES, body, c_ref[...])
    c_ref[...] = carry
    y_ref[...] = b_ref[...] * gy_ref[...]

    @pl.when(s == pl.num_programs(1) - 1)
    def _():
        hout_ref[...] = c_ref[0:1, :]


def _rg_seq(proj, conv0, h0, conv_w, conv_b, wa, wi, b_a, b_i, lam, l, *, bn, seq, tm=256):
    c = h0.shape[-1]
    n_s = seq // tm
    depth = conv_w.shape[0]
    vec = lambda a: a.reshape(depth, 1, c)
    y, hout = pl.pallas_call(
        functools.partial(_rg_seq_kernel, tm=tm),
        grid=(bn, n_s),
        in_specs=[pl.BlockSpec((tm, c), lambda b, s: (b * n_s + s, 1)),
                  pl.BlockSpec((tm, c), lambda b, s: (b * n_s + s, 2)),
                  pl.BlockSpec((None, CONV_W - 1, c), lambda b, s: (b, 0, 0)),
                  pl.BlockSpec((None, 1, c), lambda b, s: (b, 0, 0)),
                  pl.BlockSpec((None, CONV_W, c), lambda b, s: (l, 0, 0)),
                  pl.BlockSpec((None, 1, c), lambda b, s: (l, 0, 0)),
                  _const_spec(wa), _const_spec(wi),
                  pl.BlockSpec((None, 1, c), lambda b, s: (l, 0, 0)),
                  pl.BlockSpec((None, 1, c), lambda b, s: (l, 0, 0)),
                  pl.BlockSpec((None, 1, c), lambda b, s: (l, 0, 0))],
        out_specs=[pl.BlockSpec((tm, c), lambda b, s: (b * n_s + s, 0)),
                   pl.BlockSpec((None, 1, c), lambda b, s: (b, 0, 0))],
        out_shape=[jax.ShapeDtypeStruct((bn * seq, c), F32), jax.ShapeDtypeStruct((bn, 1, c), F32)],
        scratch_shapes=[pltpu.VMEM((tm + 2 * V7X_SUBLANES, c), F32), pltpu.VMEM((tm, c), F32),
                        pltpu.VMEM((tm, c), F32), pltpu.VMEM((V7X_SUBLANES, c), F32)],
        compiler_params=_params("arbitrary", "arbitrary"),
        name="rg_seq",
    )(proj, proj, conv0, h0.reshape(bn, 1, c), conv_w, vec(conv_b), wa, wi, vec(b_a), vec(b_i), vec(lam))
    return y, hout.reshape(bn, c)


def _rg_step_kernel(x_ref, gy_ref, conv0_ref, h0_ref, cw_ref, cb_ref, wa_ref, wi_ref, ba_ref, bi_ref, lam_ref,
                    y_ref, hout_ref, a_ref, b_ref, *, seq, bn):
    nconv = conv0_ref.shape[0]

    def xpad(i):
        if i < nconv:
            return conv0_ref[i]
        return x_ref[(i - nconv) * bn:(i - nconv + 1) * bn, :]

    for t in range(seq):
        acc = cw_ref[0:1, :] * xpad(t)
        for k in range(1, nconv + 1):
            acc = acc + cw_ref[k:k + 1, :] * xpad(t + k)
        xc = cb_ref[...] + acc
        _rg_gates(xc, wa_ref, wi_ref, ba_ref, bi_ref, lam_ref, a_ref, b_ref, slice(t * bn, (t + 1) * bn))
    h = h0_ref[...]
    for t in range(seq):
        rows = slice(t * bn, (t + 1) * bn)
        h = a_ref[rows, :] * h + b_ref[rows, :]
        y_ref[rows, :] = h * gy_ref[rows, :]
    hout_ref[...] = h


def _rg_step(proj, conv0_tm, h0, conv_w, conv_b, wa, wi, b_a, b_i, lam, l, *, bn, seq):
    c = h0.shape[-1]
    t = bn * seq
    depth = conv_w.shape[0]
    vec = lambda a: a.reshape(depth, 1, c)
    lspec = pl.BlockSpec((None, 1, c), lambda i: (l, 0, 0))
    return pl.pallas_call(
        functools.partial(_rg_step_kernel, seq=seq, bn=bn),
        grid=(1,),
        in_specs=[pl.BlockSpec((t, c), lambda i: (0, 1)),
                  pl.BlockSpec((t, c), lambda i: (0, 2)),
                  pl.BlockSpec(conv0_tm.shape, lambda i: (0, 0, 0)),
                  pl.BlockSpec((bn, c), lambda i: (0, 0)),
                  pl.BlockSpec((None, CONV_W, c), lambda i: (l, 0, 0)),
                  lspec, _const_spec(wa), _const_spec(wi), lspec, lspec, lspec],
        out_specs=[pl.BlockSpec((t, c), lambda i: (0, 0)), pl.BlockSpec((bn, c), lambda i: (0, 0))],
        out_shape=[jax.ShapeDtypeStruct((t, c), F32), jax.ShapeDtypeStruct((bn, c), F32)],
        scratch_shapes=[pltpu.VMEM((t, c), F32), pltpu.VMEM((t, c), F32)],
        compiler_params=_params("arbitrary"),
        name="rg_step",
    )(proj, proj, conv0_tm, h0, conv_w, vec(conv_b), wa, wi, vec(b_a), vec(b_i), vec(lam))


def _merge_kernel(ys_ref, yr_ref, ws_ref, wr_ref, gs_ref, gr_ref, o_ref):
    o_ref[...] = gs_ref[...] * _dot(ys_ref[...], ws_ref[...]) + gr_ref[...] * _dot(yr_ref[...], wr_ref[...])


def _branch_merge(ys, yr, w_s, w_r, gates, l, *, tm, tn):
    t, c = ys.shape
    d = w_s.shape[-1]
    off = d // tn
    return pl.pallas_call(
        _merge_kernel,
        grid=(t // tm, d // tn),
        in_specs=[pl.BlockSpec((tm, c), lambda m, n: (m, 0)),
                  pl.BlockSpec((tm, c), lambda m, n: (m, 0)),
                  pl.BlockSpec((None, c, tn), lambda m, n: (l, 0, n)),
                  pl.BlockSpec((None, c, tn), lambda m, n: (l, 0, n)),
                  pl.BlockSpec((tm, tn), lambda m, n: (m, n)),
                  pl.BlockSpec((tm, tn), lambda m, n: (m, n + off))],
        out_specs=pl.BlockSpec((tm, tn), lambda m, n: (m, n)),
        out_shape=jax.ShapeDtypeStruct((t, d), F32),
        compiler_params=_params("arbitrary", "arbitrary"),
        name="branch_merge",
    )(ys, yr, w_s, w_r, gates, gates)


def _linear_residual_kernel(a_ref, w_ref, x_ref, g_ref, o_ref, acc_ref):
    k = pl.program_id(2)

    @pl.when(k == 0)
    def _():
        acc_ref[...] = jnp.zeros_like(acc_ref)

    acc_ref[...] += _dot(a_ref[...], w_ref[...])

    @pl.when(k == pl.num_programs(2) - 1)
    def _():
        o_ref[...] = x_ref[...] + g_ref[...] * acc_ref[...]


def _linear_residual(a, w, l, x, mod, j_g, *, tm, tn, tk, tiles_per_block):
    t, kdim = a.shape
    d = w.shape[-1]
    rows = mod.shape[2]
    return pl.pallas_call(
        _linear_residual_kernel,
        grid=(t // tm, d // tn, kdim // tk),
        in_specs=[pl.BlockSpec((tm, tk), lambda m, n, k: (m, k)),
                  pl.BlockSpec((None, tk, tn), lambda m, n, k: (l, k, n)),
                  pl.BlockSpec((tm, tn), lambda m, n, k: (m, n)),
                  _mod_spec(j_g, rows, tn, tiles_per_block, ncol_axis=0)],
        out_specs=pl.BlockSpec((tm, tn), lambda m, n, k: (m, n)),
        out_shape=jax.ShapeDtypeStruct((t, d), F32),
        scratch_shapes=[pltpu.VMEM((tm, tn), F32)],
        compiler_params=_params("arbitrary", "arbitrary", "arbitrary"),
        name="linear_residual",
    )(a, w, x, mod)


def _moe_down_kernel(a_ref, w_ref, ce_ref, x_ref, g_ref, o_ref, acc_ref, y_ref):
    e = pl.program_id(2)
    k = pl.program_id(3)
    last_k = k == pl.num_programs(3) - 1

    @pl.when((e == 0) & (k == 0))
    def _():
        acc_ref[...] = jnp.zeros_like(acc_ref)

    @pl.when(k == 0)
    def _():
        y_ref[...] = jnp.zeros_like(y_ref)

    y_ref[...] += _dot(a_ref[...], w_ref[...])

    @pl.when(last_k)
    def _():
        acc_ref[...] += ce_ref[...] * y_ref[...]

    @pl.when(last_k & (e == pl.num_programs(2) - 1))
    def _():
        o_ref[...] = x_ref[...] + g_ref[...] * acc_ref[...]


def _moe_down(act, w2, lm, comb_t, x, mod, j_g, *, tm, tn, tk, tiles_per_block):
    ne, t, f = act.shape
    d = w2.shape[-1]
    rows = mod.shape[2]
    return pl.pallas_call(
        _moe_down_kernel,
        grid=(t // tm, d // tn, ne, f // tk),
        in_specs=[pl.BlockSpec((None, tm, tk), lambda m, n, e, k: (e, m, k)),
                  pl.BlockSpec((None, None, tk, tn), lambda m, n, e, k: (lm, e, k, n)),
                  pl.BlockSpec((None, tm, 1), lambda m, n, e, k: (e, m, 0)),
                  pl.BlockSpec((tm, tn), lambda m, n, e, k: (m, n)),
                  _mod_spec(j_g, rows, tn, tiles_per_block, ncol_axis=0)],
        out_specs=pl.BlockSpec((tm, tn), lambda m, n, e, k: (m, n)),
        out_shape=jax.ShapeDtypeStruct((t, d), F32),
        scratch_shapes=[pltpu.VMEM((tm, tn), F32), pltpu.VMEM((tm, tn), F32)],
        compiler_params=_params("arbitrary", "arbitrary", "arbitrary", "arbitrary"),
        name="moe_down",
    )(act, w2, comb_t, x, mod)


def _router_kernel(x_ref, sh_ref, sc_ref, g_ref, wr_ref, hn_ref, comb_ref, *, n_experts):
    _norm_mod_rows(x_ref, g_ref, sc_ref, sh_ref, hn_ref)
    logits = jnp.dot(hn_ref[...], wr_ref[...], preferred_element_type=F32, precision=lax.Precision.HIGHEST)
    lane = lax.broadcasted_iota(jnp.int32, logits.shape, 1).astype(F32)
    width = float(logits.shape[-1])
    lg = jnp.where(lane < n_experts, logits, -jnp.inf)
    v1 = jnp.max(lg, axis=-1, keepdims=True)
    i1 = jnp.min(jnp.where(lg == v1, lane, width), axis=-1, keepdims=True)
    lg2 = jnp.where(lane == i1, -jnp.inf, lg)
    v2 = jnp.max(lg2, axis=-1, keepdims=True)
    i2 = jnp.min(jnp.where(lg2 == v2, lane, width), axis=-1, keepdims=True)
    e2 = jnp.exp(v2 - v1)
    den = 1.0 + e2
    comb_ref[...] = jnp.where(lane == i1, 1.0 / den, 0.0) + jnp.where(lane == i2, e2 / den, 0.0)


def _router(x, mod, j_sh, g, w_router_pad, l, *, tm, tiles_per_block, n_experts):
    t, d = x.shape
    rows = mod.shape[2]
    width = w_router_pad.shape[-1]
    return pl.pallas_call(
        functools.partial(_router_kernel, n_experts=n_experts),
        grid=(t // tm,),
        in_specs=[pl.BlockSpec((tm, d), lambda m: (m, 0)),
                  _mod_spec(j_sh, rows, d, tiles_per_block),
                  _mod_spec(j_sh + 1, rows, d, tiles_per_block),
                  pl.BlockSpec((None, 1, d), lambda m: (l, 0, 0)),
                  pl.BlockSpec((d, width), lambda m: (0, 0))],
        out_specs=[pl.BlockSpec((tm, d), lambda m: (m, 0)), pl.BlockSpec((tm, width), lambda m: (m, 0))],
        out_shape=[jax.ShapeDtypeStruct((t, d), F32), jax.ShapeDtypeStruct((t, width), F32)],
        compiler_params=_params("arbitrary"),
        name="moe_router",
    )(x, mod, mod, g.reshape(g.shape[0], 1, d), w_router_pad)


def _moe_up_kernel(h_ref, w1_ref, w3_ref, o_ref):
    h = h_ref[...]
    o_ref[...] = _silu(_dot(h, w1_ref[...])) * _dot(h, w3_ref[...])


def _moe_up(hn, w1, w3, lm, *, tm, tn):
    t, d = hn.shape
    ne, f = w1.shape[1], w1.shape[-1]
    wspec = pl.BlockSpec((None, None, d, tn), lambda e, m, n: (lm, e, 0, n))
    return pl.pallas_call(
        _moe_up_kernel,
        grid=(ne, t // tm, f // tn),
        in_specs=[pl.BlockSpec((tm, d), lambda e, m, n: (m, 0)), wspec, wspec],
        out_specs=pl.BlockSpec((None, tm, tn), lambda e, m, n: (e, m, n)),
        out_shape=jax.ShapeDtypeStruct((ne, t, f), F32),
        compiler_params=_params("arbitrary", "arbitrary", "arbitrary"),
        name="moe_up",
    )(hn, w1, w3)


def _final_norm_kernel(x_ref, g_ref, o_ref):
    o_ref[...] = _rmsnorm(x_ref[...], g_ref[...])


def _final_norm(x, g, *, tm):
    t, d = x.shape
    return pl.pallas_call(
        _final_norm_kernel,
        grid=(t // tm,),
        in_specs=[pl.BlockSpec((tm, d), lambda m: (m, 0)), pl.BlockSpec((1, d), lambda m: (0, 0))],
        out_specs=pl.BlockSpec((tm, d), lambda m: (m, 0)),
        out_shape=jax.ShapeDtypeStruct((t, d), F32),
        compiler_params=_params("arbitrary"),
        name="final_norm",
    )(x, g.reshape(1, d))


def _run(x, mods, state, p, *, bn, seq, time_major, tm):
    t, d = x.shape
    depth = p["w_in"].shape[0]
    c = p["s5_w_glu"].shape[-1]
    def tiled(tile):
        return dict(tm=tile, tiles_per_block=1 if time_major else seq // tile)

    kw = tiled(tm)
    nconv = CONV_W - 1
    assert seq >= nconv
    s5_re, s5_im, rg_h, conv = [], [], [], []
    for l in range(depth):
        mod = mods[l]
        tb = p["s5_tables"][l]
        wa, wi = p["rg_tables"][l]
        proj = _normed_linear(x, mod, 0, p["norm_mix"], l, [p["w_in"]], l, mode="in", tn=512,
                              plain_cols=2 * c, **kw)
        gates = _normed_linear(x, mod, 0, p["norm_mix"], l, [p["w_gate"]], l, mode="gate", tn=512,
                               bias=p["b_gate"], **kw)
        h0re = state["s5_re"][l].reshape(bn, -1)
        h0im = state["s5_im"][l].reshape(bn, -1)
        rg_args = (p["rg_conv_w"], p["rg_conv_b"], wa, wi, p["rg_b_a"].reshape(depth, c), p["rg_b_i"].reshape(depth, c),
                   p["rg_lam"], l)
        if time_major:
            y_s5, hre, him = _s5_step(proj, h0re, h0im, tb, p["s5_d"].reshape(depth, c), p["s5_w_glu"], p["s5_b_glu"], l,
                                      bn=bn, seq=seq)
            y_rg, hrg = _rg_step(proj, state["conv"][l].transpose(1, 0, 2), state["rg"][l], *rg_args, bn=bn, seq=seq)
            conv.append(proj[:, c:2 * c].reshape(seq, bn, c)[seq - nconv:].transpose(1, 0, 2))
        else:
            y_s5, hre, him = _s5_seq(proj, h0re, h0im, tb, p["s5_d"].reshape(depth, c), p["s5_w_glu"], p["s5_b_glu"], l,
                                     bn=bn, seq=seq)
            y_rg, hrg = _rg_seq(proj, state["conv"][l], state["rg"][l], *rg_args, bn=bn, seq=seq)
            conv.append(proj.reshape(bn, seq, 3 * c)[:, seq - nconv:, c:2 * c])
        s5_re.append(hre.reshape(state["s5_re"][l].shape))
        s5_im.append(him.reshape(state["s5_im"][l].shape))
        rg_h.append(hrg)
        merged = _branch_merge(y_s5, y_rg, p["w_br_s5"], p["w_br_rg"], gates, l, tm=tm, tn=512)
        x = _linear_residual(merged, p["w_out"], l, x, mod, 2, tn=1024, tk=1024, **kw)
        if l % 2 == 0:
            act = _normed_linear(x, mod, 3, p["norm_ffn"], l, [p["ffn_w1"], p["ffn_w3"]], l // 2, mode="glu",
                                 tn=512, **kw)
            x = _linear_residual(act, p["ffn_w2"], l // 2, x, mod, 5, tn=1024, tk=1024, **kw)
        else:
            lm = l // 2
            ne = p["moe_router"].shape[-1]
            w_r = jnp.pad(p["moe_router"][lm], ((0, 0), (0, V7X_LANES - ne)))
            hn, comb = _router(x, mod, 3, p["norm_ffn"], w_r, l, n_experts=ne, **tiled(min(tm, 512)))
            act = _moe_up(hn, p["moe_w1"], p["moe_w3"], lm, tm=tm, tn=512)
            comb_t = comb[:, :ne].T.reshape(ne, t, 1)
            x = _moe_down(act, p["moe_w2"], lm, comb_t, x, mod, 5, tn=1024, tk=1024, **kw)
    y = _final_norm(x, p["norm_f"], tm=min(tm, 256))
    return y, jnp.stack(s5_re), jnp.stack(s5_im), jnp.stack(rg_h), jnp.stack(conv)


def kernel(x_prompt, x_sample, c_prompt, c_sample, state_s5_re, state_s5_im, state_rglru, state_conv, norm_mix, norm_ffn, norm_f, w_ada, b_ada, w_in, s5_lam_re, s5_lam_im, s5_log_dt, s5_b_re, s5_b_im, s5_c_re, s5_c_im, s5_d, s5_w_glu, s5_b_glu, rg_conv_w, rg_conv_b, rg_w_a, rg_b_a, rg_w_i, rg_b_i, rg_lam, w_gate, b_gate, w_br_s5, w_br_rg, w_out, ffn_w1, ffn_w3, ffn_w2, moe_router, moe_w1, moe_w3, moe_w2):
    p = dict(norm_mix=norm_mix, norm_ffn=norm_ffn, norm_f=norm_f, w_in=w_in,
             s5_lam_re=s5_lam_re, s5_lam_im=s5_lam_im, s5_log_dt=s5_log_dt, s5_b_re=s5_b_re, s5_b_im=s5_b_im,
             s5_c_re=s5_c_re, s5_c_im=s5_c_im, s5_d=s5_d, s5_w_glu=s5_w_glu, s5_b_glu=s5_b_glu,
             rg_conv_w=rg_conv_w, rg_conv_b=rg_conv_b, rg_w_a=rg_w_a, rg_b_a=rg_b_a, rg_w_i=rg_w_i, rg_b_i=rg_b_i,
             rg_lam=rg_lam, w_gate=w_gate, b_gate=b_gate, w_br_s5=w_br_s5, w_br_rg=w_br_rg, w_out=w_out,
             ffn_w1=ffn_w1, ffn_w3=ffn_w3, ffn_w2=ffn_w2, moe_router=moe_router, moe_w1=moe_w1, moe_w3=moe_w3,
             moe_w2=moe_w2)
    depth = w_in.shape[0]
    p["s5_tables"] = [_s5_tables(s5_lam_re[l], s5_lam_im[l], s5_log_dt[l], s5_b_re[l], s5_b_im[l], s5_c_re[l],
                                 s5_c_im[l]) for l in range(depth)]
    p["rg_tables"] = [_rg_tables(rg_w_a[l], rg_w_i[l]) for l in range(depth)]
    bp, sp, d = x_prompt.shape
    bs, ss, _ = x_sample.shape
    g, pst = state_s5_re.shape[2:]
    c = state_rglru.shape[-1]

    n_c = bp + bs
    rows_c = -(-n_c // V7X_SUBLANES) * V7X_SUBLANES
    c_all = jnp.pad(jnp.concatenate([c_prompt, c_sample], axis=0), ((0, rows_c - n_c), (0, 0)))
    mod_all = _ada_mod(c_all, w_ada, b_ada)
    mods_p, mods_s = [], []
    for l in range(depth):
        mp = mod_all[l, :bp].reshape(bp, 6, 1, d).transpose(1, 0, 2, 3)
        ms = mod_all[l, bp:n_c].reshape(bs, 6, d).transpose(1, 0, 2)
        ms = jnp.tile(ms, (1, ss, 1)).reshape(6, 1, ss * bs, d)
        mods_p.append(mp)
        mods_s.append(ms)

    zero_state = dict(s5_re=jnp.zeros((depth, bp, g, pst), F32), s5_im=jnp.zeros((depth, bp, g, pst), F32),
                      rg=jnp.zeros((depth, bp, c), F32), conv=jnp.zeros((depth, bp, CONV_W - 1, c), x_prompt.dtype))
    y_p, p_re, p_im, p_rg, p_conv = _run(x_prompt.reshape(bp * sp, d), mods_p, zero_state, p,
                                         bn=bp, seq=sp, time_major=False, tm=min(PROMPT_TILE, sp))
    sample_state = dict(s5_re=state_s5_re, s5_im=state_s5_im, rg=state_rglru, conv=state_conv)
    x_s = x_sample.transpose(1, 0, 2).reshape(ss * bs, d)
    y_s, s_re, s_im, s_rg, s_conv = _run(x_s, mods_s, sample_state, p, bn=bs, seq=ss, time_major=True, tm=ss * bs)
    y_s = y_s.reshape(ss, bs, d).transpose(1, 0, 2)
    return (y_p.reshape(bp, sp, d), y_s, p_re, p_im, p_rg, p_conv, s_re, s_im, s_rg, s_conv)
```

```python
import functools
import math

import jax
import jax.numpy as jnp
from jax import lax
from jax.experimental import pallas as pl
from jax.experimental.pallas import tpu as pltpu

F32 = jnp.float32
EPS = 1e-6
RG_C = 8.0
CONV_W = 4
S5_GROUP = 16
S5_STATE = 64
RG_HEAD_DIM = 64
TOP_K = 2

V7X_LANES = 128
V7X_SUBLANES = 8
V7X_MXU_DIM = 256
VMEM_LIMIT_BYTES = 56 * 1024 * 1024

GELU_C = math.sqrt(2.0 / math.pi)


def _gelu(x):
    return 0.5 * x * (1.0 + jnp.tanh(GELU_C * (x + 0.044715 * (x * x * x))))


def _sigmoid(x):
    return 1.0 / (1.0 + jnp.exp(-x))


def _silu(x):
    return x * _sigmoid(x)


def _dot(a, b):
    return jnp.dot(a, b, preferred_element_type=F32)


def _params(*sem):
    return pltpu.CompilerParams(dimension_semantics=sem, vmem_limit_bytes=VMEM_LIMIT_BYTES)


def _mod_spec(j, rows, width, tiles_per_block, tile, ncol_axis=None):
    col = (lambda idx: 0) if ncol_axis is None else (lambda idx: idx[ncol_axis])
    if rows == 1:
        return pl.BlockSpec((None, None, 1, width), lambda m, *idx: (j, m // tiles_per_block, 0, col(idx)))
    return pl.BlockSpec((None, None, tile, width), lambda m, *idx: (j, 0, m, col(idx)))


ROW_CHUNK = 128
PROMPT_TILE = 1024


def _rmsnorm(x, g):
    ms = jnp.mean(x * x, axis=-1, keepdims=True)
    return (x * lax.rsqrt(ms + EPS)) * g


def _norm_mod_rows(x_ref, g_ref, sc_ref, sh_ref, o_ref):
    tm = x_ref.shape[0]
    chunk = min(ROW_CHUNK, tm)
    per_row = sc_ref.shape[0] != 1

    def body(i, carry):
        rows = pl.ds(pl.multiple_of(i * chunk, chunk), chunk)
        sc = sc_ref[rows, :] if per_row else sc_ref[...]
        sh = sh_ref[rows, :] if per_row else sh_ref[...]
        o_ref[rows, :] = _rmsnorm(x_ref[rows, :], g_ref[...]) * (1.0 + sc) + sh
        return carry

    lax.fori_loop(0, tm // chunk, body, 0)


def _ada_kernel(c_ref, w_ref, b_ref, o_ref):
    o_ref[...] = _dot(_silu(c_ref[...]), w_ref[...]) + b_ref[...]


def _ada_mod(c_all, w_ada, b_ada, *, tn=1024):
    depth, d, n6 = w_ada.shape
    rows = c_all.shape[0]
    return pl.pallas_call(
        _ada_kernel,
        grid=(depth, n6 // tn),
        in_specs=[pl.BlockSpec((rows, d), lambda l, n: (0, 0)),
                  pl.BlockSpec((None, d, tn), lambda l, n: (l, 0, n)),
                  pl.BlockSpec((None, 1, tn), lambda l, n: (l, 0, n))],
        out_specs=pl.BlockSpec((None, rows, tn), lambda l, n: (l, 0, n)),
        out_shape=jax.ShapeDtypeStruct((depth, rows, n6), F32),
        compiler_params=_params("arbitrary", "arbitrary"),
        name="ada_mod",
    )(c_all, w_ada, b_ada.reshape(depth, 1, n6))


def _normed_linear_kernel(*refs, mode, n_plain):
    if mode == "glu":
        x_ref, sh_ref, sc_ref, g_ref, w1_ref, w3_ref, o_ref, hn_ref = refs
    elif mode == "gate":
        x_ref, sh_ref, sc_ref, g_ref, w_ref, b_ref, o_ref, hn_ref = refs
    else:
        x_ref, sh_ref, sc_ref, g_ref, w_ref, o_ref, hn_ref = refs
    n = pl.program_id(1)

    @pl.when(n == 0)
    def _():
        _norm_mod_rows(x_ref, g_ref, sc_ref, sh_ref, hn_ref)

    h = hn_ref[...]
    if mode == "glu":
        o_ref[...] = _silu(_dot(h, w1_ref[...])) * _dot(h, w3_ref[...])
    elif mode == "gate":
        o_ref[...] = _sigmoid(_dot(h, w_ref[...]) + b_ref[...])
    else:
        z = _dot(h, w_ref[...])

        @pl.when(n < n_plain)
        def _():
            o_ref[...] = z

        @pl.when(n >= n_plain)
        def _():
            o_ref[...] = _gelu(z)


def _normed_linear(x, mod, j_sh, g, l_g, ws, l, *, mode, tm, tn, tiles_per_block, bias=None, plain_cols=0):
    t, d = x.shape
    n_out = ws[0].shape[-1]
    rows = mod.shape[2]
    w_spec = pl.BlockSpec((None, d, tn), lambda m, n: (l, 0, n))
    in_specs = [pl.BlockSpec((tm, d), lambda m, n: (m, 0)),
                _mod_spec(j_sh, rows, d, tiles_per_block, tm),
                _mod_spec(j_sh + 1, rows, d, tiles_per_block, tm),
                pl.BlockSpec((None, 1, d), lambda m, n: (l_g, 0, 0))]
    args = [x, mod, mod, g.reshape(g.shape[0], 1, d)]
    for w in ws:
        in_specs.append(w_spec)
        args.append(w)
    if bias is not None:
        in_specs.append(pl.BlockSpec((None, 1, tn), lambda m, n: (l, 0, n)))
        args.append(bias.reshape(bias.shape[0], 1, n_out))
    return pl.pallas_call(
        functools.partial(_normed_linear_kernel, mode=mode, n_plain=plain_cols // tn),
        grid=(t // tm, n_out // tn),
        in_specs=in_specs,
        out_specs=pl.BlockSpec((tm, tn), lambda m, n: (m, n)),
        out_shape=jax.ShapeDtypeStruct((t, n_out), F32),
        scratch_shapes=[pltpu.VMEM((tm, d), F32)],
        compiler_params=_params("arbitrary", "arbitrary"),
        name="normed_linear_" + mode,
    )(*args)


def _s5_tables(lam_re, lam_im, log_dt, b_re, b_im, c_re, c_im):
    g, p = lam_re.shape
    h = b_re.shape[-1]
    gpc = V7X_MXU_DIM // h
    nck = g // gpc
    dt = jnp.exp(log_dt)[:, None]
    mag = jnp.exp(lam_re * dt)
    lb_re = mag * jnp.cos(lam_im * dt)
    lb_im = mag * jnp.sin(lam_im * dt)
    den = lam_re * lam_re + lam_im * lam_im
    cf_re = ((lb_re - 1.0) * lam_re + lb_im * lam_im) / den
    cf_im = (lb_im * lam_re - (lb_re - 1.0) * lam_im) / den
    bb_re = cf_re[:, :, None] * b_re - cf_im[:, :, None] * b_im
    bb_im = cf_re[:, :, None] * b_im + cf_im[:, :, None] * b_re
    eye = jnp.eye(gpc, dtype=F32)

    def expand_b(m):
        m = m.reshape(nck, gpc, p, h).transpose(0, 1, 3, 2)
        return jnp.einsum("kghp,gj->kghjp", m, eye).reshape(nck, gpc * h, gpc * p)

    def expand_c(m):
        m = m.reshape(nck, gpc, h, p).transpose(0, 1, 3, 2)
        return jnp.einsum("kgph,gj->kgpjh", m, eye).reshape(nck, gpc * p, gpc * h)

    lb_re = lb_re.reshape(1, g * p)
    lb_im = lb_im.reshape(1, g * p)
    pw_re, pw_im = [lb_re], [lb_im]
    for _ in range(V7X_SUBLANES - 1):
        pw_re, pw_im = (pw_re + [pw_re[-1] * lb_re - pw_im[-1] * lb_im],
                        pw_im + [pw_re[-1] * lb_im + pw_im[-1] * lb_re])
    row = jnp.arange(V7X_SUBLANES)[:, None]
    tabs = []
    for k in (1, 2, 4):
        tabs += [jnp.where(row >= k, pw_re[k - 1], 0.0), jnp.where(row >= k, pw_im[k - 1], 0.0)]
    tabs += [jnp.concatenate(pw_re, axis=0), jnp.concatenate(pw_im, axis=0)]
    return dict(wb_re=expand_b(bb_re), wb_im=expand_b(bb_im),
                wc_re=expand_c(c_re), wc_imneg=expand_c(-c_im),
                tabs=jnp.stack(tabs).astype(F32),
                lb_re=lb_re, lb_im=lb_im)


def _s5_input_proj(u, wbre_ref, wbim_ref, bre_ref, bim_ref):
    nck, kc, nc = wbre_ref.shape
    for k in range(nck):
        uk = u[:, k * kc:(k + 1) * kc]
        bre_ref[:, k * nc:(k + 1) * nc] = _dot(uk, wbre_ref[k])
        bim_ref[:, k * nc:(k + 1) * nc] = _dot(uk, wbim_ref[k])


def _s5_output(u, bre_ref, bim_ref, wcre_ref, wcim_ref, d_ref, wglu_ref, bglu_ref, y_ref):
    nck, nc, kc = wcre_ref.shape
    for k in range(nck):
        yk = _dot(bre_ref[:, k * nc:(k + 1) * nc], wcre_ref[k]) + _dot(bim_ref[:, k * nc:(k + 1) * nc], wcim_ref[k])
        yk = yk + d_ref[:, k * kc:(k + 1) * kc] * u[:, k * kc:(k + 1) * kc]
        y_ref[:, k * kc:(k + 1) * kc] = _gelu(yk)
    yg = y_ref[...]
    y_ref[...] = yg * _sigmoid(_dot(yg, wglu_ref[...]) + bglu_ref[...])


def _s5_seq_kernel(u_ref, h0re_ref, h0im_ref, wbre_ref, wbim_ref, wcre_ref, wcim_ref, tabs_ref, d_ref,
                   wglu_ref, bglu_ref, y_ref, hre_ref, him_ref, bre_ref, bim_ref, cre_ref, cim_ref,
                   *, tm, lane_chunk):
    s = pl.program_id(1)
    ns = cre_ref.shape[1]

    @pl.when(s == 0)
    def _():
        cre_ref[...] = jnp.broadcast_to(h0re_ref[...], cre_ref.shape)
        cim_ref[...] = jnp.broadcast_to(h0im_ref[...], cim_ref.shape)

    u = u_ref[...]
    _s5_input_proj(u, wbre_ref, wbim_ref, bre_ref, bim_ref)

    for c in range(ns // lane_chunk):
        sl = slice(c * lane_chunk, (c + 1) * lane_chunk)
        tab = [tabs_ref[i, :, sl] for i in range(8)]

        def body(r, carry, sl=sl, tab=tab):
            cr, ci = carry
            rows = pl.ds(pl.multiple_of(r * V7X_SUBLANES, V7X_SUBLANES), V7X_SUBLANES)
            xr = bre_ref[rows, sl]
            xi = bim_ref[rows, sl]
            for i, k in enumerate((1, 2, 4)):
                lr, li = tab[2 * i], tab[2 * i + 1]
                sr = pltpu.roll(xr, k, 0)
                si = pltpu.roll(xi, k, 0)
                xr, xi = xr + (lr * sr - li * si), xi + (lr * si + li * sr)
            pr, pi = tab[6], tab[7]
            hr = xr + (pr * cr - pi * ci)
            hi = xi + (pr * ci + pi * cr)
            bre_ref[rows, sl] = hr
            bim_ref[rows, sl] = hi
            last = V7X_SUBLANES - 1
            return (jnp.broadcast_to(hr[last:last + 1], hr.shape), jnp.broadcast_to(hi[last:last + 1], hi.shape))

        cr, ci = lax.fori_loop(0, tm // V7X_SUBLANES, body, (cre_ref[:, sl], cim_ref[:, sl]))
        cre_ref[:, sl] = cr
        cim_ref[:, sl] = ci

    _s5_output(u, bre_ref, bim_ref, wcre_ref, wcim_ref, d_ref, wglu_ref, bglu_ref, y_ref)

    @pl.when(s == pl.num_programs(1) - 1)
    def _():
        hre_ref[...] = cre_ref[0:1, :]
        him_ref[...] = cim_ref[0:1, :]


def _const_spec(a):
    nd = a.ndim
    return pl.BlockSpec(a.shape, lambda *_: (0,) * nd, pipeline_mode=pl.Buffered(1))


def _s5_seq(proj, h0re, h0im, tb, d_skip, w_glu, b_glu, l, *, bn, seq, tm=256, lane_chunk=256):
    c = w_glu.shape[-1]
    ns = h0re.shape[-1]
    n_s = seq // tm
    consts = [tb["wb_re"], tb["wb_im"], tb["wc_re"], tb["wc_imneg"], tb["tabs"]]
    in_specs = ([pl.BlockSpec((tm, c), lambda b, s: (b * n_s + s, 0)),
                 pl.BlockSpec((None, 1, ns), lambda b, s: (b, 0, 0)),
                 pl.BlockSpec((None, 1, ns), lambda b, s: (b, 0, 0))]
                + [_const_spec(a) for a in consts]
                + [pl.BlockSpec((None, 1, c), lambda b, s: (l, 0, 0)),
                   pl.BlockSpec((None, c, c), lambda b, s: (l, 0, 0), pipeline_mode=pl.Buffered(1)),
                   pl.BlockSpec((None, 1, c), lambda b, s: (l, 0, 0))])
    y, hre, him = pl.pallas_call(
        functools.partial(_s5_seq_kernel, tm=tm, lane_chunk=lane_chunk),
        grid=(bn, n_s),
        in_specs=in_specs,
        out_specs=[pl.BlockSpec((tm, c), lambda b, s: (b * n_s + s, 0)),
                   pl.BlockSpec((None, 1, ns), lambda b, s: (b, 0, 0)),
                   pl.BlockSpec((None, 1, ns), lambda b, s: (b, 0, 0))],
        out_shape=[jax.ShapeDtypeStruct((bn * seq, c), F32),
                   jax.ShapeDtypeStruct((bn, 1, ns), F32),
                   jax.ShapeDtypeStruct((bn, 1, ns), F32)],
        scratch_shapes=[pltpu.VMEM((tm, ns), F32), pltpu.VMEM((tm, ns), F32),
                        pltpu.VMEM((V7X_SUBLANES, ns), F32), pltpu.VMEM((V7X_SUBLANES, ns), F32)],
        compiler_params=_params("arbitrary", "arbitrary"),
        name="s5_seq",
    )(proj, h0re.reshape(bn, 1, ns), h0im.reshape(bn, 1, ns), *consts,
      d_skip.reshape(d_skip.shape[0], 1, c), w_glu, b_glu.reshape(b_glu.shape[0], 1, c))
    return y, hre.reshape(bn, ns), him.reshape(bn, ns)


def _s5_step_kernel(u_ref, h0re_ref, h0im_ref, wbre_ref, wbim_ref, wcre_ref, wcim_ref, lbre_ref, lbim_ref, d_ref,
                    wglu_ref, bglu_ref, y_ref, hre_ref, him_ref, bre_ref, bim_ref, *, seq, bn, lane_chunk):
    u = u_ref[...]
    _s5_input_proj(u, wbre_ref, wbim_ref, bre_ref, bim_ref)
    ns = bre_ref.shape[1]
    for c in range(ns // lane_chunk):
        sl = slice(c * lane_chunk, (c + 1) * lane_chunk)
        lr = lbre_ref[:, sl]
        li = lbim_ref[:, sl]
        hr = h0re_ref[:, sl]
        hi = h0im_ref[:, sl]
        for t in range(seq):
            rows = slice(t * bn, (t + 1) * bn)
            hr, hi = (lr * hr - li * hi) + bre_ref[rows, sl], (lr * hi + li * hr) + bim_ref[rows, sl]
            bre_ref[rows, sl] = hr
            bim_ref[rows, sl] = hi
        hre_ref[:, sl] = hr
        him_ref[:, sl] = hi
    _s5_output(u, bre_ref, bim_ref, wcre_ref, wcim_ref, d_ref, wglu_ref, bglu_ref, y_ref)


def _s5_step(proj, h0re, h0im, tb, d_skip, w_glu, b_glu, l, *, bn, seq, lane_chunk=128):
    c = w_glu.shape[-1]
    ns = h0re.shape[-1]
    t = bn * seq
    consts = [tb["wb_re"], tb["wb_im"], tb["wc_re"], tb["wc_imneg"], tb["lb_re"], tb["lb_im"]]
    in_specs = ([pl.BlockSpec((t, c), lambda i: (0, 0)),
                 pl.BlockSpec((bn, ns), lambda i: (0, 0)),
                 pl.BlockSpec((bn, ns), lambda i: (0, 0))]
                + [_const_spec(a) for a in consts]
                + [pl.BlockSpec((None, 1, c), lambda i: (l, 0, 0)),
                   pl.BlockSpec((None, c, c), lambda i: (l, 0, 0), pipeline_mode=pl.Buffered(1)),
                   pl.BlockSpec((None, 1, c), lambda i: (l, 0, 0))])
    return pl.pallas_call(
        functools.partial(_s5_step_kernel, seq=seq, bn=bn, lane_chunk=lane_chunk),
        grid=(1,),
        in_specs=in_specs,
        out_specs=[pl.BlockSpec((t, c), lambda i: (0, 0)),
                   pl.BlockSpec((bn, ns), lambda i: (0, 0)),
                   pl.BlockSpec((bn, ns), lambda i: (0, 0))],
        out_shape=[jax.ShapeDtypeStruct((t, c), F32),
                   jax.ShapeDtypeStruct((bn, ns), F32),
                   jax.ShapeDtypeStruct((bn, ns), F32)],
        scratch_shapes=[pltpu.VMEM((t, ns), F32), pltpu.VMEM((t, ns), F32)],
        compiler_params=_params("arbitrary"),
        name="s5_step",
    )(proj, h0re, h0im, *consts,
      d_skip.reshape(d_skip.shape[0], 1, c), w_glu, b_glu.reshape(b_glu.shape[0], 1, c))


def _rg_tables(w_a, w_i):
    nh, hd, _ = w_a.shape
    hpc = V7X_MXU_DIM // hd
    eye = jnp.eye(hpc, dtype=F32)

    def expand(w):
        w = w.reshape(nh // hpc, hpc, hd, hd)
        return jnp.einsum("khij,hg->khigj", w, eye).reshape(nh // hpc, hpc * hd, hpc * hd)

    return expand(w_a), expand(w_i)


def _rg_gates(xc, wa_ref, wi_ref, ba_ref, bi_ref, lam_ref, a_ref, b_ref, rows):
    nck, kc, _ = wa_ref.shape
    lam = lam_ref[...]
    neg = -lam
    softplus = jnp.maximum(neg, 0.0) + jnp.log(1.0 + jnp.exp(-jnp.abs(neg)))
    for k in range(nck):
        cs = slice(k * kc, (k + 1) * kc)
        xk = xc[:, cs]
        r = _sigmoid(_dot(xk, wa_ref[k]) + ba_ref[:, cs])
        i = _sigmoid(_dot(xk, wi_ref[k]) + bi_ref[:, cs])
        log_a = (-RG_C * r) * softplus[:, cs]
        a_ref[rows, cs] = jnp.exp(log_a)
        th = jnp.tanh(log_a)
        b_ref[rows, cs] = jnp.sqrt(-2.0 * th / (1.0 - th)) * (i * xk)


def _rg_seq_kernel(x_ref, gy_ref, conv0_ref, h0_ref, cw_ref, cb_ref, wa_ref, wi_ref, ba_ref, bi_ref, lam_ref,
                   y_ref, hout_ref, xe_ref, a_ref, b_ref, c_ref, *, tm):
    s = pl.program_id(1)
    pad = V7X_SUBLANES
    nconv = conv0_ref.shape[0]

    @pl.when(s == 0)
    def _():
        xe_ref[pad - nconv:pad, :] = conv0_ref[...]
        c_ref[...] = jnp.broadcast_to(h0_ref[...], c_ref.shape)

    @pl.when(s > 0)
    def _():
        xe_ref[0:pad, :] = xe_ref[tm:tm + pad, :]

    xe_ref[pad:pad + tm, :] = x_ref[...]
    acc = cw_ref[0:1, :] * xe_ref[pad - nconv:pad - nconv + tm, :]
    for k in range(1, nconv + 1):
        acc = acc + cw_ref[k:k + 1, :] * xe_ref[pad - nconv + k:pad - nconv + k + tm, :]
    xc = cb_ref[...] + acc
    _rg_gates(xc, wa_ref, wi_ref, ba_ref, bi_ref, lam_ref, a_ref, b_ref, slice(None))

    row = lax.broadcasted_iota(jnp.int32, (V7X_SUBLANES, a_ref.shape[1]), 0)

    def body(r, carry):
        rows = pl.ds(pl.multiple_of(r * V7X_SUBLANES, V7X_SUBLANES), V7X_SUBLANES)
        a = a_ref[rows, :]
        b = b_ref[rows, :]
        for k in (1, 2, 4):
            keep = row >= k
            b = jnp.where(keep, b + a * pltpu.roll(b, k, 0), b)
            a = jnp.where(keep, a * pltpu.roll(a, k, 0), a)
        h = b + a * carry
        b_ref[rows, :] = h
        last = V7X_SUBLANES - 1
        return jnp.broadcast_to(h[last:last + 1], h.shape)

    carry = lax.fori_loop(0, tm // V7X_SUBLANES, body, c_ref[...])
    c_ref[...] = carry
    y_ref[...] = b_ref[...] * gy_ref[...]

    @pl.when(s == pl.num_programs(1) - 1)
    def _():
        hout_ref[...] = c_ref[0:1, :]


def _rg_seq(proj, conv0, h0, conv_w, conv_b, wa, wi, b_a, b_i, lam, l, *, bn, seq, tm=256):
    c = h0.shape[-1]
    n_s = seq // tm
    depth = conv_w.shape[0]
    vec = lambda a: a.reshape(depth, 1, c)
    y, hout = pl.pallas_call(
        functools.partial(_rg_seq_kernel, tm=tm),
        grid=(bn, n_s),
        in_specs=[pl.BlockSpec((tm, c), lambda b, s: (b * n_s + s, 1)),
                  pl.BlockSpec((tm, c), lambda b, s: (b * n_s + s, 2)),
                  pl.BlockSpec((None, CONV_W - 1, c), lambda b, s: (b, 0, 0)),
                  pl.BlockSpec((None, 1, c), lambda b, s: (b, 0, 0)),
                  pl.BlockSpec((None, CONV_W, c), lambda b, s: (l, 0, 0)),
                  pl.BlockSpec((None, 1, c), lambda b, s: (l, 0, 0)),
                  _const_spec(wa), _const_spec(wi),
                  pl.BlockSpec((None, 1, c), lambda b, s: (l, 0, 0)),
                  pl.BlockSpec((None, 1, c), lambda b, s: (l, 0, 0)),
                  pl.BlockSpec((None, 1, c), lambda b, s: (l, 0, 0))],
        out_specs=[pl.BlockSpec((tm, c), lambda b, s: (b * n_s + s, 0)),
                   pl.BlockSpec((None, 1, c), lambda b, s: (b, 0, 0))],
        out_shape=[jax.ShapeDtypeStruct((bn * seq, c), F32), jax.ShapeDtypeStruct((bn, 1, c), F32)],
        scratch_shapes=[pltpu.VMEM((tm + 2 * V7X_SUBLANES, c), F32), pltpu.VMEM((tm, c), F32),
                        pltpu.VMEM((tm, c), F32), pltpu.VMEM((V7X_SUBLANES, c), F32)],
        compiler_params=_params("arbitrary", "arbitrary"),
        name="rg_seq",
    )(proj, proj, conv0, h0.reshape(bn, 1, c), conv_w, vec(conv_b), wa, wi, vec(b_a), vec(b_i), vec(lam))
    return y, hout.reshape(bn, c)


def _rg_step_kernel(x_ref, gy_ref, conv0_ref, h0_ref, cw_ref, cb_ref, wa_ref, wi_ref, ba_ref, bi_ref, lam_ref,
                    y_ref, hout_ref, a_ref, b_ref, *, seq, bn):
    nconv = conv0_ref.shape[0]

    def xpad(i):
        if i < nconv:
            return conv0_ref[i]
        return x_ref[(i - nconv) * bn:(i - nconv + 1) * bn, :]

    for t in range(seq):
        acc = cw_ref[0:1, :] * xpad(t)
        for k in range(1, nconv + 1):
            acc = acc + cw_ref[k:k + 1, :] * xpad(t + k)
        xc = cb_ref[...] + acc
        _rg_gates(xc, wa_ref, wi_ref, ba_ref, bi_ref, lam_ref, a_ref, b_ref, slice(t * bn, (t + 1) * bn))
    h = h0_ref[...]
    for t in range(seq):
        rows = slice(t * bn, (t + 1) * bn)
        h = a_ref[rows, :] * h + b_ref[rows, :]
        y_ref[rows, :] = h * gy_ref[rows, :]
    hout_ref[...] = h


def _rg_step(proj, conv0_tm, h0, conv_w, conv_b, wa, wi, b_a, b_i, lam, l, *, bn, seq):
    c = h0.shape[-1]
    t = bn * seq
    depth = conv_w.shape[0]
    vec = lambda a: a.reshape(depth, 1, c)
    lspec = pl.BlockSpec((None, 1, c), lambda i: (l, 0, 0))
    return pl.pallas_call(
        functools.partial(_rg_step_kernel, seq=seq, bn=bn),
        grid=(1,),
        in_specs=[pl.BlockSpec((t, c), lambda i: (0, 1)),
                  pl.BlockSpec((t, c), lambda i: (0, 2)),
                  pl.BlockSpec(conv0_tm.shape, lambda i: (0, 0, 0)),
                  pl.BlockSpec((bn, c), lambda i: (0, 0)),
                  pl.BlockSpec((None, CONV_W, c), lambda i: (l, 0, 0)),
                  lspec, _const_spec(wa), _const_spec(wi), lspec, lspec, lspec],
        out_specs=[pl.BlockSpec((t, c), lambda i: (0, 0)), pl.BlockSpec((bn, c), lambda i: (0, 0))],
        out_shape=[jax.ShapeDtypeStruct((t, c), F32), jax.ShapeDtypeStruct((bn, c), F32)],
        scratch_shapes=[pltpu.VMEM((t, c), F32), pltpu.VMEM((t, c), F32)],
        compiler_params=_params("arbitrary"),
        name="rg_step",
    )(proj, proj, conv0_tm, h0, conv_w, vec(conv_b), wa, wi, vec(b_a), vec(b_i), vec(lam))


def _merge_kernel(ys_ref, yr_ref, ws_ref, wr_ref, gs_ref, gr_ref, o_ref):
    o_ref[...] = gs_ref[...] * _dot(ys_ref[...], ws_ref[...]) + gr_ref[...] * _dot(yr_ref[...], wr_ref[...])


def _branch_merge(ys, yr, w_s, w_r, gates, l, *, tm, tn):
    t, c = ys.shape
    d = w_s.shape[-1]
    off = d // tn
    return pl.pallas_call(
        _merge_kernel,
        grid=(t // tm, d // tn),
        in_specs=[pl.BlockSpec((tm, c), lambda m, n: (m, 0)),
                  pl.BlockSpec((tm, c), lambda m, n: (m, 0)),
                  pl.BlockSpec((None, c, tn), lambda m, n: (l, 0, n)),
                  pl.BlockSpec((None, c, tn), lambda m, n: (l, 0, n)),
                  pl.BlockSpec((tm, tn), lambda m, n: (m, n)),
                  pl.BlockSpec((tm, tn), lambda m, n: (m, n + off))],
        out_specs=pl.BlockSpec((tm, tn), lambda m, n: (m, n)),
        out_shape=jax.ShapeDtypeStruct((t, d), F32),
        compiler_params=_params("arbitrary", "arbitrary"),
        name="branch_merge",
    )(ys, yr, w_s, w_r, gates, gates)


def _linear_residual_kernel(a_ref, w_ref, x_ref, g_ref, o_ref, acc_ref):
    k = pl.program_id(2)

    @pl.when(k == 0)
    def _():
        acc_ref[...] = jnp.zeros_like(acc_ref)

    acc_ref[...] += _dot(a_ref[...], w_ref[...])

    @pl.when(k == pl.num_programs(2) - 1)
    def _():
        o_ref[...] = x_ref[...] + g_ref[...] * acc_ref[...]


def _linear_residual(a, w, l, x, mod, j_g, *, tm, tn, tk, tiles_per_block):
    t, kdim = a.shape
    d = w.shape[-1]
    rows = mod.shape[2]
    return pl.pallas_call(
        _linear_residual_kernel,
        grid=(t // tm, d // tn, kdim // tk),
        in_specs=[pl.BlockSpec((tm, tk), lambda m, n, k: (m, k)),
                  pl.BlockSpec((None, tk, tn), lambda m, n, k: (l, k, n)),
                  pl.BlockSpec((tm, tn), lambda m, n, k: (m, n)),
                  _mod_spec(j_g, rows, tn, tiles_per_block, tm, ncol_axis=0)],
        out_specs=pl.BlockSpec((tm, tn), lambda m, n, k: (m, n)),
        out_shape=jax.ShapeDtypeStruct((t, d), F32),
        scratch_shapes=[pltpu.VMEM((tm, tn), F32)],
        compiler_params=_params("arbitrary", "arbitrary", "arbitrary"),
        name="linear_residual",
    )(a, w, x, mod)


ROUTE_I1, ROUTE_I2, ROUTE_W1, ROUTE_W2, ROUTE_RANK1, ROUTE_RANK2 = range(6)


def _router_kernel(x_ref, sh_ref, sc_ref, g_ref, wr_ref, hn_ref, route_ref, cnt_ref, *, n_experts):
    m = pl.program_id(0)

    @pl.when(m == 0)
    def _():
        cnt_ref[...] = jnp.zeros_like(cnt_ref)

    _norm_mod_rows(x_ref, g_ref, sc_ref, sh_ref, hn_ref)
    logits = jnp.dot(hn_ref[...], wr_ref[...], preferred_element_type=F32, precision=lax.Precision.HIGHEST)
    tm, width = logits.shape
    lane = lax.broadcasted_iota(jnp.int32, logits.shape, 1).astype(F32)
    lg = jnp.where(lane < n_experts, logits, -jnp.inf)
    v1 = jnp.max(lg, axis=-1, keepdims=True)
    i1 = jnp.min(jnp.where(lg == v1, lane, float(width)), axis=-1, keepdims=True)
    lg2 = jnp.where(lane == i1, -jnp.inf, lg)
    v2 = jnp.max(lg2, axis=-1, keepdims=True)
    i2 = jnp.min(jnp.where(lg2 == v2, lane, float(width)), axis=-1, keepdims=True)
    e2 = jnp.exp(v2 - v1)
    den = 1.0 + e2
    sel1 = lane == i1
    sel2 = lane == i2
    picked = jnp.where(sel1 | sel2, 1.0, 0.0)
    r = lax.broadcasted_iota(jnp.int32, (tm, tm), 0)
    c = lax.broadcasted_iota(jnp.int32, (tm, tm), 1)
    before = jnp.where(c < r, 1.0, 0.0)
    seen = _dot(before, picked) + cnt_ref[...]
    rank1 = jnp.sum(jnp.where(sel1, seen, 0.0), axis=-1, keepdims=True)
    rank2 = jnp.sum(jnp.where(sel2, seen, 0.0), axis=-1, keepdims=True)
    cnt_ref[...] += jnp.sum(picked, axis=0, keepdims=True)
    cols = ((ROUTE_I1, i1), (ROUTE_I2, i2), (ROUTE_W1, 1.0 / den), (ROUTE_W2, e2 / den),
            (ROUTE_RANK1, rank1), (ROUTE_RANK2, rank2))
    route = jnp.zeros_like(logits)
    for j, v in cols:
        route = jnp.where(lane == j, v, route)
    route_ref[...] = route


def _router(x, mod, j_sh, g, w_router_pad, l, *, tm, tiles_per_block, n_experts):
    t, d = x.shape
    rows = mod.shape[2]
    width = w_router_pad.shape[-1]
    return pl.pallas_call(
        functools.partial(_router_kernel, n_experts=n_experts),
        grid=(t // tm,),
        in_specs=[pl.BlockSpec((tm, d), lambda m: (m, 0)),
                  _mod_spec(j_sh, rows, d, tiles_per_block, tm),
                  _mod_spec(j_sh + 1, rows, d, tiles_per_block, tm),
                  pl.BlockSpec((None, 1, d), lambda m: (l, 0, 0)),
                  pl.BlockSpec((d, width), lambda m: (0, 0))],
        out_specs=[pl.BlockSpec((tm, d), lambda m: (m, 0)), pl.BlockSpec((tm, width), lambda m: (m, 0)),
                   pl.BlockSpec((1, width), lambda m: (0, 0))],
        out_shape=[jax.ShapeDtypeStruct((t, d), F32), jax.ShapeDtypeStruct((t, width), F32),
                   jax.ShapeDtypeStruct((1, width), F32)],
        compiler_params=_params("arbitrary"),
        name="moe_router",
    )(x, mod, mod, g.reshape(g.shape[0], 1, d), w_router_pad)


def _dispatch_kernel(p1_ref, p2_ref, hn_ref, xs_ref, zero_ref, sem, *, chunk):
    t = hn_ref.shape[0]
    zrows = zero_ref.shape[0]
    nz = xs_ref.shape[0] // zrows
    zero_ref[...] = jnp.zeros_like(zero_ref)

    def zero_copy(i):
        return pltpu.make_async_copy(zero_ref, xs_ref.at[pl.ds(i * zrows, zrows)], sem.at[2])

    def row_copy(tok, row, slot):
        return pltpu.make_async_copy(hn_ref.at[pl.ds(tok, 1)], xs_ref.at[pl.ds(row, 1)], sem.at[slot])

    lax.fori_loop(0, nz, lambda i, c: (zero_copy(i).start(), c)[1], 0)
    lax.fori_loop(0, nz, lambda i, c: (zero_copy(0).wait(), c)[1], 0)

    def issue(ck, slot):
        def body(i, c):
            tok = ck * chunk + i
            row_copy(tok, p1_ref[tok], slot).start()
            row_copy(tok, p2_ref[tok], slot).start()
            return c
        lax.fori_loop(0, chunk, body, 0)

    def drain(slot):
        lax.fori_loop(0, 2 * chunk, lambda i, c: (row_copy(0, 0, slot).wait(), c)[1], 0)

    npairs = t // chunk // 2
    issue(0, 0)

    def pair(j, c):
        issue(2 * j + 1, 1)
        drain(0)

        @pl.when(j + 1 < npairs)
        def _():
            issue(2 * j + 2, 0)

        drain(1)
        return c

    lax.fori_loop(0, npairs, pair, 0)


def _dispatch(pos1, pos2, hn, *, rows, zrows, chunk=128):
    t, d = hn.shape
    chunk = min(chunk, t // 2)
    assert t % (2 * chunk) == 0 and rows % zrows == 0
    return pl.pallas_call(
        functools.partial(_dispatch_kernel, chunk=chunk),
        grid_spec=pltpu.PrefetchScalarGridSpec(
            num_scalar_prefetch=2, grid=(1,),
            in_specs=[pl.BlockSpec(memory_space=pl.ANY)],
            out_specs=pl.BlockSpec(memory_space=pl.ANY),
            scratch_shapes=[pltpu.VMEM((zrows, d), F32), pltpu.SemaphoreType.DMA((3,))]),
        out_shape=jax.ShapeDtypeStruct((rows, d), F32),
        compiler_params=_params("arbitrary"),
        name="moe_dispatch",
    )(pos1, pos2, hn)


def _grouped_up_kernel(te_ref, nu_ref, xs_ref, w1_ref, w3_ref, o_ref):
    m = pl.program_id(1)

    @pl.when(m < nu_ref[0])
    def _():
        h = xs_ref[...]
        o_ref[...] = _silu(_dot(h, w1_ref[...])) * _dot(h, w3_ref[...])

    @pl.when(m >= nu_ref[0])
    def _():
        o_ref[...] = jnp.zeros_like(o_ref)


def _grouped_up(tile_expert, n_used, xs, w1, w3, lm, *, tg, tn):
    r, d = xs.shape
    f = w1.shape[-1]
    wspec = pl.BlockSpec((None, None, d, tn), lambda n, m, te, nu: (lm, te[m], 0, n))
    return pl.pallas_call(
        _grouped_up_kernel,
        grid_spec=pltpu.PrefetchScalarGridSpec(
            num_scalar_prefetch=2, grid=(f // tn, r // tg),
            in_specs=[pl.BlockSpec((tg, d), lambda n, m, te, nu: (m, 0)), wspec, wspec],
            out_specs=pl.BlockSpec((tg, tn), lambda n, m, te, nu: (m, n))),
        out_shape=jax.ShapeDtypeStruct((r, f), F32),
        compiler_params=_params("arbitrary", "arbitrary"),
        name="moe_up",
    )(tile_expert, n_used, xs, w1, w3)


def _grouped_down_kernel(te_ref, nu_ref, a_ref, w_ref, o_ref):
    m = pl.program_id(1)

    @pl.when(m < nu_ref[0])
    def _():
        o_ref[...] = _dot(a_ref[...], w_ref[...])

    @pl.when(m >= nu_ref[0])
    def _():
        o_ref[...] = jnp.zeros_like(o_ref)


def _grouped_down(tile_expert, n_used, act, w2, lm, *, tg, tn):
    r, f = act.shape
    d = w2.shape[-1]
    return pl.pallas_call(
        _grouped_down_kernel,
        grid_spec=pltpu.PrefetchScalarGridSpec(
            num_scalar_prefetch=2, grid=(d // tn, r // tg),
            in_specs=[pl.BlockSpec((tg, f), lambda n, m, te, nu: (m, 0)),
                      pl.BlockSpec((None, None, f, tn), lambda n, m, te, nu: (lm, te[m], 0, n))],
            out_specs=pl.BlockSpec((tg, tn), lambda n, m, te, nu: (m, n))),
        out_shape=jax.ShapeDtypeStruct((r, d), F32),
        compiler_params=_params("arbitrary", "arbitrary"),
        name="moe_down",
    )(tile_expert, n_used, act, w2)


def _combine_kernel(p1_ref, p2_ref, y_ref, route_ref, x_ref, g_ref, o_ref, ya_ref, yb_ref, sem):
    tc = ya_ref.shape[0]
    base = pl.program_id(0) * tc

    def row_copy(row, dst_ref, i):
        return pltpu.make_async_copy(y_ref.at[pl.ds(row, 1)], dst_ref.at[pl.ds(i, 1)], sem.at[0])

    def body(i, c):
        row_copy(p1_ref[base + i], ya_ref, i).start()
        row_copy(p2_ref[base + i], yb_ref, i).start()
        return c

    lax.fori_loop(0, tc, body, 0)
    lax.fori_loop(0, 2 * tc, lambda i, c: (row_copy(0, ya_ref, 0).wait(), c)[1], 0)
    w1 = route_ref[:, ROUTE_W1:ROUTE_W1 + 1]
    w2 = route_ref[:, ROUTE_W2:ROUTE_W2 + 1]
    o_ref[...] = x_ref[...] + g_ref[...] * (w1 * ya_ref[...] + w2 * yb_ref[...])


def _combine(pos1, pos2, y, route, x, mod, j_g, *, tc, tiles_per_block):
    t, d = x.shape
    rows = mod.shape[2]
    width = route.shape[-1]
    return pl.pallas_call(
        _combine_kernel,
        grid_spec=pltpu.PrefetchScalarGridSpec(
            num_scalar_prefetch=2, grid=(t // tc,),
            in_specs=[pl.BlockSpec(memory_space=pl.ANY),
                      pl.BlockSpec((tc, width), lambda m, p1, p2: (m, 0)),
                      pl.BlockSpec((tc, d), lambda m, p1, p2: (m, 0)),
                      _mod_spec(j_g, rows, d, tiles_per_block, tc)],
            out_specs=pl.BlockSpec((tc, d), lambda m, p1, p2: (m, 0)),
            scratch_shapes=[pltpu.VMEM((tc, d), F32), pltpu.VMEM((tc, d), F32), pltpu.SemaphoreType.DMA((1,))]),
        out_shape=jax.ShapeDtypeStruct((t, d), F32),
        compiler_params=_params("arbitrary"),
        name="moe_combine",
    )(pos1, pos2, y, route, x, mod)


def _routed_moe(x, mod, g, w_router, w1, w3, w2, l, lm, *, tm, tg, tiles_per_tile):
    t, d = x.shape
    ne = w_router.shape[-1]
    w_r = jnp.pad(w_router[lm], ((0, 0), (0, V7X_LANES - ne)))
    hn, route, cnt = _router(x, mod, 3, g, w_r, l, n_experts=ne, **tiles_per_tile(tm))
    counts = cnt[0, :ne].astype(jnp.int32)
    tiles_e = (counts + (tg - 1)) // tg
    tile_end = jnp.cumsum(tiles_e)
    row_start = (tile_end - tiles_e) * tg
    col = lambda j: route[:, j].astype(jnp.int32)
    pos1 = row_start[col(ROUTE_I1)] + col(ROUTE_RANK1)
    pos2 = row_start[col(ROUTE_I2)] + col(ROUTE_RANK2)
    n_tiles = -(-TOP_K * t // tg) + ne
    tile_expert = jnp.minimum(jnp.sum(jnp.arange(n_tiles)[:, None] >= tile_end[None, :], axis=1), ne - 1)
    tile_expert = tile_expert.astype(jnp.int32)
    n_used = tile_end[-1:].astype(jnp.int32)
    xs = _dispatch(pos1, pos2, hn, rows=n_tiles * tg, zrows=tg)
    act = _grouped_up(tile_expert, n_used, xs, w1, w3, lm, tg=tg, tn=1024)
    y = _grouped_down(tile_expert, n_used, act, w2, lm, tg=tg, tn=1024)
    tc = min(tm, 256)
    return _combine(pos1, pos2, y, route, x, mod, 5, tc=tc, tiles_per_block=tiles_per_tile(tc)["tiles_per_block"])


def _final_norm_kernel(x_ref, g_ref, o_ref):
    o_ref[...] = _rmsnorm(x_ref[...], g_ref[...])


def _final_norm(x, g, *, tm):
    t, d = x.shape
    return pl.pallas_call(
        _final_norm_kernel,
        grid=(t // tm,),
        in_specs=[pl.BlockSpec((tm, d), lambda m: (m, 0)), pl.BlockSpec((1, d), lambda m: (0, 0))],
        out_specs=pl.BlockSpec((tm, d), lambda m: (m, 0)),
        out_shape=jax.ShapeDtypeStruct((t, d), F32),
        compiler_params=_params("arbitrary"),
        name="final_norm",
    )(x, g.reshape(1, d))


def _run(x, mods, state, p, *, bn, seq, time_major, tm):
    t, d = x.shape
    depth = p["w_in"].shape[0]
    c = p["s5_w_glu"].shape[-1]
    def tiled(tile):
        return dict(tm=tile, tiles_per_block=1 if time_major else seq // tile)

    kw = tiled(tm)
    nconv = CONV_W - 1
    assert seq >= nconv
    s5_re, s5_im, rg_h, conv = [], [], [], []
    for l in range(depth):
        mod = mods[l]
        tb = p["s5_tables"][l]
        wa, wi = p["rg_tables"][l]
        proj = _normed_linear(x, mod, 0, p["norm_mix"], l, [p["w_in"]], l, mode="in", tn=512,
                              plain_cols=2 * c, **kw)
        gates = _normed_linear(x, mod, 0, p["norm_mix"], l, [p["w_gate"]], l, mode="gate", tn=512,
                               bias=p["b_gate"], **kw)
        h0re = state["s5_re"][l].reshape(bn, -1)
        h0im = state["s5_im"][l].reshape(bn, -1)
        rg_args = (p["rg_conv_w"], p["rg_conv_b"], wa, wi, p["rg_b_a"].reshape(depth, c), p["rg_b_i"].reshape(depth, c),
                   p["rg_lam"], l)
        if time_major:
            y_s5, hre, him = _s5_step(proj, h0re, h0im, tb, p["s5_d"].reshape(depth, c), p["s5_w_glu"], p["s5_b_glu"], l,
                                      bn=bn, seq=seq)
            y_rg, hrg = _rg_step(proj, state["conv"][l].transpose(1, 0, 2), state["rg"][l], *rg_args, bn=bn, seq=seq)
            conv.append(proj[:, c:2 * c].reshape(seq, bn, c)[seq - nconv:].transpose(1, 0, 2))
        else:
            y_s5, hre, him = _s5_seq(proj, h0re, h0im, tb, p["s5_d"].reshape(depth, c), p["s5_w_glu"], p["s5_b_glu"], l,
                                     bn=bn, seq=seq)
            y_rg, hrg = _rg_seq(proj, state["conv"][l], state["rg"][l], *rg_args, bn=bn, seq=seq)
            conv.append(proj.reshape(bn, seq, 3 * c)[:, seq - nconv:, c:2 * c])
        s5_re.append(hre.reshape(state["s5_re"][l].shape))
        s5_im.append(him.reshape(state["s5_im"][l].shape))
        rg_h.append(hrg)
        merged = _branch_merge(y_s5, y_rg, p["w_br_s5"], p["w_br_rg"], gates, l, tm=tm, tn=512)
        x = _linear_residual(merged, p["w_out"], l, x, mod, 2, tn=1024, tk=1024, **kw)
        if l % 2 == 0:
            act = _normed_linear(x, mod, 3, p["norm_ffn"], l, [p["ffn_w1"], p["ffn_w3"]], l // 2, mode="glu",
                                 tn=512, **kw)
            x = _linear_residual(act, p["ffn_w2"], l // 2, x, mod, 5, tn=1024, tk=1024, **kw)
        else:
            x = _routed_moe(x, mod, p["norm_ffn"], p["moe_router"], p["moe_w1"], p["moe_w3"], p["moe_w2"], l, l // 2,
                            tm=min(tm, 512), tg=512 if t >= 4096 else 128, tiles_per_tile=tiled)
    y = _final_norm(x, p["norm_f"], tm=min(tm, 256))
    return y, jnp.stack(s5_re), jnp.stack(s5_im), jnp.stack(rg_h), jnp.stack(conv)


def kernel(x_prompt, x_sample, c_prompt, c_sample, state_s5_re, state_s5_im, state_rglru, state_conv, norm_mix, norm_ffn, norm_f, w_ada, b_ada, w_in, s5_lam_re, s5_lam_im, s5_log_dt, s5_b_re, s5_b_im, s5_c_re, s5_c_im, s5_d, s5_w_glu, s5_b_glu, rg_conv_w, rg_conv_b, rg_w_a, rg_b_a, rg_w_i, rg_b_i, rg_lam, w_gate, b_gate, w_br_s5, w_br_rg, w_out, ffn_w1, ffn_w3, ffn_w2, moe_router, moe_w1, moe_w3, moe_w2):
    p = dict(norm_mix=norm_mix, norm_ffn=norm_ffn, norm_f=norm_f, w_in=w_in,
             s5_lam_re=s5_lam_re, s5_lam_im=s5_lam_im, s5_log_dt=s5_log_dt, s5_b_re=s5_b_re, s5_b_im=s5_b_im,
             s5_c_re=s5_c_re, s5_c_im=s5_c_im, s5_d=s5_d, s5_w_glu=s5_w_glu, s5_b_glu=s5_b_glu,
             rg_conv_w=rg_conv_w, rg_conv_b=rg_conv_b, rg_w_a=rg_w_a, rg_b_a=rg_b_a, rg_w_i=rg_w_i, rg_b_i=rg_b_i,
             rg_lam=rg_lam, w_gate=w_gate, b_gate=b_gate, w_br_s5=w_br_s5, w_br_rg=w_br_rg, w_out=w_out,
             ffn_w1=ffn_w1, ffn_w3=ffn_w3, ffn_w2=ffn_w2, moe_router=moe_router, moe_w1=moe_w1, moe_w3=moe_w3,
             moe_w2=moe_w2)
    depth = w_in.shape[0]
    p["s5_tables"] = [_s5_tables(s5_lam_re[l], s5_lam_im[l], s5_log_dt[l], s5_b_re[l], s5_b_im[l], s5_c_re[l],
                                 s5_c_im[l]) for l in range(depth)]
    p["rg_tables"] = [_rg_tables(rg_w_a[l], rg_w_i[l]) for l in range(depth)]
    bp, sp, d = x_prompt.shape
    bs, ss, _ = x_sample.shape
    g, pst = state_s5_re.shape[2:]
    c = state_rglru.shape[-1]

    n_c = bp + bs
    rows_c = -(-n_c // V7X_SUBLANES) * V7X_SUBLANES
    c_all = jnp.pad(jnp.concatenate([c_prompt, c_sample], axis=0), ((0, rows_c - n_c), (0, 0)))
    mod_all = _ada_mod(c_all, w_ada, b_ada)
    mods_p, mods_s = [], []
    for l in range(depth):
        mp = mod_all[l, :bp].reshape(bp, 6, 1, d).transpose(1, 0, 2, 3)
        ms = mod_all[l, bp:n_c].reshape(bs, 6, d).transpose(1, 0, 2)
        ms = jnp.tile(ms, (1, ss, 1)).reshape(6, 1, ss * bs, d)
        mods_p.append(mp)
        mods_s.append(ms)

    zero_state = dict(s5_re=jnp.zeros((depth, bp, g, pst), F32), s5_im=jnp.zeros((depth, bp, g, pst), F32),
                      rg=jnp.zeros((depth, bp, c), F32), conv=jnp.zeros((depth, bp, CONV_W - 1, c), x_prompt.dtype))
    y_p, p_re, p_im, p_rg, p_conv = _run(x_prompt.reshape(bp * sp, d), mods_p, zero_state, p,
                                         bn=bp, seq=sp, time_major=False, tm=min(PROMPT_TILE, sp))
    sample_state = dict(s5_re=state_s5_re, s5_im=state_s5_im, rg=state_rglru, conv=state_conv)
    x_s = x_sample.transpose(1, 0, 2).reshape(ss * bs, d)
    y_s, s_re, s_im, s_rg, s_conv = _run(x_s, mods_s, sample_state, p, bn=bs, seq=ss, time_major=True, tm=ss * bs)
    y_s = y_s.reshape(ss, bs, d).transpose(1, 0, 2)
    return (y_p.reshape(bp, sp, d), y_s, p_re, p_im, p_rg, p_conv, s_re, s_im, s_rg, s_conv)
```

```python
import functools
import math

import jax
import jax.numpy as jnp
from jax import lax
from jax.experimental import pallas as pl
from jax.experimental.pallas import tpu as pltpu

F32 = jnp.float32
EPS = 1e-6
RG_C = 8.0
CONV_W = 4
S5_GROUP = 16
S5_STATE = 64
RG_HEAD_DIM = 64
TOP_K = 2

V7X_LANES = 128
V7X_SUBLANES = 8
V7X_MXU_DIM = 256
VMEM_LIMIT_BYTES = 56 * 1024 * 1024

GELU_C = math.sqrt(2.0 / math.pi)


def _gelu(x):
    return 0.5 * x * (1.0 + jnp.tanh(GELU_C * (x + 0.044715 * (x * x * x))))


def _sigmoid(x):
    return 1.0 / (1.0 + jnp.exp(-x))


def _silu(x):
    return x * _sigmoid(x)


def _dot(a, b):
    return jnp.dot(a, b, preferred_element_type=F32)


def _params(*sem):
    return pltpu.CompilerParams(dimension_semantics=sem, vmem_limit_bytes=VMEM_LIMIT_BYTES)


def _mod_spec(j, rows, width, tiles_per_block, tile, ncol_axis=None):
    col = (lambda idx: 0) if ncol_axis is None else (lambda idx: idx[ncol_axis])
    if rows == 1:
        return pl.BlockSpec((None, None, 1, width), lambda m, *idx: (j, m // tiles_per_block, 0, col(idx)))
    return pl.BlockSpec((None, None, tile, width), lambda m, *idx: (j, 0, m, col(idx)))


ROW_CHUNK = 128
PROMPT_TILE = 1024


def _rmsnorm(x, g):
    ms = jnp.mean(x * x, axis=-1, keepdims=True)
    return (x * lax.rsqrt(ms + EPS)) * g


def _norm_mod_rows(x_ref, g_ref, sc_ref, sh_ref, o_ref):
    tm = x_ref.shape[0]
    chunk = min(ROW_CHUNK, tm)
    per_row = sc_ref.shape[0] != 1

    def body(i, carry):
        rows = pl.ds(pl.multiple_of(i * chunk, chunk), chunk)
        sc = sc_ref[rows, :] if per_row else sc_ref[...]
        sh = sh_ref[rows, :] if per_row else sh_ref[...]
        o_ref[rows, :] = _rmsnorm(x_ref[rows, :], g_ref[...]) * (1.0 + sc) + sh
        return carry

    lax.fori_loop(0, tm // chunk, body, 0)


def _ada_kernel(c_ref, w_ref, b_ref, o_ref):
    o_ref[...] = _dot(_silu(c_ref[...]), w_ref[...]) + b_ref[...]


def _ada_mod(c_all, w_ada, b_ada, *, tn=1024):
    depth, d, n6 = w_ada.shape
    rows = c_all.shape[0]
    return pl.pallas_call(
        _ada_kernel,
        grid=(depth, n6 // tn),
        in_specs=[pl.BlockSpec((rows, d), lambda l, n: (0, 0)),
                  pl.BlockSpec((None, d, tn), lambda l, n: (l, 0, n)),
                  pl.BlockSpec((None, 1, tn), lambda l, n: (l, 0, n))],
        out_specs=pl.BlockSpec((None, rows, tn), lambda l, n: (l, 0, n)),
        out_shape=jax.ShapeDtypeStruct((depth, rows, n6), F32),
        compiler_params=_params("arbitrary", "arbitrary"),
        name="ada_mod",
    )(c_all, w_ada, b_ada.reshape(depth, 1, n6))


def _normed_linear_kernel(*refs, mode, n_plain):
    if mode == "glu":
        x_ref, sh_ref, sc_ref, g_ref, w1_ref, w3_ref, o_ref, hn_ref = refs
    elif mode == "gate":
        x_ref, sh_ref, sc_ref, g_ref, w_ref, b_ref, o_ref, hn_ref = refs
    else:
        x_ref, sh_ref, sc_ref, g_ref, w_ref, o_ref, hn_ref = refs
    n = pl.program_id(1)

    @pl.when(n == 0)
    def _():
        _norm_mod_rows(x_ref, g_ref, sc_ref, sh_ref, hn_ref)

    h = hn_ref[...]
    if mode == "glu":
        o_ref[...] = _silu(_dot(h, w1_ref[...])) * _dot(h, w3_ref[...])
    elif mode == "gate":
        o_ref[...] = _sigmoid(_dot(h, w_ref[...]) + b_ref[...])
    else:
        z = _dot(h, w_ref[...])

        @pl.when(n < n_plain)
        def _():
            o_ref[...] = z

        @pl.when(n >= n_plain)
        def _():
            o_ref[...] = _gelu(z)


def _normed_linear(x, mod, j_sh, g, l_g, ws, l, *, mode, tm, tn, tiles_per_block, bias=None, plain_cols=0):
    t, d = x.shape
    n_out = ws[0].shape[-1]
    rows = mod.shape[2]
    w_spec = pl.BlockSpec((None, d, tn), lambda m, n: (l, 0, n))
    in_specs = [pl.BlockSpec((tm, d), lambda m, n: (m, 0)),
                _mod_spec(j_sh, rows, d, tiles_per_block, tm),
                _mod_spec(j_sh + 1, rows, d, tiles_per_block, tm),
                pl.BlockSpec((None, 1, d), lambda m, n: (l_g, 0, 0))]
    args = [x, mod, mod, g.reshape(g.shape[0], 1, d)]
    for w in ws:
        in_specs.append(w_spec)
        args.append(w)
    if bias is not None:
        in_specs.append(pl.BlockSpec((None, 1, tn), lambda m, n: (l, 0, n)))
        args.append(bias.reshape(bias.shape[0], 1, n_out))
    return pl.pallas_call(
        functools.partial(_normed_linear_kernel, mode=mode, n_plain=plain_cols // tn),
        grid=(t // tm, n_out // tn),
        in_specs=in_specs,
        out_specs=pl.BlockSpec((tm, tn), lambda m, n: (m, n)),
        out_shape=jax.ShapeDtypeStruct((t, n_out), F32),
        scratch_shapes=[pltpu.VMEM((tm, d), F32)],
        compiler_params=_params("arbitrary", "arbitrary"),
        name="normed_linear_" + mode,
    )(*args)


def _s5_tables(lam_re, lam_im, log_dt, b_re, b_im, c_re, c_im):
    g, p = lam_re.shape
    h = b_re.shape[-1]
    gpc = V7X_MXU_DIM // h
    nck = g // gpc
    dt = jnp.exp(log_dt)[:, None]
    mag = jnp.exp(lam_re * dt)
    lb_re = mag * jnp.cos(lam_im * dt)
    lb_im = mag * jnp.sin(lam_im * dt)
    den = lam_re * lam_re + lam_im * lam_im
    cf_re = ((lb_re - 1.0) * lam_re + lb_im * lam_im) / den
    cf_im = (lb_im * lam_re - (lb_re - 1.0) * lam_im) / den
    bb_re = cf_re[:, :, None] * b_re - cf_im[:, :, None] * b_im
    bb_im = cf_re[:, :, None] * b_im + cf_im[:, :, None] * b_re
    eye = jnp.eye(gpc, dtype=F32)

    def expand_b(m):
        m = m.reshape(nck, gpc, p, h).transpose(0, 1, 3, 2)
        return jnp.einsum("kghp,gj->kghjp", m, eye).reshape(nck, gpc * h, gpc * p)

    def expand_c(m):
        m = m.reshape(nck, gpc, h, p).transpose(0, 1, 3, 2)
        return jnp.einsum("kgph,gj->kgpjh", m, eye).reshape(nck, gpc * p, gpc * h)

    lb_re = lb_re.reshape(1, g * p)
    lb_im = lb_im.reshape(1, g * p)
    pw_re, pw_im = [lb_re], [lb_im]
    for _ in range(V7X_SUBLANES - 1):
        pw_re, pw_im = (pw_re + [pw_re[-1] * lb_re - pw_im[-1] * lb_im],
                        pw_im + [pw_re[-1] * lb_im + pw_im[-1] * lb_re])
    row = jnp.arange(V7X_SUBLANES)[:, None]
    tabs = []
    for k in (1, 2, 4):
        tabs += [jnp.where(row >= k, pw_re[k - 1], 0.0), jnp.where(row >= k, pw_im[k - 1], 0.0)]
    tabs += [jnp.concatenate(pw_re, axis=0), jnp.concatenate(pw_im, axis=0)]
    return dict(wb_re=expand_b(bb_re), wb_im=expand_b(bb_im),
                wc_re=expand_c(c_re), wc_imneg=expand_c(-c_im),
                tabs=jnp.stack(tabs).astype(F32),
                lb_re=lb_re, lb_im=lb_im)


def _s5_input_proj(u, wbre_ref, wbim_ref, bre_ref, bim_ref):
    nck, kc, nc = wbre_ref.shape
    for k in range(nck):
        uk = u[:, k * kc:(k + 1) * kc]
        bre_ref[:, k * nc:(k + 1) * nc] = _dot(uk, wbre_ref[k])
        bim_ref[:, k * nc:(k + 1) * nc] = _dot(uk, wbim_ref[k])


def _s5_output(u, bre_ref, bim_ref, wcre_ref, wcim_ref, d_ref, wglu_ref, bglu_ref, y_ref):
    nck, nc, kc = wcre_ref.shape
    for k in range(nck):
        yk = _dot(bre_ref[:, k * nc:(k + 1) * nc], wcre_ref[k]) + _dot(bim_ref[:, k * nc:(k + 1) * nc], wcim_ref[k])
        yk = yk + d_ref[:, k * kc:(k + 1) * kc] * u[:, k * kc:(k + 1) * kc]
        y_ref[:, k * kc:(k + 1) * kc] = _gelu(yk)
    yg = y_ref[...]
    y_ref[...] = yg * _sigmoid(_dot(yg, wglu_ref[...]) + bglu_ref[...])


def _s5_seq_kernel(u_ref, h0re_ref, h0im_ref, wbre_ref, wbim_ref, wcre_ref, wcim_ref, tabs_ref, d_ref,
                   wglu_ref, bglu_ref, y_ref, hre_ref, him_ref, bre_ref, bim_ref, cre_ref, cim_ref,
                   *, tm, lane_chunk):
    s = pl.program_id(1)
    ns = cre_ref.shape[1]

    @pl.when(s == 0)
    def _():
        cre_ref[...] = jnp.broadcast_to(h0re_ref[...], cre_ref.shape)
        cim_ref[...] = jnp.broadcast_to(h0im_ref[...], cim_ref.shape)

    u = u_ref[...]
    _s5_input_proj(u, wbre_ref, wbim_ref, bre_ref, bim_ref)

    for c in range(ns // lane_chunk):
        sl = slice(c * lane_chunk, (c + 1) * lane_chunk)
        tab = [tabs_ref[i, :, sl] for i in range(8)]

        def body(r, carry, sl=sl, tab=tab):
            cr, ci = carry
            rows = pl.ds(pl.multiple_of(r * V7X_SUBLANES, V7X_SUBLANES), V7X_SUBLANES)
            xr = bre_ref[rows, sl]
            xi = bim_ref[rows, sl]
            for i, k in enumerate((1, 2, 4)):
                lr, li = tab[2 * i], tab[2 * i + 1]
                sr = pltpu.roll(xr, k, 0)
                si = pltpu.roll(xi, k, 0)
                xr, xi = xr + (lr * sr - li * si), xi + (lr * si + li * sr)
            pr, pi = tab[6], tab[7]
            hr = xr + (pr * cr - pi * ci)
            hi = xi + (pr * ci + pi * cr)
            bre_ref[rows, sl] = hr
            bim_ref[rows, sl] = hi
            last = V7X_SUBLANES - 1
            return (jnp.broadcast_to(hr[last:last + 1], hr.shape), jnp.broadcast_to(hi[last:last + 1], hi.shape))

        cr, ci = lax.fori_loop(0, tm // V7X_SUBLANES, body, (cre_ref[:, sl], cim_ref[:, sl]))
        cre_ref[:, sl] = cr
        cim_ref[:, sl] = ci

    _s5_output(u, bre_ref, bim_ref, wcre_ref, wcim_ref, d_ref, wglu_ref, bglu_ref, y_ref)

    @pl.when(s == pl.num_programs(1) - 1)
    def _():
        hre_ref[...] = cre_ref[0:1, :]
        him_ref[...] = cim_ref[0:1, :]


def _const_spec(a):
    nd = a.ndim
    return pl.BlockSpec(a.shape, lambda *_: (0,) * nd, pipeline_mode=pl.Buffered(1))


def _s5_seq(proj, h0re, h0im, tb, d_skip, w_glu, b_glu, l, *, bn, seq, tm=256, lane_chunk=256):
    c = w_glu.shape[-1]
    ns = h0re.shape[-1]
    n_s = seq // tm
    consts = [tb["wb_re"], tb["wb_im"], tb["wc_re"], tb["wc_imneg"], tb["tabs"]]
    in_specs = ([pl.BlockSpec((tm, c), lambda b, s: (b * n_s + s, 0)),
                 pl.BlockSpec((None, 1, ns), lambda b, s: (b, 0, 0)),
                 pl.BlockSpec((None, 1, ns), lambda b, s: (b, 0, 0))]
                + [_const_spec(a) for a in consts]
                + [pl.BlockSpec((None, 1, c), lambda b, s: (l, 0, 0)),
                   pl.BlockSpec((None, c, c), lambda b, s: (l, 0, 0), pipeline_mode=pl.Buffered(1)),
                   pl.BlockSpec((None, 1, c), lambda b, s: (l, 0, 0))])
    y, hre, him = pl.pallas_call(
        functools.partial(_s5_seq_kernel, tm=tm, lane_chunk=lane_chunk),
        grid=(bn, n_s),
        in_specs=in_specs,
        out_specs=[pl.BlockSpec((tm, c), lambda b, s: (b * n_s + s, 0)),
                   pl.BlockSpec((None, 1, ns), lambda b, s: (b, 0, 0)),
                   pl.BlockSpec((None, 1, ns), lambda b, s: (b, 0, 0))],
        out_shape=[jax.ShapeDtypeStruct((bn * seq, c), F32),
                   jax.ShapeDtypeStruct((bn, 1, ns), F32),
                   jax.ShapeDtypeStruct((bn, 1, ns), F32)],
        scratch_shapes=[pltpu.VMEM((tm, ns), F32), pltpu.VMEM((tm, ns), F32),
                        pltpu.VMEM((V7X_SUBLANES, ns), F32), pltpu.VMEM((V7X_SUBLANES, ns), F32)],
        compiler_params=_params("arbitrary", "arbitrary"),
        name="s5_seq",
    )(proj, h0re.reshape(bn, 1, ns), h0im.reshape(bn, 1, ns), *consts,
      d_skip.reshape(d_skip.shape[0], 1, c), w_glu, b_glu.reshape(b_glu.shape[0], 1, c))
    return y, hre.reshape(bn, ns), him.reshape(bn, ns)


def _s5_step_kernel(u_ref, h0re_ref, h0im_ref, wbre_ref, wbim_ref, wcre_ref, wcim_ref, lbre_ref, lbim_ref, d_ref,
                    wglu_ref, bglu_ref, y_ref, hre_ref, him_ref, bre_ref, bim_ref, *, seq, bn, lane_chunk):
    u = u_ref[...]
    _s5_input_proj(u, wbre_ref, wbim_ref, bre_ref, bim_ref)
    ns = bre_ref.shape[1]
    for c in range(ns // lane_chunk):
        sl = slice(c * lane_chunk, (c + 1) * lane_chunk)
        lr = lbre_ref[:, sl]
        li = lbim_ref[:, sl]
        hr = h0re_ref[:, sl]
        hi = h0im_ref[:, sl]
        for t in range(seq):
            rows = slice(t * bn, (t + 1) * bn)
            hr, hi = (lr * hr - li * hi) + bre_ref[rows, sl], (lr * hi + li * hr) + bim_ref[rows, sl]
            bre_ref[rows, sl] = hr
            bim_ref[rows, sl] = hi
        hre_ref[:, sl] = hr
        him_ref[:, sl] = hi
    _s5_output(u, bre_ref, bim_ref, wcre_ref, wcim_ref, d_ref, wglu_ref, bglu_ref, y_ref)


def _s5_step(proj, h0re, h0im, tb, d_skip, w_glu, b_glu, l, *, bn, seq, lane_chunk=128):
    c = w_glu.shape[-1]
    ns = h0re.shape[-1]
    t = bn * seq
    consts = [tb["wb_re"], tb["wb_im"], tb["wc_re"], tb["wc_imneg"], tb["lb_re"], tb["lb_im"]]
    in_specs = ([pl.BlockSpec((t, c), lambda i: (0, 0)),
                 pl.BlockSpec((bn, ns), lambda i: (0, 0)),
                 pl.BlockSpec((bn, ns), lambda i: (0, 0))]
                + [_const_spec(a) for a in consts]
                + [pl.BlockSpec((None, 1, c), lambda i: (l, 0, 0)),
                   pl.BlockSpec((None, c, c), lambda i: (l, 0, 0), pipeline_mode=pl.Buffered(1)),
                   pl.BlockSpec((None, 1, c), lambda i: (l, 0, 0))])
    return pl.pallas_call(
        functools.partial(_s5_step_kernel, seq=seq, bn=bn, lane_chunk=lane_chunk),
        grid=(1,),
        in_specs=in_specs,
        out_specs=[pl.BlockSpec((t, c), lambda i: (0, 0)),
                   pl.BlockSpec((bn, ns), lambda i: (0, 0)),
                   pl.BlockSpec((bn, ns), lambda i: (0, 0))],
        out_shape=[jax.ShapeDtypeStruct((t, c), F32),
                   jax.ShapeDtypeStruct((bn, ns), F32),
                   jax.ShapeDtypeStruct((bn, ns), F32)],
        scratch_shapes=[pltpu.VMEM((t, ns), F32), pltpu.VMEM((t, ns), F32)],
        compiler_params=_params("arbitrary"),
        name="s5_step",
    )(proj, h0re, h0im, *consts,
      d_skip.reshape(d_skip.shape[0], 1, c), w_glu, b_glu.reshape(b_glu.shape[0], 1, c))


def _rg_tables(w_a, w_i):
    nh, hd, _ = w_a.shape
    hpc = V7X_MXU_DIM // hd
    eye = jnp.eye(hpc, dtype=F32)

    def expand(w):
        w = w.reshape(nh // hpc, hpc, hd, hd)
        return jnp.einsum("khij,hg->khigj", w, eye).reshape(nh // hpc, hpc * hd, hpc * hd)

    return expand(w_a), expand(w_i)


def _rg_gates(xc, wa_ref, wi_ref, ba_ref, bi_ref, lam_ref, a_ref, b_ref, rows):
    nck, kc, _ = wa_ref.shape
    lam = lam_ref[...]
    neg = -lam
    softplus = jnp.maximum(neg, 0.0) + jnp.log(1.0 + jnp.exp(-jnp.abs(neg)))
    for k in range(nck):
        cs = slice(k * kc, (k + 1) * kc)
        xk = xc[:, cs]
        r = _sigmoid(_dot(xk, wa_ref[k]) + ba_ref[:, cs])
        i = _sigmoid(_dot(xk, wi_ref[k]) + bi_ref[:, cs])
        log_a = (-RG_C * r) * softplus[:, cs]
        a_ref[rows, cs] = jnp.exp(log_a)
        th = jnp.tanh(log_a)
        b_ref[rows, cs] = jnp.sqrt(-2.0 * th / (1.0 - th)) * (i * xk)


def _rg_seq_kernel(x_ref, gy_ref, conv0_ref, h0_ref, cw_ref, cb_ref, wa_ref, wi_ref, ba_ref, bi_ref, lam_ref,
                   y_ref, hout_ref, xe_ref, a_ref, b_ref, c_ref, *, tm):
    s = pl.program_id(1)
    pad = V7X_SUBLANES
    nconv = conv0_ref.shape[0]

    @pl.when(s == 0)
    def _():
        xe_ref[pad - nconv:pad, :] = conv0_ref[...]
        c_ref[...] = jnp.broadcast_to(h0_ref[...], c_ref.shape)

    @pl.when(s > 0)
    def _():
        xe_ref[0:pad, :] = xe_ref[tm:tm + pad, :]

    xe_ref[pad:pad + tm, :] = x_ref[...]
    acc = cw_ref[0:1, :] * xe_ref[pad - nconv:pad - nconv + tm, :]
    for k in range(1, nconv + 1):
        acc = acc + cw_ref[k:k + 1, :] * xe_ref[pad - nconv + k:pad - nconv + k + tm, :]
    xc = cb_ref[...] + acc
    _rg_gates(xc, wa_ref, wi_ref, ba_ref, bi_ref, lam_ref, a_ref, b_ref, slice(None))

    row = lax.broadcasted_iota(jnp.int32, (V7X_SUBLANES, a_ref.shape[1]), 0)

    def body(r, carry):
        rows = pl.ds(pl.multiple_of(r * V7X_SUBLANES, V7X_SUBLANES), V7X_SUBLANES)
        a = a_ref[rows, :]
        b = b_ref[rows, :]
        for k in (1, 2, 4):
            keep = row >= k
            b = jnp.where(keep, b + a * pltpu.roll(b, k, 0), b)
            a = jnp.where(keep, a * pltpu.roll(a, k, 0), a)
        h = b + a * carry
        b_ref[rows, :] = h
        last = V7X_SUBLANES - 1
        return jnp.broadcast_to(h[last:last + 1], h.shape)

    carry = lax.fori_loop(0, tm // V7X_SUBLANES, body, c_ref[...])
    c_ref[...] = carry
    y_ref[...] = b_ref[...] * gy_ref[...]

    @pl.when(s == pl.num_programs(1) - 1)
    def _():
        hout_ref[...] = c_ref[0:1, :]


def _rg_seq(proj, conv0, h0, conv_w, conv_b, wa, wi, b_a, b_i, lam, l, *, bn, seq, tm=256):
    c = h0.shape[-1]
    n_s = seq // tm
    depth = conv_w.shape[0]
    vec = lambda a: a.reshape(depth, 1, c)
    y, hout = pl.pallas_call(
        functools.partial(_rg_seq_kernel, tm=tm),
        grid=(bn, n_s),
        in_specs=[pl.BlockSpec((tm, c), lambda b, s: (b * n_s + s, 1)),
                  pl.BlockSpec((tm, c), lambda b, s: (b * n_s + s, 2)),
                  pl.BlockSpec((None, CONV_W - 1, c), lambda b, s: (b, 0, 0)),
                  pl.BlockSpec((None, 1, c), lambda b, s: (b, 0, 0)),
                  pl.BlockSpec((None, CONV_W, c), lambda b, s: (l, 0, 0)),
                  pl.BlockSpec((None, 1, c), lambda b, s: (l, 0, 0)),
                  _const_spec(wa), _const_spec(wi),
                  pl.BlockSpec((None, 1, c), lambda b, s: (l, 0, 0)),
                  pl.BlockSpec((None, 1, c), lambda b, s: (l, 0, 0)),
                  pl.BlockSpec((None, 1, c), lambda b, s: (l, 0, 0))],
        out_specs=[pl.BlockSpec((tm, c), lambda b, s: (b * n_s + s, 0)),
                   pl.BlockSpec((None, 1, c), lambda b, s: (b, 0, 0))],
        out_shape=[jax.ShapeDtypeStruct((bn * seq, c), F32), jax.ShapeDtypeStruct((bn, 1, c), F32)],
        scratch_shapes=[pltpu.VMEM((tm + 2 * V7X_SUBLANES, c), F32), pltpu.VMEM((tm, c), F32),
                        pltpu.VMEM((tm, c), F32), pltpu.VMEM((V7X_SUBLANES, c), F32)],
        compiler_params=_params("arbitrary", "arbitrary"),
        name="rg_seq",
    )(proj, proj, conv0, h0.reshape(bn, 1, c), conv_w, vec(conv_b), wa, wi, vec(b_a), vec(b_i), vec(lam))
    return y, hout.reshape(bn, c)


def _rg_step_kernel(x_ref, gy_ref, conv0_ref, h0_ref, cw_ref, cb_ref, wa_ref, wi_ref, ba_ref, bi_ref, lam_ref,
                    y_ref, hout_ref, a_ref, b_ref, *, seq, bn):
    nconv = conv0_ref.shape[0]

    def xpad(i):
        if i < nconv:
            return conv0_ref[i]
        return x_ref[(i - nconv) * bn:(i - nconv + 1) * bn, :]

    for t in range(seq):
        acc = cw_ref[0:1, :] * xpad(t)
        for k in range(1, nconv + 1):
            acc = acc + cw_ref[k:k + 1, :] * xpad(t + k)
        xc = cb_ref[...] + acc
        _rg_gates(xc, wa_ref, wi_ref, ba_ref, bi_ref, lam_ref, a_ref, b_ref, slice(t * bn, (t + 1) * bn))
    h = h0_ref[...]
    for t in range(seq):
        rows = slice(t * bn, (t + 1) * bn)
        h = a_ref[rows, :] * h + b_ref[rows, :]
        y_ref[rows, :] = h * gy_ref[rows, :]
    hout_ref[...] = h


def _rg_step(proj, conv0_tm, h0, conv_w, conv_b, wa, wi, b_a, b_i, lam, l, *, bn, seq):
    c = h0.shape[-1]
    t = bn * seq
    depth = conv_w.shape[0]
    vec = lambda a: a.reshape(depth, 1, c)
    lspec = pl.BlockSpec((None, 1, c), lambda i: (l, 0, 0))
    return pl.pallas_call(
        functools.partial(_rg_step_kernel, seq=seq, bn=bn),
        grid=(1,),
        in_specs=[pl.BlockSpec((t, c), lambda i: (0, 1)),
                  pl.BlockSpec((t, c), lambda i: (0, 2)),
                  pl.BlockSpec(conv0_tm.shape, lambda i: (0, 0, 0)),
                  pl.BlockSpec((bn, c), lambda i: (0, 0)),
                  pl.BlockSpec((None, CONV_W, c), lambda i: (l, 0, 0)),
                  lspec, _const_spec(wa), _const_spec(wi), lspec, lspec, lspec],
        out_specs=[pl.BlockSpec((t, c), lambda i: (0, 0)), pl.BlockSpec((bn, c), lambda i: (0, 0))],
        out_shape=[jax.ShapeDtypeStruct((t, c), F32), jax.ShapeDtypeStruct((bn, c), F32)],
        scratch_shapes=[pltpu.VMEM((t, c), F32), pltpu.VMEM((t, c), F32)],
        compiler_params=_params("arbitrary"),
        name="rg_step",
    )(proj, proj, conv0_tm, h0, conv_w, vec(conv_b), wa, wi, vec(b_a), vec(b_i), vec(lam))


def _merge_kernel(ys_ref, yr_ref, ws_ref, wr_ref, gs_ref, gr_ref, o_ref):
    o_ref[...] = gs_ref[...] * _dot(ys_ref[...], ws_ref[...]) + gr_ref[...] * _dot(yr_ref[...], wr_ref[...])


def _branch_merge(ys, yr, w_s, w_r, gates, l, *, tm, tn):
    t, c = ys.shape
    d = w_s.shape[-1]
    off = d // tn
    return pl.pallas_call(
        _merge_kernel,
        grid=(t // tm, d // tn),
        in_specs=[pl.BlockSpec((tm, c), lambda m, n: (m, 0)),
                  pl.BlockSpec((tm, c), lambda m, n: (m, 0)),
                  pl.BlockSpec((None, c, tn), lambda m, n: (l, 0, n)),
                  pl.BlockSpec((None, c, tn), lambda m, n: (l, 0, n)),
                  pl.BlockSpec((tm, tn), lambda m, n: (m, n)),
                  pl.BlockSpec((tm, tn), lambda m, n: (m, n + off))],
        out_specs=pl.BlockSpec((tm, tn), lambda m, n: (m, n)),
        out_shape=jax.ShapeDtypeStruct((t, d), F32),
        compiler_params=_params("arbitrary", "arbitrary"),
        name="branch_merge",
    )(ys, yr, w_s, w_r, gates, gates)


def _linear_residual_kernel(a_ref, w_ref, x_ref, g_ref, o_ref, acc_ref):
    k = pl.program_id(2)

    @pl.when(k == 0)
    def _():
        acc_ref[...] = jnp.zeros_like(acc_ref)

    acc_ref[...] += _dot(a_ref[...], w_ref[...])

    @pl.when(k == pl.num_programs(2) - 1)
    def _():
        o_ref[...] = x_ref[...] + g_ref[...] * acc_ref[...]


def _linear_residual(a, w, l, x, mod, j_g, *, tm, tn, tk, tiles_per_block):
    t, kdim = a.shape
    d = w.shape[-1]
    rows = mod.shape[2]
    return pl.pallas_call(
        _linear_residual_kernel,
        grid=(t // tm, d // tn, kdim // tk),
        in_specs=[pl.BlockSpec((tm, tk), lambda m, n, k: (m, k)),
                  pl.BlockSpec((None, tk, tn), lambda m, n, k: (l, k, n)),
                  pl.BlockSpec((tm, tn), lambda m, n, k: (m, n)),
                  _mod_spec(j_g, rows, tn, tiles_per_block, tm, ncol_axis=0)],
        out_specs=pl.BlockSpec((tm, tn), lambda m, n, k: (m, n)),
        out_shape=jax.ShapeDtypeStruct((t, d), F32),
        scratch_shapes=[pltpu.VMEM((tm, tn), F32)],
        compiler_params=_params("arbitrary", "arbitrary", "arbitrary"),
        name="linear_residual",
    )(a, w, x, mod)


ROUTE_I1, ROUTE_I2, ROUTE_W1, ROUTE_W2, ROUTE_RANK1, ROUTE_RANK2 = range(6)


def _router_kernel(x_ref, sh_ref, sc_ref, g_ref, wr_ref, hn_ref, route_ref, cnt_ref, *, n_experts):
    m = pl.program_id(0)

    @pl.when(m == 0)
    def _():
        cnt_ref[...] = jnp.zeros_like(cnt_ref)

    _norm_mod_rows(x_ref, g_ref, sc_ref, sh_ref, hn_ref)
    logits = jnp.dot(hn_ref[...], wr_ref[...], preferred_element_type=F32, precision=lax.Precision.HIGHEST)
    tm, width = logits.shape
    lane = lax.broadcasted_iota(jnp.int32, logits.shape, 1).astype(F32)
    lg = jnp.where(lane < n_experts, logits, -jnp.inf)
    v1 = jnp.max(lg, axis=-1, keepdims=True)
    i1 = jnp.min(jnp.where(lg == v1, lane, float(width)), axis=-1, keepdims=True)
    lg2 = jnp.where(lane == i1, -jnp.inf, lg)
    v2 = jnp.max(lg2, axis=-1, keepdims=True)
    i2 = jnp.min(jnp.where(lg2 == v2, lane, float(width)), axis=-1, keepdims=True)
    e2 = jnp.exp(v2 - v1)
    den = 1.0 + e2
    sel1 = lane == i1
    sel2 = lane == i2
    picked = jnp.where(sel1 | sel2, 1.0, 0.0)
    r = lax.broadcasted_iota(jnp.int32, (tm, tm), 0)
    c = lax.broadcasted_iota(jnp.int32, (tm, tm), 1)
    before = jnp.where(c < r, 1.0, 0.0)
    seen = _dot(before, picked) + cnt_ref[...]
    rank1 = jnp.sum(jnp.where(sel1, seen, 0.0), axis=-1, keepdims=True)
    rank2 = jnp.sum(jnp.where(sel2, seen, 0.0), axis=-1, keepdims=True)
    cnt_ref[...] += jnp.sum(picked, axis=0, keepdims=True)
    cols = ((ROUTE_I1, i1), (ROUTE_I2, i2), (ROUTE_W1, 1.0 / den), (ROUTE_W2, e2 / den),
            (ROUTE_RANK1, rank1), (ROUTE_RANK2, rank2))
    route = jnp.zeros_like(logits)
    for j, v in cols:
        route = jnp.where(lane == j, v, route)
    route_ref[...] = route


def _router(x, mod, j_sh, g, w_router_pad, l, *, tm, tiles_per_block, n_experts):
    t, d = x.shape
    rows = mod.shape[2]
    width = w_router_pad.shape[-1]
    return pl.pallas_call(
        functools.partial(_router_kernel, n_experts=n_experts),
        grid=(t // tm,),
        in_specs=[pl.BlockSpec((tm, d), lambda m: (m, 0)),
                  _mod_spec(j_sh, rows, d, tiles_per_block, tm),
                  _mod_spec(j_sh + 1, rows, d, tiles_per_block, tm),
                  pl.BlockSpec((None, 1, d), lambda m: (l, 0, 0)),
                  pl.BlockSpec((d, width), lambda m: (0, 0))],
        out_specs=[pl.BlockSpec((tm, d), lambda m: (m, 0)), pl.BlockSpec((tm, width), lambda m: (m, 0)),
                   pl.BlockSpec((1, width), lambda m: (0, 0))],
        out_shape=[jax.ShapeDtypeStruct((t, d), F32), jax.ShapeDtypeStruct((t, width), F32),
                   jax.ShapeDtypeStruct((1, width), F32)],
        compiler_params=_params("arbitrary"),
        name="moe_router",
    )(x, mod, mod, g.reshape(g.shape[0], 1, d), w_router_pad)


def _dispatch_kernel(tok_ref, nu_ref, hn_ref, xs_ref, sem):
    tg = xs_ref.shape[0]
    m = pl.program_id(0)

    def row_copy(tok, i):
        return pltpu.make_async_copy(hn_ref.at[pl.ds(tok, 1)], xs_ref.at[pl.ds(i, 1)], sem.at[0])

    @pl.when(m < nu_ref[0])
    def _():
        lax.fori_loop(0, tg, lambda i, c: (row_copy(tok_ref[m * tg + i], i).start(), c)[1], 0)
        lax.fori_loop(0, tg, lambda i, c: (row_copy(0, 0).wait(), c)[1], 0)

    @pl.when(m >= nu_ref[0])
    def _():
        xs_ref[...] = jnp.zeros_like(xs_ref)


def _dispatch(tok_of_row, n_used, hn, *, tg):
    rows = tok_of_row.shape[0]
    d = hn.shape[1]
    return pl.pallas_call(
        _dispatch_kernel,
        grid_spec=pltpu.PrefetchScalarGridSpec(
            num_scalar_prefetch=2, grid=(rows // tg,),
            in_specs=[pl.BlockSpec(memory_space=pl.ANY)],
            out_specs=pl.BlockSpec((tg, d), lambda m, tok, nu: (m, 0)),
            scratch_shapes=[pltpu.SemaphoreType.DMA((1,))]),
        out_shape=jax.ShapeDtypeStruct((rows, d), F32),
        compiler_params=_params("arbitrary"),
        name="moe_dispatch",
    )(tok_of_row, n_used, hn)


def _grouped_up_kernel(te_ref, nu_ref, xs_ref, w1_ref, w3_ref, o_ref):
    m = pl.program_id(1)

    @pl.when(m < nu_ref[0])
    def _():
        h = xs_ref[...]
        o_ref[...] = _silu(_dot(h, w1_ref[...])) * _dot(h, w3_ref[...])

    @pl.when(m >= nu_ref[0])
    def _():
        o_ref[...] = jnp.zeros_like(o_ref)


def _grouped_up(tile_expert, n_used, xs, w1, w3, lm, *, tg, tn):
    r, d = xs.shape
    f = w1.shape[-1]
    wspec = pl.BlockSpec((None, None, d, tn), lambda n, m, te, nu: (lm, te[m], 0, n))
    return pl.pallas_call(
        _grouped_up_kernel,
        grid_spec=pltpu.PrefetchScalarGridSpec(
            num_scalar_prefetch=2, grid=(f // tn, r // tg),
            in_specs=[pl.BlockSpec((tg, d), lambda n, m, te, nu: (m, 0)), wspec, wspec],
            out_specs=pl.BlockSpec((tg, tn), lambda n, m, te, nu: (m, n))),
        out_shape=jax.ShapeDtypeStruct((r, f), F32),
        compiler_params=_params("arbitrary", "arbitrary"),
        name="moe_up",
    )(tile_expert, n_used, xs, w1, w3)


def _grouped_down_kernel(te_ref, nu_ref, a_ref, w_ref, o_ref):
    m = pl.program_id(1)

    @pl.when(m < nu_ref[0])
    def _():
        o_ref[...] = _dot(a_ref[...], w_ref[...])

    @pl.when(m >= nu_ref[0])
    def _():
        o_ref[...] = jnp.zeros_like(o_ref)


def _grouped_down(tile_expert, n_used, act, w2, lm, *, tg, tn):
    r, f = act.shape
    d = w2.shape[-1]
    return pl.pallas_call(
        _grouped_down_kernel,
        grid_spec=pltpu.PrefetchScalarGridSpec(
            num_scalar_prefetch=2, grid=(d // tn, r // tg),
            in_specs=[pl.BlockSpec((tg, f), lambda n, m, te, nu: (m, 0)),
                      pl.BlockSpec((None, None, f, tn), lambda n, m, te, nu: (lm, te[m], 0, n))],
            out_specs=pl.BlockSpec((tg, tn), lambda n, m, te, nu: (m, n))),
        out_shape=jax.ShapeDtypeStruct((r, d), F32),
        compiler_params=_params("arbitrary", "arbitrary"),
        name="moe_down",
    )(tile_expert, n_used, act, w2)


def _combine_kernel(p1_ref, p2_ref, y_ref, route_ref, x_ref, g_ref, o_ref, ya_ref, yb_ref, sem):
    tc = ya_ref.shape[0]
    base = pl.program_id(0) * tc

    def row_copy(row, dst_ref, i):
        return pltpu.make_async_copy(y_ref.at[pl.ds(row, 1)], dst_ref.at[pl.ds(i, 1)], sem.at[0])

    def body(i, c):
        row_copy(p1_ref[base + i], ya_ref, i).start()
        row_copy(p2_ref[base + i], yb_ref, i).start()
        return c

    lax.fori_loop(0, tc, body, 0)
    lax.fori_loop(0, 2 * tc, lambda i, c: (row_copy(0, ya_ref, 0).wait(), c)[1], 0)
    w1 = route_ref[:, ROUTE_W1:ROUTE_W1 + 1]
    w2 = route_ref[:, ROUTE_W2:ROUTE_W2 + 1]
    o_ref[...] = x_ref[...] + g_ref[...] * (w1 * ya_ref[...] + w2 * yb_ref[...])


def _combine(pos1, pos2, y, route, x, mod, j_g, *, tc, tiles_per_block):
    t, d = x.shape
    rows = mod.shape[2]
    width = route.shape[-1]
    return pl.pallas_call(
        _combine_kernel,
        grid_spec=pltpu.PrefetchScalarGridSpec(
            num_scalar_prefetch=2, grid=(t // tc,),
            in_specs=[pl.BlockSpec(memory_space=pl.ANY),
                      pl.BlockSpec((tc, width), lambda m, p1, p2: (m, 0)),
                      pl.BlockSpec((tc, d), lambda m, p1, p2: (m, 0)),
                      _mod_spec(j_g, rows, d, tiles_per_block, tc)],
            out_specs=pl.BlockSpec((tc, d), lambda m, p1, p2: (m, 0)),
            scratch_shapes=[pltpu.VMEM((tc, d), F32), pltpu.VMEM((tc, d), F32), pltpu.SemaphoreType.DMA((1,))]),
        out_shape=jax.ShapeDtypeStruct((t, d), F32),
        compiler_params=_params("arbitrary"),
        name="moe_combine",
    )(pos1, pos2, y, route, x, mod)


def _routed_moe(x, mod, g, w_router, w1, w3, w2, l, lm, *, tm, tg, tiles_per_tile):
    t, d = x.shape
    ne = w_router.shape[-1]
    w_r = jnp.pad(w_router[lm], ((0, 0), (0, V7X_LANES - ne)))
    hn, route, cnt = _router(x, mod, 3, g, w_r, l, n_experts=ne, **tiles_per_tile(tm))
    counts = cnt[0, :ne].astype(jnp.int32)
    tiles_e = (counts + (tg - 1)) // tg
    tile_end = jnp.cumsum(tiles_e)
    row_start = (tile_end - tiles_e) * tg
    col = lambda j: route[:, j].astype(jnp.int32)
    pos1 = row_start[col(ROUTE_I1)] + col(ROUTE_RANK1)
    pos2 = row_start[col(ROUTE_I2)] + col(ROUTE_RANK2)
    n_tiles = -(-TOP_K * t // tg) + ne
    tile_expert = jnp.minimum(jnp.sum(jnp.arange(n_tiles)[:, None] >= tile_end[None, :], axis=1), ne - 1)
    tile_expert = tile_expert.astype(jnp.int32)
    n_used = tile_end[-1:].astype(jnp.int32)
    tok = jnp.arange(t, dtype=jnp.int32)
    tok_of_row = jnp.zeros((n_tiles * tg,), jnp.int32).at[jnp.concatenate([pos1, pos2])].set(
        jnp.concatenate([tok, tok]), unique_indices=True)
    xs = _dispatch(tok_of_row, n_used, hn, tg=tg)
    act = _grouped_up(tile_expert, n_used, xs, w1, w3, lm, tg=tg, tn=1024)
    y = _grouped_down(tile_expert, n_used, act, w2, lm, tg=tg, tn=1024)
    tc = min(tm, 256)
    return _combine(pos1, pos2, y, route, x, mod, 5, tc=tc, tiles_per_block=tiles_per_tile(tc)["tiles_per_block"])


def _final_norm_kernel(x_ref, g_ref, o_ref):
    o_ref[...] = _rmsnorm(x_ref[...], g_ref[...])


def _final_norm(x, g, *, tm):
    t, d = x.shape
    return pl.pallas_call(
        _final_norm_kernel,
        grid=(t // tm,),
        in_specs=[pl.BlockSpec((tm, d), lambda m: (m, 0)), pl.BlockSpec((1, d), lambda m: (0, 0))],
        out_specs=pl.BlockSpec((tm, d), lambda m: (m, 0)),
        out_shape=jax.ShapeDtypeStruct((t, d), F32),
        compiler_params=_params("arbitrary"),
        name="final_norm",
    )(x, g.reshape(1, d))


def _run(x, mods, state, p, *, bn, seq, time_major, tm):
    t, d = x.shape
    depth = p["w_in"].shape[0]
    c = p["s5_w_glu"].shape[-1]
    def tiled(tile):
        return dict(tm=tile, tiles_per_block=1 if time_major else seq // tile)

    kw = tiled(tm)
    nconv = CONV_W - 1
    assert seq >= nconv
    s5_re, s5_im, rg_h, conv = [], [], [], []
    for l in range(depth):
        mod = mods[l]
        tb = p["s5_tables"][l]
        wa, wi = p["rg_tables"][l]
        proj = _normed_linear(x, mod, 0, p["norm_mix"], l, [p["w_in"]], l, mode="in", tn=512,
                              plain_cols=2 * c, **kw)
        gates = _normed_linear(x, mod, 0, p["norm_mix"], l, [p["w_gate"]], l, mode="gate", tn=512,
                               bias=p["b_gate"], **kw)
        h0re = state["s5_re"][l].reshape(bn, -1)
        h0im = state["s5_im"][l].reshape(bn, -1)
        rg_args = (p["rg_conv_w"], p["rg_conv_b"], wa, wi, p["rg_b_a"].reshape(depth, c), p["rg_b_i"].reshape(depth, c),
                   p["rg_lam"], l)
        if time_major:
            y_s5, hre, him = _s5_step(proj, h0re, h0im, tb, p["s5_d"].reshape(depth, c), p["s5_w_glu"], p["s5_b_glu"], l,
                                      bn=bn, seq=seq)
            y_rg, hrg = _rg_step(proj, state["conv"][l].transpose(1, 0, 2), state["rg"][l], *rg_args, bn=bn, seq=seq)
            conv.append(proj[:, c:2 * c].reshape(seq, bn, c)[seq - nconv:].transpose(1, 0, 2))
        else:
            y_s5, hre, him = _s5_seq(proj, h0re, h0im, tb, p["s5_d"].reshape(depth, c), p["s5_w_glu"], p["s5_b_glu"], l,
                                     bn=bn, seq=seq)
            y_rg, hrg = _rg_seq(proj, state["conv"][l], state["rg"][l], *rg_args, bn=bn, seq=seq)
            conv.append(proj.reshape(bn, seq, 3 * c)[:, seq - nconv:, c:2 * c])
        s5_re.append(hre.reshape(state["s5_re"][l].shape))
        s5_im.append(him.reshape(state["s5_im"][l].shape))
        rg_h.append(hrg)
        merged = _branch_merge(y_s5, y_rg, p["w_br_s5"], p["w_br_rg"], gates, l, tm=tm, tn=512)
        x = _linear_residual(merged, p["w_out"], l, x, mod, 2, tn=1024, tk=1024, **kw)
        if l % 2 == 0:
            act = _normed_linear(x, mod, 3, p["norm_ffn"], l, [p["ffn_w1"], p["ffn_w3"]], l // 2, mode="glu",
                                 tn=512, **kw)
            x = _linear_residual(act, p["ffn_w2"], l // 2, x, mod, 5, tn=1024, tk=1024, **kw)
        else:
            x = _routed_moe(x, mod, p["norm_ffn"], p["moe_router"], p["moe_w1"], p["moe_w3"], p["moe_w2"], l, l // 2,
                            tm=min(tm, 512), tg=512 if t >= 4096 else 128, tiles_per_tile=tiled)
    y = _final_norm(x, p["norm_f"], tm=min(tm, 256))
    return y, jnp.stack(s5_re), jnp.stack(s5_im), jnp.stack(rg_h), jnp.stack(conv)


def kernel(x_prompt, x_sample, c_prompt, c_sample, state_s5_re, state_s5_im, state_rglru, state_conv, norm_mix, norm_ffn, norm_f, w_ada, b_ada, w_in, s5_lam_re, s5_lam_im, s5_log_dt, s5_b_re, s5_b_im, s5_c_re, s5_c_im, s5_d, s5_w_glu, s5_b_glu, rg_conv_w, rg_conv_b, rg_w_a, rg_b_a, rg_w_i, rg_b_i, rg_lam, w_gate, b_gate, w_br_s5, w_br_rg, w_out, ffn_w1, ffn_w3, ffn_w2, moe_router, moe_w1, moe_w3, moe_w2):
    p = dict(norm_mix=norm_mix, norm_ffn=norm_ffn, norm_f=norm_f, w_in=w_in,
             s5_lam_re=s5_lam_re, s5_lam_im=s5_lam_im, s5_log_dt=s5_log_dt, s5_b_re=s5_b_re, s5_b_im=s5_b_im,
             s5_c_re=s5_c_re, s5_c_im=s5_c_im, s5_d=s5_d, s5_w_glu=s5_w_glu, s5_b_glu=s5_b_glu,
             rg_conv_w=rg_conv_w, rg_conv_b=rg_conv_b, rg_w_a=rg_w_a, rg_b_a=rg_b_a, rg_w_i=rg_w_i, rg_b_i=rg_b_i,
             rg_lam=rg_lam, w_gate=w_gate, b_gate=b_gate, w_br_s5=w_br_s5, w_br_rg=w_br_rg, w_out=w_out,
             ffn_w1=ffn_w1, ffn_w3=ffn_w3, ffn_w2=ffn_w2, moe_router=moe_router, moe_w1=moe_w1, moe_w3=moe_w3,
             moe_w2=moe_w2)
    depth = w_in.shape[0]
    p["s5_tables"] = [_s5_tables(s5_lam_re[l], s5_lam_im[l], s5_log_dt[l], s5_b_re[l], s5_b_im[l], s5_c_re[l],
                                 s5_c_im[l]) for l in range(depth)]
    p["rg_tables"] = [_rg_tables(rg_w_a[l], rg_w_i[l]) for l in range(depth)]
    bp, sp, d = x_prompt.shape
    bs, ss, _ = x_sample.shape
    g, pst = state_s5_re.shape[2:]
    c = state_rglru.shape[-1]

    n_c = bp + bs
    rows_c = -(-n_c // V7X_SUBLANES) * V7X_SUBLANES
    c_all = jnp.pad(jnp.concatenate([c_prompt, c_sample], axis=0), ((0, rows_c - n_c), (0, 0)))
    mod_all = _ada_mod(c_all, w_ada, b_ada)
    mods_p, mods_s = [], []
    for l in range(depth):
        mp = mod_all[l, :bp].reshape(bp, 6, 1, d).transpose(1, 0, 2, 3)
        ms = mod_all[l, bp:n_c].reshape(bs, 6, d).transpose(1, 0, 2)
        ms = jnp.tile(ms, (1, ss, 1)).reshape(6, 1, ss * bs, d)
        mods_p.append(mp)
        mods_s.append(ms)

    zero_state = dict(s5_re=jnp.zeros((depth, bp, g, pst), F32), s5_im=jnp.zeros((depth, bp, g, pst), F32),
                      rg=jnp.zeros((depth, bp, c), F32), conv=jnp.zeros((depth, bp, CONV_W - 1, c), x_prompt.dtype))
    y_p, p_re, p_im, p_rg, p_conv = _run(x_prompt.reshape(bp * sp, d), mods_p, zero_state, p,
                                         bn=bp, seq=sp, time_major=False, tm=min(PROMPT_TILE, sp))
    sample_state = dict(s5_re=state_s5_re, s5_im=state_s5_im, rg=state_rglru, conv=state_conv)
    x_s = x_sample.transpose(1, 0, 2).reshape(ss * bs, d)
    y_s, s_re, s_im, s_rg, s_conv = _run(x_s, mods_s, sample_state, p, bn=bs, seq=ss, time_major=True, tm=ss * bs)
    y_s = y_s.reshape(ss, bs, d).transpose(1, 0, 2)
    return (y_p.reshape(bp, sp, d), y_s, p_re, p_im, p_rg, p_conv, s_re, s_im, s_rg, s_conv)
```

```python
import functools
import math

import jax
import jax.numpy as jnp
from jax import lax
from jax.experimental import pallas as pl
from jax.experimental.pallas import tpu as pltpu

F32 = jnp.float32
EPS = 1e-6
RG_C = 8.0
CONV_W = 4
S5_GROUP = 16
S5_STATE = 64
RG_HEAD_DIM = 64
TOP_K = 2

V7X_LANES = 128
V7X_SUBLANES = 8
V7X_MXU_DIM = 256
VMEM_LIMIT_BYTES = 56 * 1024 * 1024

GELU_C = math.sqrt(2.0 / math.pi)


def _gelu(x):
    return 0.5 * x * (1.0 + jnp.tanh(GELU_C * (x + 0.044715 * (x * x * x))))


def _sigmoid(x):
    return 1.0 / (1.0 + jnp.exp(-x))


def _silu(x):
    return x * _sigmoid(x)


def _dot(a, b):
    return jnp.dot(a, b, preferred_element_type=F32)


def _params(*sem):
    return pltpu.CompilerParams(dimension_semantics=sem, vmem_limit_bytes=VMEM_LIMIT_BYTES)


def _mod_spec(j, rows, width, tiles_per_block, tile, ncol_axis=None):
    col = (lambda idx: 0) if ncol_axis is None else (lambda idx: idx[ncol_axis])
    if rows == 1:
        return pl.BlockSpec((None, None, 1, width), lambda m, *idx: (j, m // tiles_per_block, 0, col(idx)))
    return pl.BlockSpec((None, None, tile, width), lambda m, *idx: (j, 0, m, col(idx)))


ROW_CHUNK = 128
PROMPT_TILE = 1024
SEG_LEN = 32
SEG_TILE = SEG_LEN * V7X_SUBLANES


def _rmsnorm(x, g):
    ms = jnp.mean(x * x, axis=-1, keepdims=True)
    return (x * lax.rsqrt(ms + EPS)) * g


def _norm_mod_rows(x_ref, g_ref, sc_ref, sh_ref, o_ref):
    tm = x_ref.shape[0]
    chunk = min(ROW_CHUNK, tm)
    per_row = sc_ref.shape[0] != 1

    def body(i, carry):
        rows = pl.ds(pl.multiple_of(i * chunk, chunk), chunk)
        sc = sc_ref[rows, :] if per_row else sc_ref[...]
        sh = sh_ref[rows, :] if per_row else sh_ref[...]
        o_ref[rows, :] = _rmsnorm(x_ref[rows, :], g_ref[...]) * (1.0 + sc) + sh
        return carry

    lax.fori_loop(0, tm // chunk, body, 0)


def _ada_kernel(c_ref, w_ref, b_ref, o_ref):
    o_ref[...] = _dot(_silu(c_ref[...]), w_ref[...]) + b_ref[...]


def _ada_mod(c_all, w_ada, b_ada, *, tn=1024):
    depth, d, n6 = w_ada.shape
    rows = c_all.shape[0]
    return pl.pallas_call(
        _ada_kernel,
        grid=(depth, n6 // tn),
        in_specs=[pl.BlockSpec((rows, d), lambda l, n: (0, 0)),
                  pl.BlockSpec((None, d, tn), lambda l, n: (l, 0, n)),
                  pl.BlockSpec((None, 1, tn), lambda l, n: (l, 0, n))],
        out_specs=pl.BlockSpec((None, rows, tn), lambda l, n: (l, 0, n)),
        out_shape=jax.ShapeDtypeStruct((depth, rows, n6), F32),
        compiler_params=_params("arbitrary", "arbitrary"),
        name="ada_mod",
    )(c_all, w_ada, b_ada.reshape(depth, 1, n6))


def _normed_linear_kernel(*refs, mode, n_plain):
    if mode == "glu":
        x_ref, sh_ref, sc_ref, g_ref, w1_ref, w3_ref, o_ref, hn_ref = refs
    elif mode == "gate":
        x_ref, sh_ref, sc_ref, g_ref, w_ref, b_ref, o_ref, hn_ref = refs
    else:
        x_ref, sh_ref, sc_ref, g_ref, w_ref, o_ref, hn_ref = refs
    n = pl.program_id(1)

    @pl.when(n == 0)
    def _():
        _norm_mod_rows(x_ref, g_ref, sc_ref, sh_ref, hn_ref)

    h = hn_ref[...]
    if mode == "glu":
        o_ref[...] = _silu(_dot(h, w1_ref[...])) * _dot(h, w3_ref[...])
    elif mode == "gate":
        o_ref[...] = _sigmoid(_dot(h, w_ref[...]) + b_ref[...])
    else:
        z = _dot(h, w_ref[...])

        @pl.when(n < n_plain)
        def _():
            o_ref[...] = z

        @pl.when(n >= n_plain)
        def _():
            o_ref[...] = _gelu(z)


def _normed_linear(x, mod, j_sh, g, l_g, ws, l, *, mode, tm, tn, tiles_per_block, bias=None, plain_cols=0):
    t, d = x.shape
    n_out = ws[0].shape[-1]
    rows = mod.shape[2]
    w_spec = pl.BlockSpec((None, d, tn), lambda m, n: (l, 0, n))
    in_specs = [pl.BlockSpec((tm, d), lambda m, n: (m, 0)),
                _mod_spec(j_sh, rows, d, tiles_per_block, tm),
                _mod_spec(j_sh + 1, rows, d, tiles_per_block, tm),
                pl.BlockSpec((None, 1, d), lambda m, n: (l_g, 0, 0))]
    args = [x, mod, mod, g.reshape(g.shape[0], 1, d)]
    for w in ws:
        in_specs.append(w_spec)
        args.append(w)
    if bias is not None:
        in_specs.append(pl.BlockSpec((None, 1, tn), lambda m, n: (l, 0, n)))
        args.append(bias.reshape(bias.shape[0], 1, n_out))
    return pl.pallas_call(
        functools.partial(_normed_linear_kernel, mode=mode, n_plain=plain_cols // tn),
        grid=(t // tm, n_out // tn),
        in_specs=in_specs,
        out_specs=pl.BlockSpec((tm, tn), lambda m, n: (m, n)),
        out_shape=jax.ShapeDtypeStruct((t, n_out), F32),
        scratch_shapes=[pltpu.VMEM((tm, d), F32)],
        compiler_params=_params("arbitrary", "arbitrary"),
        name="normed_linear_" + mode,
    )(*args)


def _s5_tables(lam_re, lam_im, log_dt, b_re, b_im, c_re, c_im):
    g, p = lam_re.shape
    h = b_re.shape[-1]
    gpc = V7X_MXU_DIM // h
    nck = g // gpc
    dt = jnp.exp(log_dt)[:, None]
    mag = jnp.exp(lam_re * dt)
    lb_re = mag * jnp.cos(lam_im * dt)
    lb_im = mag * jnp.sin(lam_im * dt)
    den = lam_re * lam_re + lam_im * lam_im
    cf_re = ((lb_re - 1.0) * lam_re + lb_im * lam_im) / den
    cf_im = (lb_im * lam_re - (lb_re - 1.0) * lam_im) / den
    bb_re = cf_re[:, :, None] * b_re - cf_im[:, :, None] * b_im
    bb_im = cf_re[:, :, None] * b_im + cf_im[:, :, None] * b_re
    eye = jnp.eye(gpc, dtype=F32)

    def expand_b(m):
        m = m.reshape(nck, gpc, p, h).transpose(0, 1, 3, 2)
        return jnp.einsum("kghp,gj->kghjp", m, eye).reshape(nck, gpc * h, gpc * p)

    def expand_c(m):
        m = m.reshape(nck, gpc, h, p).transpose(0, 1, 3, 2)
        return jnp.einsum("kgph,gj->kgpjh", m, eye).reshape(nck, gpc * p, gpc * h)

    lb_re = lb_re.reshape(1, g * p)
    lb_im = lb_im.reshape(1, g * p)

    def powers(re, im, n):
        out_re, out_im = [re], [im]
        for _ in range(n - 1):
            out_re, out_im = (out_re + [out_re[-1] * re - out_im[-1] * im],
                              out_im + [out_re[-1] * im + out_im[-1] * re])
        return out_re, out_im

    pw_re, pw_im = powers(lb_re, lb_im, SEG_LEN)
    sg_re, sg_im = powers(pw_re[-1], pw_im[-1], V7X_SUBLANES)
    row = jnp.arange(V7X_SUBLANES)[:, None]
    segtabs = []
    for k in (1, 2, 4):
        segtabs += [jnp.where(row >= k, sg_re[k - 1], 0.0), jnp.where(row >= k, sg_im[k - 1], 0.0)]
    segtabs += [jnp.concatenate(sg_re, axis=0), jnp.concatenate(sg_im, axis=0)]
    bcast = lambda rows: jnp.repeat(jnp.concatenate(rows, axis=0), V7X_SUBLANES, axis=0)
    return dict(wb_re=expand_b(bb_re), wb_im=expand_b(bb_im),
                wc_re=expand_c(c_re), wc_imneg=expand_c(-c_im),
                lam8=jnp.stack([jnp.broadcast_to(lb_re, (V7X_SUBLANES, g * p)),
                                jnp.broadcast_to(lb_im, (V7X_SUBLANES, g * p))]),
                segtabs=jnp.stack(segtabs).astype(F32),
                fix=jnp.stack([bcast(pw_re), bcast(pw_im)]),
                lb_re=lb_re, lb_im=lb_im)


def _s5_input_proj(u, wbre_ref, wbim_ref, bre_ref, bim_ref):
    nck, kc, nc = wbre_ref.shape
    for k in range(nck):
        uk = u[:, k * kc:(k + 1) * kc]
        bre_ref[:, k * nc:(k + 1) * nc] = _dot(uk, wbre_ref[k])
        bim_ref[:, k * nc:(k + 1) * nc] = _dot(uk, wbim_ref[k])


def _s5_output(u, bre_ref, bim_ref, wcre_ref, wcim_ref, d_ref, wglu_ref, bglu_ref, y_ref):
    nck, nc, kc = wcre_ref.shape
    for k in range(nck):
        yk = _dot(bre_ref[:, k * nc:(k + 1) * nc], wcre_ref[k]) + _dot(bim_ref[:, k * nc:(k + 1) * nc], wcim_ref[k])
        yk = yk + d_ref[:, k * kc:(k + 1) * kc] * u[:, k * kc:(k + 1) * kc]
        y_ref[:, k * kc:(k + 1) * kc] = _gelu(yk)
    yg = y_ref[...]
    y_ref[...] = yg * _sigmoid(_dot(yg, wglu_ref[...]) + bglu_ref[...])


def _row_block(j):
    return pl.ds(pl.multiple_of(j * V7X_SUBLANES, V7X_SUBLANES), V7X_SUBLANES)


def _s5_seq_kernel(u_ref, h0re_ref, h0im_ref, wbre_ref, wbim_ref, wcre_ref, wcim_ref, lam8_ref, seg_ref, fix_ref,
                   d_ref, wglu_ref, bglu_ref, y_ref, hre_ref, him_ref, bre_ref, bim_ref, cre_ref, cim_ref,
                   *, lane_chunk):
    s = pl.program_id(1)
    ns = cre_ref.shape[1]

    @pl.when(s == 0)
    def _():
        cre_ref[...] = jnp.broadcast_to(h0re_ref[...], cre_ref.shape)
        cim_ref[...] = jnp.broadcast_to(h0im_ref[...], cim_ref.shape)

    u = u_ref[...]
    _s5_input_proj(u, wbre_ref, wbim_ref, bre_ref, bim_ref)
    first_sublane = lax.broadcasted_iota(jnp.int32, (V7X_SUBLANES, lane_chunk), 0) == 0
    last = V7X_SUBLANES - 1

    for c in range(ns // lane_chunk):
        sl = slice(c * lane_chunk, (c + 1) * lane_chunk)
        lr, li = lam8_ref[0, :, sl], lam8_ref[1, :, sl]

        def local(j, st, sl=sl, lr=lr, li=li):
            sr, si = st
            rows = _row_block(j)
            sr, si = (lr * sr - li * si) + bre_ref[rows, sl], (lr * si + li * sr) + bim_ref[rows, sl]
            bre_ref[rows, sl] = sr
            bim_ref[rows, sl] = si
            return sr, si

        zero = jnp.zeros((V7X_SUBLANES, lane_chunk), F32)
        xr, xi = lax.fori_loop(0, SEG_LEN, local, (zero, zero), unroll=2)
        for i, k in enumerate((1, 2, 4)):
            tr, ti = seg_ref[2 * i, :, sl], seg_ref[2 * i + 1, :, sl]
            pr, pi = pltpu.roll(xr, k, 0), pltpu.roll(xi, k, 0)
            xr, xi = xr + (tr * pr - ti * pi), xi + (tr * pi + ti * pr)
        cr, ci = cre_ref[:, sl], cim_ref[:, sl]
        tr, ti = seg_ref[6, :, sl], seg_ref[7, :, sl]
        gr = xr + (tr * cr - ti * ci)
        gi = xi + (tr * ci + ti * cr)
        h0r = jnp.where(first_sublane, cr, pltpu.roll(gr, 1, 0))
        h0i = jnp.where(first_sublane, ci, pltpu.roll(gi, 1, 0))
        cre_ref[:, sl] = jnp.broadcast_to(gr[last:last + 1], gr.shape)
        cim_ref[:, sl] = jnp.broadcast_to(gi[last:last + 1], gi.shape)

        def fixup(j, carry, sl=sl, h0r=h0r, h0i=h0i):
            rows = _row_block(j)
            fr, fi = fix_ref[0, rows, sl], fix_ref[1, rows, sl]
            bre_ref[rows, sl] += fr * h0r - fi * h0i
            bim_ref[rows, sl] += fr * h0i + fi * h0r
            return carry

        lax.fori_loop(0, SEG_LEN, fixup, 0, unroll=2)

    _s5_output(u, bre_ref, bim_ref, wcre_ref, wcim_ref, d_ref, wglu_ref, bglu_ref, y_ref)

    @pl.when(s == pl.num_programs(1) - 1)
    def _():
        hre_ref[...] = cre_ref[0:1, :]
        him_ref[...] = cim_ref[0:1, :]


def _const_spec(a):
    nd = a.ndim
    return pl.BlockSpec(a.shape, lambda *_: (0,) * nd, pipeline_mode=pl.Buffered(1))


def _s5_seq(proj, h0re, h0im, tb, d_skip, w_glu, b_glu, l, *, bn, seq, lane_chunk=512):
    c = w_glu.shape[-1]
    ns = h0re.shape[-1]
    tm = SEG_TILE
    n_s = seq // tm
    consts = [tb["wb_re"], tb["wb_im"], tb["wc_re"], tb["wc_imneg"], tb["lam8"], tb["segtabs"], tb["fix"]]
    in_specs = ([pl.BlockSpec((tm, c), lambda b, s: (b * n_s + s, 0)),
                 pl.BlockSpec((None, 1, ns), lambda b, s: (b, 0, 0)),
                 pl.BlockSpec((None, 1, ns), lambda b, s: (b, 0, 0))]
                + [_const_spec(a) for a in consts]
                + [pl.BlockSpec((None, 1, c), lambda b, s: (l, 0, 0)),
                   pl.BlockSpec((None, c, c), lambda b, s: (l, 0, 0), pipeline_mode=pl.Buffered(1)),
                   pl.BlockSpec((None, 1, c), lambda b, s: (l, 0, 0))])
    y, hre, him = pl.pallas_call(
        functools.partial(_s5_seq_kernel, lane_chunk=lane_chunk),
        grid=(bn, n_s),
        in_specs=in_specs,
        out_specs=[pl.BlockSpec((tm, c), lambda b, s: (b * n_s + s, 0)),
                   pl.BlockSpec((None, 1, ns), lambda b, s: (b, 0, 0)),
                   pl.BlockSpec((None, 1, ns), lambda b, s: (b, 0, 0))],
        out_shape=[jax.ShapeDtypeStruct((bn * seq, c), F32),
                   jax.ShapeDtypeStruct((bn, 1, ns), F32),
                   jax.ShapeDtypeStruct((bn, 1, ns), F32)],
        scratch_shapes=[pltpu.VMEM((tm, ns), F32), pltpu.VMEM((tm, ns), F32),
                        pltpu.VMEM((V7X_SUBLANES, ns), F32), pltpu.VMEM((V7X_SUBLANES, ns), F32)],
        compiler_params=_params("arbitrary", "arbitrary"),
        name="s5_seq",
    )(proj, h0re.reshape(bn, 1, ns), h0im.reshape(bn, 1, ns), *consts,
      d_skip.reshape(d_skip.shape[0], 1, c), w_glu, b_glu.reshape(b_glu.shape[0], 1, c))
    return y, hre.reshape(bn, ns), him.reshape(bn, ns)


def _s5_step_kernel(u_ref, h0re_ref, h0im_ref, wbre_ref, wbim_ref, wcre_ref, wcim_ref, lbre_ref, lbim_ref, d_ref,
                    wglu_ref, bglu_ref, y_ref, hre_ref, him_ref, bre_ref, bim_ref, *, seq, bn, lane_chunk):
    u = u_ref[...]
    _s5_input_proj(u, wbre_ref, wbim_ref, bre_ref, bim_ref)
    ns = bre_ref.shape[1]
    for c in range(ns // lane_chunk):
        sl = slice(c * lane_chunk, (c + 1) * lane_chunk)
        lr = lbre_ref[:, sl]
        li = lbim_ref[:, sl]
        hr = h0re_ref[:, sl]
        hi = h0im_ref[:, sl]
        for t in range(seq):
            rows = slice(t * bn, (t + 1) * bn)
            hr, hi = (lr * hr - li * hi) + bre_ref[rows, sl], (lr * hi + li * hr) + bim_ref[rows, sl]
            bre_ref[rows, sl] = hr
            bim_ref[rows, sl] = hi
        hre_ref[:, sl] = hr
        him_ref[:, sl] = hi
    _s5_output(u, bre_ref, bim_ref, wcre_ref, wcim_ref, d_ref, wglu_ref, bglu_ref, y_ref)


def _s5_step(proj, h0re, h0im, tb, d_skip, w_glu, b_glu, l, *, bn, seq, lane_chunk=128):
    c = w_glu.shape[-1]
    ns = h0re.shape[-1]
    t = bn * seq
    consts = [tb["wb_re"], tb["wb_im"], tb["wc_re"], tb["wc_imneg"], tb["lb_re"], tb["lb_im"]]
    in_specs = ([pl.BlockSpec((t, c), lambda i: (0, 0)),
                 pl.BlockSpec((bn, ns), lambda i: (0, 0)),
                 pl.BlockSpec((bn, ns), lambda i: (0, 0))]
                + [_const_spec(a) for a in consts]
                + [pl.BlockSpec((None, 1, c), lambda i: (l, 0, 0)),
                   pl.BlockSpec((None, c, c), lambda i: (l, 0, 0), pipeline_mode=pl.Buffered(1)),
                   pl.BlockSpec((None, 1, c), lambda i: (l, 0, 0))])
    return pl.pallas_call(
        functools.partial(_s5_step_kernel, seq=seq, bn=bn, lane_chunk=lane_chunk),
        grid=(1,),
        in_specs=in_specs,
        out_specs=[pl.BlockSpec((t, c), lambda i: (0, 0)),
                   pl.BlockSpec((bn, ns), lambda i: (0, 0)),
                   pl.BlockSpec((bn, ns), lambda i: (0, 0))],
        out_shape=[jax.ShapeDtypeStruct((t, c), F32),
                   jax.ShapeDtypeStruct((bn, ns), F32),
                   jax.ShapeDtypeStruct((bn, ns), F32)],
        scratch_shapes=[pltpu.VMEM((t, ns), F32), pltpu.VMEM((t, ns), F32)],
        compiler_params=_params("arbitrary"),
        name="s5_step",
    )(proj, h0re, h0im, *consts,
      d_skip.reshape(d_skip.shape[0], 1, c), w_glu, b_glu.reshape(b_glu.shape[0], 1, c))


def _rg_tables(w_a, w_i):
    nh, hd, _ = w_a.shape
    hpc = V7X_MXU_DIM // hd
    eye = jnp.eye(hpc, dtype=F32)

    def expand(w):
        w = w.reshape(nh // hpc, hpc, hd, hd)
        return jnp.einsum("khij,hg->khigj", w, eye).reshape(nh // hpc, hpc * hd, hpc * hd)

    return expand(w_a), expand(w_i)


def _rg_gates(xc, wa_ref, wi_ref, ba_ref, bi_ref, lam_ref, a_ref, b_ref, rows):
    nck, kc, _ = wa_ref.shape
    lam = lam_ref[...]
    neg = -lam
    softplus = jnp.maximum(neg, 0.0) + jnp.log(1.0 + jnp.exp(-jnp.abs(neg)))
    for k in range(nck):
        cs = slice(k * kc, (k + 1) * kc)
        xk = xc[:, cs]
        r = _sigmoid(_dot(xk, wa_ref[k]) + ba_ref[:, cs])
        i = _sigmoid(_dot(xk, wi_ref[k]) + bi_ref[:, cs])
        log_a = (-RG_C * r) * softplus[:, cs]
        a_ref[rows, cs] = jnp.exp(log_a)
        th = jnp.tanh(log_a)
        b_ref[rows, cs] = jnp.sqrt(-2.0 * th / (1.0 - th)) * (i * xk)


def _rg_seq_kernel(x_ref, gy_ref, conv0_ref, h0_ref, cw_ref, cb_ref, wa_ref, wi_ref, ba_ref, bi_ref, lam_ref,
                   y_ref, hout_ref, xe_ref, a_ref, b_ref, c_ref, tail_ref):
    s = pl.program_id(1)
    sub = V7X_SUBLANES
    nconv = conv0_ref.shape[0]
    tm, c = x_ref.shape
    first_sublane = lax.broadcasted_iota(jnp.int32, (sub, c), 0) == 0
    last = sub - 1

    @pl.when(s == 0)
    def _():
        tail_ref[0:nconv, :] = conv0_ref[...]
        c_ref[...] = jnp.broadcast_to(h0_ref[...], c_ref.shape)

    for f in range(nconv):
        blk = x_ref[(SEG_LEN - nconv + f) * sub:(SEG_LEN - nconv + f + 1) * sub, :]
        xe_ref[f * sub:(f + 1) * sub, :] = jnp.where(first_sublane, tail_ref[f:f + 1, :], pltpu.roll(blk, 1, 0))
    for f in range(nconv):
        tail_ref[f:f + 1, :] = x_ref[(SEG_LEN - nconv + f) * sub + last:(SEG_LEN - nconv + f + 1) * sub, :]
    xe_ref[nconv * sub:nconv * sub + tm, :] = x_ref[...]
    acc = cw_ref[0:1, :] * xe_ref[0:tm, :]
    for k in range(1, nconv + 1):
        acc = acc + cw_ref[k:k + 1, :] * xe_ref[k * sub:k * sub + tm, :]
    xc = cb_ref[...] + acc
    _rg_gates(xc, wa_ref, wi_ref, ba_ref, bi_ref, lam_ref, a_ref, b_ref, slice(None))

    def local(j, st):
        h, p = st
        rows = _row_block(j)
        a = a_ref[rows, :]
        h = a * h + b_ref[rows, :]
        p = p * a
        b_ref[rows, :] = h
        a_ref[rows, :] = p
        return h, p

    e, p = lax.fori_loop(0, SEG_LEN, local, (jnp.zeros((sub, c), F32), jnp.ones((sub, c), F32)), unroll=2)
    row = lax.broadcasted_iota(jnp.int32, (sub, c), 0)
    for k in (1, 2, 4):
        keep = row >= k
        e = jnp.where(keep, e + p * pltpu.roll(e, k, 0), e)
        p = jnp.where(keep, p * pltpu.roll(p, k, 0), p)
    carry = c_ref[...]
    g = e + p * carry
    h_in = jnp.where(first_sublane, carry, pltpu.roll(g, 1, 0))
    c_ref[...] = jnp.broadcast_to(g[last:last + 1], g.shape)

    def fixup(j, cc):
        rows = _row_block(j)
        y_ref[rows, :] = (b_ref[rows, :] + a_ref[rows, :] * h_in) * gy_ref[rows, :]
        return cc

    lax.fori_loop(0, SEG_LEN, fixup, 0, unroll=2)

    @pl.when(s == pl.num_programs(1) - 1)
    def _():
        hout_ref[...] = c_ref[0:1, :]


def _rg_seq(proj, conv0, h0, conv_w, conv_b, wa, wi, b_a, b_i, lam, l, *, bn, seq):
    c = h0.shape[-1]
    tm = SEG_TILE
    n_s = seq // tm
    depth = conv_w.shape[0]
    vec = lambda a: a.reshape(depth, 1, c)
    y, hout = pl.pallas_call(
        _rg_seq_kernel,
        grid=(bn, n_s),
        in_specs=[pl.BlockSpec((tm, c), lambda b, s: (b * n_s + s, 1)),
                  pl.BlockSpec((tm, c), lambda b, s: (b * n_s + s, 2)),
                  pl.BlockSpec((None, CONV_W - 1, c), lambda b, s: (b, 0, 0)),
                  pl.BlockSpec((None, 1, c), lambda b, s: (b, 0, 0)),
                  pl.BlockSpec((None, CONV_W, c), lambda b, s: (l, 0, 0)),
                  pl.BlockSpec((None, 1, c), lambda b, s: (l, 0, 0)),
                  _const_spec(wa), _const_spec(wi),
                  pl.BlockSpec((None, 1, c), lambda b, s: (l, 0, 0)),
                  pl.BlockSpec((None, 1, c), lambda b, s: (l, 0, 0)),
                  pl.BlockSpec((None, 1, c), lambda b, s: (l, 0, 0))],
        out_specs=[pl.BlockSpec((tm, c), lambda b, s: (b * n_s + s, 0)),
                   pl.BlockSpec((None, 1, c), lambda b, s: (b, 0, 0))],
        out_shape=[jax.ShapeDtypeStruct((bn * seq, c), F32), jax.ShapeDtypeStruct((bn, 1, c), F32)],
        scratch_shapes=[pltpu.VMEM((tm + (CONV_W - 1) * V7X_SUBLANES, c), F32), pltpu.VMEM((tm, c), F32),
                        pltpu.VMEM((tm, c), F32), pltpu.VMEM((V7X_SUBLANES, c), F32),
                        pltpu.VMEM((V7X_SUBLANES, c), F32)],
        compiler_params=_params("arbitrary", "arbitrary"),
        name="rg_seq",
    )(proj, proj, conv0, h0.reshape(bn, 1, c), conv_w, vec(conv_b), wa, wi, vec(b_a), vec(b_i), vec(lam))
    return y, hout.reshape(bn, c)


def _rg_step_kernel(x_ref, gy_ref, conv0_ref, h0_ref, cw_ref, cb_ref, wa_ref, wi_ref, ba_ref, bi_ref, lam_ref,
                    y_ref, hout_ref, a_ref, b_ref, *, seq, bn):
    nconv = conv0_ref.shape[0]

    def xpad(i):
        if i < nconv:
            return conv0_ref[i]
        return x_ref[(i - nconv) * bn:(i - nconv + 1) * bn, :]

    for t in range(seq):
        acc = cw_ref[0:1, :] * xpad(t)
        for k in range(1, nconv + 1):
            acc = acc + cw_ref[k:k + 1, :] * xpad(t + k)
        xc = cb_ref[...] + acc
        _rg_gates(xc, wa_ref, wi_ref, ba_ref, bi_ref, lam_ref, a_ref, b_ref, slice(t * bn, (t + 1) * bn))
    h = h0_ref[...]
    for t in range(seq):
        rows = slice(t * bn, (t + 1) * bn)
        h = a_ref[rows, :] * h + b_ref[rows, :]
        y_ref[rows, :] = h * gy_ref[rows, :]
    hout_ref[...] = h


def _rg_step(proj, conv0_tm, h0, conv_w, conv_b, wa, wi, b_a, b_i, lam, l, *, bn, seq):
    c = h0.shape[-1]
    t = bn * seq
    depth = conv_w.shape[0]
    vec = lambda a: a.reshape(depth, 1, c)
    lspec = pl.BlockSpec((None, 1, c), lambda i: (l, 0, 0))
    return pl.pallas_call(
        functools.partial(_rg_step_kernel, seq=seq, bn=bn),
        grid=(1,),
        in_specs=[pl.BlockSpec((t, c), lambda i: (0, 1)),
                  pl.BlockSpec((t, c), lambda i: (0, 2)),
                  pl.BlockSpec(conv0_tm.shape, lambda i: (0, 0, 0)),
                  pl.BlockSpec((bn, c), lambda i: (0, 0)),
                  pl.BlockSpec((None, CONV_W, c), lambda i: (l, 0, 0)),
                  lspec, _const_spec(wa), _const_spec(wi), lspec, lspec, lspec],
        out_specs=[pl.BlockSpec((t, c), lambda i: (0, 0)), pl.BlockSpec((bn, c), lambda i: (0, 0))],
        out_shape=[jax.ShapeDtypeStruct((t, c), F32), jax.ShapeDtypeStruct((bn, c), F32)],
        scratch_shapes=[pltpu.VMEM((t, c), F32), pltpu.VMEM((t, c), F32)],
        compiler_params=_params("arbitrary"),
        name="rg_step",
    )(proj, proj, conv0_tm, h0, conv_w, vec(conv_b), wa, wi, vec(b_a), vec(b_i), vec(lam))


def _merge_kernel(ys_ref, yr_ref, ws_ref, wr_ref, gs_ref, gr_ref, o_ref):
    o_ref[...] = gs_ref[...] * _dot(ys_ref[...], ws_ref[...]) + gr_ref[...] * _dot(yr_ref[...], wr_ref[...])


def _branch_merge(ys, yr, w_s, w_r, gates, l, *, tm, tn):
    t, c = ys.shape
    d = w_s.shape[-1]
    off = d // tn
    return pl.pallas_call(
        _merge_kernel,
        grid=(t // tm, d // tn),
        in_specs=[pl.BlockSpec((tm, c), lambda m, n: (m, 0)),
                  pl.BlockSpec((tm, c), lambda m, n: (m, 0)),
                  pl.BlockSpec((None, c, tn), lambda m, n: (l, 0, n)),
                  pl.BlockSpec((None, c, tn), lambda m, n: (l, 0, n)),
                  pl.BlockSpec((tm, tn), lambda m, n: (m, n)),
                  pl.BlockSpec((tm, tn), lambda m, n: (m, n + off))],
        out_specs=pl.BlockSpec((tm, tn), lambda m, n: (m, n)),
        out_shape=jax.ShapeDtypeStruct((t, d), F32),
        compiler_params=_params("arbitrary", "arbitrary"),
        name="branch_merge",
    )(ys, yr, w_s, w_r, gates, gates)


def _linear_residual_kernel(a_ref, w_ref, x_ref, g_ref, o_ref, acc_ref):
    k = pl.program_id(2)

    @pl.when(k == 0)
    def _():
        acc_ref[...] = jnp.zeros_like(acc_ref)

    acc_ref[...] += _dot(a_ref[...], w_ref[...])

    @pl.when(k == pl.num_programs(2) - 1)
    def _():
        o_ref[...] = x_ref[...] + g_ref[...] * acc_ref[...]


def _linear_residual(a, w, l, x, mod, j_g, *, tm, tn, tk, tiles_per_block):
    t, kdim = a.shape
    d = w.shape[-1]
    rows = mod.shape[2]
    return pl.pallas_call(
        _linear_residual_kernel,
        grid=(t // tm, d // tn, kdim // tk),
        in_specs=[pl.BlockSpec((tm, tk), lambda m, n, k: (m, k)),
                  pl.BlockSpec((None, tk, tn), lambda m, n, k: (l, k, n)),
                  pl.BlockSpec((tm, tn), lambda m, n, k: (m, n)),
                  _mod_spec(j_g, rows, tn, tiles_per_block, tm, ncol_axis=0)],
        out_specs=pl.BlockSpec((tm, tn), lambda m, n, k: (m, n)),
        out_shape=jax.ShapeDtypeStruct((t, d), F32),
        scratch_shapes=[pltpu.VMEM((tm, tn), F32)],
        compiler_params=_params("arbitrary", "arbitrary", "arbitrary"),
        name="linear_residual",
    )(a, w, x, mod)


ROUTE_I1, ROUTE_I2, ROUTE_W1, ROUTE_W2, ROUTE_RANK1, ROUTE_RANK2 = range(6)
DMA_UNROLL = 8


def _router_kernel(x_ref, sh_ref, sc_ref, g_ref, wr_ref, hn_ref, route_ref, cnt_ref, *, n_experts):
    m = pl.program_id(0)

    @pl.when(m == 0)
    def _():
        cnt_ref[...] = jnp.zeros_like(cnt_ref)

    _norm_mod_rows(x_ref, g_ref, sc_ref, sh_ref, hn_ref)
    logits = jnp.dot(hn_ref[...], wr_ref[...], preferred_element_type=F32, precision=lax.Precision.HIGHEST)
    tm, width = logits.shape
    lane = lax.broadcasted_iota(jnp.int32, logits.shape, 1).astype(F32)
    lg = jnp.where(lane < n_experts, logits, -jnp.inf)
    v1 = jnp.max(lg, axis=-1, keepdims=True)
    i1 = jnp.min(jnp.where(lg == v1, lane, float(width)), axis=-1, keepdims=True)
    lg2 = jnp.where(lane == i1, -jnp.inf, lg)
    v2 = jnp.max(lg2, axis=-1, keepdims=True)
    i2 = jnp.min(jnp.where(lg2 == v2, lane, float(width)), axis=-1, keepdims=True)
    e2 = jnp.exp(v2 - v1)
    den = 1.0 + e2
    sel1 = lane == i1
    sel2 = lane == i2
    picked = jnp.where(sel1 | sel2, 1.0, 0.0)
    r = lax.broadcasted_iota(jnp.int32, (tm, tm), 0)
    c = lax.broadcasted_iota(jnp.int32, (tm, tm), 1)
    before = jnp.where(c < r, 1.0, 0.0)
    seen = _dot(before, picked) + cnt_ref[...]
    rank1 = jnp.sum(jnp.where(sel1, seen, 0.0), axis=-1, keepdims=True)
    rank2 = jnp.sum(jnp.where(sel2, seen, 0.0), axis=-1, keepdims=True)
    cnt_ref[...] += jnp.sum(picked, axis=0, keepdims=True)
    cols = ((ROUTE_I1, i1), (ROUTE_I2, i2), (ROUTE_W1, 1.0 / den), (ROUTE_W2, e2 / den),
            (ROUTE_RANK1, rank1), (ROUTE_RANK2, rank2))
    route = jnp.zeros_like(logits)
    for j, v in cols:
        route = jnp.where(lane == j, v, route)
    route_ref[...] = route


def _router(x, mod, j_sh, g, w_router_pad, l, *, tm, tiles_per_block, n_experts):
    t, d = x.shape
    rows = mod.shape[2]
    width = w_router_pad.shape[-1]
    return pl.pallas_call(
        functools.partial(_router_kernel, n_experts=n_experts),
        grid=(t // tm,),
        in_specs=[pl.BlockSpec((tm, d), lambda m: (m, 0)),
                  _mod_spec(j_sh, rows, d, tiles_per_block, tm),
                  _mod_spec(j_sh + 1, rows, d, tiles_per_block, tm),
                  pl.BlockSpec((None, 1, d), lambda m: (l, 0, 0)),
                  pl.BlockSpec((d, width), lambda m: (0, 0))],
        out_specs=[pl.BlockSpec((tm, d), lambda m: (m, 0)), pl.BlockSpec((tm, width), lambda m: (m, 0)),
                   pl.BlockSpec((1, width), lambda m: (0, 0))],
        out_shape=[jax.ShapeDtypeStruct((t, d), F32), jax.ShapeDtypeStruct((t, width), F32),
                   jax.ShapeDtypeStruct((1, width), F32)],
        compiler_params=_params("arbitrary"),
        name="moe_router",
    )(x, mod, mod, g.reshape(g.shape[0], 1, d), w_router_pad)


def _dispatch_kernel(tok_ref, nu_ref, hn_ref, xs_ref, sem):
    tg = xs_ref.shape[0]
    m = pl.program_id(0)

    def row_copy(tok, i):
        return pltpu.make_async_copy(hn_ref.at[pl.ds(tok, 1)], xs_ref.at[pl.ds(i, 1)], sem.at[0])

    @pl.when(m < nu_ref[0])
    def _():
        lax.fori_loop(0, tg, lambda i, c: (row_copy(tok_ref[m * tg + i], i).start(), c)[1], 0, unroll=DMA_UNROLL)
        pltpu.make_async_copy(hn_ref.at[pl.ds(0, tg)], xs_ref, sem.at[0]).wait()

    @pl.when(m >= nu_ref[0])
    def _():
        xs_ref[...] = jnp.zeros_like(xs_ref)


def _dispatch(tok_of_row, n_used, hn, *, tg):
    rows = tok_of_row.shape[0]
    d = hn.shape[1]
    return pl.pallas_call(
        _dispatch_kernel,
        grid_spec=pltpu.PrefetchScalarGridSpec(
            num_scalar_prefetch=2, grid=(rows // tg,),
            in_specs=[pl.BlockSpec(memory_space=pl.ANY)],
            out_specs=pl.BlockSpec((tg, d), lambda m, tok, nu: (m, 0)),
            scratch_shapes=[pltpu.SemaphoreType.DMA((1,))]),
        out_shape=jax.ShapeDtypeStruct((rows, d), F32),
        compiler_params=_params("arbitrary"),
        name="moe_dispatch",
    )(tok_of_row, n_used, hn)


def _grouped_up_kernel(te_ref, nu_ref, xs_ref, w1_ref, w3_ref, o_ref):
    m = pl.program_id(1)

    @pl.when(m < nu_ref[0])
    def _():
        h = xs_ref[...]
        o_ref[...] = _silu(_dot(h, w1_ref[...])) * _dot(h, w3_ref[...])

    @pl.when(m >= nu_ref[0])
    def _():
        o_ref[...] = jnp.zeros_like(o_ref)


def _grouped_up(tile_expert, n_used, xs, w1, w3, lm, *, tg, tn):
    r, d = xs.shape
    f = w1.shape[-1]
    wspec = pl.BlockSpec((None, None, d, tn), lambda n, m, te, nu: (lm, te[m], 0, n))
    return pl.pallas_call(
        _grouped_up_kernel,
        grid_spec=pltpu.PrefetchScalarGridSpec(
            num_scalar_prefetch=2, grid=(f // tn, r // tg),
            in_specs=[pl.BlockSpec((tg, d), lambda n, m, te, nu: (m, 0)), wspec, wspec],
            out_specs=pl.BlockSpec((tg, tn), lambda n, m, te, nu: (m, n))),
        out_shape=jax.ShapeDtypeStruct((r, f), F32),
        compiler_params=_params("arbitrary", "arbitrary"),
        name="moe_up",
    )(tile_expert, n_used, xs, w1, w3)


def _grouped_down_kernel(te_ref, nu_ref, a_ref, w_ref, o_ref):
    m = pl.program_id(1)

    @pl.when(m < nu_ref[0])
    def _():
        o_ref[...] = _dot(a_ref[...], w_ref[...])

    @pl.when(m >= nu_ref[0])
    def _():
        o_ref[...] = jnp.zeros_like(o_ref)


def _grouped_down(tile_expert, n_used, act, w2, lm, *, tg, tn):
    r, f = act.shape
    d = w2.shape[-1]
    return pl.pallas_call(
        _grouped_down_kernel,
        grid_spec=pltpu.PrefetchScalarGridSpec(
            num_scalar_prefetch=2, grid=(d // tn, r // tg),
            in_specs=[pl.BlockSpec((tg, f), lambda n, m, te, nu: (m, 0)),
                      pl.BlockSpec((None, None, f, tn), lambda n, m, te, nu: (lm, te[m], 0, n))],
            out_specs=pl.BlockSpec((tg, tn), lambda n, m, te, nu: (m, n))),
        out_shape=jax.ShapeDtypeStruct((r, d), F32),
        compiler_params=_params("arbitrary", "arbitrary"),
        name="moe_down",
    )(tile_expert, n_used, act, w2)


def _combine_kernel(p1_ref, p2_ref, y_ref, route_ref, x_ref, g_ref, o_ref, ya_ref, yb_ref, sem):
    tc = ya_ref.shape[0]
    base = pl.program_id(0) * tc

    def row_copy(row, dst_ref, i):
        return pltpu.make_async_copy(y_ref.at[pl.ds(row, 1)], dst_ref.at[pl.ds(i, 1)], sem.at[0])

    def body(i, c):
        row_copy(p1_ref[base + i], ya_ref, i).start()
        row_copy(p2_ref[base + i], yb_ref, i).start()
        return c

    lax.fori_loop(0, tc, body, 0, unroll=DMA_UNROLL)
    for dst_ref in (ya_ref, yb_ref):
        pltpu.make_async_copy(y_ref.at[pl.ds(0, tc)], dst_ref, sem.at[0]).wait()
    w1 = route_ref[:, ROUTE_W1:ROUTE_W1 + 1]
    w2 = route_ref[:, ROUTE_W2:ROUTE_W2 + 1]
    o_ref[...] = x_ref[...] + g_ref[...] * (w1 * ya_ref[...] + w2 * yb_ref[...])


def _combine(pos1, pos2, y, route, x, mod, j_g, *, tc, tiles_per_block):
    t, d = x.shape
    rows = mod.shape[2]
    width = route.shape[-1]
    return pl.pallas_call(
        _combine_kernel,
        grid_spec=pltpu.PrefetchScalarGridSpec(
            num_scalar_prefetch=2, grid=(t // tc,),
            in_specs=[pl.BlockSpec(memory_space=pl.ANY),
                      pl.BlockSpec((tc, width), lambda m, p1, p2: (m, 0)),
                      pl.BlockSpec((tc, d), lambda m, p1, p2: (m, 0)),
                      _mod_spec(j_g, rows, d, tiles_per_block, tc)],
            out_specs=pl.BlockSpec((tc, d), lambda m, p1, p2: (m, 0)),
            scratch_shapes=[pltpu.VMEM((tc, d), F32), pltpu.VMEM((tc, d), F32), pltpu.SemaphoreType.DMA((1,))]),
        out_shape=jax.ShapeDtypeStruct((t, d), F32),
        compiler_params=_params("arbitrary"),
        name="moe_combine",
    )(pos1, pos2, y, route, x, mod)


def _routed_moe(x, mod, g, w_router, w1, w3, w2, l, lm, *, tm, tg, tiles_per_tile):
    t, d = x.shape
    ne = w_router.shape[-1]
    w_r = jnp.pad(w_router[lm], ((0, 0), (0, V7X_LANES - ne)))
    hn, route, cnt = _router(x, mod, 3, g, w_r, l, n_experts=ne, **tiles_per_tile(tm))
    counts = cnt[0, :ne].astype(jnp.int32)
    tiles_e = (counts + (tg - 1)) // tg
    tile_end = jnp.cumsum(tiles_e)
    row_start = (tile_end - tiles_e) * tg
    col = lambda j: route[:, j].astype(jnp.int32)
    pos1 = row_start[col(ROUTE_I1)] + col(ROUTE_RANK1)
    pos2 = row_start[col(ROUTE_I2)] + col(ROUTE_RANK2)
    n_tiles = -(-TOP_K * t // tg) + ne
    tile_expert = jnp.minimum(jnp.sum(jnp.arange(n_tiles)[:, None] >= tile_end[None, :], axis=1), ne - 1)
    tile_expert = tile_expert.astype(jnp.int32)
    n_used = tile_end[-1:].astype(jnp.int32)
    tok = jnp.arange(t, dtype=jnp.int32)
    tok_of_row = jnp.zeros((n_tiles * tg,), jnp.int32).at[jnp.concatenate([pos1, pos2])].set(
        jnp.concatenate([tok, tok]), unique_indices=True)
    xs = _dispatch(tok_of_row, n_used, hn, tg=tg)
    act = _grouped_up(tile_expert, n_used, xs, w1, w3, lm, tg=tg, tn=1024)
    y = _grouped_down(tile_expert, n_used, act, w2, lm, tg=tg, tn=1024)
    tc = min(tm, 256)
    return _combine(pos1, pos2, y, route, x, mod, 5, tc=tc, tiles_per_block=tiles_per_tile(tc)["tiles_per_block"])


def _final_norm_kernel(x_ref, g_ref, o_ref):
    o_ref[...] = _rmsnorm(x_ref[...], g_ref[...])


def _final_norm(x, g, *, tm):
    t, d = x.shape
    return pl.pallas_call(
        _final_norm_kernel,
        grid=(t // tm,),
        in_specs=[pl.BlockSpec((tm, d), lambda m: (m, 0)), pl.BlockSpec((1, d), lambda m: (0, 0))],
        out_specs=pl.BlockSpec((tm, d), lambda m: (m, 0)),
        out_shape=jax.ShapeDtypeStruct((t, d), F32),
        compiler_params=_params("arbitrary"),
        name="final_norm",
    )(x, g.reshape(1, d))


def _run(x, mods, state, p, *, bn, seq, time_major, tm):
    t, d = x.shape
    depth = p["w_in"].shape[0]
    c = p["s5_w_glu"].shape[-1]
    def tiled(tile):
        return dict(tm=tile, tiles_per_block=1 if time_major else seq // tile)

    kw = tiled(tm)
    nconv = CONV_W - 1
    assert seq >= nconv
    s5_re, s5_im, rg_h, conv = [], [], [], []
    for l in range(depth):
        mod = mods[l]
        tb = p["s5_tables"][l]
        wa, wi = p["rg_tables"][l]
        proj = _normed_linear(x, mod, 0, p["norm_mix"], l, [p["w_in"]], l, mode="in", tn=512,
                              plain_cols=2 * c, **kw)
        gates = _normed_linear(x, mod, 0, p["norm_mix"], l, [p["w_gate"]], l, mode="gate", tn=512,
                               bias=p["b_gate"], **kw)
        h0re = state["s5_re"][l].reshape(bn, -1)
        h0im = state["s5_im"][l].reshape(bn, -1)
        rg_args = (p["rg_conv_w"], p["rg_conv_b"], wa, wi, p["rg_b_a"].reshape(depth, c), p["rg_b_i"].reshape(depth, c),
                   p["rg_lam"], l)
        if time_major:
            y_s5, hre, him = _s5_step(proj, h0re, h0im, tb, p["s5_d"].reshape(depth, c), p["s5_w_glu"], p["s5_b_glu"], l,
                                      bn=bn, seq=seq)
            y_rg, hrg = _rg_step(proj, state["conv"][l].transpose(1, 0, 2), state["rg"][l], *rg_args, bn=bn, seq=seq)
            conv.append(proj[:, c:2 * c].reshape(seq, bn, c)[seq - nconv:].transpose(1, 0, 2))
        else:
            y_s5, hre, him = _s5_seq(proj, h0re, h0im, tb, p["s5_d"].reshape(depth, c), p["s5_w_glu"], p["s5_b_glu"], l,
                                     bn=bn, seq=seq)
            y_rg, hrg = _rg_seq(proj, state["conv"][l], state["rg"][l], *rg_args, bn=bn, seq=seq)
            tail_rows = [seq - SEG_TILE + (SEG_LEN - nconv + f) * V7X_SUBLANES + V7X_SUBLANES - 1 for f in range(nconv)]
            proj3 = proj.reshape(bn, seq, 3 * c)
            conv.append(jnp.stack([proj3[:, r, c:2 * c] for r in tail_rows], axis=1))
        s5_re.append(hre.reshape(state["s5_re"][l].shape))
        s5_im.append(him.reshape(state["s5_im"][l].shape))
        rg_h.append(hrg)
        merged = _branch_merge(y_s5, y_rg, p["w_br_s5"], p["w_br_rg"], gates, l, tm=tm, tn=512)
        x = _linear_residual(merged, p["w_out"], l, x, mod, 2, tn=1024, tk=1024, **kw)
        if l % 2 == 0:
            act = _normed_linear(x, mod, 3, p["norm_ffn"], l, [p["ffn_w1"], p["ffn_w3"]], l // 2, mode="glu",
                                 tn=512, **kw)
            x = _linear_residual(act, p["ffn_w2"], l // 2, x, mod, 5, tn=1024, tk=1024, **kw)
        else:
            x = _routed_moe(x, mod, p["norm_ffn"], p["moe_router"], p["moe_w1"], p["moe_w3"], p["moe_w2"], l, l // 2,
                            tm=min(tm, 512), tg=512 if t >= 4096 else 128, tiles_per_tile=tiled)
    y = _final_norm(x, p["norm_f"], tm=min(tm, 256))
    return y, jnp.stack(s5_re), jnp.stack(s5_im), jnp.stack(rg_h), jnp.stack(conv)


def kernel(x_prompt, x_sample, c_prompt, c_sample, state_s5_re, state_s5_im, state_rglru, state_conv, norm_mix, norm_ffn, norm_f, w_ada, b_ada, w_in, s5_lam_re, s5_lam_im, s5_log_dt, s5_b_re, s5_b_im, s5_c_re, s5_c_im, s5_d, s5_w_glu, s5_b_glu, rg_conv_w, rg_conv_b, rg_w_a, rg_b_a, rg_w_i, rg_b_i, rg_lam, w_gate, b_gate, w_br_s5, w_br_rg, w_out, ffn_w1, ffn_w3, ffn_w2, moe_router, moe_w1, moe_w3, moe_w2):
    p = dict(norm_mix=norm_mix, norm_ffn=norm_ffn, norm_f=norm_f, w_in=w_in,
             s5_lam_re=s5_lam_re, s5_lam_im=s5_lam_im, s5_log_dt=s5_log_dt, s5_b_re=s5_b_re, s5_b_im=s5_b_im,
             s5_c_re=s5_c_re, s5_c_im=s5_c_im, s5_d=s5_d, s5_w_glu=s5_w_glu, s5_b_glu=s5_b_glu,
             rg_conv_w=rg_conv_w, rg_conv_b=rg_conv_b, rg_w_a=rg_w_a, rg_b_a=rg_b_a, rg_w_i=rg_w_i, rg_b_i=rg_b_i,
             rg_lam=rg_lam, w_gate=w_gate, b_gate=b_gate, w_br_s5=w_br_s5, w_br_rg=w_br_rg, w_out=w_out,
             ffn_w1=ffn_w1, ffn_w3=ffn_w3, ffn_w2=ffn_w2, moe_router=moe_router, moe_w1=moe_w1, moe_w3=moe_w3,
             moe_w2=moe_w2)
    depth = w_in.shape[0]
    p["s5_tables"] = [_s5_tables(s5_lam_re[l], s5_lam_im[l], s5_log_dt[l], s5_b_re[l], s5_b_im[l], s5_c_re[l],
                                 s5_c_im[l]) for l in range(depth)]
    p["rg_tables"] = [_rg_tables(rg_w_a[l], rg_w_i[l]) for l in range(depth)]
    bp, sp, d = x_prompt.shape
    bs, ss, _ = x_sample.shape
    g, pst = state_s5_re.shape[2:]
    c = state_rglru.shape[-1]

    n_c = bp + bs
    rows_c = -(-n_c // V7X_SUBLANES) * V7X_SUBLANES
    c_all = jnp.pad(jnp.concatenate([c_prompt, c_sample], axis=0), ((0, rows_c - n_c), (0, 0)))
    mod_all = _ada_mod(c_all, w_ada, b_ada)
    mods_p, mods_s = [], []
    for l in range(depth):
        mp = mod_all[l, :bp].reshape(bp, 6, 1, d).transpose(1, 0, 2, 3)
        ms = mod_all[l, bp:n_c].reshape(bs, 6, d).transpose(1, 0, 2)
        ms = jnp.tile(ms, (1, ss, 1)).reshape(6, 1, ss * bs, d)
        mods_p.append(mp)
        mods_s.append(ms)

    zero_state = dict(s5_re=jnp.zeros((depth, bp, g, pst), F32), s5_im=jnp.zeros((depth, bp, g, pst), F32),
                      rg=jnp.zeros((depth, bp, c), F32), conv=jnp.zeros((depth, bp, CONV_W - 1, c), x_prompt.dtype))
    seg_shape = (bp, sp // SEG_TILE, V7X_SUBLANES, SEG_LEN, d)
    x_p = x_prompt.reshape(seg_shape).transpose(0, 1, 3, 2, 4).reshape(bp * sp, d)
    y_p, p_re, p_im, p_rg, p_conv = _run(x_p, mods_p, zero_state, p,
                                         bn=bp, seq=sp, time_major=False, tm=min(PROMPT_TILE, sp))
    y_p = y_p.reshape(bp, sp // SEG_TILE, SEG_LEN, V7X_SUBLANES, d).transpose(0, 1, 3, 2, 4)
    sample_state = dict(s5_re=state_s5_re, s5_im=state_s5_im, rg=state_rglru, conv=state_conv)
    x_s = x_sample.transpose(1, 0, 2).reshape(ss * bs, d)
    y_s, s_re, s_im, s_rg, s_conv = _run(x_s, mods_s, sample_state, p, bn=bs, seq=ss, time_major=True, tm=ss * bs)
    y_s = y_s.reshape(ss, bs, d).transpose(1, 0, 2)
    return (y_p.reshape(bp, sp, d), y_s, p_re, p_im, p_rg, p_conv, s_re, s_im, s_rg, s_conv)
```

```python
import functools
import math

import jax
import jax.numpy as jnp
from jax import lax
from jax.experimental import pallas as pl
from jax.experimental.pallas import tpu as pltpu

F32 = jnp.float32
EPS = 1e-6
RG_C = 8.0
CONV_W = 4
S5_GROUP = 16
S5_STATE = 64
RG_HEAD_DIM = 64
TOP_K = 2

V7X_LANES = 128
V7X_SUBLANES = 8
V7X_MXU_DIM = 256
VMEM_LIMIT_BYTES = 56 * 1024 * 1024

GELU_C = math.sqrt(2.0 / math.pi)


def _gelu(x):
    return 0.5 * x * (1.0 + jnp.tanh(GELU_C * (x + 0.044715 * (x * x * x))))


def _sigmoid(x):
    return 1.0 / (1.0 + jnp.exp(-x))


def _silu(x):
    return x * _sigmoid(x)


def _dot(a, b):
    return jnp.dot(a, b, preferred_element_type=F32)


def _params(*sem):
    return pltpu.CompilerParams(dimension_semantics=sem, vmem_limit_bytes=VMEM_LIMIT_BYTES)


def _mod_spec(j, rows, width, tiles_per_block, tile, ncol_axis=None):
    col = (lambda idx: 0) if ncol_axis is None else (lambda idx: idx[ncol_axis])
    if rows == 1:
        return pl.BlockSpec((None, None, 1, width), lambda m, *idx: (j, m // tiles_per_block, 0, col(idx)))
    return pl.BlockSpec((None, None, tile, width), lambda m, *idx: (j, 0, m, col(idx)))


ROW_CHUNK = 128
PROMPT_TILE = 1024
SEG_LEN = 32
SEG_TILE = SEG_LEN * V7X_SUBLANES


def _rmsnorm(x, g):
    ms = jnp.mean(x * x, axis=-1, keepdims=True)
    return (x * lax.rsqrt(ms + EPS)) * g


def _norm_mod_rows(x_ref, g_ref, sc_ref, sh_ref, o_ref):
    tm = x_ref.shape[0]
    chunk = min(ROW_CHUNK, tm)
    per_row = sc_ref.shape[0] != 1

    def body(i, carry):
        rows = pl.ds(pl.multiple_of(i * chunk, chunk), chunk)
        sc = sc_ref[rows, :] if per_row else sc_ref[...]
        sh = sh_ref[rows, :] if per_row else sh_ref[...]
        o_ref[rows, :] = _rmsnorm(x_ref[rows, :], g_ref[...]) * (1.0 + sc) + sh
        return carry

    lax.fori_loop(0, tm // chunk, body, 0)


def _ada_kernel(c_ref, w_ref, b_ref, o_ref):
    o_ref[...] = _dot(_silu(c_ref[...]), w_ref[...]) + b_ref[...]


def _ada_mod(c_all, w_ada, b_ada, *, tn=1024):
    depth, d, n6 = w_ada.shape
    rows = c_all.shape[0]
    return pl.pallas_call(
        _ada_kernel,
        grid=(depth, n6 // tn),
        in_specs=[pl.BlockSpec((rows, d), lambda l, n: (0, 0)),
                  pl.BlockSpec((None, d, tn), lambda l, n: (l, 0, n)),
                  pl.BlockSpec((None, 1, tn), lambda l, n: (l, 0, n))],
        out_specs=pl.BlockSpec((None, rows, tn), lambda l, n: (l, 0, n)),
        out_shape=jax.ShapeDtypeStruct((depth, rows, n6), F32),
        compiler_params=_params("arbitrary", "arbitrary"),
        name="ada_mod",
    )(c_all, w_ada, b_ada.reshape(depth, 1, n6))


def _normed_linear_kernel(*refs, mode, n_plain):
    if mode == "glu":
        x_ref, sh_ref, sc_ref, g_ref, w1_ref, w3_ref, o_ref, hn_ref = refs
    else:
        x_ref, sh_ref, sc_ref, g_ref, w_ref, o_ref, hn_ref = refs
    n = pl.program_id(1)

    @pl.when(n == 0)
    def _():
        _norm_mod_rows(x_ref, g_ref, sc_ref, sh_ref, hn_ref)

    h = hn_ref[...]
    if mode == "glu":
        o_ref[...] = _silu(_dot(h, w1_ref[...])) * _dot(h, w3_ref[...])
    else:
        z = _dot(h, w_ref[...])

        @pl.when(n < n_plain)
        def _():
            o_ref[...] = z

        @pl.when(n >= n_plain)
        def _():
            o_ref[...] = _gelu(z)


def _normed_linear(x, mod, j_sh, g, l_g, ws, l, *, mode, tm, tn, tiles_per_block, plain_cols=0):
    t, d = x.shape
    n_out = ws[0].shape[-1]
    rows = mod.shape[2]
    w_spec = pl.BlockSpec((None, d, tn), lambda m, n: (l, 0, n))
    in_specs = [pl.BlockSpec((tm, d), lambda m, n: (m, 0)),
                _mod_spec(j_sh, rows, d, tiles_per_block, tm),
                _mod_spec(j_sh + 1, rows, d, tiles_per_block, tm),
                pl.BlockSpec((None, 1, d), lambda m, n: (l_g, 0, 0))]
    args = [x, mod, mod, g.reshape(g.shape[0], 1, d)]
    for w in ws:
        in_specs.append(w_spec)
        args.append(w)
    return pl.pallas_call(
        functools.partial(_normed_linear_kernel, mode=mode, n_plain=plain_cols // tn),
        grid=(t // tm, n_out // tn),
        in_specs=in_specs,
        out_specs=pl.BlockSpec((tm, tn), lambda m, n: (m, n)),
        out_shape=jax.ShapeDtypeStruct((t, n_out), F32),
        scratch_shapes=[pltpu.VMEM((tm, d), F32)],
        compiler_params=_params("arbitrary", "arbitrary"),
        name="normed_linear_" + mode,
    )(*args)


def _s5_tables(lam_re, lam_im, log_dt, b_re, b_im, c_re, c_im):
    g, p = lam_re.shape
    h = b_re.shape[-1]
    gpc = V7X_MXU_DIM // h
    nck = g // gpc
    dt = jnp.exp(log_dt)[:, None]
    mag = jnp.exp(lam_re * dt)
    lb_re = mag * jnp.cos(lam_im * dt)
    lb_im = mag * jnp.sin(lam_im * dt)
    den = lam_re * lam_re + lam_im * lam_im
    cf_re = ((lb_re - 1.0) * lam_re + lb_im * lam_im) / den
    cf_im = (lb_im * lam_re - (lb_re - 1.0) * lam_im) / den
    bb_re = cf_re[:, :, None] * b_re - cf_im[:, :, None] * b_im
    bb_im = cf_re[:, :, None] * b_im + cf_im[:, :, None] * b_re
    by_ghp = lambda m: m.transpose(0, 2, 1).reshape(nck, gpc * h, p)
    by_gph = lambda m: m.transpose(0, 2, 1).reshape(nck, gpc * p, h)

    def powers(steps):
        k = jnp.asarray(steps, F32)[:, None]
        zr = (lam_re * dt).reshape(1, g * p)
        zi = (lam_im * dt).reshape(1, g * p)
        return jnp.exp(k * zr) * jnp.cos(k * zi), jnp.exp(k * zr) * jnp.sin(k * zi)

    sub = V7X_SUBLANES
    pw_re, pw_im = powers(range(1, SEG_LEN + 1))
    sg_re, sg_im = powers(range(SEG_LEN, SEG_LEN * (sub + 1), SEG_LEN))
    row = jnp.arange(sub)[:, None]
    segtabs = []
    for k in (1, 2, 4):
        segtabs += [jnp.where(row >= k, sg_re[k - 1:k], 0.0), jnp.where(row >= k, sg_im[k - 1:k], 0.0)]
    segtabs += [sg_re, sg_im]
    return dict(wb_re=_block_diag(by_ghp(bb_re), gpc), wb_im=_block_diag(by_ghp(bb_im), gpc),
                wc_re=_block_diag(by_gph(c_re), gpc), wc_imneg=_block_diag(by_gph(-c_im), gpc),
                lam8=jnp.stack([jnp.broadcast_to(pw_re[0:1], (sub, g * p)),
                                jnp.broadcast_to(pw_im[0:1], (sub, g * p))]),
                segtabs=jnp.stack(segtabs),
                fix=jnp.stack([jnp.repeat(pw_re, sub, axis=0), jnp.repeat(pw_im, sub, axis=0)]),
                lb_re=pw_re[0:1], lb_im=pw_im[0:1])


def _block_diag(src, nblk):
    _, rows, cb = src.shape
    r = jnp.arange(rows)[:, None] // (rows // nblk)
    c = jnp.arange(nblk * cb)[None, :] // cb
    return jnp.where(r == c, jnp.tile(src, (1, 1, nblk)), 0.0)


def _s5_input_proj(u, wbre_ref, wbim_ref, bre_ref, bim_ref):
    nck, kc, nc = wbre_ref.shape
    for k in range(nck):
        uk = u[:, k * kc:(k + 1) * kc]
        bre_ref[:, k * nc:(k + 1) * nc] = _dot(uk, wbre_ref[k])
        bim_ref[:, k * nc:(k + 1) * nc] = _dot(uk, wbim_ref[k])


def _s5_output(u, bre_ref, bim_ref, wcre_ref, wcim_ref, d_ref, wglu_ref, bglu_ref, y_ref):
    nck, nc, kc = wcre_ref.shape
    for k in range(nck):
        yk = _dot(bre_ref[:, k * nc:(k + 1) * nc], wcre_ref[k]) + _dot(bim_ref[:, k * nc:(k + 1) * nc], wcim_ref[k])
        yk = yk + d_ref[:, k * kc:(k + 1) * kc] * u[:, k * kc:(k + 1) * kc]
        y_ref[:, k * kc:(k + 1) * kc] = _gelu(yk)
    yg = y_ref[...]
    y_ref[...] = yg * _sigmoid(_dot(yg, wglu_ref[...]) + bglu_ref[...])


def _row_block(j):
    return pl.ds(pl.multiple_of(j * V7X_SUBLANES, V7X_SUBLANES), V7X_SUBLANES)


def _s5_seq_kernel(u_ref, h0re_ref, h0im_ref, wbre_ref, wbim_ref, wcre_ref, wcim_ref, lam8_ref, seg_ref, fix_ref,
                   d_ref, wglu_ref, bglu_ref, y_ref, hre_ref, him_ref, bre_ref, bim_ref, cre_ref, cim_ref,
                   *, lane_chunk):
    s = pl.program_id(1)
    ns = cre_ref.shape[1]

    @pl.when(s == 0)
    def _():
        cre_ref[...] = jnp.broadcast_to(h0re_ref[...], cre_ref.shape)
        cim_ref[...] = jnp.broadcast_to(h0im_ref[...], cim_ref.shape)

    u = u_ref[...]
    _s5_input_proj(u, wbre_ref, wbim_ref, bre_ref, bim_ref)
    first_sublane = lax.broadcasted_iota(jnp.int32, (V7X_SUBLANES, lane_chunk), 0) == 0
    last = V7X_SUBLANES - 1

    for c in range(ns // lane_chunk):
        sl = slice(c * lane_chunk, (c + 1) * lane_chunk)
        lr, li = lam8_ref[0, :, sl], lam8_ref[1, :, sl]

        def local(j, st, sl=sl, lr=lr, li=li):
            sr, si = st
            rows = _row_block(j)
            sr, si = (lr * sr - li * si) + bre_ref[rows, sl], (lr * si + li * sr) + bim_ref[rows, sl]
            bre_ref[rows, sl] = sr
            bim_ref[rows, sl] = si
            return sr, si

        zero = jnp.zeros((V7X_SUBLANES, lane_chunk), F32)
        xr, xi = lax.fori_loop(0, SEG_LEN, local, (zero, zero), unroll=2)
        for i, k in enumerate((1, 2, 4)):
            tr, ti = seg_ref[2 * i, :, sl], seg_ref[2 * i + 1, :, sl]
            pr, pi = pltpu.roll(xr, k, 0), pltpu.roll(xi, k, 0)
            xr, xi = xr + (tr * pr - ti * pi), xi + (tr * pi + ti * pr)
        cr, ci = cre_ref[:, sl], cim_ref[:, sl]
        tr, ti = seg_ref[6, :, sl], seg_ref[7, :, sl]
        gr = xr + (tr * cr - ti * ci)
        gi = xi + (tr * ci + ti * cr)
        h0r = jnp.where(first_sublane, cr, pltpu.roll(gr, 1, 0))
        h0i = jnp.where(first_sublane, ci, pltpu.roll(gi, 1, 0))
        cre_ref[:, sl] = jnp.broadcast_to(gr[last:last + 1], gr.shape)
        cim_ref[:, sl] = jnp.broadcast_to(gi[last:last + 1], gi.shape)

        def fixup(j, carry, sl=sl, h0r=h0r, h0i=h0i):
            rows = _row_block(j)
            fr, fi = fix_ref[0, rows, sl], fix_ref[1, rows, sl]
            bre_ref[rows, sl] += fr * h0r - fi * h0i
            bim_ref[rows, sl] += fr * h0i + fi * h0r
            return carry

        lax.fori_loop(0, SEG_LEN, fixup, 0, unroll=2)

    _s5_output(u, bre_ref, bim_ref, wcre_ref, wcim_ref, d_ref, wglu_ref, bglu_ref, y_ref)

    @pl.when(s == pl.num_programs(1) - 1)
    def _():
        hre_ref[...] = cre_ref[0:1, :]
        him_ref[...] = cim_ref[0:1, :]


def _const_spec(a):
    nd = a.ndim
    return pl.BlockSpec(a.shape, lambda *_: (0,) * nd, pipeline_mode=pl.Buffered(1))


def _s5_seq(proj, h0re, h0im, tb, d_skip, w_glu, b_glu, l, *, bn, seq, lane_chunk=512):
    c = w_glu.shape[-1]
    ns = h0re.shape[-1]
    tm = SEG_TILE
    n_s = seq // tm
    consts = [tb["wb_re"], tb["wb_im"], tb["wc_re"], tb["wc_imneg"], tb["lam8"], tb["segtabs"], tb["fix"]]
    in_specs = ([pl.BlockSpec((tm, c), lambda b, s: (b * n_s + s, 0)),
                 pl.BlockSpec((None, 1, ns), lambda b, s: (b, 0, 0)),
                 pl.BlockSpec((None, 1, ns), lambda b, s: (b, 0, 0))]
                + [_const_spec(a) for a in consts]
                + [pl.BlockSpec((None, 1, c), lambda b, s: (l, 0, 0)),
                   pl.BlockSpec((None, c, c), lambda b, s: (l, 0, 0), pipeline_mode=pl.Buffered(1)),
                   pl.BlockSpec((None, 1, c), lambda b, s: (l, 0, 0))])
    y, hre, him = pl.pallas_call(
        functools.partial(_s5_seq_kernel, lane_chunk=lane_chunk),
        grid=(bn, n_s),
        in_specs=in_specs,
        out_specs=[pl.BlockSpec((tm, c), lambda b, s: (b * n_s + s, 0)),
                   pl.BlockSpec((None, 1, ns), lambda b, s: (b, 0, 0)),
                   pl.BlockSpec((None, 1, ns), lambda b, s: (b, 0, 0))],
        out_shape=[jax.ShapeDtypeStruct((bn * seq, c), F32),
                   jax.ShapeDtypeStruct((bn, 1, ns), F32),
                   jax.ShapeDtypeStruct((bn, 1, ns), F32)],
        scratch_shapes=[pltpu.VMEM((tm, ns), F32), pltpu.VMEM((tm, ns), F32),
                        pltpu.VMEM((V7X_SUBLANES, ns), F32), pltpu.VMEM((V7X_SUBLANES, ns), F32)],
        compiler_params=_params("arbitrary", "arbitrary"),
        name="s5_seq",
    )(proj, h0re.reshape(bn, 1, ns), h0im.reshape(bn, 1, ns), *consts,
      d_skip.reshape(d_skip.shape[0], 1, c), w_glu, b_glu.reshape(b_glu.shape[0], 1, c))
    return y, hre.reshape(bn, ns), him.reshape(bn, ns)


def _s5_step_kernel(u_ref, h0re_ref, h0im_ref, wbre_ref, wbim_ref, wcre_ref, wcim_ref, lbre_ref, lbim_ref, d_ref,
                    wglu_ref, bglu_ref, y_ref, hre_ref, him_ref, bre_ref, bim_ref, *, seq, bn, lane_chunk):
    u = u_ref[...]
    _s5_input_proj(u, wbre_ref, wbim_ref, bre_ref, bim_ref)
    ns = bre_ref.shape[1]
    for c in range(ns // lane_chunk):
        sl = slice(c * lane_chunk, (c + 1) * lane_chunk)
        lr = lbre_ref[:, sl]
        li = lbim_ref[:, sl]
        hr = h0re_ref[:, sl]
        hi = h0im_ref[:, sl]
        for t in range(seq):
            rows = slice(t * bn, (t + 1) * bn)
            hr, hi = (lr * hr - li * hi) + bre_ref[rows, sl], (lr * hi + li * hr) + bim_ref[rows, sl]
            bre_ref[rows, sl] = hr
            bim_ref[rows, sl] = hi
        hre_ref[:, sl] = hr
        him_ref[:, sl] = hi
    _s5_output(u, bre_ref, bim_ref, wcre_ref, wcim_ref, d_ref, wglu_ref, bglu_ref, y_ref)


def _s5_step(proj, h0re, h0im, tb, d_skip, w_glu, b_glu, l, *, bn, seq, lane_chunk=128):
    c = w_glu.shape[-1]
    ns = h0re.shape[-1]
    t = bn * seq
    consts = [tb["wb_re"], tb["wb_im"], tb["wc_re"], tb["wc_imneg"], tb["lb_re"], tb["lb_im"]]
    in_specs = ([pl.BlockSpec((t, c), lambda i: (0, 0)),
                 pl.BlockSpec((bn, ns), lambda i: (0, 0)),
                 pl.BlockSpec((bn, ns), lambda i: (0, 0))]
                + [_const_spec(a) for a in consts]
                + [pl.BlockSpec((None, 1, c), lambda i: (l, 0, 0)),
                   pl.BlockSpec((None, c, c), lambda i: (l, 0, 0), pipeline_mode=pl.Buffered(1)),
                   pl.BlockSpec((None, 1, c), lambda i: (l, 0, 0))])
    return pl.pallas_call(
        functools.partial(_s5_step_kernel, seq=seq, bn=bn, lane_chunk=lane_chunk),
        grid=(1,),
        in_specs=in_specs,
        out_specs=[pl.BlockSpec((t, c), lambda i: (0, 0)),
                   pl.BlockSpec((bn, ns), lambda i: (0, 0)),
                   pl.BlockSpec((bn, ns), lambda i: (0, 0))],
        out_shape=[jax.ShapeDtypeStruct((t, c), F32),
                   jax.ShapeDtypeStruct((bn, ns), F32),
                   jax.ShapeDtypeStruct((bn, ns), F32)],
        scratch_shapes=[pltpu.VMEM((t, ns), F32), pltpu.VMEM((t, ns), F32)],
        compiler_params=_params("arbitrary"),
        name="s5_step",
    )(proj, h0re, h0im, *consts,
      d_skip.reshape(d_skip.shape[0], 1, c), w_glu, b_glu.reshape(b_glu.shape[0], 1, c))


def _rg_tables(w_a, w_i):
    nh, hd, _ = w_a.shape
    hpc = V7X_MXU_DIM // hd
    expand = lambda w: _block_diag(w.reshape(nh // hpc, hpc * hd, hd), hpc)
    return expand(w_a), expand(w_i)


def _rg_gates(xc, wa_ref, wi_ref, ba_ref, bi_ref, lam_ref, a_ref, b_ref, rows):
    nck, kc, _ = wa_ref.shape
    lam = lam_ref[...]
    neg = -lam
    softplus = jnp.maximum(neg, 0.0) + jnp.log(1.0 + jnp.exp(-jnp.abs(neg)))
    for k in range(nck):
        cs = slice(k * kc, (k + 1) * kc)
        xk = xc[:, cs]
        r = _sigmoid(_dot(xk, wa_ref[k]) + ba_ref[:, cs])
        i = _sigmoid(_dot(xk, wi_ref[k]) + bi_ref[:, cs])
        log_a = (-RG_C * r) * softplus[:, cs]
        a_ref[rows, cs] = jnp.exp(log_a)
        th = jnp.tanh(log_a)
        b_ref[rows, cs] = jnp.sqrt(-2.0 * th / (1.0 - th)) * (i * xk)


def _rg_seq_kernel(x_ref, gy_ref, conv0_ref, h0_ref, cw_ref, cb_ref, wa_ref, wi_ref, ba_ref, bi_ref, lam_ref,
                   y_ref, hout_ref, xe_ref, a_ref, b_ref, c_ref, tail_ref):
    s = pl.program_id(1)
    sub = V7X_SUBLANES
    nconv = conv0_ref.shape[0]
    tm, c = x_ref.shape
    first_sublane = lax.broadcasted_iota(jnp.int32, (sub, c), 0) == 0
    last = sub - 1

    @pl.when(s == 0)
    def _():
        tail_ref[0:nconv, :] = conv0_ref[...]
        c_ref[...] = jnp.broadcast_to(h0_ref[...], c_ref.shape)

    for f in range(nconv):
        blk = x_ref[(SEG_LEN - nconv + f) * sub:(SEG_LEN - nconv + f + 1) * sub, :]
        xe_ref[f * sub:(f + 1) * sub, :] = jnp.where(first_sublane, tail_ref[f:f + 1, :], pltpu.roll(blk, 1, 0))
    for f in range(nconv):
        tail_ref[f:f + 1, :] = x_ref[(SEG_LEN - nconv + f) * sub + last:(SEG_LEN - nconv + f + 1) * sub, :]
    xe_ref[nconv * sub:nconv * sub + tm, :] = x_ref[...]
    acc = cw_ref[0:1, :] * xe_ref[0:tm, :]
    for k in range(1, nconv + 1):
        acc = acc + cw_ref[k:k + 1, :] * xe_ref[k * sub:k * sub + tm, :]
    xc = cb_ref[...] + acc
    _rg_gates(xc, wa_ref, wi_ref, ba_ref, bi_ref, lam_ref, a_ref, b_ref, slice(None))

    def local(j, st):
        h, p = st
        rows = _row_block(j)
        a = a_ref[rows, :]
        h = a * h + b_ref[rows, :]
        p = p * a
        b_ref[rows, :] = h
        a_ref[rows, :] = p
        return h, p

    e, p = lax.fori_loop(0, SEG_LEN, local, (jnp.zeros((sub, c), F32), jnp.ones((sub, c), F32)), unroll=2)
    row = lax.broadcasted_iota(jnp.int32, (sub, c), 0)
    for k in (1, 2, 4):
        keep = row >= k
        e = jnp.where(keep, e + p * pltpu.roll(e, k, 0), e)
        p = jnp.where(keep, p * pltpu.roll(p, k, 0), p)
    carry = c_ref[...]
    g = e + p * carry
    h_in = jnp.where(first_sublane, carry, pltpu.roll(g, 1, 0))
    c_ref[...] = jnp.broadcast_to(g[last:last + 1], g.shape)

    def fixup(j, cc):
        rows = _row_block(j)
        y_ref[rows, :] = (b_ref[rows, :] + a_ref[rows, :] * h_in) * gy_ref[rows, :]
        return cc

    lax.fori_loop(0, SEG_LEN, fixup, 0, unroll=2)

    @pl.when(s == pl.num_programs(1) - 1)
    def _():
        hout_ref[...] = c_ref[0:1, :]


def _rg_seq(proj, conv0, h0, conv_w, conv_b, wa, wi, b_a, b_i, lam, l, *, bn, seq):
    c = h0.shape[-1]
    tm = SEG_TILE
    n_s = seq // tm
    depth = conv_w.shape[0]
    vec = lambda a: a.reshape(depth, 1, c)
    y, hout = pl.pallas_call(
        _rg_seq_kernel,
        grid=(bn, n_s),
        in_specs=[pl.BlockSpec((tm, c), lambda b, s: (b * n_s + s, 1)),
                  pl.BlockSpec((tm, c), lambda b, s: (b * n_s + s, 2)),
                  pl.BlockSpec((None, CONV_W - 1, c), lambda b, s: (b, 0, 0)),
                  pl.BlockSpec((None, 1, c), lambda b, s: (b, 0, 0)),
                  pl.BlockSpec((None, CONV_W, c), lambda b, s: (l, 0, 0)),
                  pl.BlockSpec((None, 1, c), lambda b, s: (l, 0, 0)),
                  _const_spec(wa), _const_spec(wi),
                  pl.BlockSpec((None, 1, c), lambda b, s: (l, 0, 0)),
                  pl.BlockSpec((None, 1, c), lambda b, s: (l, 0, 0)),
                  pl.BlockSpec((None, 1, c), lambda b, s: (l, 0, 0))],
        out_specs=[pl.BlockSpec((tm, c), lambda b, s: (b * n_s + s, 0)),
                   pl.BlockSpec((None, 1, c), lambda b, s: (b, 0, 0))],
        out_shape=[jax.ShapeDtypeStruct((bn * seq, c), F32), jax.ShapeDtypeStruct((bn, 1, c), F32)],
        scratch_shapes=[pltpu.VMEM((tm + (CONV_W - 1) * V7X_SUBLANES, c), F32), pltpu.VMEM((tm, c), F32),
                        pltpu.VMEM((tm, c), F32), pltpu.VMEM((V7X_SUBLANES, c), F32),
                        pltpu.VMEM((V7X_SUBLANES, c), F32)],
        compiler_params=_params("arbitrary", "arbitrary"),
        name="rg_seq",
    )(proj, proj, conv0, h0.reshape(bn, 1, c), conv_w, vec(conv_b), wa, wi, vec(b_a), vec(b_i), vec(lam))
    return y, hout.reshape(bn, c)


def _rg_step_kernel(x_ref, gy_ref, conv0_ref, h0_ref, cw_ref, cb_ref, wa_ref, wi_ref, ba_ref, bi_ref, lam_ref,
                    y_ref, hout_ref, a_ref, b_ref, *, seq, bn):
    nconv = conv0_ref.shape[0]

    def xpad(i):
        if i < nconv:
            return conv0_ref[i]
        return x_ref[(i - nconv) * bn:(i - nconv + 1) * bn, :]

    for t in range(seq):
        acc = cw_ref[0:1, :] * xpad(t)
        for k in range(1, nconv + 1):
            acc = acc + cw_ref[k:k + 1, :] * xpad(t + k)
        xc = cb_ref[...] + acc
        _rg_gates(xc, wa_ref, wi_ref, ba_ref, bi_ref, lam_ref, a_ref, b_ref, slice(t * bn, (t + 1) * bn))
    h = h0_ref[...]
    for t in range(seq):
        rows = slice(t * bn, (t + 1) * bn)
        h = a_ref[rows, :] * h + b_ref[rows, :]
        y_ref[rows, :] = h * gy_ref[rows, :]
    hout_ref[...] = h


def _rg_step(proj, conv0_tm, h0, conv_w, conv_b, wa, wi, b_a, b_i, lam, l, *, bn, seq):
    c = h0.shape[-1]
    t = bn * seq
    depth = conv_w.shape[0]
    vec = lambda a: a.reshape(depth, 1, c)
    lspec = pl.BlockSpec((None, 1, c), lambda i: (l, 0, 0))
    return pl.pallas_call(
        functools.partial(_rg_step_kernel, seq=seq, bn=bn),
        grid=(1,),
        in_specs=[pl.BlockSpec((t, c), lambda i: (0, 1)),
                  pl.BlockSpec((t, c), lambda i: (0, 2)),
                  pl.BlockSpec(conv0_tm.shape, lambda i: (0, 0, 0)),
                  pl.BlockSpec((bn, c), lambda i: (0, 0)),
                  pl.BlockSpec((None, CONV_W, c), lambda i: (l, 0, 0)),
                  lspec, _const_spec(wa), _const_spec(wi), lspec, lspec, lspec],
        out_specs=[pl.BlockSpec((t, c), lambda i: (0, 0)), pl.BlockSpec((bn, c), lambda i: (0, 0))],
        out_shape=[jax.ShapeDtypeStruct((t, c), F32), jax.ShapeDtypeStruct((bn, c), F32)],
        scratch_shapes=[pltpu.VMEM((t, c), F32), pltpu.VMEM((t, c), F32)],
        compiler_params=_params("arbitrary"),
        name="rg_step",
    )(proj, proj, conv0_tm, h0, conv_w, vec(conv_b), wa, wi, vec(b_a), vec(b_i), vec(lam))


def _gate_merge_kernel(x_ref, sh_ref, sc_ref, g_ref, wgs_ref, wgr_ref, bgs_ref, bgr_ref, ys_ref, yr_ref,
                       ws_ref, wr_ref, o_ref, hn_ref):
    @pl.when(pl.program_id(1) == 0)
    def _():
        _norm_mod_rows(x_ref, g_ref, sc_ref, sh_ref, hn_ref)

    h = hn_ref[...]
    gate_s = _sigmoid(_dot(h, wgs_ref[...]) + bgs_ref[...])
    gate_r = _sigmoid(_dot(h, wgr_ref[...]) + bgr_ref[...])
    o_ref[...] = gate_s * _dot(ys_ref[...], ws_ref[...]) + gate_r * _dot(yr_ref[...], wr_ref[...])


def _gate_merge(x, mod, g, w_gate, b_gate, ys, yr, w_s, w_r, l, *, tm, tn, tiles_per_block):
    t, d = x.shape
    c = ys.shape[1]
    rows = mod.shape[2]
    off = d // tn
    depth = w_gate.shape[0]
    once = dict(pipeline_mode=pl.Buffered(1))
    return pl.pallas_call(
        _gate_merge_kernel,
        grid=(t // tm, d // tn),
        in_specs=[pl.BlockSpec((tm, d), lambda m, n: (m, 0), **once),
                  _mod_spec(0, rows, d, tiles_per_block, tm),
                  _mod_spec(1, rows, d, tiles_per_block, tm),
                  pl.BlockSpec((None, 1, d), lambda m, n: (l, 0, 0)),
                  pl.BlockSpec((None, d, tn), lambda m, n: (l, 0, n)),
                  pl.BlockSpec((None, d, tn), lambda m, n: (l, 0, n + off)),
                  pl.BlockSpec((None, 1, tn), lambda m, n: (l, 0, n)),
                  pl.BlockSpec((None, 1, tn), lambda m, n: (l, 0, n + off)),
                  pl.BlockSpec((tm, c), lambda m, n: (m, 0), **once),
                  pl.BlockSpec((tm, c), lambda m, n: (m, 0), **once),
                  pl.BlockSpec((None, c, tn), lambda m, n: (l, 0, n)),
                  pl.BlockSpec((None, c, tn), lambda m, n: (l, 0, n))],
        out_specs=pl.BlockSpec((tm, tn), lambda m, n: (m, n)),
        out_shape=jax.ShapeDtypeStruct((t, d), F32),
        scratch_shapes=[pltpu.VMEM((tm, d), F32)],
        compiler_params=_params("arbitrary", "arbitrary"),
        name="gate_merge",
    )(x, mod, mod, g.reshape(depth, 1, d), w_gate, w_gate, b_gate.reshape(depth, 1, 2 * d),
      b_gate.reshape(depth, 1, 2 * d), ys, yr, w_s, w_r)


def _linear_residual_kernel(a_ref, w_ref, x_ref, g_ref, o_ref, acc_ref):
    k = pl.program_id(2)

    @pl.when(k == 0)
    def _():
        acc_ref[...] = jnp.zeros_like(acc_ref)

    acc_ref[...] += _dot(a_ref[...], w_ref[...])

    @pl.when(k == pl.num_programs(2) - 1)
    def _():
        o_ref[...] = x_ref[...] + g_ref[...] * acc_ref[...]


def _linear_residual(a, w, l, x, mod, j_g, *, tm, tn, tk, tiles_per_block):
    t, kdim = a.shape
    d = w.shape[-1]
    rows = mod.shape[2]
    return pl.pallas_call(
        _linear_residual_kernel,
        grid=(t // tm, d // tn, kdim // tk),
        in_specs=[pl.BlockSpec((tm, tk), lambda m, n, k: (m, k)),
                  pl.BlockSpec((None, tk, tn), lambda m, n, k: (l, k, n)),
                  pl.BlockSpec((tm, tn), lambda m, n, k: (m, n)),
                  _mod_spec(j_g, rows, tn, tiles_per_block, tm, ncol_axis=0)],
        out_specs=pl.BlockSpec((tm, tn), lambda m, n, k: (m, n)),
        out_shape=jax.ShapeDtypeStruct((t, d), F32),
        scratch_shapes=[pltpu.VMEM((tm, tn), F32)],
        compiler_params=_params("arbitrary", "arbitrary", "arbitrary"),
        name="linear_residual",
    )(a, w, x, mod)


ROUTE_I1, ROUTE_I2, ROUTE_W1, ROUTE_W2, ROUTE_RANK1, ROUTE_RANK2 = range(6)
DMA_UNROLL = 8


def _router_kernel(x_ref, sh_ref, sc_ref, g_ref, wr_ref, hn_ref, route_ref, cnt_ref, *, n_experts):
    m = pl.program_id(0)

    @pl.when(m == 0)
    def _():
        cnt_ref[...] = jnp.zeros_like(cnt_ref)

    _norm_mod_rows(x_ref, g_ref, sc_ref, sh_ref, hn_ref)
    logits = jnp.dot(hn_ref[...], wr_ref[...], preferred_element_type=F32, precision=lax.Precision.HIGHEST)
    tm, width = logits.shape
    lane = lax.broadcasted_iota(jnp.int32, logits.shape, 1).astype(F32)
    lg = jnp.where(lane < n_experts, logits, -jnp.inf)
    v1 = jnp.max(lg, axis=-1, keepdims=True)
    i1 = jnp.min(jnp.where(lg == v1, lane, float(width)), axis=-1, keepdims=True)
    lg2 = jnp.where(lane == i1, -jnp.inf, lg)
    v2 = jnp.max(lg2, axis=-1, keepdims=True)
    i2 = jnp.min(jnp.where(lg2 == v2, lane, float(width)), axis=-1, keepdims=True)
    e2 = jnp.exp(v2 - v1)
    den = 1.0 + e2
    sel1 = lane == i1
    sel2 = lane == i2
    picked = jnp.where(sel1 | sel2, 1.0, 0.0)
    r = lax.broadcasted_iota(jnp.int32, (tm, tm), 0)
    c = lax.broadcasted_iota(jnp.int32, (tm, tm), 1)
    before = jnp.where(c < r, 1.0, 0.0)
    seen = _dot(before, picked) + cnt_ref[...]
    rank1 = jnp.sum(jnp.where(sel1, seen, 0.0), axis=-1, keepdims=True)
    rank2 = jnp.sum(jnp.where(sel2, seen, 0.0), axis=-1, keepdims=True)
    cnt_ref[...] += jnp.sum(picked, axis=0, keepdims=True)
    cols = ((ROUTE_I1, i1), (ROUTE_I2, i2), (ROUTE_W1, 1.0 / den), (ROUTE_W2, e2 / den),
            (ROUTE_RANK1, rank1), (ROUTE_RANK2, rank2))
    route = jnp.zeros_like(logits)
    for j, v in cols:
        route = jnp.where(lane == j, v, route)
    route_ref[...] = route


def _router(x, mod, j_sh, g, w_router_pad, l, *, tm, tiles_per_block, n_experts):
    t, d = x.shape
    rows = mod.shape[2]
    width = w_router_pad.shape[-1]
    return pl.pallas_call(
        functools.partial(_router_kernel, n_experts=n_experts),
        grid=(t // tm,),
        in_specs=[pl.BlockSpec((tm, d), lambda m: (m, 0)),
                  _mod_spec(j_sh, rows, d, tiles_per_block, tm),
                  _mod_spec(j_sh + 1, rows, d, tiles_per_block, tm),
                  pl.BlockSpec((None, 1, d), lambda m: (l, 0, 0)),
                  pl.BlockSpec((d, width), lambda m: (0, 0))],
        out_specs=[pl.BlockSpec((tm, d), lambda m: (m, 0)), pl.BlockSpec((tm, width), lambda m: (m, 0)),
                   pl.BlockSpec((1, width), lambda m: (0, 0))],
        out_shape=[jax.ShapeDtypeStruct((t, d), F32), jax.ShapeDtypeStruct((t, width), F32),
                   jax.ShapeDtypeStruct((1, width), F32)],
        compiler_params=_params("arbitrary"),
        name="moe_router",
    )(x, mod, mod, g.reshape(g.shape[0], 1, d), w_router_pad)


def _dispatch_kernel(tok_ref, nu_ref, hn_ref, xs_ref, sem):
    tg = xs_ref.shape[0]
    m = pl.program_id(0)

    def row_copy(tok, i):
        return pltpu.make_async_copy(hn_ref.at[pl.ds(tok, 1)], xs_ref.at[pl.ds(i, 1)], sem.at[0])

    @pl.when(m < nu_ref[0])
    def _():
        lax.fori_loop(0, tg, lambda i, c: (row_copy(tok_ref[m * tg + i], i).start(), c)[1], 0, unroll=DMA_UNROLL)
        pltpu.make_async_copy(hn_ref.at[pl.ds(0, tg)], xs_ref, sem.at[0]).wait()

    @pl.when(m >= nu_ref[0])
    def _():
        xs_ref[...] = jnp.zeros_like(xs_ref)


def _dispatch(tok_of_row, n_used, hn, *, tg):
    rows = tok_of_row.shape[0]
    d = hn.shape[1]
    return pl.pallas_call(
        _dispatch_kernel,
        grid_spec=pltpu.PrefetchScalarGridSpec(
            num_scalar_prefetch=2, grid=(rows // tg,),
            in_specs=[pl.BlockSpec(memory_space=pl.ANY)],
            out_specs=pl.BlockSpec((tg, d), lambda m, tok, nu: (m, 0)),
            scratch_shapes=[pltpu.SemaphoreType.DMA((1,))]),
        out_shape=jax.ShapeDtypeStruct((rows, d), F32),
        compiler_params=_params("arbitrary"),
        name="moe_dispatch",
    )(tok_of_row, n_used, hn)


def _grouped_up_kernel(te_ref, nu_ref, xs_ref, w1_ref, w3_ref, o_ref):
    m = pl.program_id(1)

    @pl.when(m < nu_ref[0])
    def _():
        h = xs_ref[...]
        o_ref[...] = _silu(_dot(h, w1_ref[...])) * _dot(h, w3_ref[...])

    @pl.when(m >= nu_ref[0])
    def _():
        o_ref[...] = jnp.zeros_like(o_ref)


def _grouped_up(tile_expert, n_used, xs, w1, w3, lm, *, tg, tn):
    r, d = xs.shape
    f = w1.shape[-1]
    wspec = pl.BlockSpec((None, None, d, tn), lambda n, m, te, nu: (lm, te[m], 0, n))
    return pl.pallas_call(
        _grouped_up_kernel,
        grid_spec=pltpu.PrefetchScalarGridSpec(
            num_scalar_prefetch=2, grid=(f // tn, r // tg),
            in_specs=[pl.BlockSpec((tg, d), lambda n, m, te, nu: (m, 0)), wspec, wspec],
            out_specs=pl.BlockSpec((tg, tn), lambda n, m, te, nu: (m, n))),
        out_shape=jax.ShapeDtypeStruct((r, f), F32),
        compiler_params=_params("arbitrary", "arbitrary"),
        name="moe_up",
    )(tile_expert, n_used, xs, w1, w3)


def _grouped_down_kernel(te_ref, nu_ref, a_ref, w_ref, o_ref):
    m = pl.program_id(1)

    @pl.when(m < nu_ref[0])
    def _():
        o_ref[...] = _dot(a_ref[...], w_ref[...])

    @pl.when(m >= nu_ref[0])
    def _():
        o_ref[...] = jnp.zeros_like(o_ref)


def _grouped_down(tile_expert, n_used, act, w2, lm, *, tg, tn):
    r, f = act.shape
    d = w2.shape[-1]
    return pl.pallas_call(
        _grouped_down_kernel,
        grid_spec=pltpu.PrefetchScalarGridSpec(
            num_scalar_prefetch=2, grid=(d // tn, r // tg),
            in_specs=[pl.BlockSpec((tg, f), lambda n, m, te, nu: (m, 0)),
                      pl.BlockSpec((None, None, f, tn), lambda n, m, te, nu: (lm, te[m], 0, n))],
            out_specs=pl.BlockSpec((tg, tn), lambda n, m, te, nu: (m, n))),
        out_shape=jax.ShapeDtypeStruct((r, d), F32),
        compiler_params=_params("arbitrary", "arbitrary"),
        name="moe_down",
    )(tile_expert, n_used, act, w2)


def _combine_kernel(p1_ref, p2_ref, y_ref, route_ref, x_ref, g_ref, *rest, final_norm):
    if final_norm:
        gf_ref, o_ref, ya_ref, yb_ref, sem = rest
    else:
        o_ref, ya_ref, yb_ref, sem = rest
    tc = ya_ref.shape[0]
    base = pl.program_id(0) * tc

    def row_copy(row, dst_ref, i):
        return pltpu.make_async_copy(y_ref.at[pl.ds(row, 1)], dst_ref.at[pl.ds(i, 1)], sem.at[0])

    def body(i, c):
        row_copy(p1_ref[base + i], ya_ref, i).start()
        row_copy(p2_ref[base + i], yb_ref, i).start()
        return c

    lax.fori_loop(0, tc, body, 0, unroll=DMA_UNROLL)
    for dst_ref in (ya_ref, yb_ref):
        pltpu.make_async_copy(y_ref.at[pl.ds(0, tc)], dst_ref, sem.at[0]).wait()
    w1 = route_ref[:, ROUTE_W1:ROUTE_W1 + 1]
    w2 = route_ref[:, ROUTE_W2:ROUTE_W2 + 1]
    out = x_ref[...] + g_ref[...] * (w1 * ya_ref[...] + w2 * yb_ref[...])
    o_ref[...] = _rmsnorm(out, gf_ref[...]) if final_norm else out


def _combine(pos1, pos2, y, route, x, mod, j_g, *, tc, tiles_per_block, final_g=None):
    t, d = x.shape
    rows = mod.shape[2]
    width = route.shape[-1]
    in_specs = [pl.BlockSpec(memory_space=pl.ANY),
                pl.BlockSpec((tc, width), lambda m, p1, p2: (m, 0)),
                pl.BlockSpec((tc, d), lambda m, p1, p2: (m, 0)),
                _mod_spec(j_g, rows, d, tiles_per_block, tc)]
    args = [pos1, pos2, y, route, x, mod]
    if final_g is not None:
        in_specs.append(pl.BlockSpec((1, d), lambda m, p1, p2: (0, 0)))
        args.append(final_g.reshape(1, d))
    return pl.pallas_call(
        functools.partial(_combine_kernel, final_norm=final_g is not None),
        grid_spec=pltpu.PrefetchScalarGridSpec(
            num_scalar_prefetch=2, grid=(t // tc,),
            in_specs=in_specs,
            out_specs=pl.BlockSpec((tc, d), lambda m, p1, p2: (m, 0)),
            scratch_shapes=[pltpu.VMEM((tc, d), F32), pltpu.VMEM((tc, d), F32), pltpu.SemaphoreType.DMA((1,))]),
        out_shape=jax.ShapeDtypeStruct((t, d), F32),
        compiler_params=_params("arbitrary"),
        name="moe_combine",
    )(*args)


def _routed_moe(x, mod, g, w_router, w1, w3, w2, l, lm, *, tm, tg, tiles_per_tile, final_g=None):
    t, d = x.shape
    ne = w_router.shape[-1]
    w_r = jnp.pad(w_router[lm], ((0, 0), (0, V7X_LANES - ne)))
    hn, route, cnt = _router(x, mod, 3, g, w_r, l, n_experts=ne, **tiles_per_tile(tm))
    counts = cnt[0, :ne].astype(jnp.int32)
    tiles_e = (counts + (tg - 1)) // tg
    tile_end = jnp.cumsum(tiles_e)
    row_start = (tile_end - tiles_e) * tg
    col = lambda j: route[:, j].astype(jnp.int32)
    pos1 = row_start[col(ROUTE_I1)] + col(ROUTE_RANK1)
    pos2 = row_start[col(ROUTE_I2)] + col(ROUTE_RANK2)
    n_tiles = -(-TOP_K * t // tg) + ne
    tile_expert = jnp.minimum(jnp.sum(jnp.arange(n_tiles)[:, None] >= tile_end[None, :], axis=1), ne - 1)
    tile_expert = tile_expert.astype(jnp.int32)
    n_used = tile_end[-1:].astype(jnp.int32)
    tok = jnp.arange(t, dtype=jnp.int32)
    tok_of_row = jnp.zeros((n_tiles * tg,), jnp.int32).at[jnp.concatenate([pos1, pos2])].set(
        jnp.concatenate([tok, tok]), unique_indices=True)
    xs = _dispatch(tok_of_row, n_used, hn, tg=tg)
    act = _grouped_up(tile_expert, n_used, xs, w1, w3, lm, tg=tg, tn=1024)
    y = _grouped_down(tile_expert, n_used, act, w2, lm, tg=tg, tn=1024)
    tc = min(tm, 256)
    return _combine(pos1, pos2, y, route, x, mod, 5, tc=tc, tiles_per_block=tiles_per_tile(tc)["tiles_per_block"],
                    final_g=final_g)


def _final_norm_kernel(x_ref, g_ref, o_ref):
    o_ref[...] = _rmsnorm(x_ref[...], g_ref[...])


def _final_norm(x, g, *, tm):
    t, d = x.shape
    return pl.pallas_call(
        _final_norm_kernel,
        grid=(t // tm,),
        in_specs=[pl.BlockSpec((tm, d), lambda m: (m, 0)), pl.BlockSpec((1, d), lambda m: (0, 0))],
        out_specs=pl.BlockSpec((tm, d), lambda m: (m, 0)),
        out_shape=jax.ShapeDtypeStruct((t, d), F32),
        compiler_params=_params("arbitrary"),
        name="final_norm",
    )(x, g.reshape(1, d))


def _run(x, mods, state, p, *, bn, seq, time_major, tm):
    t, d = x.shape
    depth = p["w_in"].shape[0]
    c = p["s5_w_glu"].shape[-1]
    def tiled(tile):
        return dict(tm=tile, tiles_per_block=1 if time_major else seq // tile)

    kw = tiled(tm)
    nconv = CONV_W - 1
    assert seq >= nconv
    s5_re, s5_im, rg_h, conv = [], [], [], []
    for l in range(depth):
        mod = mods[l]
        tb = p["s5_tables"][l]
        wa, wi = p["rg_tables"][l]
        proj = _normed_linear(x, mod, 0, p["norm_mix"], l, [p["w_in"]], l, mode="in", tn=512,
                              plain_cols=2 * c, **kw)
        h0re = state["s5_re"][l].reshape(bn, -1)
        h0im = state["s5_im"][l].reshape(bn, -1)
        rg_args = (p["rg_conv_w"], p["rg_conv_b"], wa, wi, p["rg_b_a"].reshape(depth, c), p["rg_b_i"].reshape(depth, c),
                   p["rg_lam"], l)
        if time_major:
            y_s5, hre, him = _s5_step(proj, h0re, h0im, tb, p["s5_d"].reshape(depth, c), p["s5_w_glu"], p["s5_b_glu"], l,
                                      bn=bn, seq=seq)
            y_rg, hrg = _rg_step(proj, state["conv"][l].transpose(1, 0, 2), state["rg"][l], *rg_args, bn=bn, seq=seq)
            conv.append(proj[:, c:2 * c].reshape(seq, bn, c)[seq - nconv:].transpose(1, 0, 2))
        else:
            y_s5, hre, him = _s5_seq(proj, h0re, h0im, tb, p["s5_d"].reshape(depth, c), p["s5_w_glu"], p["s5_b_glu"], l,
                                     bn=bn, seq=seq)
            y_rg, hrg = _rg_seq(proj, state["conv"][l], state["rg"][l], *rg_args, bn=bn, seq=seq)
            tail_rows = [seq - SEG_TILE + (SEG_LEN - nconv + f) * V7X_SUBLANES + V7X_SUBLANES - 1 for f in range(nconv)]
            proj3 = proj.reshape(bn, seq, 3 * c)
            conv.append(jnp.stack([proj3[:, r, c:2 * c] for r in tail_rows], axis=1))
        s5_re.append(hre.reshape(state["s5_re"][l].shape))
        s5_im.append(him.reshape(state["s5_im"][l].shape))
        rg_h.append(hrg)
        merged = _gate_merge(x, mod, p["norm_mix"], p["w_gate"], p["b_gate"], y_s5, y_rg, p["w_br_s5"], p["w_br_rg"], l,
                             tn=256, **kw)
        x = _linear_residual(merged, p["w_out"], l, x, mod, 2, tn=1024, tk=1024, **kw)
        last = l == depth - 1
        if l % 2 == 0:
            act = _normed_linear(x, mod, 3, p["norm_ffn"], l, [p["ffn_w1"], p["ffn_w3"]], l // 2, mode="glu",
                                 tn=512, **kw)
            x = _linear_residual(act, p["ffn_w2"], l // 2, x, mod, 5, tn=1024, tk=1024, **kw)
        else:
            x = _routed_moe(x, mod, p["norm_ffn"], p["moe_router"], p["moe_w1"], p["moe_w3"], p["moe_w2"], l, l // 2,
                            tm=min(tm, 512), tg=512 if t >= 4096 else 128, tiles_per_tile=tiled,
                            final_g=p["norm_f"] if last else None)
    y = x if depth % 2 == 0 else _final_norm(x, p["norm_f"], tm=min(tm, 256))
    return y, jnp.stack(s5_re), jnp.stack(s5_im), jnp.stack(rg_h), jnp.stack(conv)


def kernel(x_prompt, x_sample, c_prompt, c_sample, state_s5_re, state_s5_im, state_rglru, state_conv, norm_mix, norm_ffn, norm_f, w_ada, b_ada, w_in, s5_lam_re, s5_lam_im, s5_log_dt, s5_b_re, s5_b_im, s5_c_re, s5_c_im, s5_d, s5_w_glu, s5_b_glu, rg_conv_w, rg_conv_b, rg_w_a, rg_b_a, rg_w_i, rg_b_i, rg_lam, w_gate, b_gate, w_br_s5, w_br_rg, w_out, ffn_w1, ffn_w3, ffn_w2, moe_router, moe_w1, moe_w3, moe_w2):
    p = dict(norm_mix=norm_mix, norm_ffn=norm_ffn, norm_f=norm_f, w_in=w_in,
             s5_lam_re=s5_lam_re, s5_lam_im=s5_lam_im, s5_log_dt=s5_log_dt, s5_b_re=s5_b_re, s5_b_im=s5_b_im,
             s5_c_re=s5_c_re, s5_c_im=s5_c_im, s5_d=s5_d, s5_w_glu=s5_w_glu, s5_b_glu=s5_b_glu,
             rg_conv_w=rg_conv_w, rg_conv_b=rg_conv_b, rg_w_a=rg_w_a, rg_b_a=rg_b_a, rg_w_i=rg_w_i, rg_b_i=rg_b_i,
             rg_lam=rg_lam, w_gate=w_gate, b_gate=b_gate, w_br_s5=w_br_s5, w_br_rg=w_br_rg, w_out=w_out,
             ffn_w1=ffn_w1, ffn_w3=ffn_w3, ffn_w2=ffn_w2, moe_router=moe_router, moe_w1=moe_w1, moe_w3=moe_w3,
             moe_w2=moe_w2)
    depth = w_in.shape[0]
    p["s5_tables"] = [_s5_tables(s5_lam_re[l], s5_lam_im[l], s5_log_dt[l], s5_b_re[l], s5_b_im[l], s5_c_re[l],
                                 s5_c_im[l]) for l in range(depth)]
    p["rg_tables"] = [_rg_tables(rg_w_a[l], rg_w_i[l]) for l in range(depth)]
    bp, sp, d = x_prompt.shape
    bs, ss, _ = x_sample.shape
    g, pst = state_s5_re.shape[2:]
    c = state_rglru.shape[-1]

    n_c = bp + bs
    rows_c = -(-n_c // V7X_SUBLANES) * V7X_SUBLANES
    c_all = jnp.pad(jnp.concatenate([c_prompt, c_sample], axis=0), ((0, rows_c - n_c), (0, 0)))
    mod_all = _ada_mod(c_all, w_ada, b_ada)
    mods_p, mods_s = [], []
    for l in range(depth):
        mp = mod_all[l, :bp].reshape(bp, 6, 1, d).transpose(1, 0, 2, 3)
        ms = mod_all[l, bp:n_c].reshape(bs, 6, d).transpose(1, 0, 2)
        ms = jnp.tile(ms, (1, ss, 1)).reshape(6, 1, ss * bs, d)
        mods_p.append(mp)
        mods_s.append(ms)

    zero_state = dict(s5_re=jnp.zeros((depth, bp, g, pst), F32), s5_im=jnp.zeros((depth, bp, g, pst), F32),
                      rg=jnp.zeros((depth, bp, c), F32), conv=jnp.zeros((depth, bp, CONV_W - 1, c), x_prompt.dtype))
    seg_shape = (bp, sp // SEG_TILE, V7X_SUBLANES, SEG_LEN, d)
    x_p = x_prompt.reshape(seg_shape).transpose(0, 1, 3, 2, 4).reshape(bp * sp, d)
    y_p, p_re, p_im, p_rg, p_conv = _run(x_p, mods_p, zero_state, p,
                                         bn=bp, seq=sp, time_major=False, tm=min(PROMPT_TILE, sp))
    y_p = y_p.reshape(bp, sp // SEG_TILE, SEG_LEN, V7X_SUBLANES, d).transpose(0, 1, 3, 2, 4)
    sample_state = dict(s5_re=state_s5_re, s5_im=state_s5_im, rg=state_rglru, conv=state_conv)
    x_s = x_sample.transpose(1, 0, 2).reshape(ss * bs, d)
    y_s, s_re, s_im, s_rg, s_conv = _run(x_s, mods_s, sample_state, p, bn=bs, seq=ss, time_major=True, tm=ss * bs)
    y_s = y_s.reshape(ss, bs, d).transpose(1, 0, 2)
    return (y_p.reshape(bp, sp, d), y_s, p_re, p_im, p_rg, p_conv, s_re, s_im, s_rg, s_conv)
```

```python
import functools
import math

import jax
import jax.numpy as jnp
from jax import lax
from jax.experimental import pallas as pl
from jax.experimental.pallas import tpu as pltpu

F32 = jnp.float32
EPS = 1e-6
RG_C = 8.0
CONV_W = 4
S5_GROUP = 16
S5_STATE = 64
RG_HEAD_DIM = 64
TOP_K = 2

V7X_LANES = 128
V7X_SUBLANES = 8
V7X_MXU_DIM = 256
VMEM_LIMIT_BYTES = 56 * 1024 * 1024

GELU_C = math.sqrt(2.0 / math.pi)


def _gelu(x):
    return 0.5 * x * (1.0 + jnp.tanh(GELU_C * (x + 0.044715 * (x * x * x))))


def _sigmoid(x):
    return 1.0 / (1.0 + jnp.exp(-x))


def _silu(x):
    return x * _sigmoid(x)


def _dot(a, b):
    return jnp.dot(a, b, preferred_element_type=F32)


def _params(*sem):
    return pltpu.CompilerParams(dimension_semantics=sem, vmem_limit_bytes=VMEM_LIMIT_BYTES)


def _mod_spec(j, rows, width, tiles_per_block, tile, ncol_axis=None):
    col = (lambda idx: 0) if ncol_axis is None else (lambda idx: idx[ncol_axis])
    if rows == 1:
        return pl.BlockSpec((None, None, 1, width), lambda m, *idx: (j, m // tiles_per_block, 0, col(idx)))
    return pl.BlockSpec((None, None, tile, width), lambda m, *idx: (j, 0, m, col(idx)))


ROW_CHUNK = 128
PROMPT_TILE = 1024
SEG_LEN = 32
SEG_TILE = SEG_LEN * V7X_SUBLANES
SCAN_UNROLL = True


def _rmsnorm(x, g):
    ms = jnp.mean(x * x, axis=-1, keepdims=True)
    return (x * lax.rsqrt(ms + EPS)) * g


def _norm_mod_rows(x_ref, g_ref, sc_ref, sh_ref, o_ref):
    tm = x_ref.shape[0]
    chunk = min(ROW_CHUNK, tm)
    per_row = sc_ref.shape[0] != 1

    def body(i, carry):
        rows = pl.ds(pl.multiple_of(i * chunk, chunk), chunk)
        sc = sc_ref[rows, :] if per_row else sc_ref[...]
        sh = sh_ref[rows, :] if per_row else sh_ref[...]
        o_ref[rows, :] = _rmsnorm(x_ref[rows, :], g_ref[...]) * (1.0 + sc) + sh
        return carry

    lax.fori_loop(0, tm // chunk, body, 0)


def _ada_kernel(c_ref, w_ref, b_ref, o_ref):
    o_ref[...] = _dot(_silu(c_ref[...]), w_ref[...]) + b_ref[...]


def _ada_mod(c_all, w_ada, b_ada, *, tn=1024):
    depth, d, n6 = w_ada.shape
    rows = c_all.shape[0]
    return pl.pallas_call(
        _ada_kernel,
        grid=(depth, n6 // tn),
        in_specs=[pl.BlockSpec((rows, d), lambda l, n: (0, 0)),
                  pl.BlockSpec((None, d, tn), lambda l, n: (l, 0, n)),
                  pl.BlockSpec((None, 1, tn), lambda l, n: (l, 0, n))],
        out_specs=pl.BlockSpec((None, rows, tn), lambda l, n: (l, 0, n)),
        out_shape=jax.ShapeDtypeStruct((depth, rows, n6), F32),
        compiler_params=_params("arbitrary", "arbitrary"),
        name="ada_mod",
    )(c_all, w_ada, b_ada.reshape(depth, 1, n6))


def _normed_linear_kernel(*refs, mode, n_plain):
    if mode == "glu":
        x_ref, sh_ref, sc_ref, g_ref, w1_ref, w3_ref, o_ref, hn_ref = refs
    else:
        x_ref, sh_ref, sc_ref, g_ref, w_ref, o_ref, hn_ref = refs
    n = pl.program_id(1)

    @pl.when(n == 0)
    def _():
        _norm_mod_rows(x_ref, g_ref, sc_ref, sh_ref, hn_ref)

    h = hn_ref[...]
    if mode == "glu":
        o_ref[...] = _silu(_dot(h, w1_ref[...])) * _dot(h, w3_ref[...])
    else:
        z = _dot(h, w_ref[...])

        @pl.when(n < n_plain)
        def _():
            o_ref[...] = z

        @pl.when(n >= n_plain)
        def _():
            o_ref[...] = _gelu(z)


def _normed_linear(x, mod, j_sh, g, l_g, ws, l, *, mode, tm, tn, tiles_per_block, plain_cols=0):
    t, d = x.shape
    n_out = ws[0].shape[-1]
    rows = mod.shape[2]
    w_spec = pl.BlockSpec((None, d, tn), lambda m, n: (l, 0, n))
    in_specs = [pl.BlockSpec((tm, d), lambda m, n: (m, 0)),
                _mod_spec(j_sh, rows, d, tiles_per_block, tm),
                _mod_spec(j_sh + 1, rows, d, tiles_per_block, tm),
                pl.BlockSpec((None, 1, d), lambda m, n: (l_g, 0, 0))]
    args = [x, mod, mod, g.reshape(g.shape[0], 1, d)]
    for w in ws:
        in_specs.append(w_spec)
        args.append(w)
    return pl.pallas_call(
        functools.partial(_normed_linear_kernel, mode=mode, n_plain=plain_cols // tn),
        grid=(t // tm, n_out // tn),
        in_specs=in_specs,
        out_specs=pl.BlockSpec((tm, tn), lambda m, n: (m, n)),
        out_shape=jax.ShapeDtypeStruct((t, n_out), F32),
        scratch_shapes=[pltpu.VMEM((tm, d), F32)],
        compiler_params=_params("arbitrary", "arbitrary"),
        name="normed_linear_" + mode,
    )(*args)


def _s5_tables(lam_re, lam_im, log_dt, b_re, b_im, c_re, c_im):
    g, p = lam_re.shape
    h = b_re.shape[-1]
    gpc = V7X_MXU_DIM // h
    nck = g // gpc
    dt = jnp.exp(log_dt)[:, None]
    mag = jnp.exp(lam_re * dt)
    lb_re = mag * jnp.cos(lam_im * dt)
    lb_im = mag * jnp.sin(lam_im * dt)
    den = lam_re * lam_re + lam_im * lam_im
    cf_re = ((lb_re - 1.0) * lam_re + lb_im * lam_im) / den
    cf_im = (lb_im * lam_re - (lb_re - 1.0) * lam_im) / den
    bb_re = cf_re[:, :, None] * b_re - cf_im[:, :, None] * b_im
    bb_im = cf_re[:, :, None] * b_im + cf_im[:, :, None] * b_re
    by_ghp = lambda m: m.transpose(0, 2, 1).reshape(nck, gpc * h, p)
    by_gph = lambda m: m.transpose(0, 2, 1).reshape(nck, gpc * p, h)

    def powers(steps):
        k = jnp.asarray(steps, F32)[:, None]
        zr = (lam_re * dt).reshape(1, g * p)
        zi = (lam_im * dt).reshape(1, g * p)
        return jnp.exp(k * zr) * jnp.cos(k * zi), jnp.exp(k * zr) * jnp.sin(k * zi)

    sub = V7X_SUBLANES
    pw_re, pw_im = powers(range(1, SEG_LEN + 1))
    sg_re, sg_im = powers(range(SEG_LEN, SEG_LEN * (sub + 1), SEG_LEN))
    row = jnp.arange(sub)[:, None]
    segtabs = []
    for k in (1, 2, 4):
        segtabs += [jnp.where(row >= k, sg_re[k - 1:k], 0.0), jnp.where(row >= k, sg_im[k - 1:k], 0.0)]
    segtabs += [sg_re, sg_im]
    return dict(wb_re=_block_diag(by_ghp(bb_re), gpc), wb_im=_block_diag(by_ghp(bb_im), gpc),
                wc_re=_block_diag(by_gph(c_re), gpc), wc_imneg=_block_diag(by_gph(-c_im), gpc),
                lam8=jnp.stack([jnp.broadcast_to(pw_re[0:1], (sub, g * p)),
                                jnp.broadcast_to(pw_im[0:1], (sub, g * p))]),
                segtabs=jnp.stack(segtabs),
                fix=jnp.stack([jnp.repeat(pw_re, sub, axis=0), jnp.repeat(pw_im, sub, axis=0)]),
                lb_re=pw_re[0:1], lb_im=pw_im[0:1])


def _block_diag(src, nblk):
    _, rows, cb = src.shape
    r = jnp.arange(rows)[:, None] // (rows // nblk)
    c = jnp.arange(nblk * cb)[None, :] // cb
    return jnp.where(r == c, jnp.tile(src, (1, 1, nblk)), 0.0)


def _s5_input_proj(u, wbre_ref, wbim_ref, bre_ref, bim_ref):
    nck, kc, nc = wbre_ref.shape
    for k in range(nck):
        uk = u[:, k * kc:(k + 1) * kc]
        bre_ref[:, k * nc:(k + 1) * nc] = _dot(uk, wbre_ref[k])
        bim_ref[:, k * nc:(k + 1) * nc] = _dot(uk, wbim_ref[k])


def _s5_output(u, bre_ref, bim_ref, wcre_ref, wcim_ref, d_ref, wglu_ref, bglu_ref, y_ref):
    nck, nc, kc = wcre_ref.shape
    for k in range(nck):
        yk = _dot(bre_ref[:, k * nc:(k + 1) * nc], wcre_ref[k]) + _dot(bim_ref[:, k * nc:(k + 1) * nc], wcim_ref[k])
        yk = yk + d_ref[:, k * kc:(k + 1) * kc] * u[:, k * kc:(k + 1) * kc]
        y_ref[:, k * kc:(k + 1) * kc] = _gelu(yk)
    yg = y_ref[...]
    y_ref[...] = yg * _sigmoid(_dot(yg, wglu_ref[...]) + bglu_ref[...])


def _row_block(j):
    return pl.ds(pl.multiple_of(j * V7X_SUBLANES, V7X_SUBLANES), V7X_SUBLANES)


def _s5_seq_kernel(u_ref, h0re_ref, h0im_ref, wbre_ref, wbim_ref, wcre_ref, wcim_ref, lam8_ref, seg_ref, fix_ref,
                   d_ref, wglu_ref, bglu_ref, y_ref, hre_ref, him_ref, bre_ref, bim_ref, cre_ref, cim_ref,
                   *, lane_chunk):
    s = pl.program_id(1)
    ns = cre_ref.shape[1]

    @pl.when(s == 0)
    def _():
        cre_ref[...] = jnp.broadcast_to(h0re_ref[...], cre_ref.shape)
        cim_ref[...] = jnp.broadcast_to(h0im_ref[...], cim_ref.shape)

    u = u_ref[...]
    _s5_input_proj(u, wbre_ref, wbim_ref, bre_ref, bim_ref)
    first_sublane = lax.broadcasted_iota(jnp.int32, (V7X_SUBLANES, lane_chunk), 0) == 0
    last = V7X_SUBLANES - 1

    for c in range(ns // lane_chunk):
        sl = slice(c * lane_chunk, (c + 1) * lane_chunk)
        lr, li = lam8_ref[0, :, sl], lam8_ref[1, :, sl]

        def local(j, st, sl=sl, lr=lr, li=li):
            sr, si = st
            rows = _row_block(j)
            sr, si = (lr * sr - li * si) + bre_ref[rows, sl], (lr * si + li * sr) + bim_ref[rows, sl]
            bre_ref[rows, sl] = sr
            bim_ref[rows, sl] = si
            return sr, si

        zero = jnp.zeros((V7X_SUBLANES, lane_chunk), F32)
        xr, xi = lax.fori_loop(0, SEG_LEN, local, (zero, zero), unroll=SCAN_UNROLL)
        for i, k in enumerate((1, 2, 4)):
            tr, ti = seg_ref[2 * i, :, sl], seg_ref[2 * i + 1, :, sl]
            pr, pi = pltpu.roll(xr, k, 0), pltpu.roll(xi, k, 0)
            xr, xi = xr + (tr * pr - ti * pi), xi + (tr * pi + ti * pr)
        cr, ci = cre_ref[:, sl], cim_ref[:, sl]
        tr, ti = seg_ref[6, :, sl], seg_ref[7, :, sl]
        gr = xr + (tr * cr - ti * ci)
        gi = xi + (tr * ci + ti * cr)
        h0r = jnp.where(first_sublane, cr, pltpu.roll(gr, 1, 0))
        h0i = jnp.where(first_sublane, ci, pltpu.roll(gi, 1, 0))
        cre_ref[:, sl] = jnp.broadcast_to(gr[last:last + 1], gr.shape)
        cim_ref[:, sl] = jnp.broadcast_to(gi[last:last + 1], gi.shape)

        def fixup(j, carry, sl=sl, h0r=h0r, h0i=h0i):
            rows = _row_block(j)
            fr, fi = fix_ref[0, rows, sl], fix_ref[1, rows, sl]
            bre_ref[rows, sl] += fr * h0r - fi * h0i
            bim_ref[rows, sl] += fr * h0i + fi * h0r
            return carry

        lax.fori_loop(0, SEG_LEN, fixup, 0, unroll=SCAN_UNROLL)

    _s5_output(u, bre_ref, bim_ref, wcre_ref, wcim_ref, d_ref, wglu_ref, bglu_ref, y_ref)

    @pl.when(s == pl.num_programs(1) - 1)
    def _():
        hre_ref[...] = cre_ref[0:1, :]
        him_ref[...] = cim_ref[0:1, :]


def _const_spec(a):
    nd = a.ndim
    return pl.BlockSpec(a.shape, lambda *_: (0,) * nd, pipeline_mode=pl.Buffered(1))


def _s5_seq(proj, h0re, h0im, tb, d_skip, w_glu, b_glu, l, *, bn, seq, lane_chunk=256):
    c = w_glu.shape[-1]
    ns = h0re.shape[-1]
    tm = SEG_TILE
    n_s = seq // tm
    consts = [tb["wb_re"], tb["wb_im"], tb["wc_re"], tb["wc_imneg"], tb["lam8"], tb["segtabs"], tb["fix"]]
    in_specs = ([pl.BlockSpec((tm, c), lambda b, s: (b * n_s + s, 0)),
                 pl.BlockSpec((None, 1, ns), lambda b, s: (b, 0, 0)),
                 pl.BlockSpec((None, 1, ns), lambda b, s: (b, 0, 0))]
                + [_const_spec(a) for a in consts]
                + [pl.BlockSpec((None, 1, c), lambda b, s: (l, 0, 0)),
                   pl.BlockSpec((None, c, c), lambda b, s: (l, 0, 0), pipeline_mode=pl.Buffered(1)),
                   pl.BlockSpec((None, 1, c), lambda b, s: (l, 0, 0))])
    y, hre, him = pl.pallas_call(
        functools.partial(_s5_seq_kernel, lane_chunk=lane_chunk),
        grid=(bn, n_s),
        in_specs=in_specs,
        out_specs=[pl.BlockSpec((tm, c), lambda b, s: (b * n_s + s, 0)),
                   pl.BlockSpec((None, 1, ns), lambda b, s: (b, 0, 0)),
                   pl.BlockSpec((None, 1, ns), lambda b, s: (b, 0, 0))],
        out_shape=[jax.ShapeDtypeStruct((bn * seq, c), F32),
                   jax.ShapeDtypeStruct((bn, 1, ns), F32),
                   jax.ShapeDtypeStruct((bn, 1, ns), F32)],
        scratch_shapes=[pltpu.VMEM((tm, ns), F32), pltpu.VMEM((tm, ns), F32),
                        pltpu.VMEM((V7X_SUBLANES, ns), F32), pltpu.VMEM((V7X_SUBLANES, ns), F32)],
        compiler_params=_params("arbitrary", "arbitrary"),
        name="s5_seq",
    )(proj, h0re.reshape(bn, 1, ns), h0im.reshape(bn, 1, ns), *consts,
      d_skip.reshape(d_skip.shape[0], 1, c), w_glu, b_glu.reshape(b_glu.shape[0], 1, c))
    return y, hre.reshape(bn, ns), him.reshape(bn, ns)


def _s5_step_kernel(u_ref, h0re_ref, h0im_ref, wbre_ref, wbim_ref, wcre_ref, wcim_ref, lbre_ref, lbim_ref, d_ref,
                    wglu_ref, bglu_ref, y_ref, hre_ref, him_ref, bre_ref, bim_ref, *, seq, bn, lane_chunk):
    u = u_ref[...]
    _s5_input_proj(u, wbre_ref, wbim_ref, bre_ref, bim_ref)
    ns = bre_ref.shape[1]
    for c in range(ns // lane_chunk):
        sl = slice(c * lane_chunk, (c + 1) * lane_chunk)
        lr = lbre_ref[:, sl]
        li = lbim_ref[:, sl]
        hr = h0re_ref[:, sl]
        hi = h0im_ref[:, sl]
        for t in range(seq):
            rows = slice(t * bn, (t + 1) * bn)
            hr, hi = (lr * hr - li * hi) + bre_ref[rows, sl], (lr * hi + li * hr) + bim_ref[rows, sl]
            bre_ref[rows, sl] = hr
            bim_ref[rows, sl] = hi
        hre_ref[:, sl] = hr
        him_ref[:, sl] = hi
    _s5_output(u, bre_ref, bim_ref, wcre_ref, wcim_ref, d_ref, wglu_ref, bglu_ref, y_ref)


def _s5_step(proj, h0re, h0im, tb, d_skip, w_glu, b_glu, l, *, bn, seq, lane_chunk=128):
    c = w_glu.shape[-1]
    ns = h0re.shape[-1]
    t = bn * seq
    consts = [tb["wb_re"], tb["wb_im"], tb["wc_re"], tb["wc_imneg"], tb["lb_re"], tb["lb_im"]]
    in_specs = ([pl.BlockSpec((t, c), lambda i: (0, 0)),
                 pl.BlockSpec((bn, ns), lambda i: (0, 0)),
                 pl.BlockSpec((bn, ns), lambda i: (0, 0))]
                + [_const_spec(a) for a in consts]
                + [pl.BlockSpec((None, 1, c), lambda i: (l, 0, 0)),
                   pl.BlockSpec((None, c, c), lambda i: (l, 0, 0), pipeline_mode=pl.Buffered(1)),
                   pl.BlockSpec((None, 1, c), lambda i: (l, 0, 0))])
    return pl.pallas_call(
        functools.partial(_s5_step_kernel, seq=seq, bn=bn, lane_chunk=lane_chunk),
        grid=(1,),
        in_specs=in_specs,
        out_specs=[pl.BlockSpec((t, c), lambda i: (0, 0)),
                   pl.BlockSpec((bn, ns), lambda i: (0, 0)),
                   pl.BlockSpec((bn, ns), lambda i: (0, 0))],
        out_shape=[jax.ShapeDtypeStruct((t, c), F32),
                   jax.ShapeDtypeStruct((bn, ns), F32),
                   jax.ShapeDtypeStruct((bn, ns), F32)],
        scratch_shapes=[pltpu.VMEM((t, ns), F32), pltpu.VMEM((t, ns), F32)],
        compiler_params=_params("arbitrary"),
        name="s5_step",
    )(proj, h0re, h0im, *consts,
      d_skip.reshape(d_skip.shape[0], 1, c), w_glu, b_glu.reshape(b_glu.shape[0], 1, c))


def _rg_tables(w_a, w_i):
    nh, hd, _ = w_a.shape
    hpc = V7X_MXU_DIM // hd
    expand = lambda w: _block_diag(w.reshape(nh // hpc, hpc * hd, hd), hpc)
    return expand(w_a), expand(w_i)


def _rg_gates(xc, wa_ref, wi_ref, ba_ref, bi_ref, lam_ref, a_ref, b_ref, rows):
    nck, kc, _ = wa_ref.shape
    lam = lam_ref[...]
    neg = -lam
    softplus = jnp.maximum(neg, 0.0) + jnp.log(1.0 + jnp.exp(-jnp.abs(neg)))
    for k in range(nck):
        cs = slice(k * kc, (k + 1) * kc)
        xk = xc[:, cs]
        r = _sigmoid(_dot(xk, wa_ref[k]) + ba_ref[:, cs])
        i = _sigmoid(_dot(xk, wi_ref[k]) + bi_ref[:, cs])
        log_a = (-RG_C * r) * softplus[:, cs]
        a_ref[rows, cs] = jnp.exp(log_a)
        th = jnp.tanh(log_a)
        b_ref[rows, cs] = jnp.sqrt(-2.0 * th / (1.0 - th)) * (i * xk)


def _rg_seq_kernel(x_ref, gy_ref, conv0_ref, h0_ref, cw_ref, cb_ref, wa_ref, wi_ref, ba_ref, bi_ref, lam_ref,
                   y_ref, hout_ref, xe_ref, a_ref, b_ref, c_ref, tail_ref):
    s = pl.program_id(1)
    sub = V7X_SUBLANES
    nconv = conv0_ref.shape[0]
    tm, c = x_ref.shape
    first_sublane = lax.broadcasted_iota(jnp.int32, (sub, c), 0) == 0
    last = sub - 1

    @pl.when(s == 0)
    def _():
        tail_ref[0:nconv, :] = conv0_ref[...]
        c_ref[...] = jnp.broadcast_to(h0_ref[...], c_ref.shape)

    for f in range(nconv):
        blk = x_ref[(SEG_LEN - nconv + f) * sub:(SEG_LEN - nconv + f + 1) * sub, :]
        xe_ref[f * sub:(f + 1) * sub, :] = jnp.where(first_sublane, tail_ref[f:f + 1, :], pltpu.roll(blk, 1, 0))
    for f in range(nconv):
        tail_ref[f:f + 1, :] = x_ref[(SEG_LEN - nconv + f) * sub + last:(SEG_LEN - nconv + f + 1) * sub, :]
    xe_ref[nconv * sub:nconv * sub + tm, :] = x_ref[...]
    acc = cw_ref[0:1, :] * xe_ref[0:tm, :]
    for k in range(1, nconv + 1):
        acc = acc + cw_ref[k:k + 1, :] * xe_ref[k * sub:k * sub + tm, :]
    xc = cb_ref[...] + acc
    _rg_gates(xc, wa_ref, wi_ref, ba_ref, bi_ref, lam_ref, a_ref, b_ref, slice(None))

    def local(j, st):
        h, p = st
        rows = _row_block(j)
        a = a_ref[rows, :]
        h = a * h + b_ref[rows, :]
        p = p * a
        b_ref[rows, :] = h
        a_ref[rows, :] = p
        return h, p

    e, p = lax.fori_loop(0, SEG_LEN, local, (jnp.zeros((sub, c), F32), jnp.ones((sub, c), F32)), unroll=SCAN_UNROLL)
    row = lax.broadcasted_iota(jnp.int32, (sub, c), 0)
    for k in (1, 2, 4):
        keep = row >= k
        e = jnp.where(keep, e + p * pltpu.roll(e, k, 0), e)
        p = jnp.where(keep, p * pltpu.roll(p, k, 0), p)
    carry = c_ref[...]
    g = e + p * carry
    h_in = jnp.where(first_sublane, carry, pltpu.roll(g, 1, 0))
    c_ref[...] = jnp.broadcast_to(g[last:last + 1], g.shape)

    def fixup(j, cc):
        rows = _row_block(j)
        y_ref[rows, :] = (b_ref[rows, :] + a_ref[rows, :] * h_in) * gy_ref[rows, :]
        return cc

    lax.fori_loop(0, SEG_LEN, fixup, 0, unroll=SCAN_UNROLL)

    @pl.when(s == pl.num_programs(1) - 1)
    def _():
        hout_ref[...] = c_ref[0:1, :]


def _rg_seq(proj, conv0, h0, conv_w, conv_b, wa, wi, b_a, b_i, lam, l, *, bn, seq):
    c = h0.shape[-1]
    tm = SEG_TILE
    n_s = seq // tm
    depth = conv_w.shape[0]
    vec = lambda a: a.reshape(depth, 1, c)
    y, hout = pl.pallas_call(
        _rg_seq_kernel,
        grid=(bn, n_s),
        in_specs=[pl.BlockSpec((tm, c), lambda b, s: (b * n_s + s, 1)),
                  pl.BlockSpec((tm, c), lambda b, s: (b * n_s + s, 2)),
                  pl.BlockSpec((None, CONV_W - 1, c), lambda b, s: (b, 0, 0)),
                  pl.BlockSpec((None, 1, c), lambda b, s: (b, 0, 0)),
                  pl.BlockSpec((None, CONV_W, c), lambda b, s: (l, 0, 0)),
                  pl.BlockSpec((None, 1, c), lambda b, s: (l, 0, 0)),
                  _const_spec(wa), _const_spec(wi),
                  pl.BlockSpec((None, 1, c), lambda b, s: (l, 0, 0)),
                  pl.BlockSpec((None, 1, c), lambda b, s: (l, 0, 0)),
                  pl.BlockSpec((None, 1, c), lambda b, s: (l, 0, 0))],
        out_specs=[pl.BlockSpec((tm, c), lambda b, s: (b * n_s + s, 0)),
                   pl.BlockSpec((None, 1, c), lambda b, s: (b, 0, 0))],
        out_shape=[jax.ShapeDtypeStruct((bn * seq, c), F32), jax.ShapeDtypeStruct((bn, 1, c), F32)],
        scratch_shapes=[pltpu.VMEM((tm + (CONV_W - 1) * V7X_SUBLANES, c), F32), pltpu.VMEM((tm, c), F32),
                        pltpu.VMEM((tm, c), F32), pltpu.VMEM((V7X_SUBLANES, c), F32),
                        pltpu.VMEM((V7X_SUBLANES, c), F32)],
        compiler_params=_params("arbitrary", "arbitrary"),
        name="rg_seq",
    )(proj, proj, conv0, h0.reshape(bn, 1, c), conv_w, vec(conv_b), wa, wi, vec(b_a), vec(b_i), vec(lam))
    return y, hout.reshape(bn, c)


def _rg_step_kernel(x_ref, gy_ref, conv0_ref, h0_ref, cw_ref, cb_ref, wa_ref, wi_ref, ba_ref, bi_ref, lam_ref,
                    y_ref, hout_ref, a_ref, b_ref, *, seq, bn):
    nconv = conv0_ref.shape[0]

    def xpad(i):
        if i < nconv:
            return conv0_ref[i]
        return x_ref[(i - nconv) * bn:(i - nconv + 1) * bn, :]

    for t in range(seq):
        acc = cw_ref[0:1, :] * xpad(t)
        for k in range(1, nconv + 1):
            acc = acc + cw_ref[k:k + 1, :] * xpad(t + k)
        xc = cb_ref[...] + acc
        _rg_gates(xc, wa_ref, wi_ref, ba_ref, bi_ref, lam_ref, a_ref, b_ref, slice(t * bn, (t + 1) * bn))
    h = h0_ref[...]
    for t in range(seq):
        rows = slice(t * bn, (t + 1) * bn)
        h = a_ref[rows, :] * h + b_ref[rows, :]
        y_ref[rows, :] = h * gy_ref[rows, :]
    hout_ref[...] = h


def _rg_step(proj, conv0_tm, h0, conv_w, conv_b, wa, wi, b_a, b_i, lam, l, *, bn, seq):
    c = h0.shape[-1]
    t = bn * seq
    depth = conv_w.shape[0]
    vec = lambda a: a.reshape(depth, 1, c)
    lspec = pl.BlockSpec((None, 1, c), lambda i: (l, 0, 0))
    return pl.pallas_call(
        functools.partial(_rg_step_kernel, seq=seq, bn=bn),
        grid=(1,),
        in_specs=[pl.BlockSpec((t, c), lambda i: (0, 1)),
                  pl.BlockSpec((t, c), lambda i: (0, 2)),
                  pl.BlockSpec(conv0_tm.shape, lambda i: (0, 0, 0)),
                  pl.BlockSpec((bn, c), lambda i: (0, 0)),
                  pl.BlockSpec((None, CONV_W, c), lambda i: (l, 0, 0)),
                  lspec, _const_spec(wa), _const_spec(wi), lspec, lspec, lspec],
        out_specs=[pl.BlockSpec((t, c), lambda i: (0, 0)), pl.BlockSpec((bn, c), lambda i: (0, 0))],
        out_shape=[jax.ShapeDtypeStruct((t, c), F32), jax.ShapeDtypeStruct((bn, c), F32)],
        scratch_shapes=[pltpu.VMEM((t, c), F32), pltpu.VMEM((t, c), F32)],
        compiler_params=_params("arbitrary"),
        name="rg_step",
    )(proj, proj, conv0_tm, h0, conv_w, vec(conv_b), wa, wi, vec(b_a), vec(b_i), vec(lam))


def _gate_merge_kernel(x_ref, sh_ref, sc_ref, g_ref, wgs_ref, wgr_ref, bgs_ref, bgr_ref, ys_ref, yr_ref,
                       ws_ref, wr_ref, o_ref, hn_ref):
    @pl.when(pl.program_id(1) == 0)
    def _():
        _norm_mod_rows(x_ref, g_ref, sc_ref, sh_ref, hn_ref)

    h = hn_ref[...]
    gate_s = _sigmoid(_dot(h, wgs_ref[...]) + bgs_ref[...])
    gate_r = _sigmoid(_dot(h, wgr_ref[...]) + bgr_ref[...])
    o_ref[...] = gate_s * _dot(ys_ref[...], ws_ref[...]) + gate_r * _dot(yr_ref[...], wr_ref[...])


def _gate_merge(x, mod, g, w_gate, b_gate, ys, yr, w_s, w_r, l, *, tm, tn, tiles_per_block):
    t, d = x.shape
    c = ys.shape[1]
    rows = mod.shape[2]
    off = d // tn
    depth = w_gate.shape[0]
    once = dict(pipeline_mode=pl.Buffered(1))
    return pl.pallas_call(
        _gate_merge_kernel,
        grid=(t // tm, d // tn),
        in_specs=[pl.BlockSpec((tm, d), lambda m, n: (m, 0), **once),
                  _mod_spec(0, rows, d, tiles_per_block, tm),
                  _mod_spec(1, rows, d, tiles_per_block, tm),
                  pl.BlockSpec((None, 1, d), lambda m, n: (l, 0, 0)),
                  pl.BlockSpec((None, d, tn), lambda m, n: (l, 0, n)),
                  pl.BlockSpec((None, d, tn), lambda m, n: (l, 0, n + off)),
                  pl.BlockSpec((None, 1, tn), lambda m, n: (l, 0, n)),
                  pl.BlockSpec((None, 1, tn), lambda m, n: (l, 0, n + off)),
                  pl.BlockSpec((tm, c), lambda m, n: (m, 0), **once),
                  pl.BlockSpec((tm, c), lambda m, n: (m, 0), **once),
                  pl.BlockSpec((None, c, tn), lambda m, n: (l, 0, n)),
                  pl.BlockSpec((None, c, tn), lambda m, n: (l, 0, n))],
        out_specs=pl.BlockSpec((tm, tn), lambda m, n: (m, n)),
        out_shape=jax.ShapeDtypeStruct((t, d), F32),
        scratch_shapes=[pltpu.VMEM((tm, d), F32)],
        compiler_params=_params("arbitrary", "arbitrary"),
        name="gate_merge",
    )(x, mod, mod, g.reshape(depth, 1, d), w_gate, w_gate, b_gate.reshape(depth, 1, 2 * d),
      b_gate.reshape(depth, 1, 2 * d), ys, yr, w_s, w_r)


def _linear_residual_kernel(a_ref, w_ref, x_ref, g_ref, o_ref, acc_ref):
    k = pl.program_id(2)

    @pl.when(k == 0)
    def _():
        acc_ref[...] = jnp.zeros_like(acc_ref)

    acc_ref[...] += _dot(a_ref[...], w_ref[...])

    @pl.when(k == pl.num_programs(2) - 1)
    def _():
        o_ref[...] = x_ref[...] + g_ref[...] * acc_ref[...]


def _linear_residual(a, w, l, x, mod, j_g, *, tm, tn, tk, tiles_per_block):
    t, kdim = a.shape
    d = w.shape[-1]
    rows = mod.shape[2]
    return pl.pallas_call(
        _linear_residual_kernel,
        grid=(t // tm, d // tn, kdim // tk),
        in_specs=[pl.BlockSpec((tm, tk), lambda m, n, k: (m, k)),
                  pl.BlockSpec((None, tk, tn), lambda m, n, k: (l, k, n)),
                  pl.BlockSpec((tm, tn), lambda m, n, k: (m, n)),
                  _mod_spec(j_g, rows, tn, tiles_per_block, tm, ncol_axis=0)],
        out_specs=pl.BlockSpec((tm, tn), lambda m, n, k: (m, n)),
        out_shape=jax.ShapeDtypeStruct((t, d), F32),
        scratch_shapes=[pltpu.VMEM((tm, tn), F32)],
        compiler_params=_params("arbitrary", "arbitrary", "arbitrary"),
        name="linear_residual",
    )(a, w, x, mod)


ROUTE_I1, ROUTE_I2, ROUTE_W1, ROUTE_W2, ROUTE_RANK1, ROUTE_RANK2 = range(6)
DMA_UNROLL = 8
MOE_ROW_TILE = 512


def _router_kernel(x_ref, sh_ref, sc_ref, g_ref, wr_ref, hn_ref, route_ref, cnt_ref, *, n_experts):
    m = pl.program_id(0)

    @pl.when(m == 0)
    def _():
        cnt_ref[...] = jnp.zeros_like(cnt_ref)

    _norm_mod_rows(x_ref, g_ref, sc_ref, sh_ref, hn_ref)
    logits = jnp.dot(hn_ref[...], wr_ref[...], preferred_element_type=F32, precision=lax.Precision.HIGHEST)
    tm, width = logits.shape
    lane = lax.broadcasted_iota(jnp.int32, logits.shape, 1).astype(F32)
    lg = jnp.where(lane < n_experts, logits, -jnp.inf)
    v1 = jnp.max(lg, axis=-1, keepdims=True)
    i1 = jnp.min(jnp.where(lg == v1, lane, float(width)), axis=-1, keepdims=True)
    lg2 = jnp.where(lane == i1, -jnp.inf, lg)
    v2 = jnp.max(lg2, axis=-1, keepdims=True)
    i2 = jnp.min(jnp.where(lg2 == v2, lane, float(width)), axis=-1, keepdims=True)
    e2 = jnp.exp(v2 - v1)
    den = 1.0 + e2
    sel1 = lane == i1
    sel2 = lane == i2
    picked = jnp.where(sel1 | sel2, 1.0, 0.0)
    r = lax.broadcasted_iota(jnp.int32, (tm, tm), 0)
    c = lax.broadcasted_iota(jnp.int32, (tm, tm), 1)
    before = jnp.where(c < r, 1.0, 0.0)
    seen = _dot(before, picked) + cnt_ref[...]
    rank1 = jnp.sum(jnp.where(sel1, seen, 0.0), axis=-1, keepdims=True)
    rank2 = jnp.sum(jnp.where(sel2, seen, 0.0), axis=-1, keepdims=True)
    cnt_ref[...] += jnp.sum(picked, axis=0, keepdims=True)
    cols = ((ROUTE_I1, i1), (ROUTE_I2, i2), (ROUTE_W1, 1.0 / den), (ROUTE_W2, e2 / den),
            (ROUTE_RANK1, rank1), (ROUTE_RANK2, rank2))
    route = jnp.zeros_like(logits)
    for j, v in cols:
        route = jnp.where(lane == j, v, route)
    route_ref[...] = route


def _router(x, mod, j_sh, g, w_router_pad, l, *, tm, tiles_per_block, n_experts):
    t, d = x.shape
    rows = mod.shape[2]
    width = w_router_pad.shape[-1]
    return pl.pallas_call(
        functools.partial(_router_kernel, n_experts=n_experts),
        grid=(t // tm,),
        in_specs=[pl.BlockSpec((tm, d), lambda m: (m, 0)),
                  _mod_spec(j_sh, rows, d, tiles_per_block, tm),
                  _mod_spec(j_sh + 1, rows, d, tiles_per_block, tm),
                  pl.BlockSpec((None, 1, d), lambda m: (l, 0, 0)),
                  pl.BlockSpec((d, width), lambda m: (0, 0))],
        out_specs=[pl.BlockSpec((tm, d), lambda m: (m, 0)), pl.BlockSpec((tm, width), lambda m: (m, 0)),
                   pl.BlockSpec((1, width), lambda m: (0, 0))],
        out_shape=[jax.ShapeDtypeStruct((t, d), F32), jax.ShapeDtypeStruct((t, width), F32),
                   jax.ShapeDtypeStruct((1, width), F32)],
        compiler_params=_params("arbitrary"),
        name="moe_router",
    )(x, mod, mod, g.reshape(g.shape[0], 1, d), w_router_pad)


def _dispatch_kernel(tok_ref, nu_ref, hn_ref, xs_ref, sem):
    tg = xs_ref.shape[0]
    m = pl.program_id(0)

    def row_copy(tok, i):
        return pltpu.make_async_copy(hn_ref.at[pl.ds(tok, 1)], xs_ref.at[pl.ds(i, 1)], sem.at[0])

    @pl.when(m < nu_ref[0])
    def _():
        lax.fori_loop(0, tg, lambda i, c: (row_copy(tok_ref[m * tg + i], i).start(), c)[1], 0, unroll=DMA_UNROLL)
        pltpu.make_async_copy(hn_ref.at[pl.ds(0, tg)], xs_ref, sem.at[0]).wait()

    @pl.when(m >= nu_ref[0])
    def _():
        xs_ref[...] = jnp.zeros_like(xs_ref)


def _dispatch(tok_of_row, n_used, hn, *, tg):
    rows = tok_of_row.shape[0]
    d = hn.shape[1]
    return pl.pallas_call(
        _dispatch_kernel,
        grid_spec=pltpu.PrefetchScalarGridSpec(
            num_scalar_prefetch=2, grid=(rows // tg,),
            in_specs=[pl.BlockSpec(memory_space=pl.ANY)],
            out_specs=pl.BlockSpec((tg, d), lambda m, tok, nu: (m, 0)),
            scratch_shapes=[pltpu.SemaphoreType.DMA((1,))]),
        out_shape=jax.ShapeDtypeStruct((rows, d), F32),
        compiler_params=_params("arbitrary"),
        name="moe_dispatch",
    )(tok_of_row, n_used, hn)


def _grouped_up_kernel(te_ref, nu_ref, xs_ref, w1_ref, w3_ref, o_ref):
    m = pl.program_id(1)

    @pl.when(m < nu_ref[0])
    def _():
        h = xs_ref[...]
        o_ref[...] = _silu(_dot(h, w1_ref[...])) * _dot(h, w3_ref[...])

    @pl.when(m >= nu_ref[0])
    def _():
        o_ref[...] = jnp.zeros_like(o_ref)


def _grouped_up(tile_expert, n_used, xs, w1, w3, lm, *, tg, tn):
    r, d = xs.shape
    f = w1.shape[-1]
    wspec = pl.BlockSpec((None, None, d, tn), lambda n, m, te, nu: (lm, te[m], 0, n))
    return pl.pallas_call(
        _grouped_up_kernel,
        grid_spec=pltpu.PrefetchScalarGridSpec(
            num_scalar_prefetch=2, grid=(f // tn, r // tg),
            in_specs=[pl.BlockSpec((tg, d), lambda n, m, te, nu: (m, 0)), wspec, wspec],
            out_specs=pl.BlockSpec((tg, tn), lambda n, m, te, nu: (m, n))),
        out_shape=jax.ShapeDtypeStruct((r, f), F32),
        compiler_params=_params("arbitrary", "arbitrary"),
        name="moe_up",
    )(tile_expert, n_used, xs, w1, w3)


def _grouped_down_kernel(te_ref, nu_ref, a_ref, w_ref, o_ref):
    m = pl.program_id(1)

    @pl.when(m < nu_ref[0])
    def _():
        o_ref[...] = _dot(a_ref[...], w_ref[...])

    @pl.when(m >= nu_ref[0])
    def _():
        o_ref[...] = jnp.zeros_like(o_ref)


def _grouped_down(tile_expert, n_used, act, w2, lm, *, tg, tn):
    r, f = act.shape
    d = w2.shape[-1]
    return pl.pallas_call(
        _grouped_down_kernel,
        grid_spec=pltpu.PrefetchScalarGridSpec(
            num_scalar_prefetch=2, grid=(d // tn, r // tg),
            in_specs=[pl.BlockSpec((tg, f), lambda n, m, te, nu: (m, 0)),
                      pl.BlockSpec((None, None, f, tn), lambda n, m, te, nu: (lm, te[m], 0, n))],
            out_specs=pl.BlockSpec((tg, tn), lambda n, m, te, nu: (m, n))),
        out_shape=jax.ShapeDtypeStruct((r, d), F32),
        compiler_params=_params("arbitrary", "arbitrary"),
        name="moe_down",
    )(tile_expert, n_used, act, w2)


def _combine_kernel(p1_ref, p2_ref, y_ref, route_ref, x_ref, g_ref, *rest, final_norm):
    if final_norm:
        gf_ref, o_ref, ya_ref, yb_ref, sem = rest
    else:
        o_ref, ya_ref, yb_ref, sem = rest
    tc = ya_ref.shape[0]
    base = pl.program_id(0) * tc

    def row_copy(row, dst_ref, i):
        return pltpu.make_async_copy(y_ref.at[pl.ds(row, 1)], dst_ref.at[pl.ds(i, 1)], sem.at[0])

    def body(i, c):
        row_copy(p1_ref[base + i], ya_ref, i).start()
        row_copy(p2_ref[base + i], yb_ref, i).start()
        return c

    lax.fori_loop(0, tc, body, 0, unroll=DMA_UNROLL)
    for dst_ref in (ya_ref, yb_ref):
        pltpu.make_async_copy(y_ref.at[pl.ds(0, tc)], dst_ref, sem.at[0]).wait()
    w1 = route_ref[:, ROUTE_W1:ROUTE_W1 + 1]
    w2 = route_ref[:, ROUTE_W2:ROUTE_W2 + 1]
    out = x_ref[...] + g_ref[...] * (w1 * ya_ref[...] + w2 * yb_ref[...])
    o_ref[...] = _rmsnorm(out, gf_ref[...]) if final_norm else out


def _combine(pos1, pos2, y, route, x, mod, j_g, *, tc, tiles_per_block, final_g=None):
    t, d = x.shape
    rows = mod.shape[2]
    width = route.shape[-1]
    in_specs = [pl.BlockSpec(memory_space=pl.ANY),
                pl.BlockSpec((tc, width), lambda m, p1, p2: (m, 0)),
                pl.BlockSpec((tc, d), lambda m, p1, p2: (m, 0)),
                _mod_spec(j_g, rows, d, tiles_per_block, tc)]
    args = [pos1, pos2, y, route, x, mod]
    if final_g is not None:
        in_specs.append(pl.BlockSpec((1, d), lambda m, p1, p2: (0, 0)))
        args.append(final_g.reshape(1, d))
    return pl.pallas_call(
        functools.partial(_combine_kernel, final_norm=final_g is not None),
        grid_spec=pltpu.PrefetchScalarGridSpec(
            num_scalar_prefetch=2, grid=(t // tc,),
            in_specs=in_specs,
            out_specs=pl.BlockSpec((tc, d), lambda m, p1, p2: (m, 0)),
            scratch_shapes=[pltpu.VMEM((tc, d), F32), pltpu.VMEM((tc, d), F32), pltpu.SemaphoreType.DMA((1,))]),
        out_shape=jax.ShapeDtypeStruct((t, d), F32),
        compiler_params=_params("arbitrary"),
        name="moe_combine",
    )(*args)


def _routed_moe(runs, l, g, w_router, w1, w3, w2, lm, *, tg, final_g=None):
    ne = w_router.shape[-1]
    w_r = jnp.pad(w_router[lm], ((0, 0), (0, V7X_LANES - ne)))
    routed = [_router(r["x"], r["mods"][l], 3, g, w_r, l, n_experts=ne, **r["tiled"](min(r["tm"], 512))) for r in runs]
    counts_each = [cnt[0, :ne].astype(jnp.int32) for _, _, cnt in routed]
    tiles_e = (sum(counts_each) + (tg - 1)) // tg
    tile_end = jnp.cumsum(tiles_e)
    row_start = (tile_end - tiles_e) * tg
    t_all = sum(r["x"].shape[0] for r in runs)
    n_tiles = -(-TOP_K * t_all // tg) + ne
    tile_expert = jnp.minimum(jnp.sum(jnp.arange(n_tiles)[:, None] >= tile_end[None, :], axis=1), ne - 1)
    tile_expert = tile_expert.astype(jnp.int32)
    n_used = tile_end[-1:].astype(jnp.int32)
    pos, toks, first_row, tok0 = [], [], row_start, 0
    for (_, route, _), c_e, r in zip(routed, counts_each, runs):
        col = lambda j: route[:, j].astype(jnp.int32)
        pos.append((first_row[col(ROUTE_I1)] + col(ROUTE_RANK1), first_row[col(ROUTE_I2)] + col(ROUTE_RANK2)))
        toks.append(tok0 + jnp.arange(route.shape[0], dtype=jnp.int32))
        first_row, tok0 = first_row + c_e, tok0 + route.shape[0]
    tok_of_row = jnp.zeros((n_tiles * tg,), jnp.int32).at[jnp.concatenate([q for pq in pos for q in pq])].set(
        jnp.concatenate([tk for tk in toks for _ in range(TOP_K)]), unique_indices=True)
    hn_all = jnp.concatenate([hn for hn, _, _ in routed], axis=0) if len(runs) > 1 else routed[0][0]
    xs = _dispatch(tok_of_row, n_used, hn_all, tg=tg)
    act = _grouped_up(tile_expert, n_used, xs, w1, w3, lm, tg=tg, tn=1024)
    y = _grouped_down(tile_expert, n_used, act, w2, lm, tg=tg, tn=1024)
    out = []
    for (p1, p2), (_, route, _), r in zip(pos, routed, runs):
        tc = min(r["tm"], 256)
        out.append(_combine(p1, p2, y, route, r["x"], r["mods"][l], 5, tc=tc,
                            tiles_per_block=r["tiled"](tc)["tiles_per_block"], final_g=final_g))
    return out


def _final_norm_kernel(x_ref, g_ref, o_ref):
    o_ref[...] = _rmsnorm(x_ref[...], g_ref[...])


def _final_norm(x, g, *, tm):
    t, d = x.shape
    return pl.pallas_call(
        _final_norm_kernel,
        grid=(t // tm,),
        in_specs=[pl.BlockSpec((tm, d), lambda m: (m, 0)), pl.BlockSpec((1, d), lambda m: (0, 0))],
        out_specs=pl.BlockSpec((tm, d), lambda m: (m, 0)),
        out_shape=jax.ShapeDtypeStruct((t, d), F32),
        compiler_params=_params("arbitrary"),
        name="final_norm",
    )(x, g.reshape(1, d))


def _make_run(x, mods, state, *, bn, seq, time_major, tm):
    def tiled(tile):
        return dict(tm=tile, tiles_per_block=1 if time_major else seq // tile)

    assert seq >= CONV_W - 1
    return dict(x=x, mods=mods, state=state, bn=bn, seq=seq, time_major=time_major, tm=tm, tiled=tiled,
                s5_re=[], s5_im=[], rg=[], conv=[])


def _mixer(r, p, l):
    x, mod, state, bn, seq = r["x"], r["mods"][l], r["state"], r["bn"], r["seq"]
    depth = p["w_in"].shape[0]
    c = p["s5_w_glu"].shape[-1]
    kw = r["tiled"](r["tm"])
    nconv = CONV_W - 1
    tb = p["s5_tables"][l]
    wa, wi = p["rg_tables"][l]
    proj = _normed_linear(x, mod, 0, p["norm_mix"], l, [p["w_in"]], l, mode="in", tn=512, plain_cols=2 * c, **kw)
    h0re = state["s5_re"][l].reshape(bn, -1)
    h0im = state["s5_im"][l].reshape(bn, -1)
    s5_args = (tb, p["s5_d"].reshape(depth, c), p["s5_w_glu"], p["s5_b_glu"], l)
    rg_args = (p["rg_conv_w"], p["rg_conv_b"], wa, wi, p["rg_b_a"].reshape(depth, c), p["rg_b_i"].reshape(depth, c),
               p["rg_lam"], l)
    if r["time_major"]:
        y_s5, hre, him = _s5_step(proj, h0re, h0im, *s5_args, bn=bn, seq=seq)
        y_rg, hrg = _rg_step(proj, state["conv"][l].transpose(1, 0, 2), state["rg"][l], *rg_args, bn=bn, seq=seq)
        r["conv"].append(proj[:, c:2 * c].reshape(seq, bn, c)[seq - nconv:].transpose(1, 0, 2))
    else:
        y_s5, hre, him = _s5_seq(proj, h0re, h0im, *s5_args, bn=bn, seq=seq)
        y_rg, hrg = _rg_seq(proj, state["conv"][l], state["rg"][l], *rg_args, bn=bn, seq=seq)
        tail_rows = [seq - SEG_TILE + (SEG_LEN - nconv + f) * V7X_SUBLANES + V7X_SUBLANES - 1 for f in range(nconv)]
        proj3 = proj.reshape(bn, seq, 3 * c)
        r["conv"].append(jnp.stack([proj3[:, row, c:2 * c] for row in tail_rows], axis=1))
    r["s5_re"].append(hre.reshape(state["s5_re"][l].shape))
    r["s5_im"].append(him.reshape(state["s5_im"][l].shape))
    r["rg"].append(hrg)
    merged = _gate_merge(x, mod, p["norm_mix"], p["w_gate"], p["b_gate"], y_s5, y_rg, p["w_br_s5"], p["w_br_rg"], l,
                         tn=256, **kw)
    r["x"] = _linear_residual(merged, p["w_out"], l, x, mod, 2, tn=1024, tk=1024, **kw)


def _dense_ffn(r, p, l):
    kw = r["tiled"](r["tm"])
    mod = r["mods"][l]
    act = _normed_linear(r["x"], mod, 3, p["norm_ffn"], l, [p["ffn_w1"], p["ffn_w3"]], l // 2, mode="glu", tn=512, **kw)
    r["x"] = _linear_residual(act, p["ffn_w2"], l // 2, r["x"], mod, 5, tn=1024, tk=1024, **kw)


def _forward(runs, p):
    depth = p["w_in"].shape[0]
    for l in range(depth):
        for r in runs:
            _mixer(r, p, l)
        if l % 2 == 0:
            for r in runs:
                _dense_ffn(r, p, l)
        else:
            final_g = p["norm_f"] if l == depth - 1 else None
            new_x = _routed_moe(runs, l, p["norm_ffn"], p["moe_router"], p["moe_w1"], p["moe_w3"], p["moe_w2"], l // 2,
                                tg=MOE_ROW_TILE, final_g=final_g)
            for r, x in zip(runs, new_x):
                r["x"] = x
    if depth % 2 == 0:
        return [r["x"] for r in runs]
    return [_final_norm(r["x"], p["norm_f"], tm=min(r["tm"], 256)) for r in runs]


def kernel(x_prompt, x_sample, c_prompt, c_sample, state_s5_re, state_s5_im, state_rglru, state_conv, norm_mix, norm_ffn, norm_f, w_ada, b_ada, w_in, s5_lam_re, s5_lam_im, s5_log_dt, s5_b_re, s5_b_im, s5_c_re, s5_c_im, s5_d, s5_w_glu, s5_b_glu, rg_conv_w, rg_conv_b, rg_w_a, rg_b_a, rg_w_i, rg_b_i, rg_lam, w_gate, b_gate, w_br_s5, w_br_rg, w_out, ffn_w1, ffn_w3, ffn_w2, moe_router, moe_w1, moe_w3, moe_w2):
    p = dict(norm_mix=norm_mix, norm_ffn=norm_ffn, norm_f=norm_f, w_in=w_in,
             s5_lam_re=s5_lam_re, s5_lam_im=s5_lam_im, s5_log_dt=s5_log_dt, s5_b_re=s5_b_re, s5_b_im=s5_b_im,
             s5_c_re=s5_c_re, s5_c_im=s5_c_im, s5_d=s5_d, s5_w_glu=s5_w_glu, s5_b_glu=s5_b_glu,
             rg_conv_w=rg_conv_w, rg_conv_b=rg_conv_b, rg_w_a=rg_w_a, rg_b_a=rg_b_a, rg_w_i=rg_w_i, rg_b_i=rg_b_i,
             rg_lam=rg_lam, w_gate=w_gate, b_gate=b_gate, w_br_s5=w_br_s5, w_br_rg=w_br_rg, w_out=w_out,
             ffn_w1=ffn_w1, ffn_w3=ffn_w3, ffn_w2=ffn_w2, moe_router=moe_router, moe_w1=moe_w1, moe_w3=moe_w3,
             moe_w2=moe_w2)
    depth = w_in.shape[0]
    p["s5_tables"] = [_s5_tables(s5_lam_re[l], s5_lam_im[l], s5_log_dt[l], s5_b_re[l], s5_b_im[l], s5_c_re[l],
                                 s5_c_im[l]) for l in range(depth)]
    p["rg_tables"] = [_rg_tables(rg_w_a[l], rg_w_i[l]) for l in range(depth)]
    bp, sp, d = x_prompt.shape
    bs, ss, _ = x_sample.shape
    g, pst = state_s5_re.shape[2:]
    c = state_rglru.shape[-1]

    n_c = bp + bs
    rows_c = -(-n_c // V7X_SUBLANES) * V7X_SUBLANES
    c_all = jnp.pad(jnp.concatenate([c_prompt, c_sample], axis=0), ((0, rows_c - n_c), (0, 0)))
    mod_all = _ada_mod(c_all, w_ada, b_ada)
    mods_p, mods_s = [], []
    for l in range(depth):
        mp = mod_all[l, :bp].reshape(bp, 6, 1, d).transpose(1, 0, 2, 3)
        ms = mod_all[l, bp:n_c].reshape(bs, 6, d).transpose(1, 0, 2)
        ms = jnp.tile(ms, (1, ss, 1)).reshape(6, 1, ss * bs, d)
        mods_p.append(mp)
        mods_s.append(ms)

    zero_state = dict(s5_re=jnp.zeros((depth, bp, g, pst), F32), s5_im=jnp.zeros((depth, bp, g, pst), F32),
                      rg=jnp.zeros((depth, bp, c), F32), conv=jnp.zeros((depth, bp, CONV_W - 1, c), x_prompt.dtype))
    seg_shape = (bp, sp // SEG_TILE, V7X_SUBLANES, SEG_LEN, d)
    x_p = x_prompt.reshape(seg_shape).transpose(0, 1, 3, 2, 4).reshape(bp * sp, d)
    run_p = _make_run(x_p, mods_p, zero_state, bn=bp, seq=sp, time_major=False, tm=min(PROMPT_TILE, sp))
    sample_state = dict(s5_re=state_s5_re, s5_im=state_s5_im, rg=state_rglru, conv=state_conv)
    x_s = x_sample.transpose(1, 0, 2).reshape(ss * bs, d)
    run_s = _make_run(x_s, mods_s, sample_state, bn=bs, seq=ss, time_major=True, tm=ss * bs)
    y_p, y_s = _forward([run_p, run_s], p)
    y_p = y_p.reshape(bp, sp // SEG_TILE, SEG_LEN, V7X_SUBLANES, d).transpose(0, 1, 3, 2, 4).reshape(bp, sp, d)
    y_s = y_s.reshape(ss, bs, d).transpose(1, 0, 2)
    new_state = lambda r: tuple(jnp.stack(r[k]) for k in ("s5_re", "s5_im", "rg", "conv"))
    return (y_p, y_s) + new_state(run_p) + new_state(run_s)
```

```python
import functools
import math

import jax
import jax.numpy as jnp
from jax import lax
from jax.experimental import pallas as pl
from jax.experimental.pallas import tpu as pltpu

F32 = jnp.float32
EPS = 1e-6
RG_C = 8.0
CONV_W = 4
S5_GROUP = 16
S5_STATE = 64
RG_HEAD_DIM = 64
TOP_K = 2

V7X_LANES = 128
V7X_SUBLANES = 8
V7X_MXU_DIM = 256
VMEM_LIMIT_BYTES = 56 * 1024 * 1024

GELU_C = math.sqrt(2.0 / math.pi)


def _gelu(x):
    return 0.5 * x * (1.0 + jnp.tanh(GELU_C * (x + 0.044715 * (x * x * x))))


def _sigmoid(x):
    return 1.0 / (1.0 + jnp.exp(-x))


def _silu(x):
    return x * _sigmoid(x)


def _dot(a, b):
    return jnp.dot(a, b, preferred_element_type=F32)


def _params(*sem):
    return pltpu.CompilerParams(dimension_semantics=sem, vmem_limit_bytes=VMEM_LIMIT_BYTES)


def _mod_spec(j, rows, width, tiles_per_block, tile, ncol_axis=None, tile_of=lambda m: m):
    col = (lambda idx: 0) if ncol_axis is None else (lambda idx: idx[ncol_axis])
    if rows == 1:
        return pl.BlockSpec((None, None, 1, width), lambda m, *idx: (j, tile_of(m) // tiles_per_block, 0, col(idx)))
    return pl.BlockSpec((None, None, tile, width), lambda m, *idx: (j, 0, tile_of(m), col(idx)))


ROW_CHUNK = 128
PROMPT_TILE = 1024
SEG_LEN = 32
SEG_TILE = SEG_LEN * V7X_SUBLANES
SCAN_UNROLL = True


def _rmsnorm(x, g):
    ms = jnp.mean(x * x, axis=-1, keepdims=True)
    return (x * lax.rsqrt(ms + EPS)) * g


def _norm_mod_rows(x_ref, g_ref, sc_ref, sh_ref, o_ref):
    tm = x_ref.shape[0]
    chunk = min(ROW_CHUNK, tm)
    per_row = sc_ref.shape[0] != 1

    def body(i, carry):
        rows = pl.ds(pl.multiple_of(i * chunk, chunk), chunk)
        sc = sc_ref[rows, :] if per_row else sc_ref[...]
        sh = sh_ref[rows, :] if per_row else sh_ref[...]
        o_ref[rows, :] = _rmsnorm(x_ref[rows, :], g_ref[...]) * (1.0 + sc) + sh
        return carry

    lax.fori_loop(0, tm // chunk, body, 0)


def _ada_kernel(c_ref, w_ref, b_ref, o_ref):
    o_ref[...] = _dot(_silu(c_ref[...]), w_ref[...]) + b_ref[...]


def _ada_mod(c_all, w_ada, b_ada, *, tn=1024):
    depth, d, n6 = w_ada.shape
    rows = c_all.shape[0]
    return pl.pallas_call(
        _ada_kernel,
        grid=(depth, n6 // tn),
        in_specs=[pl.BlockSpec((rows, d), lambda l, n: (0, 0)),
                  pl.BlockSpec((None, d, tn), lambda l, n: (l, 0, n)),
                  pl.BlockSpec((None, 1, tn), lambda l, n: (l, 0, n))],
        out_specs=pl.BlockSpec((None, rows, tn), lambda l, n: (l, 0, n)),
        out_shape=jax.ShapeDtypeStruct((depth, rows, n6), F32),
        compiler_params=_params("arbitrary", "arbitrary"),
        name="ada_mod",
    )(c_all, w_ada, b_ada.reshape(depth, 1, n6))


def _normed_linear_kernel(*refs, mode, n_plain):
    if mode == "glu":
        x_ref, sh_ref, sc_ref, g_ref, w1_ref, w3_ref, o_ref, hn_ref = refs
    else:
        x_ref, sh_ref, sc_ref, g_ref, w_ref, o_ref, hn_ref = refs
    n = pl.program_id(1)

    @pl.when(n == 0)
    def _():
        _norm_mod_rows(x_ref, g_ref, sc_ref, sh_ref, hn_ref)

    h = hn_ref[...]
    if mode == "glu":
        o_ref[...] = _silu(_dot(h, w1_ref[...])) * _dot(h, w3_ref[...])
    else:
        z = _dot(h, w_ref[...])

        @pl.when(n < n_plain)
        def _():
            o_ref[...] = z

        @pl.when(n >= n_plain)
        def _():
            o_ref[...] = _gelu(z)


def _normed_linear(x, mod, j_sh, g, l_g, ws, l, *, mode, tm, tn, tiles_per_block, plain_cols=0):
    t, d = x.shape
    n_out = ws[0].shape[-1]
    rows = mod.shape[2]
    w_spec = pl.BlockSpec((None, d, tn), lambda m, n: (l, 0, n))
    in_specs = [pl.BlockSpec((tm, d), lambda m, n: (m, 0)),
                _mod_spec(j_sh, rows, d, tiles_per_block, tm),
                _mod_spec(j_sh + 1, rows, d, tiles_per_block, tm),
                pl.BlockSpec((None, 1, d), lambda m, n: (l_g, 0, 0))]
    args = [x, mod, mod, g.reshape(g.shape[0], 1, d)]
    for w in ws:
        in_specs.append(w_spec)
        args.append(w)
    return pl.pallas_call(
        functools.partial(_normed_linear_kernel, mode=mode, n_plain=plain_cols // tn),
        grid=(t // tm, n_out // tn),
        in_specs=in_specs,
        out_specs=pl.BlockSpec((tm, tn), lambda m, n: (m, n)),
        out_shape=jax.ShapeDtypeStruct((t, n_out), F32),
        scratch_shapes=[pltpu.VMEM((tm, d), F32)],
        compiler_params=_params("arbitrary", "arbitrary"),
        name="normed_linear_" + mode,
    )(*args)


def _s5_tables(lam_re, lam_im, log_dt, b_re, b_im, c_re, c_im):
    g, p = lam_re.shape
    h = b_re.shape[-1]
    gpc = V7X_MXU_DIM // h
    nck = g // gpc
    dt = jnp.exp(log_dt)[:, None]
    mag = jnp.exp(lam_re * dt)
    lb_re = mag * jnp.cos(lam_im * dt)
    lb_im = mag * jnp.sin(lam_im * dt)
    den = lam_re * lam_re + lam_im * lam_im
    cf_re = ((lb_re - 1.0) * lam_re + lb_im * lam_im) / den
    cf_im = (lb_im * lam_re - (lb_re - 1.0) * lam_im) / den
    bb_re = cf_re[:, :, None] * b_re - cf_im[:, :, None] * b_im
    bb_im = cf_re[:, :, None] * b_im + cf_im[:, :, None] * b_re
    by_ghp = lambda m: m.transpose(0, 2, 1).reshape(nck, gpc * h, p)
    by_gph = lambda m: m.transpose(0, 2, 1).reshape(nck, gpc * p, h)

    def powers(steps):
        k = jnp.asarray(steps, F32)[:, None]
        zr = (lam_re * dt).reshape(1, g * p)
        zi = (lam_im * dt).reshape(1, g * p)
        return jnp.exp(k * zr) * jnp.cos(k * zi), jnp.exp(k * zr) * jnp.sin(k * zi)

    sub = V7X_SUBLANES
    pw_re, pw_im = powers(range(1, SEG_LEN + 1))
    sg_re, sg_im = powers(range(SEG_LEN, SEG_LEN * (sub + 1), SEG_LEN))
    row = jnp.arange(sub)[:, None]
    segtabs = []
    for k in (1, 2, 4):
        segtabs += [jnp.where(row >= k, sg_re[k - 1:k], 0.0), jnp.where(row >= k, sg_im[k - 1:k], 0.0)]
    segtabs += [sg_re, sg_im]
    return dict(wb_re=_block_diag(by_ghp(bb_re), gpc), wb_im=_block_diag(by_ghp(bb_im), gpc),
                wc_re=_block_diag(by_gph(c_re), gpc), wc_imneg=_block_diag(by_gph(-c_im), gpc),
                lam8=jnp.stack([jnp.broadcast_to(pw_re[0:1], (sub, g * p)),
                                jnp.broadcast_to(pw_im[0:1], (sub, g * p))]),
                segtabs=jnp.stack(segtabs),
                fix=jnp.stack([jnp.repeat(pw_re, sub, axis=0), jnp.repeat(pw_im, sub, axis=0)]),
                lb_re=pw_re[0:1], lb_im=pw_im[0:1])


def _block_diag(src, nblk):
    _, rows, cb = src.shape
    r = jnp.arange(rows)[:, None] // (rows // nblk)
    c = jnp.arange(nblk * cb)[None, :] // cb
    return jnp.where(r == c, jnp.tile(src, (1, 1, nblk)), 0.0)


def _s5_input_proj(u, wbre_ref, wbim_ref, bre_ref, bim_ref):
    nck, kc, nc = wbre_ref.shape
    for k in range(nck):
        uk = u[:, k * kc:(k + 1) * kc]
        bre_ref[:, k * nc:(k + 1) * nc] = _dot(uk, wbre_ref[k])
        bim_ref[:, k * nc:(k + 1) * nc] = _dot(uk, wbim_ref[k])


def _s5_output(u, bre_ref, bim_ref, wcre_ref, wcim_ref, d_ref, wglu_ref, bglu_ref, y_ref):
    nck, nc, kc = wcre_ref.shape
    for k in range(nck):
        yk = _dot(bre_ref[:, k * nc:(k + 1) * nc], wcre_ref[k]) + _dot(bim_ref[:, k * nc:(k + 1) * nc], wcim_ref[k])
        yk = yk + d_ref[:, k * kc:(k + 1) * kc] * u[:, k * kc:(k + 1) * kc]
        y_ref[:, k * kc:(k + 1) * kc] = _gelu(yk)
    yg = y_ref[...]
    y_ref[...] = yg * _sigmoid(_dot(yg, wglu_ref[...]) + bglu_ref[...])


def _row_block(j):
    return pl.ds(pl.multiple_of(j * V7X_SUBLANES, V7X_SUBLANES), V7X_SUBLANES)


def _s5_seq_kernel(u_ref, h0re_ref, h0im_ref, wbre_ref, wbim_ref, wcre_ref, wcim_ref, lam8_ref, seg_ref, fix_ref,
                   d_ref, wglu_ref, bglu_ref, y_ref, hre_ref, him_ref, bre_ref, bim_ref, cre_ref, cim_ref,
                   *, lane_chunk):
    s = pl.program_id(1)
    ns = cre_ref.shape[1]

    @pl.when(s == 0)
    def _():
        cre_ref[...] = jnp.broadcast_to(h0re_ref[...], cre_ref.shape)
        cim_ref[...] = jnp.broadcast_to(h0im_ref[...], cim_ref.shape)

    u = u_ref[...]
    _s5_input_proj(u, wbre_ref, wbim_ref, bre_ref, bim_ref)
    first_sublane = lax.broadcasted_iota(jnp.int32, (V7X_SUBLANES, lane_chunk), 0) == 0
    last = V7X_SUBLANES - 1

    for c in range(ns // lane_chunk):
        sl = slice(c * lane_chunk, (c + 1) * lane_chunk)
        lr, li = lam8_ref[0, :, sl], lam8_ref[1, :, sl]

        def local(j, st, sl=sl, lr=lr, li=li):
            sr, si = st
            rows = _row_block(j)
            sr, si = (lr * sr - li * si) + bre_ref[rows, sl], (lr * si + li * sr) + bim_ref[rows, sl]
            bre_ref[rows, sl] = sr
            bim_ref[rows, sl] = si
            return sr, si

        zero = jnp.zeros((V7X_SUBLANES, lane_chunk), F32)
        xr, xi = lax.fori_loop(0, SEG_LEN, local, (zero, zero), unroll=SCAN_UNROLL)
        for i, k in enumerate((1, 2, 4)):
            tr, ti = seg_ref[2 * i, :, sl], seg_ref[2 * i + 1, :, sl]
            pr, pi = pltpu.roll(xr, k, 0), pltpu.roll(xi, k, 0)
            xr, xi = xr + (tr * pr - ti * pi), xi + (tr * pi + ti * pr)
        cr, ci = cre_ref[:, sl], cim_ref[:, sl]
        tr, ti = seg_ref[6, :, sl], seg_ref[7, :, sl]
        gr = xr + (tr * cr - ti * ci)
        gi = xi + (tr * ci + ti * cr)
        h0r = jnp.where(first_sublane, cr, pltpu.roll(gr, 1, 0))
        h0i = jnp.where(first_sublane, ci, pltpu.roll(gi, 1, 0))
        cre_ref[:, sl] = jnp.broadcast_to(gr[last:last + 1], gr.shape)
        cim_ref[:, sl] = jnp.broadcast_to(gi[last:last + 1], gi.shape)

        def fixup(j, carry, sl=sl, h0r=h0r, h0i=h0i):
            rows = _row_block(j)
            fr, fi = fix_ref[0, rows, sl], fix_ref[1, rows, sl]
            bre_ref[rows, sl] += fr * h0r - fi * h0i
            bim_ref[rows, sl] += fr * h0i + fi * h0r
            return carry

        lax.fori_loop(0, SEG_LEN, fixup, 0, unroll=SCAN_UNROLL)

    _s5_output(u, bre_ref, bim_ref, wcre_ref, wcim_ref, d_ref, wglu_ref, bglu_ref, y_ref)

    @pl.when(s == pl.num_programs(1) - 1)
    def _():
        hre_ref[...] = cre_ref[0:1, :]
        him_ref[...] = cim_ref[0:1, :]


def _const_spec(a):
    nd = a.ndim
    return pl.BlockSpec(a.shape, lambda *_: (0,) * nd, pipeline_mode=pl.Buffered(1))


def _s5_seq(proj, h0re, h0im, tb, d_skip, w_glu, b_glu, l, *, bn, seq, lane_chunk=256):
    c = w_glu.shape[-1]
    ns = h0re.shape[-1]
    tm = SEG_TILE
    n_s = seq // tm
    consts = [tb["wb_re"], tb["wb_im"], tb["wc_re"], tb["wc_imneg"], tb["lam8"], tb["segtabs"], tb["fix"]]
    in_specs = ([pl.BlockSpec((tm, c), lambda b, s: (b * n_s + s, 0)),
                 pl.BlockSpec((None, 1, ns), lambda b, s: (b, 0, 0)),
                 pl.BlockSpec((None, 1, ns), lambda b, s: (b, 0, 0))]
                + [_const_spec(a) for a in consts]
                + [pl.BlockSpec((None, 1, c), lambda b, s: (l, 0, 0)),
                   pl.BlockSpec((None, c, c), lambda b, s: (l, 0, 0), pipeline_mode=pl.Buffered(1)),
                   pl.BlockSpec((None, 1, c), lambda b, s: (l, 0, 0))])
    y, hre, him = pl.pallas_call(
        functools.partial(_s5_seq_kernel, lane_chunk=lane_chunk),
        grid=(bn, n_s),
        in_specs=in_specs,
        out_specs=[pl.BlockSpec((tm, c), lambda b, s: (b * n_s + s, 0)),
                   pl.BlockSpec((None, 1, ns), lambda b, s: (b, 0, 0)),
                   pl.BlockSpec((None, 1, ns), lambda b, s: (b, 0, 0))],
        out_shape=[jax.ShapeDtypeStruct((bn * seq, c), F32),
                   jax.ShapeDtypeStruct((bn, 1, ns), F32),
                   jax.ShapeDtypeStruct((bn, 1, ns), F32)],
        scratch_shapes=[pltpu.VMEM((tm, ns), F32), pltpu.VMEM((tm, ns), F32),
                        pltpu.VMEM((V7X_SUBLANES, ns), F32), pltpu.VMEM((V7X_SUBLANES, ns), F32)],
        compiler_params=_params("arbitrary", "arbitrary"),
        name="s5_seq",
    )(proj, h0re.reshape(bn, 1, ns), h0im.reshape(bn, 1, ns), *consts,
      d_skip.reshape(d_skip.shape[0], 1, c), w_glu, b_glu.reshape(b_glu.shape[0], 1, c))
    return y, hre.reshape(bn, ns), him.reshape(bn, ns)


def _s5_step_kernel(u_ref, h0re_ref, h0im_ref, wbre_ref, wbim_ref, wcre_ref, wcim_ref, lbre_ref, lbim_ref, d_ref,
                    wglu_ref, bglu_ref, y_ref, hre_ref, him_ref, bre_ref, bim_ref, *, seq, bn, lane_chunk):
    u = u_ref[...]
    _s5_input_proj(u, wbre_ref, wbim_ref, bre_ref, bim_ref)
    ns = bre_ref.shape[1]
    for c in range(ns // lane_chunk):
        sl = slice(c * lane_chunk, (c + 1) * lane_chunk)
        lr = lbre_ref[:, sl]
        li = lbim_ref[:, sl]
        hr = h0re_ref[:, sl]
        hi = h0im_ref[:, sl]
        for t in range(seq):
            rows = slice(t * bn, (t + 1) * bn)
            hr, hi = (lr * hr - li * hi) + bre_ref[rows, sl], (lr * hi + li * hr) + bim_ref[rows, sl]
            bre_ref[rows, sl] = hr
            bim_ref[rows, sl] = hi
        hre_ref[:, sl] = hr
        him_ref[:, sl] = hi
    _s5_output(u, bre_ref, bim_ref, wcre_ref, wcim_ref, d_ref, wglu_ref, bglu_ref, y_ref)


def _s5_step(proj, h0re, h0im, tb, d_skip, w_glu, b_glu, l, *, bn, seq, lane_chunk=128):
    c = w_glu.shape[-1]
    ns = h0re.shape[-1]
    t = bn * seq
    consts = [tb["wb_re"], tb["wb_im"], tb["wc_re"], tb["wc_imneg"], tb["lb_re"], tb["lb_im"]]
    in_specs = ([pl.BlockSpec((t, c), lambda i: (0, 0)),
                 pl.BlockSpec((bn, ns), lambda i: (0, 0)),
                 pl.BlockSpec((bn, ns), lambda i: (0, 0))]
                + [_const_spec(a) for a in consts]
                + [pl.BlockSpec((None, 1, c), lambda i: (l, 0, 0)),
                   pl.BlockSpec((None, c, c), lambda i: (l, 0, 0), pipeline_mode=pl.Buffered(1)),
                   pl.BlockSpec((None, 1, c), lambda i: (l, 0, 0))])
    return pl.pallas_call(
        functools.partial(_s5_step_kernel, seq=seq, bn=bn, lane_chunk=lane_chunk),
        grid=(1,),
        in_specs=in_specs,
        out_specs=[pl.BlockSpec((t, c), lambda i: (0, 0)),
                   pl.BlockSpec((bn, ns), lambda i: (0, 0)),
                   pl.BlockSpec((bn, ns), lambda i: (0, 0))],
        out_shape=[jax.ShapeDtypeStruct((t, c), F32),
                   jax.ShapeDtypeStruct((bn, ns), F32),
                   jax.ShapeDtypeStruct((bn, ns), F32)],
        scratch_shapes=[pltpu.VMEM((t, ns), F32), pltpu.VMEM((t, ns), F32)],
        compiler_params=_params("arbitrary"),
        name="s5_step",
    )(proj, h0re, h0im, *consts,
      d_skip.reshape(d_skip.shape[0], 1, c), w_glu, b_glu.reshape(b_glu.shape[0], 1, c))


def _rg_tables(w_a, w_i):
    nh, hd, _ = w_a.shape
    hpc = V7X_MXU_DIM // hd
    expand = lambda w: _block_diag(w.reshape(nh // hpc, hpc * hd, hd), hpc)
    return expand(w_a), expand(w_i)


def _rg_gates(xc, wa_ref, wi_ref, ba_ref, bi_ref, lam_ref, a_ref, b_ref, rows):
    nck, kc, _ = wa_ref.shape
    lam = lam_ref[...]
    neg = -lam
    softplus = jnp.maximum(neg, 0.0) + jnp.log(1.0 + jnp.exp(-jnp.abs(neg)))
    for k in range(nck):
        cs = slice(k * kc, (k + 1) * kc)
        xk = xc[:, cs]
        r = _sigmoid(_dot(xk, wa_ref[k]) + ba_ref[:, cs])
        i = _sigmoid(_dot(xk, wi_ref[k]) + bi_ref[:, cs])
        log_a = (-RG_C * r) * softplus[:, cs]
        a_ref[rows, cs] = jnp.exp(log_a)
        th = jnp.tanh(log_a)
        b_ref[rows, cs] = jnp.sqrt(-2.0 * th / (1.0 - th)) * (i * xk)


def _rg_seq_kernel(x_ref, gy_ref, conv0_ref, h0_ref, cw_ref, cb_ref, wa_ref, wi_ref, ba_ref, bi_ref, lam_ref,
                   y_ref, hout_ref, xe_ref, a_ref, b_ref, c_ref, tail_ref):
    s = pl.program_id(1)
    sub = V7X_SUBLANES
    nconv = conv0_ref.shape[0]
    tm, c = x_ref.shape
    first_sublane = lax.broadcasted_iota(jnp.int32, (sub, c), 0) == 0
    last = sub - 1

    @pl.when(s == 0)
    def _():
        tail_ref[0:nconv, :] = conv0_ref[...]
        c_ref[...] = jnp.broadcast_to(h0_ref[...], c_ref.shape)

    for f in range(nconv):
        blk = x_ref[(SEG_LEN - nconv + f) * sub:(SEG_LEN - nconv + f + 1) * sub, :]
        xe_ref[f * sub:(f + 1) * sub, :] = jnp.where(first_sublane, tail_ref[f:f + 1, :], pltpu.roll(blk, 1, 0))
    for f in range(nconv):
        tail_ref[f:f + 1, :] = x_ref[(SEG_LEN - nconv + f) * sub + last:(SEG_LEN - nconv + f + 1) * sub, :]
    xe_ref[nconv * sub:nconv * sub + tm, :] = x_ref[...]
    acc = cw_ref[0:1, :] * xe_ref[0:tm, :]
    for k in range(1, nconv + 1):
        acc = acc + cw_ref[k:k + 1, :] * xe_ref[k * sub:k * sub + tm, :]
    xc = cb_ref[...] + acc
    _rg_gates(xc, wa_ref, wi_ref, ba_ref, bi_ref, lam_ref, a_ref, b_ref, slice(None))

    def local(j, st):
        h, p = st
        rows = _row_block(j)
        a = a_ref[rows, :]
        h = a * h + b_ref[rows, :]
        p = p * a
        b_ref[rows, :] = h
        a_ref[rows, :] = p
        return h, p

    e, p = lax.fori_loop(0, SEG_LEN, local, (jnp.zeros((sub, c), F32), jnp.ones((sub, c), F32)), unroll=SCAN_UNROLL)
    row = lax.broadcasted_iota(jnp.int32, (sub, c), 0)
    for k in (1, 2, 4):
        keep = row >= k
        e = jnp.where(keep, e + p * pltpu.roll(e, k, 0), e)
        p = jnp.where(keep, p * pltpu.roll(p, k, 0), p)
    carry = c_ref[...]
    g = e + p * carry
    h_in = jnp.where(first_sublane, carry, pltpu.roll(g, 1, 0))
    c_ref[...] = jnp.broadcast_to(g[last:last + 1], g.shape)

    def fixup(j, cc):
        rows = _row_block(j)
        y_ref[rows, :] = (b_ref[rows, :] + a_ref[rows, :] * h_in) * gy_ref[rows, :]
        return cc

    lax.fori_loop(0, SEG_LEN, fixup, 0, unroll=SCAN_UNROLL)

    @pl.when(s == pl.num_programs(1) - 1)
    def _():
        hout_ref[...] = c_ref[0:1, :]


def _rg_seq(proj, conv0, h0, conv_w, conv_b, wa, wi, b_a, b_i, lam, l, *, bn, seq):
    c = h0.shape[-1]
    tm = SEG_TILE
    n_s = seq // tm
    depth = conv_w.shape[0]
    vec = lambda a: a.reshape(depth, 1, c)
    y, hout = pl.pallas_call(
        _rg_seq_kernel,
        grid=(bn, n_s),
        in_specs=[pl.BlockSpec((tm, c), lambda b, s: (b * n_s + s, 1)),
                  pl.BlockSpec((tm, c), lambda b, s: (b * n_s + s, 2)),
                  pl.BlockSpec((None, CONV_W - 1, c), lambda b, s: (b, 0, 0)),
                  pl.BlockSpec((None, 1, c), lambda b, s: (b, 0, 0)),
                  pl.BlockSpec((None, CONV_W, c), lambda b, s: (l, 0, 0)),
                  pl.BlockSpec((None, 1, c), lambda b, s: (l, 0, 0)),
                  _const_spec(wa), _const_spec(wi),
                  pl.BlockSpec((None, 1, c), lambda b, s: (l, 0, 0)),
                  pl.BlockSpec((None, 1, c), lambda b, s: (l, 0, 0)),
                  pl.BlockSpec((None, 1, c), lambda b, s: (l, 0, 0))],
        out_specs=[pl.BlockSpec((tm, c), lambda b, s: (b * n_s + s, 0)),
                   pl.BlockSpec((None, 1, c), lambda b, s: (b, 0, 0))],
        out_shape=[jax.ShapeDtypeStruct((bn * seq, c), F32), jax.ShapeDtypeStruct((bn, 1, c), F32)],
        scratch_shapes=[pltpu.VMEM((tm + (CONV_W - 1) * V7X_SUBLANES, c), F32), pltpu.VMEM((tm, c), F32),
                        pltpu.VMEM((tm, c), F32), pltpu.VMEM((V7X_SUBLANES, c), F32),
                        pltpu.VMEM((V7X_SUBLANES, c), F32)],
        compiler_params=_params("arbitrary", "arbitrary"),
        name="rg_seq",
    )(proj, proj, conv0, h0.reshape(bn, 1, c), conv_w, vec(conv_b), wa, wi, vec(b_a), vec(b_i), vec(lam))
    return y, hout.reshape(bn, c)


def _rg_step_kernel(x_ref, gy_ref, conv0_ref, h0_ref, cw_ref, cb_ref, wa_ref, wi_ref, ba_ref, bi_ref, lam_ref,
                    y_ref, hout_ref, a_ref, b_ref, *, seq, bn):
    nconv = conv0_ref.shape[0]

    def xpad(i):
        if i < nconv:
            return conv0_ref[i]
        return x_ref[(i - nconv) * bn:(i - nconv + 1) * bn, :]

    for t in range(seq):
        acc = cw_ref[0:1, :] * xpad(t)
        for k in range(1, nconv + 1):
            acc = acc + cw_ref[k:k + 1, :] * xpad(t + k)
        xc = cb_ref[...] + acc
        _rg_gates(xc, wa_ref, wi_ref, ba_ref, bi_ref, lam_ref, a_ref, b_ref, slice(t * bn, (t + 1) * bn))
    h = h0_ref[...]
    for t in range(seq):
        rows = slice(t * bn, (t + 1) * bn)
        h = a_ref[rows, :] * h + b_ref[rows, :]
        y_ref[rows, :] = h * gy_ref[rows, :]
    hout_ref[...] = h


def _rg_step(proj, conv0_tm, h0, conv_w, conv_b, wa, wi, b_a, b_i, lam, l, *, bn, seq):
    c = h0.shape[-1]
    t = bn * seq
    depth = conv_w.shape[0]
    vec = lambda a: a.reshape(depth, 1, c)
    lspec = pl.BlockSpec((None, 1, c), lambda i: (l, 0, 0))
    return pl.pallas_call(
        functools.partial(_rg_step_kernel, seq=seq, bn=bn),
        grid=(1,),
        in_specs=[pl.BlockSpec((t, c), lambda i: (0, 1)),
                  pl.BlockSpec((t, c), lambda i: (0, 2)),
                  pl.BlockSpec(conv0_tm.shape, lambda i: (0, 0, 0)),
                  pl.BlockSpec((bn, c), lambda i: (0, 0)),
                  pl.BlockSpec((None, CONV_W, c), lambda i: (l, 0, 0)),
                  lspec, _const_spec(wa), _const_spec(wi), lspec, lspec, lspec],
        out_specs=[pl.BlockSpec((t, c), lambda i: (0, 0)), pl.BlockSpec((bn, c), lambda i: (0, 0))],
        out_shape=[jax.ShapeDtypeStruct((t, c), F32), jax.ShapeDtypeStruct((bn, c), F32)],
        scratch_shapes=[pltpu.VMEM((t, c), F32), pltpu.VMEM((t, c), F32)],
        compiler_params=_params("arbitrary"),
        name="rg_step",
    )(proj, proj, conv0_tm, h0, conv_w, vec(conv_b), wa, wi, vec(b_a), vec(b_i), vec(lam))


def _gate_merge_kernel(x_ref, sh_ref, sc_ref, g_ref, wgs_ref, wgr_ref, bgs_ref, bgr_ref, ys_ref, yr_ref,
                       ws_ref, wr_ref, o_ref, hn_ref):
    @pl.when(pl.program_id(1) == 0)
    def _():
        _norm_mod_rows(x_ref, g_ref, sc_ref, sh_ref, hn_ref)

    h = hn_ref[...]
    gate_s = _sigmoid(_dot(h, wgs_ref[...]) + bgs_ref[...])
    gate_r = _sigmoid(_dot(h, wgr_ref[...]) + bgr_ref[...])
    o_ref[...] = gate_s * _dot(ys_ref[...], ws_ref[...]) + gate_r * _dot(yr_ref[...], wr_ref[...])


def _gate_merge(x, mod, g, w_gate, b_gate, ys, yr, w_s, w_r, l, *, tm, tn, tiles_per_block):
    t, d = x.shape
    c = ys.shape[1]
    rows = mod.shape[2]
    off = d // tn
    depth = w_gate.shape[0]
    once = dict(pipeline_mode=pl.Buffered(1))
    return pl.pallas_call(
        _gate_merge_kernel,
        grid=(t // tm, d // tn),
        in_specs=[pl.BlockSpec((tm, d), lambda m, n: (m, 0), **once),
                  _mod_spec(0, rows, d, tiles_per_block, tm),
                  _mod_spec(1, rows, d, tiles_per_block, tm),
                  pl.BlockSpec((None, 1, d), lambda m, n: (l, 0, 0)),
                  pl.BlockSpec((None, d, tn), lambda m, n: (l, 0, n)),
                  pl.BlockSpec((None, d, tn), lambda m, n: (l, 0, n + off)),
                  pl.BlockSpec((None, 1, tn), lambda m, n: (l, 0, n)),
                  pl.BlockSpec((None, 1, tn), lambda m, n: (l, 0, n + off)),
                  pl.BlockSpec((tm, c), lambda m, n: (m, 0)),
                  pl.BlockSpec((tm, c), lambda m, n: (m, 0)),
                  pl.BlockSpec((None, c, tn), lambda m, n: (l, 0, n)),
                  pl.BlockSpec((None, c, tn), lambda m, n: (l, 0, n))],
        out_specs=pl.BlockSpec((tm, tn), lambda m, n: (m, n)),
        out_shape=jax.ShapeDtypeStruct((t, d), F32),
        scratch_shapes=[pltpu.VMEM((tm, d), F32)],
        compiler_params=_params("arbitrary", "arbitrary"),
        name="gate_merge",
    )(x, mod, mod, g.reshape(depth, 1, d), w_gate, w_gate, b_gate.reshape(depth, 1, 2 * d),
      b_gate.reshape(depth, 1, 2 * d), ys, yr, w_s, w_r)


def _linear_residual_kernel(a_ref, w_ref, x_ref, g_ref, o_ref, acc_ref):
    k = pl.program_id(2)

    @pl.when(k == 0)
    def _():
        acc_ref[...] = jnp.zeros_like(acc_ref)

    acc_ref[...] += _dot(a_ref[...], w_ref[...])

    @pl.when(k == pl.num_programs(2) - 1)
    def _():
        o_ref[...] = x_ref[...] + g_ref[...] * acc_ref[...]


def _linear_residual(a, w, l, x, mod, j_g, *, tm, tn, tk, tiles_per_block):
    t, kdim = a.shape
    d = w.shape[-1]
    rows = mod.shape[2]
    return pl.pallas_call(
        _linear_residual_kernel,
        grid=(t // tm, d // tn, kdim // tk),
        in_specs=[pl.BlockSpec((tm, tk), lambda m, n, k: (m, k)),
                  pl.BlockSpec((None, tk, tn), lambda m, n, k: (l, k, n)),
                  pl.BlockSpec((tm, tn), lambda m, n, k: (m, n)),
                  _mod_spec(j_g, rows, tn, tiles_per_block, tm, ncol_axis=0)],
        out_specs=pl.BlockSpec((tm, tn), lambda m, n, k: (m, n)),
        out_shape=jax.ShapeDtypeStruct((t, d), F32),
        scratch_shapes=[pltpu.VMEM((tm, tn), F32)],
        compiler_params=_params("arbitrary", "arbitrary", "arbitrary"),
        name="linear_residual",
    )(a, w, x, mod)


ROUTE_I1, ROUTE_I2, ROUTE_W1, ROUTE_W2, ROUTE_RANK1, ROUTE_RANK2 = range(6)
DMA_UNROLL = True
MOE_ROW_TILE = 512


def _router_kernel(x_ref, sh_ref, sc_ref, g_ref, wr_ref, *rest, n_experts, n_own):
    hn_ref, route_ref, cnt_ref = rest[-3:]
    m = pl.program_id(0)

    @pl.when(m == 0)
    def _():
        cnt_ref[...] = jnp.zeros_like(cnt_ref)

    @pl.when(m >= n_own)
    def _():
        hn_ref[...] = jnp.zeros_like(hn_ref)

    @pl.when(m < n_own)
    def _():
        _route_tile(x_ref, sh_ref, sc_ref, g_ref, wr_ref, hn_ref, route_ref, cnt_ref, n_experts)


def _route_tile(x_ref, sh_ref, sc_ref, g_ref, wr_ref, hn_ref, route_ref, cnt_ref, n_experts):
    _norm_mod_rows(x_ref, g_ref, sc_ref, sh_ref, hn_ref)
    logits = jnp.dot(hn_ref[...], wr_ref[...], preferred_element_type=F32, precision=lax.Precision.HIGHEST)
    tm, width = logits.shape
    lane = lax.broadcasted_iota(jnp.int32, logits.shape, 1).astype(F32)
    lg = jnp.where(lane < n_experts, logits, -jnp.inf)
    v1 = jnp.max(lg, axis=-1, keepdims=True)
    i1 = jnp.min(jnp.where(lg == v1, lane, float(width)), axis=-1, keepdims=True)
    lg2 = jnp.where(lane == i1, -jnp.inf, lg)
    v2 = jnp.max(lg2, axis=-1, keepdims=True)
    i2 = jnp.min(jnp.where(lg2 == v2, lane, float(width)), axis=-1, keepdims=True)
    e2 = jnp.exp(v2 - v1)
    den = 1.0 + e2
    sel1 = lane == i1
    sel2 = lane == i2
    picked = jnp.where(sel1 | sel2, 1.0, 0.0)
    r = lax.broadcasted_iota(jnp.int32, (tm, tm), 0)
    c = lax.broadcasted_iota(jnp.int32, (tm, tm), 1)
    before = jnp.where(c < r, 1.0, 0.0)
    seen = _dot(before, picked) + cnt_ref[...]
    rank1 = jnp.sum(jnp.where(sel1, seen, 0.0), axis=-1, keepdims=True)
    rank2 = jnp.sum(jnp.where(sel2, seen, 0.0), axis=-1, keepdims=True)
    cnt_ref[...] += jnp.sum(picked, axis=0, keepdims=True)
    cols = ((ROUTE_I1, i1), (ROUTE_I2, i2), (ROUTE_W1, 1.0 / den), (ROUTE_W2, e2 / den),
            (ROUTE_RANK1, rank1), (ROUTE_RANK2, rank2))
    route = jnp.zeros_like(logits)
    for j, v in cols:
        route = jnp.where(lane == j, v, route)
    route_ref[...] = route


def _router(x, mod, j_sh, g, w_router_pad, l, *, tm, tiles_per_block, n_experts, hn_rows, hn_row0, hn_buf):
    t, d = x.shape
    rows = mod.shape[2]
    width = w_router_pad.shape[-1]
    assert t % tm == 0 and hn_row0 % tm == 0 and (hn_rows - hn_row0) % tm == 0
    n_own = t // tm
    n_fill = (hn_rows - hn_row0) // tm - n_own if hn_buf is None else 0
    own = lambda m: jnp.minimum(m, n_own - 1)
    in_specs = [pl.BlockSpec((tm, d), lambda m: (own(m), 0)),
                _mod_spec(j_sh, rows, d, tiles_per_block, tm, tile_of=own),
                _mod_spec(j_sh + 1, rows, d, tiles_per_block, tm, tile_of=own),
                pl.BlockSpec((None, 1, d), lambda m: (l, 0, 0)),
                pl.BlockSpec((d, width), lambda m: (0, 0))]
    args = [x, mod, mod, g.reshape(g.shape[0], 1, d), w_router_pad]
    aliases = {}
    if hn_buf is not None:
        in_specs.append(pl.BlockSpec(memory_space=pl.ANY))
        args.append(hn_buf)
        aliases = {len(args) - 1: 0}
    return pl.pallas_call(
        functools.partial(_router_kernel, n_experts=n_experts, n_own=n_own),
        grid=(n_own + n_fill,),
        in_specs=in_specs,
        out_specs=[pl.BlockSpec((tm, d), lambda m: (m + hn_row0 // tm, 0)),
                   pl.BlockSpec((tm, width), lambda m: (own(m), 0)),
                   pl.BlockSpec((1, width), lambda m: (0, 0))],
        out_shape=[jax.ShapeDtypeStruct((hn_rows, d), F32), jax.ShapeDtypeStruct((t, width), F32),
                   jax.ShapeDtypeStruct((1, width), F32)],
        input_output_aliases=aliases,
        compiler_params=_params("arbitrary"),
        name="moe_router",
    )(*args)


def _dispatch_kernel(tok_ref, nu_ref, hn_ref, xs_ref, sem):
    tg = xs_ref.shape[0]
    m = pl.program_id(0)

    def row_copy(tok, i):
        return pltpu.make_async_copy(hn_ref.at[pl.ds(tok, 1)], xs_ref.at[pl.ds(i, 1)], sem.at[0])

    @pl.when(m < nu_ref[0])
    def _():
        lax.fori_loop(0, tg, lambda i, c: (row_copy(tok_ref[m * tg + i], i).start(), c)[1], 0, unroll=DMA_UNROLL)
        pltpu.make_async_copy(hn_ref.at[pl.ds(0, tg)], xs_ref, sem.at[0]).wait()

    @pl.when(m >= nu_ref[0])
    def _():
        xs_ref[...] = jnp.zeros_like(xs_ref)


def _dispatch(tok_of_row, n_used, hn, *, tg):
    rows = tok_of_row.shape[0]
    d = hn.shape[1]
    return pl.pallas_call(
        _dispatch_kernel,
        grid_spec=pltpu.PrefetchScalarGridSpec(
            num_scalar_prefetch=2, grid=(rows // tg,),
            in_specs=[pl.BlockSpec(memory_space=pl.ANY)],
            out_specs=pl.BlockSpec((tg, d), lambda m, tok, nu: (m, 0)),
            scratch_shapes=[pltpu.SemaphoreType.DMA((1,))]),
        out_shape=jax.ShapeDtypeStruct((rows, d), F32),
        compiler_params=_params("arbitrary"),
        name="moe_dispatch",
    )(tok_of_row, n_used, hn)


def _grouped_up_kernel(te_ref, nu_ref, xs_ref, w1_ref, w3_ref, o_ref):
    m = pl.program_id(1)

    @pl.when(m < nu_ref[0])
    def _():
        h = xs_ref[...]
        o_ref[...] = _silu(_dot(h, w1_ref[...])) * _dot(h, w3_ref[...])

    @pl.when(m >= nu_ref[0])
    def _():
        o_ref[...] = jnp.zeros_like(o_ref)


def _grouped_up(tile_expert, n_used, xs, w1, w3, lm, *, tg, tn):
    r, d = xs.shape
    f = w1.shape[-1]
    wspec = pl.BlockSpec((None, None, d, tn), lambda n, m, te, nu: (lm, te[m], 0, n))
    return pl.pallas_call(
        _grouped_up_kernel,
        grid_spec=pltpu.PrefetchScalarGridSpec(
            num_scalar_prefetch=2, grid=(f // tn, r // tg),
            in_specs=[pl.BlockSpec((tg, d), lambda n, m, te, nu: (m, 0)), wspec, wspec],
            out_specs=pl.BlockSpec((tg, tn), lambda n, m, te, nu: (m, n))),
        out_shape=jax.ShapeDtypeStruct((r, f), F32),
        compiler_params=_params("arbitrary", "arbitrary"),
        name="moe_up",
    )(tile_expert, n_used, xs, w1, w3)


def _grouped_down_kernel(te_ref, nu_ref, a_ref, w_ref, o_ref):
    m = pl.program_id(1)

    @pl.when(m < nu_ref[0])
    def _():
        o_ref[...] = _dot(a_ref[...], w_ref[...])

    @pl.when(m >= nu_ref[0])
    def _():
        o_ref[...] = jnp.zeros_like(o_ref)


def _grouped_down(tile_expert, n_used, act, w2, lm, *, tg, tn):
    r, f = act.shape
    d = w2.shape[-1]
    return pl.pallas_call(
        _grouped_down_kernel,
        grid_spec=pltpu.PrefetchScalarGridSpec(
            num_scalar_prefetch=2, grid=(d // tn, r // tg),
            in_specs=[pl.BlockSpec((tg, f), lambda n, m, te, nu: (m, 0)),
                      pl.BlockSpec((None, None, f, tn), lambda n, m, te, nu: (lm, te[m], 0, n))],
            out_specs=pl.BlockSpec((tg, tn), lambda n, m, te, nu: (m, n))),
        out_shape=jax.ShapeDtypeStruct((r, d), F32),
        compiler_params=_params("arbitrary", "arbitrary"),
        name="moe_down",
    )(tile_expert, n_used, act, w2)


def _combine_kernel(p1_ref, p2_ref, y_ref, route_ref, x_ref, g_ref, *rest, final_norm):
    if final_norm:
        gf_ref, o_ref, ya_ref, yb_ref, sem = rest
    else:
        o_ref, ya_ref, yb_ref, sem = rest
    tc = ya_ref.shape[0]
    base = pl.program_id(0) * tc

    def row_copy(row, dst_ref, i):
        return pltpu.make_async_copy(y_ref.at[pl.ds(row, 1)], dst_ref.at[pl.ds(i, 1)], sem.at[0])

    def body(i, c):
        row_copy(p1_ref[base + i], ya_ref, i).start()
        row_copy(p2_ref[base + i], yb_ref, i).start()
        return c

    lax.fori_loop(0, tc, body, 0, unroll=DMA_UNROLL)
    for dst_ref in (ya_ref, yb_ref):
        pltpu.make_async_copy(y_ref.at[pl.ds(0, tc)], dst_ref, sem.at[0]).wait()
    w1 = route_ref[:, ROUTE_W1:ROUTE_W1 + 1]
    w2 = route_ref[:, ROUTE_W2:ROUTE_W2 + 1]
    out = x_ref[...] + g_ref[...] * (w1 * ya_ref[...] + w2 * yb_ref[...])
    o_ref[...] = _rmsnorm(out, gf_ref[...]) if final_norm else out


def _combine(pos1, pos2, y, route, x, mod, j_g, *, tc, tiles_per_block, final_g=None):
    t, d = x.shape
    rows = mod.shape[2]
    width = route.shape[-1]
    in_specs = [pl.BlockSpec(memory_space=pl.ANY),
                pl.BlockSpec((tc, width), lambda m, p1, p2: (m, 0)),
                pl.BlockSpec((tc, d), lambda m, p1, p2: (m, 0)),
                _mod_spec(j_g, rows, d, tiles_per_block, tc)]
    args = [pos1, pos2, y, route, x, mod]
    if final_g is not None:
        in_specs.append(pl.BlockSpec((1, d), lambda m, p1, p2: (0, 0)))
        args.append(final_g.reshape(1, d))
    return pl.pallas_call(
        functools.partial(_combine_kernel, final_norm=final_g is not None),
        grid_spec=pltpu.PrefetchScalarGridSpec(
            num_scalar_prefetch=2, grid=(t // tc,),
            in_specs=in_specs,
            out_specs=pl.BlockSpec((tc, d), lambda m, p1, p2: (m, 0)),
            scratch_shapes=[pltpu.VMEM((tc, d), F32), pltpu.VMEM((tc, d), F32), pltpu.SemaphoreType.DMA((1,))]),
        out_shape=jax.ShapeDtypeStruct((t, d), F32),
        compiler_params=_params("arbitrary"),
        name="moe_combine",
    )(*args)


def _routed_moe(runs, l, g, w_router, w1, w3, w2, lm, *, tg, final_g=None):
    ne = w_router.shape[-1]
    w_r = jnp.pad(w_router[lm], ((0, 0), (0, V7X_LANES - ne)))
    t_all = sum(r["x"].shape[0] for r in runs)
    routed, hn_all, row0 = [], None, 0
    for r in runs:
        hn_all, route, cnt = _router(r["x"], r["mods"][l], 3, g, w_r, l, n_experts=ne, hn_rows=t_all, hn_row0=row0,
                                     hn_buf=hn_all, **r["tiled"](min(r["tm"], 512)))
        routed.append((hn_all, route, cnt))
        row0 += r["x"].shape[0]
    counts_each = [cnt[0, :ne].astype(jnp.int32) for _, _, cnt in routed]
    tiles_e = (sum(counts_each) + (tg - 1)) // tg
    tile_end = jnp.cumsum(tiles_e)
    row_start = (tile_end - tiles_e) * tg
    n_tiles = -(-TOP_K * t_all // tg) + ne
    tile_expert = jnp.minimum(jnp.sum(jnp.arange(n_tiles)[:, None] >= tile_end[None, :], axis=1), ne - 1)
    tile_expert = tile_expert.astype(jnp.int32)
    n_used = tile_end[-1:].astype(jnp.int32)
    pos, toks, first_row, tok0 = [], [], row_start, 0
    for (_, route, _), c_e, r in zip(routed, counts_each, runs):
        col = lambda j: route[:, j].astype(jnp.int32)
        pos.append((first_row[col(ROUTE_I1)] + col(ROUTE_RANK1), first_row[col(ROUTE_I2)] + col(ROUTE_RANK2)))
        toks.append(tok0 + jnp.arange(route.shape[0], dtype=jnp.int32))
        first_row, tok0 = first_row + c_e, tok0 + route.shape[0]
    tok_of_row = jnp.zeros((n_tiles * tg,), jnp.int32).at[jnp.concatenate([q for pq in pos for q in pq])].set(
        jnp.concatenate([tk for tk in toks for _ in range(TOP_K)]), unique_indices=True)
    xs = _dispatch(tok_of_row, n_used, hn_all, tg=tg)
    act = _grouped_up(tile_expert, n_used, xs, w1, w3, lm, tg=tg, tn=1024)
    y = _grouped_down(tile_expert, n_used, act, w2, lm, tg=tg, tn=1024)
    out = []
    for (p1, p2), (_, route, _), r in zip(pos, routed, runs):
        tc = min(r["tm"], 256)
        out.append(_combine(p1, p2, y, route, r["x"], r["mods"][l], 5, tc=tc,
                            tiles_per_block=r["tiled"](tc)["tiles_per_block"], final_g=final_g))
    return out


def _final_norm_kernel(x_ref, g_ref, o_ref):
    o_ref[...] = _rmsnorm(x_ref[...], g_ref[...])


def _final_norm(x, g, *, tm):
    t, d = x.shape
    return pl.pallas_call(
        _final_norm_kernel,
        grid=(t // tm,),
        in_specs=[pl.BlockSpec((tm, d), lambda m: (m, 0)), pl.BlockSpec((1, d), lambda m: (0, 0))],
        out_specs=pl.BlockSpec((tm, d), lambda m: (m, 0)),
        out_shape=jax.ShapeDtypeStruct((t, d), F32),
        compiler_params=_params("arbitrary"),
        name="final_norm",
    )(x, g.reshape(1, d))


def _make_run(x, mods, state, *, bn, seq, time_major, tm):
    def tiled(tile):
        return dict(tm=tile, tiles_per_block=1 if time_major else seq // tile)

    assert seq >= CONV_W - 1
    return dict(x=x, mods=mods, state=state, bn=bn, seq=seq, time_major=time_major, tm=tm, tiled=tiled,
                s5_re=[], s5_im=[], rg=[], conv=[])


def _mixer(r, p, l):
    x, mod, state, bn, seq = r["x"], r["mods"][l], r["state"], r["bn"], r["seq"]
    depth = p["w_in"].shape[0]
    c = p["s5_w_glu"].shape[-1]
    kw = r["tiled"](r["tm"])
    nconv = CONV_W - 1
    tb = p["s5_tables"][l]
    wa, wi = p["rg_tables"][l]
    proj = _normed_linear(x, mod, 0, p["norm_mix"], l, [p["w_in"]], l, mode="in", tn=512, plain_cols=2 * c, **kw)
    h0re = state["s5_re"][l].reshape(bn, -1)
    h0im = state["s5_im"][l].reshape(bn, -1)
    s5_args = (tb, p["s5_d"].reshape(depth, c), p["s5_w_glu"], p["s5_b_glu"], l)
    rg_args = (p["rg_conv_w"], p["rg_conv_b"], wa, wi, p["rg_b_a"].reshape(depth, c), p["rg_b_i"].reshape(depth, c),
               p["rg_lam"], l)
    if r["time_major"]:
        y_s5, hre, him = _s5_step(proj, h0re, h0im, *s5_args, bn=bn, seq=seq)
        y_rg, hrg = _rg_step(proj, state["conv"][l].transpose(1, 0, 2), state["rg"][l], *rg_args, bn=bn, seq=seq)
        r["conv"].append(proj[:, c:2 * c].reshape(seq, bn, c)[seq - nconv:].transpose(1, 0, 2))
    else:
        y_s5, hre, him = _s5_seq(proj, h0re, h0im, *s5_args, bn=bn, seq=seq)
        y_rg, hrg = _rg_seq(proj, state["conv"][l], state["rg"][l], *rg_args, bn=bn, seq=seq)
        tail_rows = [seq - SEG_TILE + (SEG_LEN - nconv + f) * V7X_SUBLANES + V7X_SUBLANES - 1 for f in range(nconv)]
        proj3 = proj.reshape(bn, seq, 3 * c)
        r["conv"].append(jnp.stack([proj3[:, row, c:2 * c] for row in tail_rows], axis=1))
    r["s5_re"].append(hre.reshape(state["s5_re"][l].shape))
    r["s5_im"].append(him.reshape(state["s5_im"][l].shape))
    r["rg"].append(hrg)
    merged = _gate_merge(x, mod, p["norm_mix"], p["w_gate"], p["b_gate"], y_s5, y_rg, p["w_br_s5"], p["w_br_rg"], l,
                         tn=256, **kw)
    r["x"] = _linear_residual(merged, p["w_out"], l, x, mod, 2, tn=1024, tk=1024, **kw)


def _dense_ffn(r, p, l):
    kw = r["tiled"](r["tm"])
    mod = r["mods"][l]
    act = _normed_linear(r["x"], mod, 3, p["norm_ffn"], l, [p["ffn_w1"], p["ffn_w3"]], l // 2, mode="glu", tn=512, **kw)
    r["x"] = _linear_residual(act, p["ffn_w2"], l // 2, r["x"], mod, 5, tn=1024, tk=1024, **kw)


def _forward(runs, p):
    depth = p["w_in"].shape[0]
    for l in range(depth):
        for r in runs:
            _mixer(r, p, l)
        if l % 2 == 0:
            for r in runs:
                _dense_ffn(r, p, l)
        else:
            final_g = p["norm_f"] if l == depth - 1 else None
            new_x = _routed_moe(runs, l, p["norm_ffn"], p["moe_router"], p["moe_w1"], p["moe_w3"], p["moe_w2"], l // 2,
                                tg=MOE_ROW_TILE, final_g=final_g)
            for r, x in zip(runs, new_x):
                r["x"] = x
    if depth % 2 == 0:
        return [r["x"] for r in runs]
    return [_final_norm(r["x"], p["norm_f"], tm=min(r["tm"], 256)) for r in runs]


def kernel(x_prompt, x_sample, c_prompt, c_sample, state_s5_re, state_s5_im, state_rglru, state_conv, norm_mix, norm_ffn, norm_f, w_ada, b_ada, w_in, s5_lam_re, s5_lam_im, s5_log_dt, s5_b_re, s5_b_im, s5_c_re, s5_c_im, s5_d, s5_w_glu, s5_b_glu, rg_conv_w, rg_conv_b, rg_w_a, rg_b_a, rg_w_i, rg_b_i, rg_lam, w_gate, b_gate, w_br_s5, w_br_rg, w_out, ffn_w1, ffn_w3, ffn_w2, moe_router, moe_w1, moe_w3, moe_w2):
    p = dict(norm_mix=norm_mix, norm_ffn=norm_ffn, norm_f=norm_f, w_in=w_in,
             s5_lam_re=s5_lam_re, s5_lam_im=s5_lam_im, s5_log_dt=s5_log_dt, s5_b_re=s5_b_re, s5_b_im=s5_b_im,
             s5_c_re=s5_c_re, s5_c_im=s5_c_im, s5_d=s5_d, s5_w_glu=s5_w_glu, s5_b_glu=s5_b_glu,
             rg_conv_w=rg_conv_w, rg_conv_b=rg_conv_b, rg_w_a=rg_w_a, rg_b_a=rg_b_a, rg_w_i=rg_w_i, rg_b_i=rg_b_i,
             rg_lam=rg_lam, w_gate=w_gate, b_gate=b_gate, w_br_s5=w_br_s5, w_br_rg=w_br_rg, w_out=w_out,
             ffn_w1=ffn_w1, ffn_w3=ffn_w3, ffn_w2=ffn_w2, moe_router=moe_router, moe_w1=moe_w1, moe_w3=moe_w3,
             moe_w2=moe_w2)
    depth = w_in.shape[0]
    p["s5_tables"] = [_s5_tables(s5_lam_re[l], s5_lam_im[l], s5_log_dt[l], s5_b_re[l], s5_b_im[l], s5_c_re[l],
                                 s5_c_im[l]) for l in range(depth)]
    p["rg_tables"] = [_rg_tables(rg_w_a[l], rg_w_i[l]) for l in range(depth)]
    bp, sp, d = x_prompt.shape
    bs, ss, _ = x_sample.shape
    g, pst = state_s5_re.shape[2:]
    c = state_rglru.shape[-1]

    n_c = bp + bs
    rows_c = -(-n_c // V7X_SUBLANES) * V7X_SUBLANES
    c_all = jnp.pad(jnp.concatenate([c_prompt, c_sample], axis=0), ((0, rows_c - n_c), (0, 0)))
    mod_all = _ada_mod(c_all, w_ada, b_ada)
    mods_p, mods_s = [], []
    for l in range(depth):
        mp = mod_all[l, :bp].reshape(bp, 6, 1, d).transpose(1, 0, 2, 3)
        ms = mod_all[l, bp:n_c].reshape(bs, 6, d).transpose(1, 0, 2)
        ms = jnp.tile(ms, (1, ss, 1)).reshape(6, 1, ss * bs, d)
        mods_p.append(mp)
        mods_s.append(ms)

    zero_state = dict(s5_re=jnp.zeros((depth, bp, g, pst), F32), s5_im=jnp.zeros((depth, bp, g, pst), F32),
                      rg=jnp.zeros((depth, bp, c), F32), conv=jnp.zeros((depth, bp, CONV_W - 1, c), x_prompt.dtype))
    seg_shape = (bp, sp // SEG_TILE, V7X_SUBLANES, SEG_LEN, d)
    x_p = x_prompt.reshape(seg_shape).transpose(0, 1, 3, 2, 4).reshape(bp * sp, d)
    run_p = _make_run(x_p, mods_p, zero_state, bn=bp, seq=sp, time_major=False, tm=min(PROMPT_TILE, sp))
    sample_state = dict(s5_re=state_s5_re, s5_im=state_s5_im, rg=state_rglru, conv=state_conv)
    x_s = x_sample.transpose(1, 0, 2).reshape(ss * bs, d)
    run_s = _make_run(x_s, mods_s, sample_state, bn=bs, seq=ss, time_major=True, tm=ss * bs)
    y_p, y_s = _forward([run_p, run_s], p)
    y_p = y_p.reshape(bp, sp // SEG_TILE, SEG_LEN, V7X_SUBLANES, d).transpose(0, 1, 3, 2, 4).reshape(bp, sp, d)
    y_s = y_s.reshape(ss, bs, d).transpose(1, 0, 2)
    new_state = lambda r: tuple(jnp.stack(r[k]) for k in ("s5_re", "s5_im", "rg", "conv"))
    return (y_p, y_s) + new_state(run_p) + new_state(run_s)
```

```python
import functools
import math

import jax
import jax.numpy as jnp
from jax import lax
from jax.experimental import pallas as pl
from jax.experimental.pallas import tpu as pltpu

F32 = jnp.float32
ACT_DTYPE = jnp.bfloat16
EPS = 1e-6
RG_C = 8.0
CONV_W = 4
S5_GROUP = 16
S5_STATE = 64
RG_HEAD_DIM = 64
TOP_K = 2

V7X_LANES = 128
V7X_SUBLANES = 8
V7X_MXU_DIM = 256
VMEM_LIMIT_BYTES = 56 * 1024 * 1024

GELU_C = math.sqrt(2.0 / math.pi)


def _gelu(x):
    return 0.5 * x * (1.0 + jnp.tanh(GELU_C * (x + 0.044715 * (x * x * x))))


def _sigmoid(x):
    return 1.0 / (1.0 + jnp.exp(-x))


def _silu(x):
    return x * _sigmoid(x)


def _dot(a, b):
    return jnp.dot(a, b, preferred_element_type=F32)


def _params(*sem):
    return pltpu.CompilerParams(dimension_semantics=sem, vmem_limit_bytes=VMEM_LIMIT_BYTES)


def _mod_spec(j, rows, width, tiles_per_block, tile, ncol_axis=None, tile_of=lambda m: m):
    col = (lambda idx: 0) if ncol_axis is None else (lambda idx: idx[ncol_axis])
    if rows == 1:
        return pl.BlockSpec((None, None, 1, width), lambda m, *idx: (j, tile_of(m) // tiles_per_block, 0, col(idx)))
    return pl.BlockSpec((None, None, tile, width), lambda m, *idx: (j, 0, tile_of(m), col(idx)))


ROW_CHUNK = 128
PROMPT_TILE = 1024
SEG_LEN = 32
SEG_TILE = SEG_LEN * V7X_SUBLANES
SCAN_UNROLL = True


def _rmsnorm(x, g):
    ms = jnp.mean(x * x, axis=-1, keepdims=True)
    return (x * lax.rsqrt(ms + EPS)) * g


def _norm_mod_rows(x_ref, g_ref, sc_ref, sh_ref, o_ref):
    tm = x_ref.shape[0]
    chunk = min(ROW_CHUNK, tm)
    per_row = sc_ref.shape[0] != 1

    def body(i, carry):
        rows = pl.ds(pl.multiple_of(i * chunk, chunk), chunk)
        sc = sc_ref[rows, :] if per_row else sc_ref[...]
        sh = sh_ref[rows, :] if per_row else sh_ref[...]
        o_ref[rows, :] = (_rmsnorm(x_ref[rows, :], g_ref[...]) * (1.0 + sc) + sh).astype(o_ref.dtype)
        return carry

    lax.fori_loop(0, tm // chunk, body, 0)


def _ada_kernel(c_ref, w_ref, b_ref, o_ref):
    o_ref[...] = _dot(_silu(c_ref[...]), w_ref[...]) + b_ref[...]


def _ada_mod(c_all, w_ada, b_ada, *, tn=1024):
    depth, d, n6 = w_ada.shape
    rows = c_all.shape[0]
    return pl.pallas_call(
        _ada_kernel,
        grid=(depth, n6 // tn),
        in_specs=[pl.BlockSpec((rows, d), lambda l, n: (0, 0)),
                  pl.BlockSpec((None, d, tn), lambda l, n: (l, 0, n)),
                  pl.BlockSpec((None, 1, tn), lambda l, n: (l, 0, n))],
        out_specs=pl.BlockSpec((None, rows, tn), lambda l, n: (l, 0, n)),
        out_shape=jax.ShapeDtypeStruct((depth, rows, n6), F32),
        compiler_params=_params("arbitrary", "arbitrary"),
        name="ada_mod",
    )(c_all, w_ada, b_ada.reshape(depth, 1, n6))


def _normed_linear_kernel(*refs, mode, n_plain):
    if mode == "glu":
        x_ref, sh_ref, sc_ref, g_ref, w1_ref, w3_ref, o_ref, hn_ref = refs
    else:
        x_ref, sh_ref, sc_ref, g_ref, w_ref, o_ref, hn_ref = refs
    n = pl.program_id(1)

    @pl.when(n == 0)
    def _():
        _norm_mod_rows(x_ref, g_ref, sc_ref, sh_ref, hn_ref)

    h = hn_ref[...]
    if mode == "glu":
        o_ref[...] = (_silu(_dot(h, w1_ref[...])) * _dot(h, w3_ref[...])).astype(o_ref.dtype)
    else:
        z = _dot(h, w_ref[...])

        @pl.when(n < n_plain)
        def _():
            o_ref[...] = z

        @pl.when(n >= n_plain)
        def _():
            o_ref[...] = _gelu(z)


def _normed_linear(x, mod, j_sh, g, l_g, ws, l, *, mode, tm, tn, tiles_per_block, plain_cols=0, out_dtype=F32):
    t, d = x.shape
    n_out = ws[0].shape[-1]
    rows = mod.shape[2]
    w_spec = pl.BlockSpec((None, d, tn), lambda m, n: (l, 0, n))
    in_specs = [pl.BlockSpec((tm, d), lambda m, n: (m, 0)),
                _mod_spec(j_sh, rows, d, tiles_per_block, tm),
                _mod_spec(j_sh + 1, rows, d, tiles_per_block, tm),
                pl.BlockSpec((None, 1, d), lambda m, n: (l_g, 0, 0))]
    args = [x, mod, mod, g.reshape(g.shape[0], 1, d)]
    for w in ws:
        in_specs.append(w_spec)
        args.append(w)
    return pl.pallas_call(
        functools.partial(_normed_linear_kernel, mode=mode, n_plain=plain_cols // tn),
        grid=(t // tm, n_out // tn),
        in_specs=in_specs,
        out_specs=pl.BlockSpec((tm, tn), lambda m, n: (m, n)),
        out_shape=jax.ShapeDtypeStruct((t, n_out), out_dtype),
        scratch_shapes=[pltpu.VMEM((tm, d), ws[0].dtype)],
        compiler_params=_params("arbitrary", "arbitrary"),
        name="normed_linear_" + mode,
    )(*args)


def _s5_tables(lam_re, lam_im, log_dt, b_re, b_im, c_re, c_im):
    g, p = lam_re.shape
    h = b_re.shape[-1]
    gpc = V7X_MXU_DIM // h
    nck = g // gpc
    dt = jnp.exp(log_dt)[:, None]
    mag = jnp.exp(lam_re * dt)
    lb_re = mag * jnp.cos(lam_im * dt)
    lb_im = mag * jnp.sin(lam_im * dt)
    den = lam_re * lam_re + lam_im * lam_im
    cf_re = ((lb_re - 1.0) * lam_re + lb_im * lam_im) / den
    cf_im = (lb_im * lam_re - (lb_re - 1.0) * lam_im) / den
    bb_re = cf_re[:, :, None] * b_re - cf_im[:, :, None] * b_im
    bb_im = cf_re[:, :, None] * b_im + cf_im[:, :, None] * b_re
    by_ghp = lambda m: m.transpose(0, 2, 1).reshape(nck, gpc * h, p)
    by_gph = lambda m: m.transpose(0, 2, 1).reshape(nck, gpc * p, h)

    def powers(steps):
        k = jnp.asarray(steps, F32)[:, None]
        zr = (lam_re * dt).reshape(1, g * p)
        zi = (lam_im * dt).reshape(1, g * p)
        return jnp.exp(k * zr) * jnp.cos(k * zi), jnp.exp(k * zr) * jnp.sin(k * zi)

    sub = V7X_SUBLANES
    pw_re, pw_im = powers(range(1, SEG_LEN + 1))
    sg_re, sg_im = powers(range(SEG_LEN, SEG_LEN * (sub + 1), SEG_LEN))
    row = jnp.arange(sub)[:, None]
    segtabs = []
    for k in (1, 2, 4):
        segtabs += [jnp.where(row >= k, sg_re[k - 1:k], 0.0), jnp.where(row >= k, sg_im[k - 1:k], 0.0)]
    segtabs += [sg_re, sg_im]
    return dict(wb_re=_block_diag(by_ghp(bb_re), gpc), wb_im=_block_diag(by_ghp(bb_im), gpc),
                wc_re=_block_diag(by_gph(c_re), gpc), wc_imneg=_block_diag(by_gph(-c_im), gpc),
                lam8=jnp.stack([jnp.broadcast_to(pw_re[0:1], (sub, g * p)),
                                jnp.broadcast_to(pw_im[0:1], (sub, g * p))]),
                segtabs=jnp.stack(segtabs),
                fix=jnp.stack([jnp.repeat(pw_re, sub, axis=0), jnp.repeat(pw_im, sub, axis=0)]),
                lb_re=pw_re[0:1], lb_im=pw_im[0:1])


def _block_diag(src, nblk):
    _, rows, cb = src.shape
    r = jnp.arange(rows)[:, None] // (rows // nblk)
    c = jnp.arange(nblk * cb)[None, :] // cb
    return jnp.where(r == c, jnp.tile(src, (1, 1, nblk)), 0.0)


def _s5_input_proj(u, wbre_ref, wbim_ref, bre_ref, bim_ref):
    nck, kc, nc = wbre_ref.shape
    for k in range(nck):
        uk = u[:, k * kc:(k + 1) * kc]
        bre_ref[:, k * nc:(k + 1) * nc] = _dot(uk, wbre_ref[k])
        bim_ref[:, k * nc:(k + 1) * nc] = _dot(uk, wbim_ref[k])


def _s5_output(u, bre_ref, bim_ref, wcre_ref, wcim_ref, d_ref, wglu_ref, bglu_ref, y_ref):
    nck, nc, kc = wcre_ref.shape
    for k in range(nck):
        yk = _dot(bre_ref[:, k * nc:(k + 1) * nc], wcre_ref[k]) + _dot(bim_ref[:, k * nc:(k + 1) * nc], wcim_ref[k])
        yk = yk + d_ref[:, k * kc:(k + 1) * kc] * u[:, k * kc:(k + 1) * kc]
        bre_ref[:, k * kc:(k + 1) * kc] = _gelu(yk)
    yg = bre_ref[:, 0:nck * kc]
    y_ref[...] = (yg * _sigmoid(_dot(yg, wglu_ref[...]) + bglu_ref[...])).astype(y_ref.dtype)


def _row_block(j):
    return pl.ds(pl.multiple_of(j * V7X_SUBLANES, V7X_SUBLANES), V7X_SUBLANES)


def _s5_seq_kernel(u_ref, h0re_ref, h0im_ref, wbre_ref, wbim_ref, wcre_ref, wcim_ref, lam8_ref, seg_ref, fix_ref,
                   d_ref, wglu_ref, bglu_ref, y_ref, hre_ref, him_ref, bre_ref, bim_ref, cre_ref, cim_ref,
                   *, lane_chunk):
    s = pl.program_id(1)
    ns = cre_ref.shape[1]

    @pl.when(s == 0)
    def _():
        cre_ref[...] = jnp.broadcast_to(h0re_ref[...], cre_ref.shape)
        cim_ref[...] = jnp.broadcast_to(h0im_ref[...], cim_ref.shape)

    u = u_ref[...]
    _s5_input_proj(u, wbre_ref, wbim_ref, bre_ref, bim_ref)
    first_sublane = lax.broadcasted_iota(jnp.int32, (V7X_SUBLANES, lane_chunk), 0) == 0
    last = V7X_SUBLANES - 1

    for c in range(ns // lane_chunk):
        sl = slice(c * lane_chunk, (c + 1) * lane_chunk)
        lr, li = lam8_ref[0, :, sl], lam8_ref[1, :, sl]

        def local(j, st, sl=sl, lr=lr, li=li):
            sr, si = st
            rows = _row_block(j)
            sr, si = (lr * sr - li * si) + bre_ref[rows, sl], (lr * si + li * sr) + bim_ref[rows, sl]
            bre_ref[rows, sl] = sr
            bim_ref[rows, sl] = si
            return sr, si

        zero = jnp.zeros((V7X_SUBLANES, lane_chunk), F32)
        xr, xi = lax.fori_loop(0, SEG_LEN, local, (zero, zero), unroll=SCAN_UNROLL)
        for i, k in enumerate((1, 2, 4)):
            tr, ti = seg_ref[2 * i, :, sl], seg_ref[2 * i + 1, :, sl]
            pr, pi = pltpu.roll(xr, k, 0), pltpu.roll(xi, k, 0)
            xr, xi = xr + (tr * pr - ti * pi), xi + (tr * pi + ti * pr)
        cr, ci = cre_ref[:, sl], cim_ref[:, sl]
        tr, ti = seg_ref[6, :, sl], seg_ref[7, :, sl]
        gr = xr + (tr * cr - ti * ci)
        gi = xi + (tr * ci + ti * cr)
        h0r = jnp.where(first_sublane, cr, pltpu.roll(gr, 1, 0))
        h0i = jnp.where(first_sublane, ci, pltpu.roll(gi, 1, 0))
        cre_ref[:, sl] = jnp.broadcast_to(gr[last:last + 1], gr.shape)
        cim_ref[:, sl] = jnp.broadcast_to(gi[last:last + 1], gi.shape)

        def fixup(j, carry, sl=sl, h0r=h0r, h0i=h0i):
            rows = _row_block(j)
            fr, fi = fix_ref[0, rows, sl], fix_ref[1, rows, sl]
            bre_ref[rows, sl] += fr * h0r - fi * h0i
            bim_ref[rows, sl] += fr * h0i + fi * h0r
            return carry

        lax.fori_loop(0, SEG_LEN, fixup, 0, unroll=SCAN_UNROLL)

    _s5_output(u, bre_ref, bim_ref, wcre_ref, wcim_ref, d_ref, wglu_ref, bglu_ref, y_ref)

    @pl.when(s == pl.num_programs(1) - 1)
    def _():
        hre_ref[...] = cre_ref[0:1, :]
        him_ref[...] = cim_ref[0:1, :]


def _const_spec(a):
    nd = a.ndim
    return pl.BlockSpec(a.shape, lambda *_: (0,) * nd, pipeline_mode=pl.Buffered(1))


def _s5_seq(proj, h0re, h0im, tb, d_skip, w_glu, b_glu, l, *, bn, seq, lane_chunk=256):
    c = w_glu.shape[-1]
    ns = h0re.shape[-1]
    tm = SEG_TILE
    n_s = seq // tm
    consts = [tb["wb_re"], tb["wb_im"], tb["wc_re"], tb["wc_imneg"], tb["lam8"], tb["segtabs"], tb["fix"]]
    in_specs = ([pl.BlockSpec((tm, c), lambda b, s: (b * n_s + s, 0)),
                 pl.BlockSpec((None, 1, ns), lambda b, s: (b, 0, 0)),
                 pl.BlockSpec((None, 1, ns), lambda b, s: (b, 0, 0))]
                + [_const_spec(a) for a in consts]
                + [pl.BlockSpec((None, 1, c), lambda b, s: (l, 0, 0)),
                   pl.BlockSpec((None, c, c), lambda b, s: (l, 0, 0), pipeline_mode=pl.Buffered(1)),
                   pl.BlockSpec((None, 1, c), lambda b, s: (l, 0, 0))])
    y, hre, him = pl.pallas_call(
        functools.partial(_s5_seq_kernel, lane_chunk=lane_chunk),
        grid=(bn, n_s),
        in_specs=in_specs,
        out_specs=[pl.BlockSpec((tm, c), lambda b, s: (b * n_s + s, 0)),
                   pl.BlockSpec((None, 1, ns), lambda b, s: (b, 0, 0)),
                   pl.BlockSpec((None, 1, ns), lambda b, s: (b, 0, 0))],
        out_shape=[jax.ShapeDtypeStruct((bn * seq, c), F32),
                   jax.ShapeDtypeStruct((bn, 1, ns), F32),
                   jax.ShapeDtypeStruct((bn, 1, ns), F32)],
        scratch_shapes=[pltpu.VMEM((tm, ns), F32), pltpu.VMEM((tm, ns), F32),
                        pltpu.VMEM((V7X_SUBLANES, ns), F32), pltpu.VMEM((V7X_SUBLANES, ns), F32)],
        compiler_params=_params("arbitrary", "arbitrary"),
        name="s5_seq",
    )(proj, h0re.reshape(bn, 1, ns), h0im.reshape(bn, 1, ns), *consts,
      d_skip.reshape(d_skip.shape[0], 1, c), w_glu, b_glu.reshape(b_glu.shape[0], 1, c))
    return y, hre.reshape(bn, ns), him.reshape(bn, ns)


def _s5_step_kernel(u_ref, h0re_ref, h0im_ref, wbre_ref, wbim_ref, wcre_ref, wcim_ref, lbre_ref, lbim_ref, d_ref,
                    wglu_ref, bglu_ref, y_ref, hre_ref, him_ref, bre_ref, bim_ref, *, seq, bn, lane_chunk):
    u = u_ref[...]
    _s5_input_proj(u, wbre_ref, wbim_ref, bre_ref, bim_ref)
    ns = bre_ref.shape[1]
    for c in range(ns // lane_chunk):
        sl = slice(c * lane_chunk, (c + 1) * lane_chunk)
        lr = lbre_ref[:, sl]
        li = lbim_ref[:, sl]
        hr = h0re_ref[:, sl]
        hi = h0im_ref[:, sl]
        for t in range(seq):
            rows = slice(t * bn, (t + 1) * bn)
            hr, hi = (lr * hr - li * hi) + bre_ref[rows, sl], (lr * hi + li * hr) + bim_ref[rows, sl]
            bre_ref[rows, sl] = hr
            bim_ref[rows, sl] = hi
        hre_ref[:, sl] = hr
        him_ref[:, sl] = hi
    _s5_output(u, bre_ref, bim_ref, wcre_ref, wcim_ref, d_ref, wglu_ref, bglu_ref, y_ref)


def _s5_step(proj, h0re, h0im, tb, d_skip, w_glu, b_glu, l, *, bn, seq, lane_chunk=128):
    c = w_glu.shape[-1]
    ns = h0re.shape[-1]
    t = bn * seq
    consts = [tb["wb_re"], tb["wb_im"], tb["wc_re"], tb["wc_imneg"], tb["lb_re"], tb["lb_im"]]
    in_specs = ([pl.BlockSpec((t, c), lambda i: (0, 0)),
                 pl.BlockSpec((bn, ns), lambda i: (0, 0)),
                 pl.BlockSpec((bn, ns), lambda i: (0, 0))]
                + [_const_spec(a) for a in consts]
                + [pl.BlockSpec((None, 1, c), lambda i: (l, 0, 0)),
                   pl.BlockSpec((None, c, c), lambda i: (l, 0, 0), pipeline_mode=pl.Buffered(1)),
                   pl.BlockSpec((None, 1, c), lambda i: (l, 0, 0))])
    return pl.pallas_call(
        functools.partial(_s5_step_kernel, seq=seq, bn=bn, lane_chunk=lane_chunk),
        grid=(1,),
        in_specs=in_specs,
        out_specs=[pl.BlockSpec((t, c), lambda i: (0, 0)),
                   pl.BlockSpec((bn, ns), lambda i: (0, 0)),
                   pl.BlockSpec((bn, ns), lambda i: (0, 0))],
        out_shape=[jax.ShapeDtypeStruct((t, c), F32),
                   jax.ShapeDtypeStruct((bn, ns), F32),
                   jax.ShapeDtypeStruct((bn, ns), F32)],
        scratch_shapes=[pltpu.VMEM((t, ns), F32), pltpu.VMEM((t, ns), F32)],
        compiler_params=_params("arbitrary"),
        name="s5_step",
    )(proj, h0re, h0im, *consts,
      d_skip.reshape(d_skip.shape[0], 1, c), w_glu, b_glu.reshape(b_glu.shape[0], 1, c))


def _rg_tables(w_a, w_i):
    nh, hd, _ = w_a.shape
    hpc = V7X_MXU_DIM // hd
    expand = lambda w: _block_diag(w.reshape(nh // hpc, hpc * hd, hd), hpc)
    return expand(w_a), expand(w_i)


def _rg_gates(xc, wa_ref, wi_ref, ba_ref, bi_ref, lam_ref, a_ref, b_ref, rows):
    nck, kc, _ = wa_ref.shape
    lam = lam_ref[...]
    neg = -lam
    softplus = jnp.maximum(neg, 0.0) + jnp.log(1.0 + jnp.exp(-jnp.abs(neg)))
    for k in range(nck):
        cs = slice(k * kc, (k + 1) * kc)
        xk = xc[:, cs]
        r = _sigmoid(_dot(xk, wa_ref[k]) + ba_ref[:, cs])
        i = _sigmoid(_dot(xk, wi_ref[k]) + bi_ref[:, cs])
        log_a = (-RG_C * r) * softplus[:, cs]
        a_ref[rows, cs] = jnp.exp(log_a)
        th = jnp.tanh(log_a)
        b_ref[rows, cs] = jnp.sqrt(-2.0 * th / (1.0 - th)) * (i * xk)


def _rg_seq_kernel(x_ref, gy_ref, conv0_ref, h0_ref, cw_ref, cb_ref, wa_ref, wi_ref, ba_ref, bi_ref, lam_ref,
                   y_ref, hout_ref, xe_ref, a_ref, b_ref, c_ref, tail_ref):
    s = pl.program_id(1)
    sub = V7X_SUBLANES
    nconv = conv0_ref.shape[0]
    tm, c = x_ref.shape
    first_sublane = lax.broadcasted_iota(jnp.int32, (sub, c), 0) == 0
    last = sub - 1

    @pl.when(s == 0)
    def _():
        tail_ref[0:nconv, :] = conv0_ref[...]
        c_ref[...] = jnp.broadcast_to(h0_ref[...], c_ref.shape)

    for f in range(nconv):
        blk = x_ref[(SEG_LEN - nconv + f) * sub:(SEG_LEN - nconv + f + 1) * sub, :]
        xe_ref[f * sub:(f + 1) * sub, :] = jnp.where(first_sublane, tail_ref[f:f + 1, :], pltpu.roll(blk, 1, 0))
    for f in range(nconv):
        tail_ref[f:f + 1, :] = x_ref[(SEG_LEN - nconv + f) * sub + last:(SEG_LEN - nconv + f + 1) * sub, :]
    xe_ref[nconv * sub:nconv * sub + tm, :] = x_ref[...]
    acc = cw_ref[0:1, :] * xe_ref[0:tm, :]
    for k in range(1, nconv + 1):
        acc = acc + cw_ref[k:k + 1, :] * xe_ref[k * sub:k * sub + tm, :]
    xc = cb_ref[...] + acc
    _rg_gates(xc, wa_ref, wi_ref, ba_ref, bi_ref, lam_ref, a_ref, b_ref, slice(None))

    def local(j, st):
        h, p = st
        rows = _row_block(j)
        a = a_ref[rows, :]
        h = a * h + b_ref[rows, :]
        p = p * a
        b_ref[rows, :] = h
        a_ref[rows, :] = p
        return h, p

    e, p = lax.fori_loop(0, SEG_LEN, local, (jnp.zeros((sub, c), F32), jnp.ones((sub, c), F32)), unroll=SCAN_UNROLL)
    row = lax.broadcasted_iota(jnp.int32, (sub, c), 0)
    for k in (1, 2, 4):
        keep = row >= k
        e = jnp.where(keep, e + p * pltpu.roll(e, k, 0), e)
        p = jnp.where(keep, p * pltpu.roll(p, k, 0), p)
    carry = c_ref[...]
    g = e + p * carry
    h_in = jnp.where(first_sublane, carry, pltpu.roll(g, 1, 0))
    c_ref[...] = jnp.broadcast_to(g[last:last + 1], g.shape)

    def fixup(j, cc):
        rows = _row_block(j)
        b_ref[rows, :] = (b_ref[rows, :] + a_ref[rows, :] * h_in) * gy_ref[rows, :]
        return cc

    lax.fori_loop(0, SEG_LEN, fixup, 0, unroll=SCAN_UNROLL)
    y_ref[...] = b_ref[...].astype(y_ref.dtype)

    @pl.when(s == pl.num_programs(1) - 1)
    def _():
        hout_ref[...] = c_ref[0:1, :]


def _rg_seq(proj, conv0, h0, conv_w, conv_b, wa, wi, b_a, b_i, lam, l, *, bn, seq):
    c = h0.shape[-1]
    tm = SEG_TILE
    n_s = seq // tm
    depth = conv_w.shape[0]
    vec = lambda a: a.reshape(depth, 1, c)
    y, hout = pl.pallas_call(
        _rg_seq_kernel,
        grid=(bn, n_s),
        in_specs=[pl.BlockSpec((tm, c), lambda b, s: (b * n_s + s, 1)),
                  pl.BlockSpec((tm, c), lambda b, s: (b * n_s + s, 2)),
                  pl.BlockSpec((None, CONV_W - 1, c), lambda b, s: (b, 0, 0)),
                  pl.BlockSpec((None, 1, c), lambda b, s: (b, 0, 0)),
                  pl.BlockSpec((None, CONV_W, c), lambda b, s: (l, 0, 0)),
                  pl.BlockSpec((None, 1, c), lambda b, s: (l, 0, 0)),
                  _const_spec(wa), _const_spec(wi),
                  pl.BlockSpec((None, 1, c), lambda b, s: (l, 0, 0)),
                  pl.BlockSpec((None, 1, c), lambda b, s: (l, 0, 0)),
                  pl.BlockSpec((None, 1, c), lambda b, s: (l, 0, 0))],
        out_specs=[pl.BlockSpec((tm, c), lambda b, s: (b * n_s + s, 0)),
                   pl.BlockSpec((None, 1, c), lambda b, s: (b, 0, 0))],
        out_shape=[jax.ShapeDtypeStruct((bn * seq, c), F32), jax.ShapeDtypeStruct((bn, 1, c), F32)],
        scratch_shapes=[pltpu.VMEM((tm + (CONV_W - 1) * V7X_SUBLANES, c), F32), pltpu.VMEM((tm, c), F32),
                        pltpu.VMEM((tm, c), F32), pltpu.VMEM((V7X_SUBLANES, c), F32),
                        pltpu.VMEM((V7X_SUBLANES, c), F32)],
        compiler_params=_params("arbitrary", "arbitrary"),
        name="rg_seq",
    )(proj, proj, conv0, h0.reshape(bn, 1, c), conv_w, vec(conv_b), wa, wi, vec(b_a), vec(b_i), vec(lam))
    return y, hout.reshape(bn, c)


def _rg_step_kernel(x_ref, gy_ref, conv0_ref, h0_ref, cw_ref, cb_ref, wa_ref, wi_ref, ba_ref, bi_ref, lam_ref,
                    y_ref, hout_ref, a_ref, b_ref, *, seq, bn):
    nconv = conv0_ref.shape[0]

    def xpad(i):
        if i < nconv:
            return conv0_ref[i]
        return x_ref[(i - nconv) * bn:(i - nconv + 1) * bn, :]

    for t in range(seq):
        acc = cw_ref[0:1, :] * xpad(t)
        for k in range(1, nconv + 1):
            acc = acc + cw_ref[k:k + 1, :] * xpad(t + k)
        xc = cb_ref[...] + acc
        _rg_gates(xc, wa_ref, wi_ref, ba_ref, bi_ref, lam_ref, a_ref, b_ref, slice(t * bn, (t + 1) * bn))
    h = h0_ref[...]
    for t in range(seq):
        rows = slice(t * bn, (t + 1) * bn)
        h = a_ref[rows, :] * h + b_ref[rows, :]
        y_ref[rows, :] = (h * gy_ref[rows, :]).astype(y_ref.dtype)
    hout_ref[...] = h


def _rg_step(proj, conv0_tm, h0, conv_w, conv_b, wa, wi, b_a, b_i, lam, l, *, bn, seq):
    c = h0.shape[-1]
    t = bn * seq
    depth = conv_w.shape[0]
    vec = lambda a: a.reshape(depth, 1, c)
    lspec = pl.BlockSpec((None, 1, c), lambda i: (l, 0, 0))
    return pl.pallas_call(
        functools.partial(_rg_step_kernel, seq=seq, bn=bn),
        grid=(1,),
        in_specs=[pl.BlockSpec((t, c), lambda i: (0, 1)),
                  pl.BlockSpec((t, c), lambda i: (0, 2)),
                  pl.BlockSpec(conv0_tm.shape, lambda i: (0, 0, 0)),
                  pl.BlockSpec((bn, c), lambda i: (0, 0)),
                  pl.BlockSpec((None, CONV_W, c), lambda i: (l, 0, 0)),
                  lspec, _const_spec(wa), _const_spec(wi), lspec, lspec, lspec],
        out_specs=[pl.BlockSpec((t, c), lambda i: (0, 0)), pl.BlockSpec((bn, c), lambda i: (0, 0))],
        out_shape=[jax.ShapeDtypeStruct((t, c), F32), jax.ShapeDtypeStruct((bn, c), F32)],
        scratch_shapes=[pltpu.VMEM((t, c), F32), pltpu.VMEM((t, c), F32)],
        compiler_params=_params("arbitrary"),
        name="rg_step",
    )(proj, proj, conv0_tm, h0, conv_w, vec(conv_b), wa, wi, vec(b_a), vec(b_i), vec(lam))


def _gate_merge_kernel(x_ref, sh_ref, sc_ref, g_ref, wgs_ref, wgr_ref, bgs_ref, bgr_ref, ys_ref, yr_ref,
                       ws_ref, wr_ref, o_ref, hn_ref):
    @pl.when(pl.program_id(1) == 0)
    def _():
        _norm_mod_rows(x_ref, g_ref, sc_ref, sh_ref, hn_ref)

    h = hn_ref[...]
    gate_s = _sigmoid(_dot(h, wgs_ref[...]) + bgs_ref[...])
    gate_r = _sigmoid(_dot(h, wgr_ref[...]) + bgr_ref[...])
    merged = gate_s * _dot(ys_ref[...], ws_ref[...]) + gate_r * _dot(yr_ref[...], wr_ref[...])
    o_ref[...] = merged.astype(o_ref.dtype)


def _gate_merge(x, mod, g, w_gate, b_gate, ys, yr, w_s, w_r, l, *, tm, tn, tiles_per_block):
    t, d = x.shape
    c = ys.shape[1]
    rows = mod.shape[2]
    off = d // tn
    depth = w_gate.shape[0]
    once = dict(pipeline_mode=pl.Buffered(1))
    return pl.pallas_call(
        _gate_merge_kernel,
        grid=(t // tm, d // tn),
        in_specs=[pl.BlockSpec((tm, d), lambda m, n: (m, 0), **once),
                  _mod_spec(0, rows, d, tiles_per_block, tm),
                  _mod_spec(1, rows, d, tiles_per_block, tm),
                  pl.BlockSpec((None, 1, d), lambda m, n: (l, 0, 0)),
                  pl.BlockSpec((None, d, tn), lambda m, n: (l, 0, n)),
                  pl.BlockSpec((None, d, tn), lambda m, n: (l, 0, n + off)),
                  pl.BlockSpec((None, 1, tn), lambda m, n: (l, 0, n)),
                  pl.BlockSpec((None, 1, tn), lambda m, n: (l, 0, n + off)),
                  pl.BlockSpec((tm, c), lambda m, n: (m, 0)),
                  pl.BlockSpec((tm, c), lambda m, n: (m, 0)),
                  pl.BlockSpec((None, c, tn), lambda m, n: (l, 0, n)),
                  pl.BlockSpec((None, c, tn), lambda m, n: (l, 0, n))],
        out_specs=pl.BlockSpec((tm, tn), lambda m, n: (m, n)),
        out_shape=jax.ShapeDtypeStruct((t, d), ACT_DTYPE),
        scratch_shapes=[pltpu.VMEM((tm, d), w_gate.dtype)],
        compiler_params=_params("arbitrary", "arbitrary"),
        name="gate_merge",
    )(x, mod, mod, g.reshape(depth, 1, d), w_gate, w_gate, b_gate.reshape(depth, 1, 2 * d),
      b_gate.reshape(depth, 1, 2 * d), ys, yr, w_s, w_r)


def _linear_residual_kernel(a_ref, w_ref, x_ref, g_ref, o_ref, acc_ref):
    k = pl.program_id(2)

    @pl.when(k == 0)
    def _():
        acc_ref[...] = jnp.zeros_like(acc_ref)

    acc_ref[...] += _dot(a_ref[...], w_ref[...])

    @pl.when(k == pl.num_programs(2) - 1)
    def _():
        o_ref[...] = x_ref[...] + g_ref[...] * acc_ref[...]


def _linear_residual(a, w, l, x, mod, j_g, *, tm, tn, tk, tiles_per_block):
    t, kdim = a.shape
    d = w.shape[-1]
    rows = mod.shape[2]
    return pl.pallas_call(
        _linear_residual_kernel,
        grid=(t // tm, d // tn, kdim // tk),
        in_specs=[pl.BlockSpec((tm, tk), lambda m, n, k: (m, k)),
                  pl.BlockSpec((None, tk, tn), lambda m, n, k: (l, k, n)),
                  pl.BlockSpec((tm, tn), lambda m, n, k: (m, n)),
                  _mod_spec(j_g, rows, tn, tiles_per_block, tm, ncol_axis=0)],
        out_specs=pl.BlockSpec((tm, tn), lambda m, n, k: (m, n)),
        out_shape=jax.ShapeDtypeStruct((t, d), F32),
        scratch_shapes=[pltpu.VMEM((tm, tn), F32)],
        compiler_params=_params("arbitrary", "arbitrary", "arbitrary"),
        name="linear_residual",
    )(a, w, x, mod)


ROUTE_I1, ROUTE_I2, ROUTE_W1, ROUTE_W2, ROUTE_RANK1, ROUTE_RANK2 = range(6)
DMA_UNROLL = 8
MOE_ROW_TILE = 512


def _router_kernel(x_ref, sh_ref, sc_ref, g_ref, wr_ref, *rest, n_experts, n_own):
    hn_ref, route_ref, cnt_ref = rest[-3:]
    m = pl.program_id(0)

    @pl.when(m == 0)
    def _():
        cnt_ref[...] = jnp.zeros_like(cnt_ref)

    @pl.when(m >= n_own)
    def _():
        hn_ref[...] = jnp.zeros_like(hn_ref)

    @pl.when(m < n_own)
    def _():
        _route_tile(x_ref, sh_ref, sc_ref, g_ref, wr_ref, hn_ref, route_ref, cnt_ref, n_experts)


def _route_tile(x_ref, sh_ref, sc_ref, g_ref, wr_ref, hn_ref, route_ref, cnt_ref, n_experts):
    _norm_mod_rows(x_ref, g_ref, sc_ref, sh_ref, hn_ref)
    logits = jnp.dot(hn_ref[...], wr_ref[...], preferred_element_type=F32, precision=lax.Precision.HIGHEST)
    tm, width = logits.shape
    lane = lax.broadcasted_iota(jnp.int32, logits.shape, 1).astype(F32)
    lg = jnp.where(lane < n_experts, logits, -jnp.inf)
    v1 = jnp.max(lg, axis=-1, keepdims=True)
    i1 = jnp.min(jnp.where(lg == v1, lane, float(width)), axis=-1, keepdims=True)
    lg2 = jnp.where(lane == i1, -jnp.inf, lg)
    v2 = jnp.max(lg2, axis=-1, keepdims=True)
    i2 = jnp.min(jnp.where(lg2 == v2, lane, float(width)), axis=-1, keepdims=True)
    e2 = jnp.exp(v2 - v1)
    den = 1.0 + e2
    sel1 = lane == i1
    sel2 = lane == i2
    picked = jnp.where(sel1 | sel2, 1.0, 0.0)
    r = lax.broadcasted_iota(jnp.int32, (tm, tm), 0)
    c = lax.broadcasted_iota(jnp.int32, (tm, tm), 1)
    before = jnp.where(c < r, 1.0, 0.0)
    seen = _dot(before, picked) + cnt_ref[...]
    rank1 = jnp.sum(jnp.where(sel1, seen, 0.0), axis=-1, keepdims=True)
    rank2 = jnp.sum(jnp.where(sel2, seen, 0.0), axis=-1, keepdims=True)
    cnt_ref[...] += jnp.sum(picked, axis=0, keepdims=True)
    cols = ((ROUTE_I1, i1), (ROUTE_I2, i2), (ROUTE_W1, 1.0 / den), (ROUTE_W2, e2 / den),
            (ROUTE_RANK1, rank1), (ROUTE_RANK2, rank2))
    route = jnp.zeros_like(logits)
    for j, v in cols:
        route = jnp.where(lane == j, v, route)
    route_ref[...] = route


def _router(x, mod, j_sh, g, w_router_pad, l, *, tm, tiles_per_block, n_experts, hn_rows, hn_row0, hn_buf):
    t, d = x.shape
    rows = mod.shape[2]
    width = w_router_pad.shape[-1]
    assert t % tm == 0 and hn_row0 % tm == 0 and (hn_rows - hn_row0) % tm == 0
    n_own = t // tm
    n_fill = (hn_rows - hn_row0) // tm - n_own if hn_buf is None else 0
    own = lambda m: jnp.minimum(m, n_own - 1)
    in_specs = [pl.BlockSpec((tm, d), lambda m: (own(m), 0)),
                _mod_spec(j_sh, rows, d, tiles_per_block, tm, tile_of=own),
                _mod_spec(j_sh + 1, rows, d, tiles_per_block, tm, tile_of=own),
                pl.BlockSpec((None, 1, d), lambda m: (l, 0, 0)),
                pl.BlockSpec((d, width), lambda m: (0, 0))]
    args = [x, mod, mod, g.reshape(g.shape[0], 1, d), w_router_pad]
    aliases = {}
    if hn_buf is not None:
        in_specs.append(pl.BlockSpec(memory_space=pl.ANY))
        args.append(hn_buf)
        aliases = {len(args) - 1: 0}
    return pl.pallas_call(
        functools.partial(_router_kernel, n_experts=n_experts, n_own=n_own),
        grid=(n_own + n_fill,),
        in_specs=in_specs,
        out_specs=[pl.BlockSpec((tm, d), lambda m: (m + hn_row0 // tm, 0)),
                   pl.BlockSpec((tm, width), lambda m: (own(m), 0)),
                   pl.BlockSpec((1, width), lambda m: (0, 0))],
        out_shape=[jax.ShapeDtypeStruct((hn_rows, d), F32), jax.ShapeDtypeStruct((t, width), F32),
                   jax.ShapeDtypeStruct((1, width), F32)],
        input_output_aliases=aliases,
        compiler_params=_params("arbitrary"),
        name="moe_router",
    )(*args)


def _dispatch_kernel(tok_ref, nu_ref, hn_ref, xs_ref, sem):
    tg = xs_ref.shape[0]
    m = pl.program_id(0)

    def row_copy(tok, i):
        return pltpu.make_async_copy(hn_ref.at[pl.ds(tok, 1)], xs_ref.at[pl.ds(i, 1)], sem.at[0])

    @pl.when(m < nu_ref[0])
    def _():
        lax.fori_loop(0, tg, lambda i, c: (row_copy(tok_ref[m * tg + i], i).start(), c)[1], 0, unroll=DMA_UNROLL)
        pltpu.make_async_copy(hn_ref.at[pl.ds(0, tg)], xs_ref, sem.at[0]).wait()

    @pl.when(m >= nu_ref[0])
    def _():
        xs_ref[...] = jnp.zeros_like(xs_ref)


def _dispatch(tok_of_row, n_used, hn, *, tg):
    rows = tok_of_row.shape[0]
    d = hn.shape[1]
    return pl.pallas_call(
        _dispatch_kernel,
        grid_spec=pltpu.PrefetchScalarGridSpec(
            num_scalar_prefetch=2, grid=(rows // tg,),
            in_specs=[pl.BlockSpec(memory_space=pl.ANY)],
            out_specs=pl.BlockSpec((tg, d), lambda m, tok, nu: (m, 0)),
            scratch_shapes=[pltpu.SemaphoreType.DMA((1,))]),
        out_shape=jax.ShapeDtypeStruct((rows, d), F32),
        compiler_params=_params("arbitrary"),
        name="moe_dispatch",
    )(tok_of_row, n_used, hn)


def _grouped_up_kernel(te_ref, nu_ref, xs_ref, w1_ref, w3_ref, o_ref):
    m = pl.program_id(1)

    @pl.when(m < nu_ref[0])
    def _():
        h = xs_ref[...]
        o_ref[...] = _silu(_dot(h, w1_ref[...])) * _dot(h, w3_ref[...])

    @pl.when(m >= nu_ref[0])
    def _():
        o_ref[...] = jnp.zeros_like(o_ref)


def _grouped_up(tile_expert, n_used, xs, w1, w3, lm, *, tg, tn):
    r, d = xs.shape
    f = w1.shape[-1]
    wspec = pl.BlockSpec((None, None, d, tn), lambda n, m, te, nu: (lm, te[m], 0, n))
    return pl.pallas_call(
        _grouped_up_kernel,
        grid_spec=pltpu.PrefetchScalarGridSpec(
            num_scalar_prefetch=2, grid=(f // tn, r // tg),
            in_specs=[pl.BlockSpec((tg, d), lambda n, m, te, nu: (m, 0)), wspec, wspec],
            out_specs=pl.BlockSpec((tg, tn), lambda n, m, te, nu: (m, n))),
        out_shape=jax.ShapeDtypeStruct((r, f), F32),
        compiler_params=_params("arbitrary", "arbitrary"),
        name="moe_up",
    )(tile_expert, n_used, xs, w1, w3)


def _grouped_down_kernel(te_ref, nu_ref, a_ref, w_ref, o_ref):
    m = pl.program_id(1)

    @pl.when(m < nu_ref[0])
    def _():
        o_ref[...] = _dot(a_ref[...], w_ref[...])

    @pl.when(m >= nu_ref[0])
    def _():
        o_ref[...] = jnp.zeros_like(o_ref)


def _grouped_down(tile_expert, n_used, act, w2, lm, *, tg, tn):
    r, f = act.shape
    d = w2.shape[-1]
    return pl.pallas_call(
        _grouped_down_kernel,
        grid_spec=pltpu.PrefetchScalarGridSpec(
            num_scalar_prefetch=2, grid=(d // tn, r // tg),
            in_specs=[pl.BlockSpec((tg, f), lambda n, m, te, nu: (m, 0)),
                      pl.BlockSpec((None, None, f, tn), lambda n, m, te, nu: (lm, te[m], 0, n))],
            out_specs=pl.BlockSpec((tg, tn), lambda n, m, te, nu: (m, n))),
        out_shape=jax.ShapeDtypeStruct((r, d), F32),
        compiler_params=_params("arbitrary", "arbitrary"),
        name="moe_down",
    )(tile_expert, n_used, act, w2)


def _combine_kernel(p1_ref, p2_ref, y_ref, route_ref, x_ref, g_ref, *rest, final_norm):
    if final_norm:
        gf_ref, o_ref, ya_ref, yb_ref, sem = rest
    else:
        o_ref, ya_ref, yb_ref, sem = rest
    tc = ya_ref.shape[0]
    base = pl.program_id(0) * tc

    def row_copy(row, dst_ref, i):
        return pltpu.make_async_copy(y_ref.at[pl.ds(row, 1)], dst_ref.at[pl.ds(i, 1)], sem.at[0])

    def body(i, c):
        row_copy(p1_ref[base + i], ya_ref, i).start()
        row_copy(p2_ref[base + i], yb_ref, i).start()
        return c

    lax.fori_loop(0, tc, body, 0, unroll=DMA_UNROLL)
    for dst_ref in (ya_ref, yb_ref):
        pltpu.make_async_copy(y_ref.at[pl.ds(0, tc)], dst_ref, sem.at[0]).wait()
    w1 = route_ref[:, ROUTE_W1:ROUTE_W1 + 1]
    w2 = route_ref[:, ROUTE_W2:ROUTE_W2 + 1]
    out = x_ref[...] + g_ref[...] * (w1 * ya_ref[...] + w2 * yb_ref[...])
    o_ref[...] = _rmsnorm(out, gf_ref[...]) if final_norm else out


def _combine(pos1, pos2, y, route, x, mod, j_g, *, tc, tiles_per_block, final_g=None):
    t, d = x.shape
    rows = mod.shape[2]
    width = route.shape[-1]
    in_specs = [pl.BlockSpec(memory_space=pl.ANY),
                pl.BlockSpec((tc, width), lambda m, p1, p2: (m, 0)),
                pl.BlockSpec((tc, d), lambda m, p1, p2: (m, 0)),
                _mod_spec(j_g, rows, d, tiles_per_block, tc)]
    args = [pos1, pos2, y, route, x, mod]
    if final_g is not None:
        in_specs.append(pl.BlockSpec((1, d), lambda m, p1, p2: (0, 0)))
        args.append(final_g.reshape(1, d))
    return pl.pallas_call(
        functools.partial(_combine_kernel, final_norm=final_g is not None),
        grid_spec=pltpu.PrefetchScalarGridSpec(
            num_scalar_prefetch=2, grid=(t // tc,),
            in_specs=in_specs,
            out_specs=pl.BlockSpec((tc, d), lambda m, p1, p2: (m, 0)),
            scratch_shapes=[pltpu.VMEM((tc, d), F32), pltpu.VMEM((tc, d), F32), pltpu.SemaphoreType.DMA((1,))]),
        out_shape=jax.ShapeDtypeStruct((t, d), F32),
        compiler_params=_params("arbitrary"),
        name="moe_combine",
    )(*args)


def _routed_moe(runs, l, g, w_router, w1, w3, w2, lm, *, tg, final_g=None):
    ne = w_router.shape[-1]
    w_r = jnp.pad(w_router[lm], ((0, 0), (0, V7X_LANES - ne)))
    t_all = sum(r["x"].shape[0] for r in runs)
    routed, hn_all, row0 = [], None, 0
    for r in runs:
        hn_all, route, cnt = _router(r["x"], r["mods"][l], 3, g, w_r, l, n_experts=ne, hn_rows=t_all, hn_row0=row0,
                                     hn_buf=hn_all, **r["tiled"](min(r["tm"], 512)))
        routed.append((hn_all, route, cnt))
        row0 += r["x"].shape[0]
    counts_each = [cnt[0, :ne].astype(jnp.int32) for _, _, cnt in routed]
    tiles_e = (sum(counts_each) + (tg - 1)) // tg
    tile_end = jnp.cumsum(tiles_e)
    row_start = (tile_end - tiles_e) * tg
    n_tiles = -(-TOP_K * t_all // tg) + ne
    tile_expert = jnp.minimum(jnp.sum(jnp.arange(n_tiles)[:, None] >= tile_end[None, :], axis=1), ne - 1)
    tile_expert = tile_expert.astype(jnp.int32)
    n_used = tile_end[-1:].astype(jnp.int32)
    pos, toks, first_row, tok0 = [], [], row_start, 0
    for (_, route, _), c_e, r in zip(routed, counts_each, runs):
        col = lambda j: route[:, j].astype(jnp.int32)
        pos.append((first_row[col(ROUTE_I1)] + col(ROUTE_RANK1), first_row[col(ROUTE_I2)] + col(ROUTE_RANK2)))
        toks.append(tok0 + jnp.arange(route.shape[0], dtype=jnp.int32))
        first_row, tok0 = first_row + c_e, tok0 + route.shape[0]
    tok_of_row = jnp.zeros((n_tiles * tg,), jnp.int32).at[jnp.concatenate([q for pq in pos for q in pq])].set(
        jnp.concatenate([tk for tk in toks for _ in range(TOP_K)]), unique_indices=True)
    xs = _dispatch(tok_of_row, n_used, hn_all, tg=tg)
    act = _grouped_up(tile_expert, n_used, xs, w1, w3, lm, tg=tg, tn=1024)
    y = _grouped_down(tile_expert, n_used, act, w2, lm, tg=tg, tn=1024)
    out = []
    for (p1, p2), (_, route, _), r in zip(pos, routed, runs):
        tc = min(r["tm"], 256)
        out.append(_combine(p1, p2, y, route, r["x"], r["mods"][l], 5, tc=tc,
                            tiles_per_block=r["tiled"](tc)["tiles_per_block"], final_g=final_g))
    return out


def _final_norm_kernel(x_ref, g_ref, o_ref):
    o_ref[...] = _rmsnorm(x_ref[...], g_ref[...])


def _final_norm(x, g, *, tm):
    t, d = x.shape
    return pl.pallas_call(
        _final_norm_kernel,
        grid=(t // tm,),
        in_specs=[pl.BlockSpec((tm, d), lambda m: (m, 0)), pl.BlockSpec((1, d), lambda m: (0, 0))],
        out_specs=pl.BlockSpec((tm, d), lambda m: (m, 0)),
        out_shape=jax.ShapeDtypeStruct((t, d), F32),
        compiler_params=_params("arbitrary"),
        name="final_norm",
    )(x, g.reshape(1, d))


def _make_run(x, mods, state, *, bn, seq, time_major, tm):
    def tiled(tile):
        return dict(tm=tile, tiles_per_block=1 if time_major else seq // tile)

    assert seq >= CONV_W - 1
    cols = dict(gate_merge=256, glu=512) if tm > 512 else dict(gate_merge=512, glu=1024)
    return dict(x=x, mods=mods, state=state, bn=bn, seq=seq, time_major=time_major, tm=tm, tiled=tiled, cols=cols,
                s5_re=[], s5_im=[], rg=[], conv=[])


def _mixer(r, p, l):
    x, mod, state, bn, seq = r["x"], r["mods"][l], r["state"], r["bn"], r["seq"]
    depth = p["w_in"].shape[0]
    c = p["s5_w_glu"].shape[-1]
    kw = r["tiled"](r["tm"])
    nconv = CONV_W - 1
    tb = p["s5_tables"][l]
    wa, wi = p["rg_tables"][l]
    proj = _normed_linear(x, mod, 0, p["norm_mix"], l, [p["w_in"]], l, mode="in", tn=1024, plain_cols=2 * c, **kw)
    h0re = state["s5_re"][l].reshape(bn, -1)
    h0im = state["s5_im"][l].reshape(bn, -1)
    s5_args = (tb, p["s5_d"].reshape(depth, c), p["s5_w_glu"], p["s5_b_glu"], l)
    rg_args = (p["rg_conv_w"], p["rg_conv_b"], wa, wi, p["rg_b_a"].reshape(depth, c), p["rg_b_i"].reshape(depth, c),
               p["rg_lam"], l)
    if r["time_major"]:
        y_s5, hre, him = _s5_step(proj, h0re, h0im, *s5_args, bn=bn, seq=seq)
        y_rg, hrg = _rg_step(proj, state["conv"][l].transpose(1, 0, 2), state["rg"][l], *rg_args, bn=bn, seq=seq)
        r["conv"].append(proj[:, c:2 * c].reshape(seq, bn, c)[seq - nconv:].transpose(1, 0, 2))
    else:
        y_s5, hre, him = _s5_seq(proj, h0re, h0im, *s5_args, bn=bn, seq=seq)
        y_rg, hrg = _rg_seq(proj, state["conv"][l], state["rg"][l], *rg_args, bn=bn, seq=seq)
        tail_rows = [seq - SEG_TILE + (SEG_LEN - nconv + f) * V7X_SUBLANES + V7X_SUBLANES - 1 for f in range(nconv)]
        proj3 = proj.reshape(bn, seq, 3 * c)
        r["conv"].append(jnp.stack([proj3[:, row, c:2 * c] for row in tail_rows], axis=1))
    r["s5_re"].append(hre.reshape(state["s5_re"][l].shape))
    r["s5_im"].append(him.reshape(state["s5_im"][l].shape))
    r["rg"].append(hrg)
    merged = _gate_merge(x, mod, p["norm_mix"], p["w_gate"], p["b_gate"], y_s5, y_rg, p["w_br_s5"], p["w_br_rg"], l,
                         tn=r["cols"]["gate_merge"], **kw)
    r["x"] = _linear_residual(merged, p["w_out"], l, x, mod, 2, tn=1024, tk=2048, **kw)


def _dense_ffn(r, p, l):
    kw = r["tiled"](r["tm"])
    mod = r["mods"][l]
    act = _normed_linear(r["x"], mod, 3, p["norm_ffn"], l, [p["ffn_w1"], p["ffn_w3"]], l // 2, mode="glu",
                         tn=r["cols"]["glu"], out_dtype=ACT_DTYPE, **kw)
    r["x"] = _linear_residual(act, p["ffn_w2"], l // 2, r["x"], mod, 5, tn=1024, tk=2048, **kw)


def _forward(runs, p):
    depth = p["w_in"].shape[0]
    for l in range(depth):
        for r in runs:
            _mixer(r, p, l)
        if l % 2 == 0:
            for r in runs:
                _dense_ffn(r, p, l)
        else:
            final_g = p["norm_f"] if l == depth - 1 else None
            new_x = _routed_moe(runs, l, p["norm_ffn"], p["moe_router"], p["moe_w1"], p["moe_w3"], p["moe_w2"], l // 2,
                                tg=MOE_ROW_TILE, final_g=final_g)
            for r, x in zip(runs, new_x):
                r["x"] = x
    if depth % 2 == 0:
        return [r["x"] for r in runs]
    return [_final_norm(r["x"], p["norm_f"], tm=min(r["tm"], 256)) for r in runs]


def kernel(x_prompt, x_sample, c_prompt, c_sample, state_s5_re, state_s5_im, state_rglru, state_conv, norm_mix, norm_ffn, norm_f, w_ada, b_ada, w_in, s5_lam_re, s5_lam_im, s5_log_dt, s5_b_re, s5_b_im, s5_c_re, s5_c_im, s5_d, s5_w_glu, s5_b_glu, rg_conv_w, rg_conv_b, rg_w_a, rg_b_a, rg_w_i, rg_b_i, rg_lam, w_gate, b_gate, w_br_s5, w_br_rg, w_out, ffn_w1, ffn_w3, ffn_w2, moe_router, moe_w1, moe_w3, moe_w2):
    p = dict(norm_mix=norm_mix, norm_ffn=norm_ffn, norm_f=norm_f, w_in=w_in,
             s5_lam_re=s5_lam_re, s5_lam_im=s5_lam_im, s5_log_dt=s5_log_dt, s5_b_re=s5_b_re, s5_b_im=s5_b_im,
             s5_c_re=s5_c_re, s5_c_im=s5_c_im, s5_d=s5_d, s5_w_glu=s5_w_glu, s5_b_glu=s5_b_glu,
             rg_conv_w=rg_conv_w, rg_conv_b=rg_conv_b, rg_w_a=rg_w_a, rg_b_a=rg_b_a, rg_w_i=rg_w_i, rg_b_i=rg_b_i,
             rg_lam=rg_lam, w_gate=w_gate, b_gate=b_gate, w_br_s5=w_br_s5, w_br_rg=w_br_rg, w_out=w_out,
             ffn_w1=ffn_w1, ffn_w3=ffn_w3, ffn_w2=ffn_w2, moe_router=moe_router, moe_w1=moe_w1, moe_w3=moe_w3,
             moe_w2=moe_w2)
    for name in ("w_out", "ffn_w2"):
        p[name] = p[name].astype(ACT_DTYPE)
    depth = w_in.shape[0]
    p["s5_tables"] = [_s5_tables(s5_lam_re[l], s5_lam_im[l], s5_log_dt[l], s5_b_re[l], s5_b_im[l], s5_c_re[l],
                                 s5_c_im[l]) for l in range(depth)]
    p["rg_tables"] = [_rg_tables(rg_w_a[l], rg_w_i[l]) for l in range(depth)]
    bp, sp, d = x_prompt.shape
    bs, ss, _ = x_sample.shape
    g, pst = state_s5_re.shape[2:]
    c = state_rglru.shape[-1]

    n_c = bp + bs
    rows_c = -(-n_c // V7X_SUBLANES) * V7X_SUBLANES
    c_all = jnp.pad(jnp.concatenate([c_prompt, c_sample], axis=0), ((0, rows_c - n_c), (0, 0)))
    mod_all = _ada_mod(c_all, w_ada, b_ada)
    mods_p, mods_s = [], []
    for l in range(depth):
        mp = mod_all[l, :bp].reshape(bp, 6, 1, d).transpose(1, 0, 2, 3)
        ms = mod_all[l, bp:n_c].reshape(bs, 6, d).transpose(1, 0, 2)
        ms = jnp.tile(ms, (1, ss, 1)).reshape(6, 1, ss * bs, d)
        mods_p.append(mp)
        mods_s.append(ms)

    zero_state = dict(s5_re=jnp.zeros((depth, bp, g, pst), F32), s5_im=jnp.zeros((depth, bp, g, pst), F32),
                      rg=jnp.zeros((depth, bp, c), F32), conv=jnp.zeros((depth, bp, CONV_W - 1, c), x_prompt.dtype))
    seg_shape = (bp, sp // SEG_TILE, V7X_SUBLANES, SEG_LEN, d)
    x_p = x_prompt.reshape(seg_shape).transpose(0, 1, 3, 2, 4).reshape(bp * sp, d)
    run_p = _make_run(x_p, mods_p, zero_state, bn=bp, seq=sp, time_major=False, tm=min(PROMPT_TILE, sp))
    sample_state = dict(s5_re=state_s5_re, s5_im=state_s5_im, rg=state_rglru, conv=state_conv)
    x_s = x_sample.transpose(1, 0, 2).reshape(ss * bs, d)
    run_s = _make_run(x_s, mods_s, sample_state, bn=bs, seq=ss, time_major=True, tm=ss * bs)
    y_p, y_s = _forward([run_p, run_s], p)
    y_p = y_p.reshape(bp, sp // SEG_TILE, SEG_LEN, V7X_SUBLANES, d).transpose(0, 1, 3, 2, 4).reshape(bp, sp, d)
    y_s = y_s.reshape(ss, bs, d).transpose(1, 0, 2)
    new_state = lambda r: tuple(jnp.stack(r[k]) for k in ("s5_re", "s5_im", "rg", "conv"))
    return (y_p, y_s) + new_state(run_p) + new_state(run_s)
```

```python
import functools
import math

import jax
import jax.numpy as jnp
from jax import lax
from jax.experimental import pallas as pl
from jax.experimental.pallas import tpu as pltpu

F32 = jnp.float32
ACT_DTYPE = jnp.bfloat16
EPS = 1e-6
RG_C = 8.0
CONV_W = 4
S5_GROUP = 16
S5_STATE = 64
RG_HEAD_DIM = 64
TOP_K = 2

V7X_LANES = 128
V7X_SUBLANES = 8
V7X_MXU_DIM = 256
VMEM_LIMIT_BYTES = 56 * 1024 * 1024

GELU_C = math.sqrt(2.0 / math.pi)


def _gelu(x):
    return 0.5 * x * (1.0 + jnp.tanh(GELU_C * (x + 0.044715 * (x * x * x))))


def _sigmoid(x):
    return 1.0 / (1.0 + jnp.exp(-x))


def _silu(x):
    return x * _sigmoid(x)


def _dot(a, b):
    return jnp.dot(a, b, preferred_element_type=F32)


def _params(*sem):
    return pltpu.CompilerParams(dimension_semantics=sem, vmem_limit_bytes=VMEM_LIMIT_BYTES)


def _mod_spec(j, rows, width, tiles_per_block, tile, ncol_axis=None, tile_of=lambda m: m):
    col = (lambda idx: 0) if ncol_axis is None else (lambda idx: idx[ncol_axis])
    if rows == 1:
        return pl.BlockSpec((None, None, 1, width), lambda m, *idx: (j, tile_of(m) // tiles_per_block, 0, col(idx)))
    return pl.BlockSpec((None, None, tile, width), lambda m, *idx: (j, 0, tile_of(m), col(idx)))


ROW_CHUNK = 128
PROMPT_TILE = 1024
SEG_LEN = 32
SEG_TILE = SEG_LEN * V7X_SUBLANES
SCAN_UNROLL = True


def _rmsnorm(x, g):
    ms = jnp.mean(x * x, axis=-1, keepdims=True)
    return (x * lax.rsqrt(ms + EPS)) * g


def _norm_mod_rows(x_ref, g_ref, sc_ref, sh_ref, o_ref):
    tm = x_ref.shape[0]
    chunk = min(ROW_CHUNK, tm)
    per_row = sc_ref.shape[0] != 1

    def body(i, carry):
        rows = pl.ds(pl.multiple_of(i * chunk, chunk), chunk)
        sc = sc_ref[rows, :] if per_row else sc_ref[...]
        sh = sh_ref[rows, :] if per_row else sh_ref[...]
        o_ref[rows, :] = (_rmsnorm(x_ref[rows, :], g_ref[...]) * (1.0 + sc) + sh).astype(o_ref.dtype)
        return carry

    lax.fori_loop(0, tm // chunk, body, 0)


def _ada_kernel(c_ref, w_ref, b_ref, o_ref):
    o_ref[...] = _dot(_silu(c_ref[...]), w_ref[...]) + b_ref[...]


def _ada_mod(c_all, w_ada, b_ada, *, tn=1024):
    depth, d, n6 = w_ada.shape
    rows = c_all.shape[0]
    return pl.pallas_call(
        _ada_kernel,
        grid=(depth, n6 // tn),
        in_specs=[pl.BlockSpec((rows, d), lambda l, n: (0, 0)),
                  pl.BlockSpec((None, d, tn), lambda l, n: (l, 0, n)),
                  pl.BlockSpec((None, 1, tn), lambda l, n: (l, 0, n))],
        out_specs=pl.BlockSpec((None, rows, tn), lambda l, n: (l, 0, n)),
        out_shape=jax.ShapeDtypeStruct((depth, rows, n6), F32),
        compiler_params=_params("arbitrary", "arbitrary"),
        name="ada_mod",
    )(c_all, w_ada, b_ada.reshape(depth, 1, n6))


def _normed_linear_kernel(*refs, mode, n_plain):
    if mode == "glu":
        x_ref, sh_ref, sc_ref, g_ref, w1_ref, w3_ref, o_ref, hn_ref = refs
    else:
        x_ref, sh_ref, sc_ref, g_ref, w_ref, o_ref, hn_ref = refs
    n = pl.program_id(1)

    @pl.when(n == 0)
    def _():
        _norm_mod_rows(x_ref, g_ref, sc_ref, sh_ref, hn_ref)

    h = hn_ref[...]
    if mode == "glu":
        o_ref[...] = (_silu(_dot(h, w1_ref[...])) * _dot(h, w3_ref[...])).astype(o_ref.dtype)
    else:
        z = _dot(h, w_ref[...])

        @pl.when(n < n_plain)
        def _():
            o_ref[...] = z

        @pl.when(n >= n_plain)
        def _():
            o_ref[...] = _gelu(z)


def _normed_linear(x, mod, j_sh, g, l_g, ws, l, *, mode, tm, tn, tiles_per_block, plain_cols=0, out_dtype=F32):
    t, d = x.shape
    n_out = ws[0].shape[-1]
    rows = mod.shape[2]
    w_spec = pl.BlockSpec((None, d, tn), lambda m, n: (l, 0, n))
    in_specs = [pl.BlockSpec((tm, d), lambda m, n: (m, 0)),
                _mod_spec(j_sh, rows, d, tiles_per_block, tm),
                _mod_spec(j_sh + 1, rows, d, tiles_per_block, tm),
                pl.BlockSpec((None, 1, d), lambda m, n: (l_g, 0, 0))]
    args = [x, mod, mod, g.reshape(g.shape[0], 1, d)]
    for w in ws:
        in_specs.append(w_spec)
        args.append(w)
    return pl.pallas_call(
        functools.partial(_normed_linear_kernel, mode=mode, n_plain=plain_cols // tn),
        grid=(t // tm, n_out // tn),
        in_specs=in_specs,
        out_specs=pl.BlockSpec((tm, tn), lambda m, n: (m, n)),
        out_shape=jax.ShapeDtypeStruct((t, n_out), out_dtype),
        scratch_shapes=[pltpu.VMEM((tm, d), ws[0].dtype)],
        compiler_params=_params("arbitrary", "arbitrary"),
        name="normed_linear_" + mode,
    )(*args)


def _s5_tables(lam_re, lam_im, log_dt, b_re, b_im, c_re, c_im):
    g, p = lam_re.shape
    h = b_re.shape[-1]
    gpc = V7X_MXU_DIM // h
    nck = g // gpc
    dt = jnp.exp(log_dt)[:, None]
    mag = jnp.exp(lam_re * dt)
    lb_re = mag * jnp.cos(lam_im * dt)
    lb_im = mag * jnp.sin(lam_im * dt)
    den = lam_re * lam_re + lam_im * lam_im
    cf_re = ((lb_re - 1.0) * lam_re + lb_im * lam_im) / den
    cf_im = (lb_im * lam_re - (lb_re - 1.0) * lam_im) / den
    bb_re = cf_re[:, :, None] * b_re - cf_im[:, :, None] * b_im
    bb_im = cf_re[:, :, None] * b_im + cf_im[:, :, None] * b_re
    by_ghp = lambda m: m.transpose(0, 2, 1).reshape(nck, gpc * h, p)
    by_gph = lambda m: m.transpose(0, 2, 1).reshape(nck, gpc * p, h)

    def powers(steps):
        k = jnp.asarray(steps, F32)[:, None]
        zr = (lam_re * dt).reshape(1, g * p)
        zi = (lam_im * dt).reshape(1, g * p)
        return jnp.exp(k * zr) * jnp.cos(k * zi), jnp.exp(k * zr) * jnp.sin(k * zi)

    sub = V7X_SUBLANES
    pw_re, pw_im = powers(range(1, SEG_LEN + 1))
    sg_re, sg_im = powers(range(SEG_LEN, SEG_LEN * (sub + 1), SEG_LEN))
    row = jnp.arange(sub)[:, None]
    segtabs = []
    for k in (1, 2, 4):
        segtabs += [jnp.where(row >= k, sg_re[k - 1:k], 0.0), jnp.where(row >= k, sg_im[k - 1:k], 0.0)]
    segtabs += [sg_re, sg_im]
    return dict(wb_re=_block_diag(by_ghp(bb_re), gpc), wb_im=_block_diag(by_ghp(bb_im), gpc),
                wc_re=_block_diag(by_gph(c_re), gpc), wc_imneg=_block_diag(by_gph(-c_im), gpc),
                lam8=jnp.stack([jnp.broadcast_to(pw_re[0:1], (sub, g * p)),
                                jnp.broadcast_to(pw_im[0:1], (sub, g * p))]),
                segtabs=jnp.stack(segtabs),
                fix=jnp.stack([jnp.repeat(pw_re, sub, axis=0), jnp.repeat(pw_im, sub, axis=0)]),
                lb_re=pw_re[0:1], lb_im=pw_im[0:1])


def _block_diag(src, nblk):
    _, rows, cb = src.shape
    r = jnp.arange(rows)[:, None] // (rows // nblk)
    c = jnp.arange(nblk * cb)[None, :] // cb
    return jnp.where(r == c, jnp.tile(src, (1, 1, nblk)), 0.0)


def _s5_input_proj(u, wbre_ref, wbim_ref, bre_ref, bim_ref):
    nck, kc, nc = wbre_ref.shape
    for k in range(nck):
        uk = u[:, k * kc:(k + 1) * kc]
        bre_ref[:, k * nc:(k + 1) * nc] = _dot(uk, wbre_ref[k])
        bim_ref[:, k * nc:(k + 1) * nc] = _dot(uk, wbim_ref[k])


def _s5_output(u, bre_ref, bim_ref, wcre_ref, wcim_ref, d_ref, wglu_ref, bglu_ref, y_ref):
    nck, nc, kc = wcre_ref.shape
    for k in range(nck):
        yk = _dot(bre_ref[:, k * nc:(k + 1) * nc], wcre_ref[k]) + _dot(bim_ref[:, k * nc:(k + 1) * nc], wcim_ref[k])
        yk = yk + d_ref[:, k * kc:(k + 1) * kc] * u[:, k * kc:(k + 1) * kc]
        bre_ref[:, k * kc:(k + 1) * kc] = _gelu(yk)
    yg = bre_ref[:, 0:nck * kc]
    y_ref[...] = (yg * _sigmoid(_dot(yg, wglu_ref[...]) + bglu_ref[...])).astype(y_ref.dtype)


def _row_block(j):
    return pl.ds(pl.multiple_of(j * V7X_SUBLANES, V7X_SUBLANES), V7X_SUBLANES)


def _s5_seq_kernel(u_ref, h0re_ref, h0im_ref, wbre_ref, wbim_ref, wcre_ref, wcim_ref, lam8_ref, seg_ref, fix_ref,
                   d_ref, wglu_ref, bglu_ref, y_ref, hre_ref, him_ref, bre_ref, bim_ref, cre_ref, cim_ref,
                   *, lane_chunk):
    s = pl.program_id(1)
    ns = cre_ref.shape[1]

    @pl.when(s == 0)
    def _():
        cre_ref[...] = jnp.broadcast_to(h0re_ref[...], cre_ref.shape)
        cim_ref[...] = jnp.broadcast_to(h0im_ref[...], cim_ref.shape)

    u = u_ref[...]
    _s5_input_proj(u, wbre_ref, wbim_ref, bre_ref, bim_ref)
    first_sublane = lax.broadcasted_iota(jnp.int32, (V7X_SUBLANES, lane_chunk), 0) == 0
    last = V7X_SUBLANES - 1

    for c in range(ns // lane_chunk):
        sl = slice(c * lane_chunk, (c + 1) * lane_chunk)
        lr, li = lam8_ref[0, :, sl], lam8_ref[1, :, sl]

        def local(j, st, sl=sl, lr=lr, li=li):
            sr, si = st
            rows = _row_block(j)
            sr, si = (lr * sr - li * si) + bre_ref[rows, sl], (lr * si + li * sr) + bim_ref[rows, sl]
            bre_ref[rows, sl] = sr
            bim_ref[rows, sl] = si
            return sr, si

        zero = jnp.zeros((V7X_SUBLANES, lane_chunk), F32)
        xr, xi = lax.fori_loop(0, SEG_LEN, local, (zero, zero), unroll=SCAN_UNROLL)
        for i, k in enumerate((1, 2, 4)):
            tr, ti = seg_ref[2 * i, :, sl], seg_ref[2 * i + 1, :, sl]
            pr, pi = pltpu.roll(xr, k, 0), pltpu.roll(xi, k, 0)
            xr, xi = xr + (tr * pr - ti * pi), xi + (tr * pi + ti * pr)
        cr, ci = cre_ref[:, sl], cim_ref[:, sl]
        tr, ti = seg_ref[6, :, sl], seg_ref[7, :, sl]
        gr = xr + (tr * cr - ti * ci)
        gi = xi + (tr * ci + ti * cr)
        h0r = jnp.where(first_sublane, cr, pltpu.roll(gr, 1, 0))
        h0i = jnp.where(first_sublane, ci, pltpu.roll(gi, 1, 0))
        cre_ref[:, sl] = jnp.broadcast_to(gr[last:last + 1], gr.shape)
        cim_ref[:, sl] = jnp.broadcast_to(gi[last:last + 1], gi.shape)

        def fixup(j, carry, sl=sl, h0r=h0r, h0i=h0i):
            rows = _row_block(j)
            fr, fi = fix_ref[0, rows, sl], fix_ref[1, rows, sl]
            bre_ref[rows, sl] += fr * h0r - fi * h0i
            bim_ref[rows, sl] += fr * h0i + fi * h0r
            return carry

        lax.fori_loop(0, SEG_LEN, fixup, 0, unroll=SCAN_UNROLL)

    _s5_output(u, bre_ref, bim_ref, wcre_ref, wcim_ref, d_ref, wglu_ref, bglu_ref, y_ref)

    @pl.when(s == pl.num_programs(1) - 1)
    def _():
        hre_ref[...] = cre_ref[0:1, :]
        him_ref[...] = cim_ref[0:1, :]


def _const_spec(a):
    nd = a.ndim
    return pl.BlockSpec(a.shape, lambda *_: (0,) * nd, pipeline_mode=pl.Buffered(1))


def _s5_seq(proj, h0re, h0im, tb, d_skip, w_glu, b_glu, l, *, bn, seq, lane_chunk=256):
    c = w_glu.shape[-1]
    ns = h0re.shape[-1]
    tm = SEG_TILE
    n_s = seq // tm
    consts = [tb["wb_re"], tb["wb_im"], tb["wc_re"], tb["wc_imneg"], tb["lam8"], tb["segtabs"], tb["fix"]]
    in_specs = ([pl.BlockSpec((tm, c), lambda b, s: (b * n_s + s, 0)),
                 pl.BlockSpec((None, 1, ns), lambda b, s: (b, 0, 0)),
                 pl.BlockSpec((None, 1, ns), lambda b, s: (b, 0, 0))]
                + [_const_spec(a) for a in consts]
                + [pl.BlockSpec((None, 1, c), lambda b, s: (l, 0, 0)),
                   pl.BlockSpec((None, c, c), lambda b, s: (l, 0, 0), pipeline_mode=pl.Buffered(1)),
                   pl.BlockSpec((None, 1, c), lambda b, s: (l, 0, 0))])
    y, hre, him = pl.pallas_call(
        functools.partial(_s5_seq_kernel, lane_chunk=lane_chunk),
        grid=(bn, n_s),
        in_specs=in_specs,
        out_specs=[pl.BlockSpec((tm, c), lambda b, s: (b * n_s + s, 0)),
                   pl.BlockSpec((None, 1, ns), lambda b, s: (b, 0, 0)),
                   pl.BlockSpec((None, 1, ns), lambda b, s: (b, 0, 0))],
        out_shape=[jax.ShapeDtypeStruct((bn * seq, c), F32),
                   jax.ShapeDtypeStruct((bn, 1, ns), F32),
                   jax.ShapeDtypeStruct((bn, 1, ns), F32)],
        scratch_shapes=[pltpu.VMEM((tm, ns), F32), pltpu.VMEM((tm, ns), F32),
                        pltpu.VMEM((V7X_SUBLANES, ns), F32), pltpu.VMEM((V7X_SUBLANES, ns), F32)],
        compiler_params=_params("arbitrary", "arbitrary"),
        name="s5_seq",
    )(proj, h0re.reshape(bn, 1, ns), h0im.reshape(bn, 1, ns), *consts,
      d_skip.reshape(d_skip.shape[0], 1, c), w_glu, b_glu.reshape(b_glu.shape[0], 1, c))
    return y, hre.reshape(bn, ns), him.reshape(bn, ns)


def _s5_step_kernel(u_ref, h0re_ref, h0im_ref, wbre_ref, wbim_ref, wcre_ref, wcim_ref, lbre_ref, lbim_ref, d_ref,
                    wglu_ref, bglu_ref, y_ref, hre_ref, him_ref, bre_ref, bim_ref, *, seq, bn, lane_chunk):
    u = u_ref[...]
    _s5_input_proj(u, wbre_ref, wbim_ref, bre_ref, bim_ref)
    ns = bre_ref.shape[1]
    for c in range(ns // lane_chunk):
        sl = slice(c * lane_chunk, (c + 1) * lane_chunk)
        lr = lbre_ref[:, sl]
        li = lbim_ref[:, sl]
        hr = h0re_ref[:, sl]
        hi = h0im_ref[:, sl]
        for t in range(seq):
            rows = slice(t * bn, (t + 1) * bn)
            hr, hi = (lr * hr - li * hi) + bre_ref[rows, sl], (lr * hi + li * hr) + bim_ref[rows, sl]
            bre_ref[rows, sl] = hr
            bim_ref[rows, sl] = hi
        hre_ref[:, sl] = hr
        him_ref[:, sl] = hi
    _s5_output(u, bre_ref, bim_ref, wcre_ref, wcim_ref, d_ref, wglu_ref, bglu_ref, y_ref)


def _s5_step(proj, h0re, h0im, tb, d_skip, w_glu, b_glu, l, *, bn, seq, lane_chunk=128):
    c = w_glu.shape[-1]
    ns = h0re.shape[-1]
    t = bn * seq
    consts = [tb["wb_re"], tb["wb_im"], tb["wc_re"], tb["wc_imneg"], tb["lb_re"], tb["lb_im"]]
    in_specs = ([pl.BlockSpec((t, c), lambda i: (0, 0)),
                 pl.BlockSpec((bn, ns), lambda i: (0, 0)),
                 pl.BlockSpec((bn, ns), lambda i: (0, 0))]
                + [_const_spec(a) for a in consts]
                + [pl.BlockSpec((None, 1, c), lambda i: (l, 0, 0)),
                   pl.BlockSpec((None, c, c), lambda i: (l, 0, 0), pipeline_mode=pl.Buffered(1)),
                   pl.BlockSpec((None, 1, c), lambda i: (l, 0, 0))])
    return pl.pallas_call(
        functools.partial(_s5_step_kernel, seq=seq, bn=bn, lane_chunk=lane_chunk),
        grid=(1,),
        in_specs=in_specs,
        out_specs=[pl.BlockSpec((t, c), lambda i: (0, 0)),
                   pl.BlockSpec((bn, ns), lambda i: (0, 0)),
                   pl.BlockSpec((bn, ns), lambda i: (0, 0))],
        out_shape=[jax.ShapeDtypeStruct((t, c), F32),
                   jax.ShapeDtypeStruct((bn, ns), F32),
                   jax.ShapeDtypeStruct((bn, ns), F32)],
        scratch_shapes=[pltpu.VMEM((t, ns), F32), pltpu.VMEM((t, ns), F32)],
        compiler_params=_params("arbitrary"),
        name="s5_step",
    )(proj, h0re, h0im, *consts,
      d_skip.reshape(d_skip.shape[0], 1, c), w_glu, b_glu.reshape(b_glu.shape[0], 1, c))


def _rg_tables(w_a, w_i):
    nh, hd, _ = w_a.shape
    hpc = V7X_MXU_DIM // hd
    expand = lambda w: _block_diag(w.reshape(nh // hpc, hpc * hd, hd), hpc)
    return expand(w_a), expand(w_i)


def _rg_gates(xc, wa_ref, wi_ref, ba_ref, bi_ref, lam_ref, a_ref, b_ref, rows):
    nck, kc, _ = wa_ref.shape
    lam = lam_ref[...]
    neg = -lam
    softplus = jnp.maximum(neg, 0.0) + jnp.log(1.0 + jnp.exp(-jnp.abs(neg)))
    for k in range(nck):
        cs = slice(k * kc, (k + 1) * kc)
        xk = xc[:, cs]
        r = _sigmoid(_dot(xk, wa_ref[k]) + ba_ref[:, cs])
        i = _sigmoid(_dot(xk, wi_ref[k]) + bi_ref[:, cs])
        log_a = (-RG_C * r) * softplus[:, cs]
        a_ref[rows, cs] = jnp.exp(log_a)
        th = jnp.tanh(log_a)
        b_ref[rows, cs] = jnp.sqrt(-2.0 * th / (1.0 - th)) * (i * xk)


def _rg_seq_kernel(x_ref, gy_ref, conv0_ref, h0_ref, cw_ref, cb_ref, wa_ref, wi_ref, ba_ref, bi_ref, lam_ref,
                   y_ref, hout_ref, xe_ref, a_ref, b_ref, c_ref, tail_ref):
    s = pl.program_id(1)
    sub = V7X_SUBLANES
    nconv = conv0_ref.shape[0]
    tm, c = x_ref.shape
    first_sublane = lax.broadcasted_iota(jnp.int32, (sub, c), 0) == 0
    last = sub - 1

    @pl.when(s == 0)
    def _():
        tail_ref[0:nconv, :] = conv0_ref[...]
        c_ref[...] = jnp.broadcast_to(h0_ref[...], c_ref.shape)

    for f in range(nconv):
        blk = x_ref[(SEG_LEN - nconv + f) * sub:(SEG_LEN - nconv + f + 1) * sub, :]
        xe_ref[f * sub:(f + 1) * sub, :] = jnp.where(first_sublane, tail_ref[f:f + 1, :], pltpu.roll(blk, 1, 0))
    for f in range(nconv):
        tail_ref[f:f + 1, :] = x_ref[(SEG_LEN - nconv + f) * sub + last:(SEG_LEN - nconv + f + 1) * sub, :]
    xe_ref[nconv * sub:nconv * sub + tm, :] = x_ref[...]
    acc = cw_ref[0:1, :] * xe_ref[0:tm, :]
    for k in range(1, nconv + 1):
        acc = acc + cw_ref[k:k + 1, :] * xe_ref[k * sub:k * sub + tm, :]
    xc = cb_ref[...] + acc
    _rg_gates(xc, wa_ref, wi_ref, ba_ref, bi_ref, lam_ref, a_ref, b_ref, slice(None))

    def local(j, st):
        h, p = st
        rows = _row_block(j)
        a = a_ref[rows, :]
        h = a * h + b_ref[rows, :]
        p = p * a
        b_ref[rows, :] = h
        a_ref[rows, :] = p
        return h, p

    e, p = lax.fori_loop(0, SEG_LEN, local, (jnp.zeros((sub, c), F32), jnp.ones((sub, c), F32)), unroll=SCAN_UNROLL)
    row = lax.broadcasted_iota(jnp.int32, (sub, c), 0)
    for k in (1, 2, 4):
        keep = row >= k
        e = jnp.where(keep, e + p * pltpu.roll(e, k, 0), e)
        p = jnp.where(keep, p * pltpu.roll(p, k, 0), p)
    carry = c_ref[...]
    g = e + p * carry
    h_in = jnp.where(first_sublane, carry, pltpu.roll(g, 1, 0))
    c_ref[...] = jnp.broadcast_to(g[last:last + 1], g.shape)

    def fixup(j, cc):
        rows = _row_block(j)
        b_ref[rows, :] = (b_ref[rows, :] + a_ref[rows, :] * h_in) * gy_ref[rows, :]
        return cc

    lax.fori_loop(0, SEG_LEN, fixup, 0, unroll=SCAN_UNROLL)
    y_ref[...] = b_ref[...].astype(y_ref.dtype)

    @pl.when(s == pl.num_programs(1) - 1)
    def _():
        hout_ref[...] = c_ref[0:1, :]


def _rg_seq(proj, conv0, h0, conv_w, conv_b, wa, wi, b_a, b_i, lam, l, *, bn, seq):
    c = h0.shape[-1]
    tm = SEG_TILE
    n_s = seq // tm
    depth = conv_w.shape[0]
    vec = lambda a: a.reshape(depth, 1, c)
    y, hout = pl.pallas_call(
        _rg_seq_kernel,
        grid=(bn, n_s),
        in_specs=[pl.BlockSpec((tm, c), lambda b, s: (b * n_s + s, 1)),
                  pl.BlockSpec((tm, c), lambda b, s: (b * n_s + s, 2)),
                  pl.BlockSpec((None, CONV_W - 1, c), lambda b, s: (b, 0, 0)),
                  pl.BlockSpec((None, 1, c), lambda b, s: (b, 0, 0)),
                  pl.BlockSpec((None, CONV_W, c), lambda b, s: (l, 0, 0)),
                  pl.BlockSpec((None, 1, c), lambda b, s: (l, 0, 0)),
                  _const_spec(wa), _const_spec(wi),
                  pl.BlockSpec((None, 1, c), lambda b, s: (l, 0, 0)),
                  pl.BlockSpec((None, 1, c), lambda b, s: (l, 0, 0)),
                  pl.BlockSpec((None, 1, c), lambda b, s: (l, 0, 0))],
        out_specs=[pl.BlockSpec((tm, c), lambda b, s: (b * n_s + s, 0)),
                   pl.BlockSpec((None, 1, c), lambda b, s: (b, 0, 0))],
        out_shape=[jax.ShapeDtypeStruct((bn * seq, c), F32), jax.ShapeDtypeStruct((bn, 1, c), F32)],
        scratch_shapes=[pltpu.VMEM((tm + (CONV_W - 1) * V7X_SUBLANES, c), F32), pltpu.VMEM((tm, c), F32),
                        pltpu.VMEM((tm, c), F32), pltpu.VMEM((V7X_SUBLANES, c), F32),
                        pltpu.VMEM((V7X_SUBLANES, c), F32)],
        compiler_params=_params("arbitrary", "arbitrary"),
        name="rg_seq",
    )(proj, proj, conv0, h0.reshape(bn, 1, c), conv_w, vec(conv_b), wa, wi, vec(b_a), vec(b_i), vec(lam))
    return y, hout.reshape(bn, c)


def _rg_step_kernel(x_ref, gy_ref, conv0_ref, h0_ref, cw_ref, cb_ref, wa_ref, wi_ref, ba_ref, bi_ref, lam_ref,
                    y_ref, hout_ref, a_ref, b_ref, *, seq, bn):
    nconv = conv0_ref.shape[0]

    def xpad(i):
        if i < nconv:
            return conv0_ref[i]
        return x_ref[(i - nconv) * bn:(i - nconv + 1) * bn, :]

    for t in range(seq):
        acc = cw_ref[0:1, :] * xpad(t)
        for k in range(1, nconv + 1):
            acc = acc + cw_ref[k:k + 1, :] * xpad(t + k)
        xc = cb_ref[...] + acc
        _rg_gates(xc, wa_ref, wi_ref, ba_ref, bi_ref, lam_ref, a_ref, b_ref, slice(t * bn, (t + 1) * bn))
    h = h0_ref[...]
    for t in range(seq):
        rows = slice(t * bn, (t + 1) * bn)
        h = a_ref[rows, :] * h + b_ref[rows, :]
        y_ref[rows, :] = (h * gy_ref[rows, :]).astype(y_ref.dtype)
    hout_ref[...] = h


def _rg_step(proj, conv0_tm, h0, conv_w, conv_b, wa, wi, b_a, b_i, lam, l, *, bn, seq):
    c = h0.shape[-1]
    t = bn * seq
    depth = conv_w.shape[0]
    vec = lambda a: a.reshape(depth, 1, c)
    lspec = pl.BlockSpec((None, 1, c), lambda i: (l, 0, 0))
    return pl.pallas_call(
        functools.partial(_rg_step_kernel, seq=seq, bn=bn),
        grid=(1,),
        in_specs=[pl.BlockSpec((t, c), lambda i: (0, 1)),
                  pl.BlockSpec((t, c), lambda i: (0, 2)),
                  pl.BlockSpec(conv0_tm.shape, lambda i: (0, 0, 0)),
                  pl.BlockSpec((bn, c), lambda i: (0, 0)),
                  pl.BlockSpec((None, CONV_W, c), lambda i: (l, 0, 0)),
                  lspec, _const_spec(wa), _const_spec(wi), lspec, lspec, lspec],
        out_specs=[pl.BlockSpec((t, c), lambda i: (0, 0)), pl.BlockSpec((bn, c), lambda i: (0, 0))],
        out_shape=[jax.ShapeDtypeStruct((t, c), F32), jax.ShapeDtypeStruct((bn, c), F32)],
        scratch_shapes=[pltpu.VMEM((t, c), F32), pltpu.VMEM((t, c), F32)],
        compiler_params=_params("arbitrary"),
        name="rg_step",
    )(proj, proj, conv0_tm, h0, conv_w, vec(conv_b), wa, wi, vec(b_a), vec(b_i), vec(lam))


def _gate_merge_kernel(x_ref, sh_ref, sc_ref, g_ref, wgs_ref, wgr_ref, bgs_ref, bgr_ref, ys_ref, yr_ref,
                       ws_ref, wr_ref, o_ref, hn_ref):
    @pl.when(pl.program_id(1) == 0)
    def _():
        _norm_mod_rows(x_ref, g_ref, sc_ref, sh_ref, hn_ref)

    h = hn_ref[...]
    gate_s = _sigmoid(_dot(h, wgs_ref[...]) + bgs_ref[...])
    gate_r = _sigmoid(_dot(h, wgr_ref[...]) + bgr_ref[...])
    merged = gate_s * _dot(ys_ref[...], ws_ref[...]) + gate_r * _dot(yr_ref[...], wr_ref[...])
    o_ref[...] = merged.astype(o_ref.dtype)


def _gate_merge(x, mod, g, w_gate, b_gate, ys, yr, w_s, w_r, l, *, tm, tn, tiles_per_block):
    t, d = x.shape
    c = ys.shape[1]
    rows = mod.shape[2]
    off = d // tn
    depth = w_gate.shape[0]
    once = dict(pipeline_mode=pl.Buffered(1))
    return pl.pallas_call(
        _gate_merge_kernel,
        grid=(t // tm, d // tn),
        in_specs=[pl.BlockSpec((tm, d), lambda m, n: (m, 0), **once),
                  _mod_spec(0, rows, d, tiles_per_block, tm),
                  _mod_spec(1, rows, d, tiles_per_block, tm),
                  pl.BlockSpec((None, 1, d), lambda m, n: (l, 0, 0)),
                  pl.BlockSpec((None, d, tn), lambda m, n: (l, 0, n)),
                  pl.BlockSpec((None, d, tn), lambda m, n: (l, 0, n + off)),
                  pl.BlockSpec((None, 1, tn), lambda m, n: (l, 0, n)),
                  pl.BlockSpec((None, 1, tn), lambda m, n: (l, 0, n + off)),
                  pl.BlockSpec((tm, c), lambda m, n: (m, 0)),
                  pl.BlockSpec((tm, c), lambda m, n: (m, 0)),
                  pl.BlockSpec((None, c, tn), lambda m, n: (l, 0, n)),
                  pl.BlockSpec((None, c, tn), lambda m, n: (l, 0, n))],
        out_specs=pl.BlockSpec((tm, tn), lambda m, n: (m, n)),
        out_shape=jax.ShapeDtypeStruct((t, d), ACT_DTYPE),
        scratch_shapes=[pltpu.VMEM((tm, d), w_gate.dtype)],
        compiler_params=_params("arbitrary", "arbitrary"),
        name="gate_merge",
    )(x, mod, mod, g.reshape(depth, 1, d), w_gate, w_gate, b_gate.reshape(depth, 1, 2 * d),
      b_gate.reshape(depth, 1, 2 * d), ys, yr, w_s, w_r)


def _linear_residual_kernel(a_ref, w_ref, x_ref, g_ref, o_ref, acc_ref):
    k = pl.program_id(2)

    @pl.when(k == 0)
    def _():
        acc_ref[...] = jnp.zeros_like(acc_ref)

    acc_ref[...] += _dot(a_ref[...], w_ref[...])

    @pl.when(k == pl.num_programs(2) - 1)
    def _():
        o_ref[...] = x_ref[...] + g_ref[...] * acc_ref[...]


def _linear_residual(a, w, l, x, mod, j_g, *, tm, tn, tk, tiles_per_block):
    t, kdim = a.shape
    d = w.shape[-1]
    rows = mod.shape[2]
    return pl.pallas_call(
        _linear_residual_kernel,
        grid=(t // tm, d // tn, kdim // tk),
        in_specs=[pl.BlockSpec((tm, tk), lambda m, n, k: (m, k)),
                  pl.BlockSpec((None, tk, tn), lambda m, n, k: (l, k, n)),
                  pl.BlockSpec((tm, tn), lambda m, n, k: (m, n)),
                  _mod_spec(j_g, rows, tn, tiles_per_block, tm, ncol_axis=0)],
        out_specs=pl.BlockSpec((tm, tn), lambda m, n, k: (m, n)),
        out_shape=jax.ShapeDtypeStruct((t, d), F32),
        scratch_shapes=[pltpu.VMEM((tm, tn), F32)],
        compiler_params=_params("arbitrary", "arbitrary", "arbitrary"),
        name="linear_residual",
    )(a, w, x, mod)


ROUTE_I1, ROUTE_I2, ROUTE_W1, ROUTE_W2, ROUTE_RANK1, ROUTE_RANK2 = range(6)
DMA_UNROLL = 8
MOE_ROW_TILE = 512


def _router_kernel(x_ref, sh_ref, sc_ref, g_ref, wr_ref, *rest, n_experts, n_own):
    hn_ref, route_ref, cnt_ref = rest[-3:]
    m = pl.program_id(0)

    @pl.when(m == 0)
    def _():
        cnt_ref[...] = jnp.zeros_like(cnt_ref)

    @pl.when(m >= n_own)
    def _():
        hn_ref[...] = jnp.zeros_like(hn_ref)

    @pl.when(m < n_own)
    def _():
        _route_tile(x_ref, sh_ref, sc_ref, g_ref, wr_ref, hn_ref, route_ref, cnt_ref, n_experts)


def _route_tile(x_ref, sh_ref, sc_ref, g_ref, wr_ref, hn_ref, route_ref, cnt_ref, n_experts):
    _norm_mod_rows(x_ref, g_ref, sc_ref, sh_ref, hn_ref)
    logits = jnp.dot(hn_ref[...], wr_ref[...], preferred_element_type=F32, precision=lax.Precision.HIGHEST)
    tm, width = logits.shape
    lane = lax.broadcasted_iota(jnp.int32, logits.shape, 1).astype(F32)
    lg = jnp.where(lane < n_experts, logits, -jnp.inf)
    v1 = jnp.max(lg, axis=-1, keepdims=True)
    i1 = jnp.min(jnp.where(lg == v1, lane, float(width)), axis=-1, keepdims=True)
    lg2 = jnp.where(lane == i1, -jnp.inf, lg)
    v2 = jnp.max(lg2, axis=-1, keepdims=True)
    i2 = jnp.min(jnp.where(lg2 == v2, lane, float(width)), axis=-1, keepdims=True)
    e2 = jnp.exp(v2 - v1)
    den = 1.0 + e2
    sel1 = lane == i1
    sel2 = lane == i2
    picked = jnp.where(sel1 | sel2, 1.0, 0.0)
    r = lax.broadcasted_iota(jnp.int32, (tm, tm), 0)
    c = lax.broadcasted_iota(jnp.int32, (tm, tm), 1)
    before = jnp.where(c < r, 1.0, 0.0)
    seen = _dot(before, picked) + cnt_ref[...]
    rank1 = jnp.sum(jnp.where(sel1, seen, 0.0), axis=-1, keepdims=True)
    rank2 = jnp.sum(jnp.where(sel2, seen, 0.0), axis=-1, keepdims=True)
    cnt_ref[...] += jnp.sum(picked, axis=0, keepdims=True)
    cols = ((ROUTE_I1, i1), (ROUTE_I2, i2), (ROUTE_W1, 1.0 / den), (ROUTE_W2, e2 / den),
            (ROUTE_RANK1, rank1), (ROUTE_RANK2, rank2))
    route = jnp.zeros_like(logits)
    for j, v in cols:
        route = jnp.where(lane == j, v, route)
    route_ref[...] = route


def _router(x, mod, j_sh, g, w_router_pad, l, *, tm, tiles_per_block, n_experts, hn_rows, hn_row0, hn_buf):
    t, d = x.shape
    rows = mod.shape[2]
    width = w_router_pad.shape[-1]
    assert t % tm == 0 and hn_row0 % tm == 0 and (hn_rows - hn_row0) % tm == 0
    n_own = t // tm
    n_fill = (hn_rows - hn_row0) // tm - n_own if hn_buf is None else 0
    own = lambda m: jnp.minimum(m, n_own - 1)
    in_specs = [pl.BlockSpec((tm, d), lambda m: (own(m), 0)),
                _mod_spec(j_sh, rows, d, tiles_per_block, tm, tile_of=own),
                _mod_spec(j_sh + 1, rows, d, tiles_per_block, tm, tile_of=own),
                pl.BlockSpec((None, 1, d), lambda m: (l, 0, 0)),
                pl.BlockSpec((d, width), lambda m: (0, 0))]
    args = [x, mod, mod, g.reshape(g.shape[0], 1, d), w_router_pad]
    aliases = {}
    if hn_buf is not None:
        in_specs.append(pl.BlockSpec(memory_space=pl.ANY))
        args.append(hn_buf)
        aliases = {len(args) - 1: 0}
    return pl.pallas_call(
        functools.partial(_router_kernel, n_experts=n_experts, n_own=n_own),
        grid=(n_own + n_fill,),
        in_specs=in_specs,
        out_specs=[pl.BlockSpec((tm, d), lambda m: (m + hn_row0 // tm, 0)),
                   pl.BlockSpec((tm, width), lambda m: (own(m), 0)),
                   pl.BlockSpec((1, width), lambda m: (0, 0))],
        out_shape=[jax.ShapeDtypeStruct((hn_rows, d), F32), jax.ShapeDtypeStruct((t, width), F32),
                   jax.ShapeDtypeStruct((1, width), F32)],
        input_output_aliases=aliases,
        compiler_params=_params("arbitrary"),
        name="moe_router",
    )(*args)


def _dispatch_kernel(tok_ref, nu_ref, hn_ref, xs_ref, sem):
    tg = xs_ref.shape[0]
    m = pl.program_id(0)

    def row_copy(tok, i):
        return pltpu.make_async_copy(hn_ref.at[pl.ds(tok, 1)], xs_ref.at[pl.ds(i, 1)], sem.at[0])

    @pl.when(m < nu_ref[0])
    def _():
        lax.fori_loop(0, tg, lambda i, c: (row_copy(tok_ref[m * tg + i], i).start(), c)[1], 0, unroll=DMA_UNROLL)
        pltpu.make_async_copy(hn_ref.at[pl.ds(0, tg)], xs_ref, sem.at[0]).wait()

    @pl.when(m >= nu_ref[0])
    def _():
        xs_ref[...] = jnp.zeros_like(xs_ref)


def _dispatch(tok_of_row, n_used, hn, *, tg):
    rows = tok_of_row.shape[0]
    d = hn.shape[1]
    return pl.pallas_call(
        _dispatch_kernel,
        grid_spec=pltpu.PrefetchScalarGridSpec(
            num_scalar_prefetch=2, grid=(rows // tg,),
            in_specs=[pl.BlockSpec(memory_space=pl.ANY)],
            out_specs=pl.BlockSpec((tg, d), lambda m, tok, nu: (m, 0)),
            scratch_shapes=[pltpu.SemaphoreType.DMA((1,))]),
        out_shape=jax.ShapeDtypeStruct((rows, d), F32),
        compiler_params=_params("arbitrary"),
        name="moe_dispatch",
    )(tok_of_row, n_used, hn)


def _grouped_up_kernel(te_ref, nu_ref, xs_ref, w1_ref, w3_ref, o_ref):
    m = pl.program_id(1)

    @pl.when(m < nu_ref[0])
    def _():
        h = xs_ref[...]
        o_ref[...] = (_silu(_dot(h, w1_ref[...])) * _dot(h, w3_ref[...])).astype(o_ref.dtype)

    @pl.when(m >= nu_ref[0])
    def _():
        o_ref[...] = jnp.zeros_like(o_ref)


def _grouped_up(tile_expert, n_used, xs, w1, w3, lm, *, tg, tn):
    r, d = xs.shape
    f = w1.shape[-1]
    wspec = pl.BlockSpec((None, None, d, tn), lambda n, m, te, nu: (lm, te[m], 0, n))
    return pl.pallas_call(
        _grouped_up_kernel,
        grid_spec=pltpu.PrefetchScalarGridSpec(
            num_scalar_prefetch=2, grid=(f // tn, r // tg),
            in_specs=[pl.BlockSpec((tg, d), lambda n, m, te, nu: (m, 0)), wspec, wspec],
            out_specs=pl.BlockSpec((tg, tn), lambda n, m, te, nu: (m, n))),
        out_shape=jax.ShapeDtypeStruct((r, f), ACT_DTYPE),
        compiler_params=_params("arbitrary", "arbitrary"),
        name="moe_up",
    )(tile_expert, n_used, xs, w1, w3)


def _grouped_down_kernel(te_ref, nu_ref, a_ref, w_ref, o_ref, wcast_ref):
    m = pl.program_id(1)
    new_block = jnp.logical_or(m == 0, te_ref[m] != te_ref[jnp.maximum(m - 1, 0)])

    @pl.when(new_block)
    def _():
        wcast_ref[...] = w_ref[...].astype(wcast_ref.dtype)

    @pl.when(m < nu_ref[0])
    def _():
        o_ref[...] = _dot(a_ref[...], wcast_ref[...])

    @pl.when(m >= nu_ref[0])
    def _():
        o_ref[...] = jnp.zeros_like(o_ref)


def _grouped_down(tile_expert, n_used, act, w2, lm, *, tg, tn):
    r, f = act.shape
    d = w2.shape[-1]
    return pl.pallas_call(
        _grouped_down_kernel,
        grid_spec=pltpu.PrefetchScalarGridSpec(
            num_scalar_prefetch=2, grid=(d // tn, r // tg),
            in_specs=[pl.BlockSpec((tg, f), lambda n, m, te, nu: (m, 0)),
                      pl.BlockSpec((None, None, f, tn), lambda n, m, te, nu: (lm, te[m], 0, n))],
            out_specs=pl.BlockSpec((tg, tn), lambda n, m, te, nu: (m, n)),
            scratch_shapes=[pltpu.VMEM((f, tn), act.dtype)]),
        out_shape=jax.ShapeDtypeStruct((r, d), F32),
        compiler_params=_params("arbitrary", "arbitrary"),
        name="moe_down",
    )(tile_expert, n_used, act, w2)


def _combine_kernel(p1_ref, p2_ref, y_ref, route_ref, x_ref, g_ref, *rest, final_norm):
    if final_norm:
        gf_ref, o_ref, ya_ref, yb_ref, sem = rest
    else:
        o_ref, ya_ref, yb_ref, sem = rest
    tc = ya_ref.shape[0]
    base = pl.program_id(0) * tc

    def row_copy(row, dst_ref, i):
        return pltpu.make_async_copy(y_ref.at[pl.ds(row, 1)], dst_ref.at[pl.ds(i, 1)], sem.at[0])

    def body(i, c):
        row_copy(p1_ref[base + i], ya_ref, i).start()
        row_copy(p2_ref[base + i], yb_ref, i).start()
        return c

    lax.fori_loop(0, tc, body, 0, unroll=DMA_UNROLL)
    for dst_ref in (ya_ref, yb_ref):
        pltpu.make_async_copy(y_ref.at[pl.ds(0, tc)], dst_ref, sem.at[0]).wait()
    w1 = route_ref[:, ROUTE_W1:ROUTE_W1 + 1]
    w2 = route_ref[:, ROUTE_W2:ROUTE_W2 + 1]
    out = x_ref[...] + g_ref[...] * (w1 * ya_ref[...] + w2 * yb_ref[...])
    o_ref[...] = _rmsnorm(out, gf_ref[...]) if final_norm else out


def _combine(pos1, pos2, y, route, x, mod, j_g, *, tc, tiles_per_block, final_g=None):
    t, d = x.shape
    rows = mod.shape[2]
    width = route.shape[-1]
    in_specs = [pl.BlockSpec(memory_space=pl.ANY),
                pl.BlockSpec((tc, width), lambda m, p1, p2: (m, 0)),
                pl.BlockSpec((tc, d), lambda m, p1, p2: (m, 0)),
                _mod_spec(j_g, rows, d, tiles_per_block, tc)]
    args = [pos1, pos2, y, route, x, mod]
    if final_g is not None:
        in_specs.append(pl.BlockSpec((1, d), lambda m, p1, p2: (0, 0)))
        args.append(final_g.reshape(1, d))
    return pl.pallas_call(
        functools.partial(_combine_kernel, final_norm=final_g is not None),
        grid_spec=pltpu.PrefetchScalarGridSpec(
            num_scalar_prefetch=2, grid=(t // tc,),
            in_specs=in_specs,
            out_specs=pl.BlockSpec((tc, d), lambda m, p1, p2: (m, 0)),
            scratch_shapes=[pltpu.VMEM((tc, d), F32), pltpu.VMEM((tc, d), F32), pltpu.SemaphoreType.DMA((1,))]),
        out_shape=jax.ShapeDtypeStruct((t, d), F32),
        compiler_params=_params("arbitrary"),
        name="moe_combine",
    )(*args)


def _routed_moe(runs, l, g, w_router, w1, w3, w2, lm, *, tg, final_g=None):
    ne = w_router.shape[-1]
    w_r = jnp.pad(w_router[lm], ((0, 0), (0, V7X_LANES - ne)))
    t_all = sum(r["x"].shape[0] for r in runs)
    routed, hn_all, row0 = [], None, 0
    for r in runs:
        hn_all, route, cnt = _router(r["x"], r["mods"][l], 3, g, w_r, l, n_experts=ne, hn_rows=t_all, hn_row0=row0,
                                     hn_buf=hn_all, **r["tiled"](min(r["tm"], 512)))
        routed.append((hn_all, route, cnt))
        row0 += r["x"].shape[0]
    counts_each = [cnt[0, :ne].astype(jnp.int32) for _, _, cnt in routed]
    tiles_e = (sum(counts_each) + (tg - 1)) // tg
    tile_end = jnp.cumsum(tiles_e)
    row_start = (tile_end - tiles_e) * tg
    n_tiles = -(-TOP_K * t_all // tg) + ne
    tile_expert = jnp.minimum(jnp.sum(jnp.arange(n_tiles)[:, None] >= tile_end[None, :], axis=1), ne - 1)
    tile_expert = tile_expert.astype(jnp.int32)
    n_used = tile_end[-1:].astype(jnp.int32)
    pos, toks, first_row, tok0 = [], [], row_start, 0
    for (_, route, _), c_e, r in zip(routed, counts_each, runs):
        col = lambda j: route[:, j].astype(jnp.int32)
        pos.append((first_row[col(ROUTE_I1)] + col(ROUTE_RANK1), first_row[col(ROUTE_I2)] + col(ROUTE_RANK2)))
        toks.append(tok0 + jnp.arange(route.shape[0], dtype=jnp.int32))
        first_row, tok0 = first_row + c_e, tok0 + route.shape[0]
    tok_of_row = jnp.zeros((n_tiles * tg,), jnp.int32).at[jnp.concatenate([q for pq in pos for q in pq])].set(
        jnp.concatenate([tk for tk in toks for _ in range(TOP_K)]), unique_indices=True)
    xs = _dispatch(tok_of_row, n_used, hn_all, tg=tg)
    act = _grouped_up(tile_expert, n_used, xs, w1, w3, lm, tg=tg, tn=1024)
    y = _grouped_down(tile_expert, n_used, act, w2, lm, tg=tg, tn=1024)
    out = []
    for (p1, p2), (_, route, _), r in zip(pos, routed, runs):
        tc = min(r["tm"], 256)
        out.append(_combine(p1, p2, y, route, r["x"], r["mods"][l], 5, tc=tc,
                            tiles_per_block=r["tiled"](tc)["tiles_per_block"], final_g=final_g))
    return out


def _final_norm_kernel(x_ref, g_ref, o_ref):
    o_ref[...] = _rmsnorm(x_ref[...], g_ref[...])


def _final_norm(x, g, *, tm):
    t, d = x.shape
    return pl.pallas_call(
        _final_norm_kernel,
        grid=(t // tm,),
        in_specs=[pl.BlockSpec((tm, d), lambda m: (m, 0)), pl.BlockSpec((1, d), lambda m: (0, 0))],
        out_specs=pl.BlockSpec((tm, d), lambda m: (m, 0)),
        out_shape=jax.ShapeDtypeStruct((t, d), F32),
        compiler_params=_params("arbitrary"),
        name="final_norm",
    )(x, g.reshape(1, d))


def _make_run(x, mods, state, *, bn, seq, time_major, tm):
    def tiled(tile):
        return dict(tm=tile, tiles_per_block=1 if time_major else seq // tile)

    assert seq >= CONV_W - 1
    cols = dict(gate_merge=256, glu=512) if tm > 512 else dict(gate_merge=512, glu=1024)
    return dict(x=x, mods=mods, state=state, bn=bn, seq=seq, time_major=time_major, tm=tm, tiled=tiled, cols=cols,
                s5_re=[], s5_im=[], rg=[], conv=[])


def _mixer(r, p, l):
    x, mod, state, bn, seq = r["x"], r["mods"][l], r["state"], r["bn"], r["seq"]
    depth = p["w_in"].shape[0]
    c = p["s5_w_glu"].shape[-1]
    kw = r["tiled"](r["tm"])
    nconv = CONV_W - 1
    tb = p["s5_tables"][l]
    wa, wi = p["rg_tables"][l]
    proj = _normed_linear(x, mod, 0, p["norm_mix"], l, [p["w_in"]], l, mode="in", tn=1024, plain_cols=2 * c, **kw)
    h0re = state["s5_re"][l].reshape(bn, -1)
    h0im = state["s5_im"][l].reshape(bn, -1)
    s5_args = (tb, p["s5_d"].reshape(depth, c), p["s5_w_glu"], p["s5_b_glu"], l)
    rg_args = (p["rg_conv_w"], p["rg_conv_b"], wa, wi, p["rg_b_a"].reshape(depth, c), p["rg_b_i"].reshape(depth, c),
               p["rg_lam"], l)
    if r["time_major"]:
        y_s5, hre, him = _s5_step(proj, h0re, h0im, *s5_args, bn=bn, seq=seq)
        y_rg, hrg = _rg_step(proj, state["conv"][l].transpose(1, 0, 2), state["rg"][l], *rg_args, bn=bn, seq=seq)
        r["conv"].append(proj[:, c:2 * c].reshape(seq, bn, c)[seq - nconv:].transpose(1, 0, 2))
    else:
        y_s5, hre, him = _s5_seq(proj, h0re, h0im, *s5_args, bn=bn, seq=seq)
        y_rg, hrg = _rg_seq(proj, state["conv"][l], state["rg"][l], *rg_args, bn=bn, seq=seq)
        tail_rows = [seq - SEG_TILE + (SEG_LEN - nconv + f) * V7X_SUBLANES + V7X_SUBLANES - 1 for f in range(nconv)]
        proj3 = proj.reshape(bn, seq, 3 * c)
        r["conv"].append(jnp.stack([proj3[:, row, c:2 * c] for row in tail_rows], axis=1))
    r["s5_re"].append(hre.reshape(state["s5_re"][l].shape))
    r["s5_im"].append(him.reshape(state["s5_im"][l].shape))
    r["rg"].append(hrg)
    merged = _gate_merge(x, mod, p["norm_mix"], p["w_gate"], p["b_gate"], y_s5, y_rg, p["w_br_s5"], p["w_br_rg"], l,
                         tn=r["cols"]["gate_merge"], **kw)
    r["x"] = _linear_residual(merged, p["w_out"], l, x, mod, 2, tn=1024, tk=2048, **kw)


def _dense_ffn(r, p, l):
    kw = r["tiled"](r["tm"])
    mod = r["mods"][l]
    act = _normed_linear(r["x"], mod, 3, p["norm_ffn"], l, [p["ffn_w1"], p["ffn_w3"]], l // 2, mode="glu",
                         tn=r["cols"]["glu"], out_dtype=ACT_DTYPE, **kw)
    r["x"] = _linear_residual(act, p["ffn_w2"], l // 2, r["x"], mod, 5, tn=1024, tk=2048, **kw)


def _forward(runs, p):
    depth = p["w_in"].shape[0]
    for l in range(depth):
        for r in runs:
            _mixer(r, p, l)
        if l % 2 == 0:
            for r in runs:
                _dense_ffn(r, p, l)
        else:
            final_g = p["norm_f"] if l == depth - 1 else None
            new_x = _routed_moe(runs, l, p["norm_ffn"], p["moe_router"], p["moe_w1"], p["moe_w3"], p["moe_w2"], l // 2,
                                tg=MOE_ROW_TILE, final_g=final_g)
            for r, x in zip(runs, new_x):
                r["x"] = x
    if depth % 2 == 0:
        return [r["x"] for r in runs]
    return [_final_norm(r["x"], p["norm_f"], tm=min(r["tm"], 256)) for r in runs]


def kernel(x_prompt, x_sample, c_prompt, c_sample, state_s5_re, state_s5_im, state_rglru, state_conv, norm_mix, norm_ffn, norm_f, w_ada, b_ada, w_in, s5_lam_re, s5_lam_im, s5_log_dt, s5_b_re, s5_b_im, s5_c_re, s5_c_im, s5_d, s5_w_glu, s5_b_glu, rg_conv_w, rg_conv_b, rg_w_a, rg_b_a, rg_w_i, rg_b_i, rg_lam, w_gate, b_gate, w_br_s5, w_br_rg, w_out, ffn_w1, ffn_w3, ffn_w2, moe_router, moe_w1, moe_w3, moe_w2):
    p = dict(norm_mix=norm_mix, norm_ffn=norm_ffn, norm_f=norm_f, w_in=w_in,
             s5_lam_re=s5_lam_re, s5_lam_im=s5_lam_im, s5_log_dt=s5_log_dt, s5_b_re=s5_b_re, s5_b_im=s5_b_im,
             s5_c_re=s5_c_re, s5_c_im=s5_c_im, s5_d=s5_d, s5_w_glu=s5_w_glu, s5_b_glu=s5_b_glu,
             rg_conv_w=rg_conv_w, rg_conv_b=rg_conv_b, rg_w_a=rg_w_a, rg_b_a=rg_b_a, rg_w_i=rg_w_i, rg_b_i=rg_b_i,
             rg_lam=rg_lam, w_gate=w_gate, b_gate=b_gate, w_br_s5=w_br_s5, w_br_rg=w_br_rg, w_out=w_out,
             ffn_w1=ffn_w1, ffn_w3=ffn_w3, ffn_w2=ffn_w2, moe_router=moe_router, moe_w1=moe_w1, moe_w3=moe_w3,
             moe_w2=moe_w2)
    for name in ("w_out", "ffn_w2"):
        p[name] = p[name].astype(ACT_DTYPE)
    depth = w_in.shape[0]
    p["s5_tables"] = [_s5_tables(s5_lam_re[l], s5_lam_im[l], s5_log_dt[l], s5_b_re[l], s5_b_im[l], s5_c_re[l],
                                 s5_c_im[l]) for l in range(depth)]
    p["rg_tables"] = [_rg_tables(rg_w_a[l], rg_w_i[l]) for l in range(depth)]
    bp, sp, d = x_prompt.shape
    bs, ss, _ = x_sample.shape
    g, pst = state_s5_re.shape[2:]
    c = state_rglru.shape[-1]

    n_c = bp + bs
    rows_c = -(-n_c // V7X_SUBLANES) * V7X_SUBLANES
    c_all = jnp.pad(jnp.concatenate([c_prompt, c_sample], axis=0), ((0, rows_c - n_c), (0, 0)))
    mod_all = _ada_mod(c_all, w_ada, b_ada)
    mods_p, mods_s = [], []
    for l in range(depth):
        mp = mod_all[l, :bp].reshape(bp, 6, 1, d).transpose(1, 0, 2, 3)
        ms = mod_all[l, bp:n_c].reshape(bs, 6, d).transpose(1, 0, 2)
        ms = jnp.tile(ms, (1, ss, 1)).reshape(6, 1, ss * bs, d)
        mods_p.append(mp)
        mods_s.append(ms)

    zero_state = dict(s5_re=jnp.zeros((depth, bp, g, pst), F32), s5_im=jnp.zeros((depth, bp, g, pst), F32),
                      rg=jnp.zeros((depth, bp, c), F32), conv=jnp.zeros((depth, bp, CONV_W - 1, c), x_prompt.dtype))
    seg_shape = (bp, sp // SEG_TILE, V7X_SUBLANES, SEG_LEN, d)
    x_p = x_prompt.reshape(seg_shape).transpose(0, 1, 3, 2, 4).reshape(bp * sp, d)
    run_p = _make_run(x_p, mods_p, zero_state, bn=bp, seq=sp, time_major=False, tm=min(PROMPT_TILE, sp))
    sample_state = dict(s5_re=state_s5_re, s5_im=state_s5_im, rg=state_rglru, conv=state_conv)
    x_s = x_sample.transpose(1, 0, 2).reshape(ss * bs, d)
    run_s = _make_run(x_s, mods_s, sample_state, bn=bs, seq=ss, time_major=True, tm=ss * bs)
    y_p, y_s = _forward([run_p, run_s], p)
    y_p = y_p.reshape(bp, sp // SEG_TILE, SEG_LEN, V7X_SUBLANES, d).transpose(0, 1, 3, 2, 4).reshape(bp, sp, d)
    y_s = y_s.reshape(ss, bs, d).transpose(1, 0, 2)
    new_state = lambda r: tuple(jnp.stack(r[k]) for k in ("s5_re", "s5_im", "rg", "conv"))
    return (y_p, y_s) + new_state(run_p) + new_state(run_s)
```

```python
import functools
import math

import jax
import jax.numpy as jnp
from jax import lax
from jax.experimental import pallas as pl
from jax.experimental.pallas import tpu as pltpu

F32 = jnp.float32
ACT_DTYPE = jnp.bfloat16
EPS = 1e-6
RG_C = 8.0
CONV_W = 4
S5_GROUP = 16
S5_STATE = 64
RG_HEAD_DIM = 64
TOP_K = 2

V7X_LANES = 128
V7X_SUBLANES = 8
V7X_MXU_DIM = 256
VMEM_LIMIT_BYTES = 56 * 1024 * 1024

GELU_C = math.sqrt(2.0 / math.pi)


def _gelu(x):
    return 0.5 * x * (1.0 + jnp.tanh(GELU_C * (x + 0.044715 * (x * x * x))))


def _sigmoid(x):
    return 1.0 / (1.0 + jnp.exp(-x))


def _silu(x):
    return x * _sigmoid(x)


def _dot(a, b):
    return jnp.dot(a, b, preferred_element_type=F32)


def _params(*sem):
    return pltpu.CompilerParams(dimension_semantics=sem, vmem_limit_bytes=VMEM_LIMIT_BYTES)


def _mod_spec(j, rows, width, tiles_per_block, tile, ncol_axis=None, tile_of=lambda m: m):
    col = (lambda idx: 0) if ncol_axis is None else (lambda idx: idx[ncol_axis])
    if rows == 1:
        return pl.BlockSpec((None, None, 1, width), lambda m, *idx: (j, tile_of(m) // tiles_per_block, 0, col(idx)))
    return pl.BlockSpec((None, None, tile, width), lambda m, *idx: (j, 0, tile_of(m), col(idx)))


ROW_CHUNK = 128
PROMPT_TILE = 1024
SEG_LEN = 32
SEG_TILE = SEG_LEN * V7X_SUBLANES
SCAN_UNROLL = True


def _rmsnorm(x, g):
    ms = jnp.mean(x * x, axis=-1, keepdims=True)
    return (x * lax.rsqrt(ms + EPS)) * g


def _norm_mod_rows(x_ref, g_ref, sc_ref, sh_ref, o_ref):
    tm = x_ref.shape[0]
    chunk = min(ROW_CHUNK, tm)
    per_row = sc_ref.shape[0] != 1

    def body(i, carry):
        rows = pl.ds(pl.multiple_of(i * chunk, chunk), chunk)
        sc = sc_ref[rows, :] if per_row else sc_ref[...]
        sh = sh_ref[rows, :] if per_row else sh_ref[...]
        o_ref[rows, :] = (_rmsnorm(x_ref[rows, :], g_ref[...]) * (1.0 + sc) + sh).astype(o_ref.dtype)
        return carry

    lax.fori_loop(0, tm // chunk, body, 0)


def _ada_kernel(c_ref, w_ref, b_ref, o_ref):
    o_ref[...] = _dot(_silu(c_ref[...]), w_ref[...]) + b_ref[...]


def _ada_mod(c_all, w_ada, b_ada, *, tn=1024):
    depth, d, n6 = w_ada.shape
    rows = c_all.shape[0]
    return pl.pallas_call(
        _ada_kernel,
        grid=(depth, n6 // tn),
        in_specs=[pl.BlockSpec((rows, d), lambda l, n: (0, 0)),
                  pl.BlockSpec((None, d, tn), lambda l, n: (l, 0, n)),
                  pl.BlockSpec((None, 1, tn), lambda l, n: (l, 0, n))],
        out_specs=pl.BlockSpec((None, rows, tn), lambda l, n: (l, 0, n)),
        out_shape=jax.ShapeDtypeStruct((depth, rows, n6), F32),
        compiler_params=_params("arbitrary", "arbitrary"),
        name="ada_mod",
    )(c_all, w_ada, b_ada.reshape(depth, 1, n6))


def _normed_linear_kernel(*refs, mode, n_plain):
    if mode == "glu":
        x_ref, sh_ref, sc_ref, g_ref, w1_ref, w3_ref, o_ref, hn_ref = refs
    else:
        x_ref, sh_ref, sc_ref, g_ref, w_ref, o_ref, hn_ref = refs
    n = pl.program_id(1)

    @pl.when(n == 0)
    def _():
        _norm_mod_rows(x_ref, g_ref, sc_ref, sh_ref, hn_ref)

    h = hn_ref[...]
    if mode == "glu":
        o_ref[...] = (_silu(_dot(h, w1_ref[...])) * _dot(h, w3_ref[...])).astype(o_ref.dtype)
    else:
        z = _dot(h, w_ref[...])

        @pl.when(n < n_plain)
        def _():
            o_ref[...] = z

        @pl.when(n >= n_plain)
        def _():
            o_ref[...] = _gelu(z)


def _normed_linear(x, mod, j_sh, g, l_g, ws, l, *, mode, tm, tn, tiles_per_block, plain_cols=0, out_dtype=F32):
    t, d = x.shape
    n_out = ws[0].shape[-1]
    rows = mod.shape[2]
    w_spec = pl.BlockSpec((None, d, tn), lambda m, n: (l, 0, n))
    in_specs = [pl.BlockSpec((tm, d), lambda m, n: (m, 0)),
                _mod_spec(j_sh, rows, d, tiles_per_block, tm),
                _mod_spec(j_sh + 1, rows, d, tiles_per_block, tm),
                pl.BlockSpec((None, 1, d), lambda m, n: (l_g, 0, 0))]
    args = [x, mod, mod, g.reshape(g.shape[0], 1, d)]
    for w in ws:
        in_specs.append(w_spec)
        args.append(w)
    return pl.pallas_call(
        functools.partial(_normed_linear_kernel, mode=mode, n_plain=plain_cols // tn),
        grid=(t // tm, n_out // tn),
        in_specs=in_specs,
        out_specs=pl.BlockSpec((tm, tn), lambda m, n: (m, n)),
        out_shape=jax.ShapeDtypeStruct((t, n_out), out_dtype),
        scratch_shapes=[pltpu.VMEM((tm, d), ws[0].dtype)],
        compiler_params=_params("arbitrary", "arbitrary"),
        name="normed_linear_" + mode,
    )(*args)


def _s5_tables(lam_re, lam_im, log_dt, b_re, b_im, c_re, c_im):
    g, p = lam_re.shape
    h = b_re.shape[-1]
    gpc = V7X_MXU_DIM // h
    nck = g // gpc
    dt = jnp.exp(log_dt)[:, None]
    mag = jnp.exp(lam_re * dt)
    lb_re = mag * jnp.cos(lam_im * dt)
    lb_im = mag * jnp.sin(lam_im * dt)
    den = lam_re * lam_re + lam_im * lam_im
    cf_re = ((lb_re - 1.0) * lam_re + lb_im * lam_im) / den
    cf_im = (lb_im * lam_re - (lb_re - 1.0) * lam_im) / den
    bb_re = cf_re[:, :, None] * b_re - cf_im[:, :, None] * b_im
    bb_im = cf_re[:, :, None] * b_im + cf_im[:, :, None] * b_re
    by_ghp = lambda m: m.transpose(0, 2, 1).reshape(nck, gpc * h, p)
    by_gph = lambda m: m.transpose(0, 2, 1).reshape(nck, gpc * p, h)

    def powers(steps):
        k = jnp.asarray(steps, F32)[:, None]
        zr = (lam_re * dt).reshape(1, g * p)
        zi = (lam_im * dt).reshape(1, g * p)
        return jnp.exp(k * zr) * jnp.cos(k * zi), jnp.exp(k * zr) * jnp.sin(k * zi)

    sub = V7X_SUBLANES
    pw_re, pw_im = powers(range(1, SEG_LEN + 1))
    sg_re, sg_im = powers(range(SEG_LEN, SEG_LEN * (sub + 1), SEG_LEN))
    row = jnp.arange(sub)[:, None]
    segtabs = []
    for k in (1, 2, 4):
        segtabs += [jnp.where(row >= k, sg_re[k - 1:k], 0.0), jnp.where(row >= k, sg_im[k - 1:k], 0.0)]
    segtabs += [sg_re, sg_im]
    return dict(wb_re=_block_diag(by_ghp(bb_re), gpc), wb_im=_block_diag(by_ghp(bb_im), gpc),
                wc_re=_block_diag(by_gph(c_re), gpc), wc_imneg=_block_diag(by_gph(-c_im), gpc),
                lam8=jnp.stack([jnp.broadcast_to(pw_re[0:1], (sub, g * p)),
                                jnp.broadcast_to(pw_im[0:1], (sub, g * p))]),
                segtabs=jnp.stack(segtabs),
                fix=jnp.stack([jnp.repeat(pw_re, sub, axis=0), jnp.repeat(pw_im, sub, axis=0)]),
                lb_re=pw_re[0:1], lb_im=pw_im[0:1])


def _block_diag(src, nblk):
    _, rows, cb = src.shape
    r = jnp.arange(rows)[:, None] // (rows // nblk)
    c = jnp.arange(nblk * cb)[None, :] // cb
    return jnp.where(r == c, jnp.tile(src, (1, 1, nblk)), 0.0)


def _s5_input_proj(u, wbre_ref, wbim_ref, bre_ref, bim_ref):
    nck, kc, nc = wbre_ref.shape
    for k in range(nck):
        uk = u[:, k * kc:(k + 1) * kc]
        bre_ref[:, k * nc:(k + 1) * nc] = _dot(uk, wbre_ref[k])
        bim_ref[:, k * nc:(k + 1) * nc] = _dot(uk, wbim_ref[k])


def _s5_output(u, bre_ref, bim_ref, wcre_ref, wcim_ref, d_ref, wglu_ref, bglu_ref, y_ref):
    nck, nc, kc = wcre_ref.shape
    for k in range(nck):
        yk = _dot(bre_ref[:, k * nc:(k + 1) * nc], wcre_ref[k]) + _dot(bim_ref[:, k * nc:(k + 1) * nc], wcim_ref[k])
        yk = yk + d_ref[:, k * kc:(k + 1) * kc] * u[:, k * kc:(k + 1) * kc]
        bre_ref[:, k * kc:(k + 1) * kc] = _gelu(yk)
    yg = bre_ref[:, 0:nck * kc]
    y_ref[...] = (yg * _sigmoid(_dot(yg, wglu_ref[...]) + bglu_ref[...])).astype(y_ref.dtype)


def _row_block(j):
    return pl.ds(pl.multiple_of(j * V7X_SUBLANES, V7X_SUBLANES), V7X_SUBLANES)


def _s5_seq_kernel(u_ref, h0re_ref, h0im_ref, wbre_ref, wbim_ref, wcre_ref, wcim_ref, lam8_ref, seg_ref, fix_ref,
                   d_ref, wglu_ref, bglu_ref, y_ref, hre_ref, him_ref, bre_ref, bim_ref, cre_ref, cim_ref,
                   *, lane_chunk):
    s = pl.program_id(1)
    ns = cre_ref.shape[1]

    @pl.when(s == 0)
    def _():
        cre_ref[...] = jnp.broadcast_to(h0re_ref[...], cre_ref.shape)
        cim_ref[...] = jnp.broadcast_to(h0im_ref[...], cim_ref.shape)

    u = u_ref[...]
    _s5_input_proj(u, wbre_ref, wbim_ref, bre_ref, bim_ref)
    first_sublane = lax.broadcasted_iota(jnp.int32, (V7X_SUBLANES, lane_chunk), 0) == 0
    last = V7X_SUBLANES - 1

    for c in range(ns // lane_chunk):
        sl = slice(c * lane_chunk, (c + 1) * lane_chunk)
        lr, li = lam8_ref[0, :, sl], lam8_ref[1, :, sl]

        def local(j, st, sl=sl, lr=lr, li=li):
            sr, si = st
            rows = _row_block(j)
            sr, si = (lr * sr - li * si) + bre_ref[rows, sl], (lr * si + li * sr) + bim_ref[rows, sl]
            bre_ref[rows, sl] = sr
            bim_ref[rows, sl] = si
            return sr, si

        zero = jnp.zeros((V7X_SUBLANES, lane_chunk), F32)
        xr, xi = lax.fori_loop(0, SEG_LEN, local, (zero, zero), unroll=SCAN_UNROLL)
        for i, k in enumerate((1, 2, 4)):
            tr, ti = seg_ref[2 * i, :, sl], seg_ref[2 * i + 1, :, sl]
            pr, pi = pltpu.roll(xr, k, 0), pltpu.roll(xi, k, 0)
            xr, xi = xr + (tr * pr - ti * pi), xi + (tr * pi + ti * pr)
        cr, ci = cre_ref[:, sl], cim_ref[:, sl]
        tr, ti = seg_ref[6, :, sl], seg_ref[7, :, sl]
        gr = xr + (tr * cr - ti * ci)
        gi = xi + (tr * ci + ti * cr)
        h0r = jnp.where(first_sublane, cr, pltpu.roll(gr, 1, 0))
        h0i = jnp.where(first_sublane, ci, pltpu.roll(gi, 1, 0))
        cre_ref[:, sl] = jnp.broadcast_to(gr[last:last + 1], gr.shape)
        cim_ref[:, sl] = jnp.broadcast_to(gi[last:last + 1], gi.shape)

        def fixup(j, carry, sl=sl, h0r=h0r, h0i=h0i):
            rows = _row_block(j)
            fr, fi = fix_ref[0, rows, sl], fix_ref[1, rows, sl]
            bre_ref[rows, sl] += fr * h0r - fi * h0i
            bim_ref[rows, sl] += fr * h0i + fi * h0r
            return carry

        lax.fori_loop(0, SEG_LEN, fixup, 0, unroll=SCAN_UNROLL)

    _s5_output(u, bre_ref, bim_ref, wcre_ref, wcim_ref, d_ref, wglu_ref, bglu_ref, y_ref)

    @pl.when(s == pl.num_programs(1) - 1)
    def _():
        hre_ref[...] = cre_ref[0:1, :]
        him_ref[...] = cim_ref[0:1, :]


def _const_spec(a):
    nd = a.ndim
    return pl.BlockSpec(a.shape, lambda *_: (0,) * nd, pipeline_mode=pl.Buffered(1))


def _s5_seq(proj, h0re, h0im, tb, d_skip, w_glu, b_glu, l, *, bn, seq, lane_chunk=256):
    c = w_glu.shape[-1]
    ns = h0re.shape[-1]
    tm = SEG_TILE
    n_s = seq // tm
    consts = [tb["wb_re"], tb["wb_im"], tb["wc_re"], tb["wc_imneg"], tb["lam8"], tb["segtabs"], tb["fix"]]
    in_specs = ([pl.BlockSpec((tm, c), lambda b, s: (b * n_s + s, 0)),
                 pl.BlockSpec((None, 1, ns), lambda b, s: (b, 0, 0)),
                 pl.BlockSpec((None, 1, ns), lambda b, s: (b, 0, 0))]
                + [_const_spec(a) for a in consts]
                + [pl.BlockSpec((None, 1, c), lambda b, s: (l, 0, 0)),
                   pl.BlockSpec((None, c, c), lambda b, s: (l, 0, 0), pipeline_mode=pl.Buffered(1)),
                   pl.BlockSpec((None, 1, c), lambda b, s: (l, 0, 0))])
    y, hre, him = pl.pallas_call(
        functools.partial(_s5_seq_kernel, lane_chunk=lane_chunk),
        grid=(bn, n_s),
        in_specs=in_specs,
        out_specs=[pl.BlockSpec((tm, c), lambda b, s: (b * n_s + s, 0)),
                   pl.BlockSpec((None, 1, ns), lambda b, s: (b, 0, 0)),
                   pl.BlockSpec((None, 1, ns), lambda b, s: (b, 0, 0))],
        out_shape=[jax.ShapeDtypeStruct((bn * seq, c), F32),
                   jax.ShapeDtypeStruct((bn, 1, ns), F32),
                   jax.ShapeDtypeStruct((bn, 1, ns), F32)],
        scratch_shapes=[pltpu.VMEM((tm, ns), F32), pltpu.VMEM((tm, ns), F32),
                        pltpu.VMEM((V7X_SUBLANES, ns), F32), pltpu.VMEM((V7X_SUBLANES, ns), F32)],
        compiler_params=_params("arbitrary", "arbitrary"),
        name="s5_seq",
    )(proj, h0re.reshape(bn, 1, ns), h0im.reshape(bn, 1, ns), *consts,
      d_skip.reshape(d_skip.shape[0], 1, c), w_glu, b_glu.reshape(b_glu.shape[0], 1, c))
    return y, hre.reshape(bn, ns), him.reshape(bn, ns)


def _s5_step_kernel(u_ref, h0re_ref, h0im_ref, wbre_ref, wbim_ref, wcre_ref, wcim_ref, lbre_ref, lbim_ref, d_ref,
                    wglu_ref, bglu_ref, y_ref, hre_ref, him_ref, bre_ref, bim_ref, *, seq, bn, lane_chunk):
    u = u_ref[...]
    _s5_input_proj(u, wbre_ref, wbim_ref, bre_ref, bim_ref)
    ns = bre_ref.shape[1]
    for c in range(ns // lane_chunk):
        sl = slice(c * lane_chunk, (c + 1) * lane_chunk)
        lr = lbre_ref[:, sl]
        li = lbim_ref[:, sl]
        hr = h0re_ref[:, sl]
        hi = h0im_ref[:, sl]
        for t in range(seq):
            rows = slice(t * bn, (t + 1) * bn)
            hr, hi = (lr * hr - li * hi) + bre_ref[rows, sl], (lr * hi + li * hr) + bim_ref[rows, sl]
            bre_ref[rows, sl] = hr
            bim_ref[rows, sl] = hi
        hre_ref[:, sl] = hr
        him_ref[:, sl] = hi
    _s5_output(u, bre_ref, bim_ref, wcre_ref, wcim_ref, d_ref, wglu_ref, bglu_ref, y_ref)


def _s5_step(proj, h0re, h0im, tb, d_skip, w_glu, b_glu, l, *, bn, seq, lane_chunk=128):
    c = w_glu.shape[-1]
    ns = h0re.shape[-1]
    t = bn * seq
    consts = [tb["wb_re"], tb["wb_im"], tb["wc_re"], tb["wc_imneg"], tb["lb_re"], tb["lb_im"]]
    in_specs = ([pl.BlockSpec((t, c), lambda i: (0, 0)),
                 pl.BlockSpec((bn, ns), lambda i: (0, 0)),
                 pl.BlockSpec((bn, ns), lambda i: (0, 0))]
                + [_const_spec(a) for a in consts]
                + [pl.BlockSpec((None, 1, c), lambda i: (l, 0, 0)),
                   pl.BlockSpec((None, c, c), lambda i: (l, 0, 0), pipeline_mode=pl.Buffered(1)),
                   pl.BlockSpec((None, 1, c), lambda i: (l, 0, 0))])
    return pl.pallas_call(
        functools.partial(_s5_step_kernel, seq=seq, bn=bn, lane_chunk=lane_chunk),
        grid=(1,),
        in_specs=in_specs,
        out_specs=[pl.BlockSpec((t, c), lambda i: (0, 0)),
                   pl.BlockSpec((bn, ns), lambda i: (0, 0)),
                   pl.BlockSpec((bn, ns), lambda i: (0, 0))],
        out_shape=[jax.ShapeDtypeStruct((t, c), F32),
                   jax.ShapeDtypeStruct((bn, ns), F32),
                   jax.ShapeDtypeStruct((bn, ns), F32)],
        scratch_shapes=[pltpu.VMEM((t, ns), F32), pltpu.VMEM((t, ns), F32)],
        compiler_params=_params("arbitrary"),
        name="s5_step",
    )(proj, h0re, h0im, *consts,
      d_skip.reshape(d_skip.shape[0], 1, c), w_glu, b_glu.reshape(b_glu.shape[0], 1, c))


def _rg_tables(w_a, w_i):
    nh, hd, _ = w_a.shape
    hpc = V7X_MXU_DIM // hd
    expand = lambda w: _block_diag(w.reshape(nh // hpc, hpc * hd, hd), hpc)
    return expand(w_a), expand(w_i)


def _rg_gates(xc, wa_ref, wi_ref, ba_ref, bi_ref, lam_ref, a_ref, b_ref, rows):
    nck, kc, _ = wa_ref.shape
    lam = lam_ref[...]
    neg = -lam
    softplus = jnp.maximum(neg, 0.0) + jnp.log(1.0 + jnp.exp(-jnp.abs(neg)))
    for k in range(nck):
        cs = slice(k * kc, (k + 1) * kc)
        xk = xc[:, cs]
        r = _sigmoid(_dot(xk, wa_ref[k]) + ba_ref[:, cs])
        i = _sigmoid(_dot(xk, wi_ref[k]) + bi_ref[:, cs])
        log_a = (-RG_C * r) * softplus[:, cs]
        a_ref[rows, cs] = jnp.exp(log_a)
        th = jnp.tanh(log_a)
        b_ref[rows, cs] = jnp.sqrt(-2.0 * th / (1.0 - th)) * (i * xk)


def _rg_seq_kernel(x_ref, gy_ref, conv0_ref, h0_ref, cw_ref, cb_ref, wa_ref, wi_ref, ba_ref, bi_ref, lam_ref,
                   y_ref, hout_ref, xe_ref, a_ref, b_ref, c_ref, tail_ref):
    s = pl.program_id(1)
    sub = V7X_SUBLANES
    nconv = conv0_ref.shape[0]
    tm, c = x_ref.shape
    first_sublane = lax.broadcasted_iota(jnp.int32, (sub, c), 0) == 0
    last = sub - 1

    @pl.when(s == 0)
    def _():
        tail_ref[0:nconv, :] = conv0_ref[...]
        c_ref[...] = jnp.broadcast_to(h0_ref[...], c_ref.shape)

    for f in range(nconv):
        blk = x_ref[(SEG_LEN - nconv + f) * sub:(SEG_LEN - nconv + f + 1) * sub, :]
        xe_ref[f * sub:(f + 1) * sub, :] = jnp.where(first_sublane, tail_ref[f:f + 1, :], pltpu.roll(blk, 1, 0))
    for f in range(nconv):
        tail_ref[f:f + 1, :] = x_ref[(SEG_LEN - nconv + f) * sub + last:(SEG_LEN - nconv + f + 1) * sub, :]
    xe_ref[nconv * sub:nconv * sub + tm, :] = x_ref[...]
    acc = cw_ref[0:1, :] * xe_ref[0:tm, :]
    for k in range(1, nconv + 1):
        acc = acc + cw_ref[k:k + 1, :] * xe_ref[k * sub:k * sub + tm, :]
    xc = cb_ref[...] + acc
    _rg_gates(xc, wa_ref, wi_ref, ba_ref, bi_ref, lam_ref, a_ref, b_ref, slice(None))

    def local(j, st):
        h, p = st
        rows = _row_block(j)
        a = a_ref[rows, :]
        h = a * h + b_ref[rows, :]
        p = p * a
        b_ref[rows, :] = h
        a_ref[rows, :] = p
        return h, p

    e, p = lax.fori_loop(0, SEG_LEN, local, (jnp.zeros((sub, c), F32), jnp.ones((sub, c), F32)), unroll=SCAN_UNROLL)
    row = lax.broadcasted_iota(jnp.int32, (sub, c), 0)
    for k in (1, 2, 4):
        keep = row >= k
        e = jnp.where(keep, e + p * pltpu.roll(e, k, 0), e)
        p = jnp.where(keep, p * pltpu.roll(p, k, 0), p)
    carry = c_ref[...]
    g = e + p * carry
    h_in = jnp.where(first_sublane, carry, pltpu.roll(g, 1, 0))
    c_ref[...] = jnp.broadcast_to(g[last:last + 1], g.shape)

    def fixup(j, cc):
        rows = _row_block(j)
        b_ref[rows, :] = (b_ref[rows, :] + a_ref[rows, :] * h_in) * gy_ref[rows, :]
        return cc

    lax.fori_loop(0, SEG_LEN, fixup, 0, unroll=SCAN_UNROLL)
    y_ref[...] = b_ref[...].astype(y_ref.dtype)

    @pl.when(s == pl.num_programs(1) - 1)
    def _():
        hout_ref[...] = c_ref[0:1, :]


def _rg_seq(proj, conv0, h0, conv_w, conv_b, wa, wi, b_a, b_i, lam, l, *, bn, seq):
    c = h0.shape[-1]
    tm = SEG_TILE
    n_s = seq // tm
    depth = conv_w.shape[0]
    vec = lambda a: a.reshape(depth, 1, c)
    y, hout = pl.pallas_call(
        _rg_seq_kernel,
        grid=(bn, n_s),
        in_specs=[pl.BlockSpec((tm, c), lambda b, s: (b * n_s + s, 1)),
                  pl.BlockSpec((tm, c), lambda b, s: (b * n_s + s, 2)),
                  pl.BlockSpec((None, CONV_W - 1, c), lambda b, s: (b, 0, 0)),
                  pl.BlockSpec((None, 1, c), lambda b, s: (b, 0, 0)),
                  pl.BlockSpec((None, CONV_W, c), lambda b, s: (l, 0, 0)),
                  pl.BlockSpec((None, 1, c), lambda b, s: (l, 0, 0)),
                  _const_spec(wa), _const_spec(wi),
                  pl.BlockSpec((None, 1, c), lambda b, s: (l, 0, 0)),
                  pl.BlockSpec((None, 1, c), lambda b, s: (l, 0, 0)),
                  pl.BlockSpec((None, 1, c), lambda b, s: (l, 0, 0))],
        out_specs=[pl.BlockSpec((tm, c), lambda b, s: (b * n_s + s, 0)),
                   pl.BlockSpec((None, 1, c), lambda b, s: (b, 0, 0))],
        out_shape=[jax.ShapeDtypeStruct((bn * seq, c), F32), jax.ShapeDtypeStruct((bn, 1, c), F32)],
        scratch_shapes=[pltpu.VMEM((tm + (CONV_W - 1) * V7X_SUBLANES, c), F32), pltpu.VMEM((tm, c), F32),
                        pltpu.VMEM((tm, c), F32), pltpu.VMEM((V7X_SUBLANES, c), F32),
                        pltpu.VMEM((V7X_SUBLANES, c), F32)],
        compiler_params=_params("arbitrary", "arbitrary"),
        name="rg_seq",
    )(proj, proj, conv0, h0.reshape(bn, 1, c), conv_w, vec(conv_b), wa, wi, vec(b_a), vec(b_i), vec(lam))
    return y, hout.reshape(bn, c)


def _rg_step_kernel(x_ref, gy_ref, conv0_ref, h0_ref, cw_ref, cb_ref, wa_ref, wi_ref, ba_ref, bi_ref, lam_ref,
                    y_ref, hout_ref, a_ref, b_ref, *, seq, bn):
    nconv = conv0_ref.shape[0]

    def xpad(i):
        if i < nconv:
            return conv0_ref[i]
        return x_ref[(i - nconv) * bn:(i - nconv + 1) * bn, :]

    for t in range(seq):
        acc = cw_ref[0:1, :] * xpad(t)
        for k in range(1, nconv + 1):
            acc = acc + cw_ref[k:k + 1, :] * xpad(t + k)
        xc = cb_ref[...] + acc
        _rg_gates(xc, wa_ref, wi_ref, ba_ref, bi_ref, lam_ref, a_ref, b_ref, slice(t * bn, (t + 1) * bn))
    h = h0_ref[...]
    for t in range(seq):
        rows = slice(t * bn, (t + 1) * bn)
        h = a_ref[rows, :] * h + b_ref[rows, :]
        y_ref[rows, :] = (h * gy_ref[rows, :]).astype(y_ref.dtype)
    hout_ref[...] = h


def _rg_step(proj, conv0_tm, h0, conv_w, conv_b, wa, wi, b_a, b_i, lam, l, *, bn, seq):
    c = h0.shape[-1]
    t = bn * seq
    depth = conv_w.shape[0]
    vec = lambda a: a.reshape(depth, 1, c)
    lspec = pl.BlockSpec((None, 1, c), lambda i: (l, 0, 0))
    return pl.pallas_call(
        functools.partial(_rg_step_kernel, seq=seq, bn=bn),
        grid=(1,),
        in_specs=[pl.BlockSpec((t, c), lambda i: (0, 1)),
                  pl.BlockSpec((t, c), lambda i: (0, 2)),
                  pl.BlockSpec(conv0_tm.shape, lambda i: (0, 0, 0)),
                  pl.BlockSpec((bn, c), lambda i: (0, 0)),
                  pl.BlockSpec((None, CONV_W, c), lambda i: (l, 0, 0)),
                  lspec, _const_spec(wa), _const_spec(wi), lspec, lspec, lspec],
        out_specs=[pl.BlockSpec((t, c), lambda i: (0, 0)), pl.BlockSpec((bn, c), lambda i: (0, 0))],
        out_shape=[jax.ShapeDtypeStruct((t, c), F32), jax.ShapeDtypeStruct((bn, c), F32)],
        scratch_shapes=[pltpu.VMEM((t, c), F32), pltpu.VMEM((t, c), F32)],
        compiler_params=_params("arbitrary"),
        name="rg_step",
    )(proj, proj, conv0_tm, h0, conv_w, vec(conv_b), wa, wi, vec(b_a), vec(b_i), vec(lam))


def _gate_merge_kernel(x_ref, sh_ref, sc_ref, g_ref, wgs_ref, wgr_ref, bgs_ref, bgr_ref, ys_ref, yr_ref,
                       ws_ref, wr_ref, o_ref, hn_ref):
    @pl.when(pl.program_id(1) == 0)
    def _():
        _norm_mod_rows(x_ref, g_ref, sc_ref, sh_ref, hn_ref)

    h = hn_ref[...]
    gate_s = _sigmoid(_dot(h, wgs_ref[...]) + bgs_ref[...])
    gate_r = _sigmoid(_dot(h, wgr_ref[...]) + bgr_ref[...])
    merged = gate_s * _dot(ys_ref[...], ws_ref[...]) + gate_r * _dot(yr_ref[...], wr_ref[...])
    o_ref[...] = merged.astype(o_ref.dtype)


def _gate_merge(x, mod, g, w_gate, b_gate, ys, yr, w_s, w_r, l, *, tm, tn, tiles_per_block):
    t, d = x.shape
    c = ys.shape[1]
    rows = mod.shape[2]
    off = d // tn
    depth = w_gate.shape[0]
    once = dict(pipeline_mode=pl.Buffered(1))
    return pl.pallas_call(
        _gate_merge_kernel,
        grid=(t // tm, d // tn),
        in_specs=[pl.BlockSpec((tm, d), lambda m, n: (m, 0), **once),
                  _mod_spec(0, rows, d, tiles_per_block, tm),
                  _mod_spec(1, rows, d, tiles_per_block, tm),
                  pl.BlockSpec((None, 1, d), lambda m, n: (l, 0, 0)),
                  pl.BlockSpec((None, d, tn), lambda m, n: (l, 0, n)),
                  pl.BlockSpec((None, d, tn), lambda m, n: (l, 0, n + off)),
                  pl.BlockSpec((None, 1, tn), lambda m, n: (l, 0, n)),
                  pl.BlockSpec((None, 1, tn), lambda m, n: (l, 0, n + off)),
                  pl.BlockSpec((tm, c), lambda m, n: (m, 0)),
                  pl.BlockSpec((tm, c), lambda m, n: (m, 0)),
                  pl.BlockSpec((None, c, tn), lambda m, n: (l, 0, n)),
                  pl.BlockSpec((None, c, tn), lambda m, n: (l, 0, n))],
        out_specs=pl.BlockSpec((tm, tn), lambda m, n: (m, n)),
        out_shape=jax.ShapeDtypeStruct((t, d), ACT_DTYPE),
        scratch_shapes=[pltpu.VMEM((tm, d), w_gate.dtype)],
        compiler_params=_params("arbitrary", "arbitrary"),
        name="gate_merge",
    )(x, mod, mod, g.reshape(depth, 1, d), w_gate, w_gate, b_gate.reshape(depth, 1, 2 * d),
      b_gate.reshape(depth, 1, 2 * d), ys, yr, w_s, w_r)


def _linear_residual_kernel(a_ref, w_ref, x_ref, g_ref, o_ref, acc_ref):
    k = pl.program_id(2)

    @pl.when(k == 0)
    def _():
        acc_ref[...] = jnp.zeros_like(acc_ref)

    acc_ref[...] += _dot(a_ref[...], w_ref[...])

    @pl.when(k == pl.num_programs(2) - 1)
    def _():
        o_ref[...] = x_ref[...] + g_ref[...] * acc_ref[...]


def _linear_residual(a, w, l, x, mod, j_g, *, tm, tn, tk, tiles_per_block):
    t, kdim = a.shape
    d = w.shape[-1]
    rows = mod.shape[2]
    return pl.pallas_call(
        _linear_residual_kernel,
        grid=(t // tm, d // tn, kdim // tk),
        in_specs=[pl.BlockSpec((tm, tk), lambda m, n, k: (m, k)),
                  pl.BlockSpec((None, tk, tn), lambda m, n, k: (l, k, n)),
                  pl.BlockSpec((tm, tn), lambda m, n, k: (m, n)),
                  _mod_spec(j_g, rows, tn, tiles_per_block, tm, ncol_axis=0)],
        out_specs=pl.BlockSpec((tm, tn), lambda m, n, k: (m, n)),
        out_shape=jax.ShapeDtypeStruct((t, d), F32),
        scratch_shapes=[pltpu.VMEM((tm, tn), F32)],
        compiler_params=_params("arbitrary", "arbitrary", "arbitrary"),
        name="linear_residual",
    )(a, w, x, mod)


ROUTE_I1, ROUTE_I2, ROUTE_W1, ROUTE_W2, ROUTE_RANK1, ROUTE_RANK2 = range(6)
DMA_UNROLL = 8
MOE_ROW_TILE = 512
DISPATCH_TILES_PER_STEP = 2
COMBINE_TILE = 512


def _router_kernel(x_ref, sh_ref, sc_ref, g_ref, wr_ref, *rest, n_experts, n_own):
    hn_ref, route_ref, cnt_ref = rest[-3:]
    m = pl.program_id(0)

    @pl.when(m == 0)
    def _():
        cnt_ref[...] = jnp.zeros_like(cnt_ref)

    @pl.when(m >= n_own)
    def _():
        hn_ref[...] = jnp.zeros_like(hn_ref)

    @pl.when(m < n_own)
    def _():
        _route_tile(x_ref, sh_ref, sc_ref, g_ref, wr_ref, hn_ref, route_ref, cnt_ref, n_experts)


def _route_tile(x_ref, sh_ref, sc_ref, g_ref, wr_ref, hn_ref, route_ref, cnt_ref, n_experts):
    _norm_mod_rows(x_ref, g_ref, sc_ref, sh_ref, hn_ref)
    logits = jnp.dot(hn_ref[...], wr_ref[...], preferred_element_type=F32, precision=lax.Precision.HIGHEST)
    tm, width = logits.shape
    lane = lax.broadcasted_iota(jnp.int32, logits.shape, 1).astype(F32)
    lg = jnp.where(lane < n_experts, logits, -jnp.inf)
    v1 = jnp.max(lg, axis=-1, keepdims=True)
    i1 = jnp.min(jnp.where(lg == v1, lane, float(width)), axis=-1, keepdims=True)
    lg2 = jnp.where(lane == i1, -jnp.inf, lg)
    v2 = jnp.max(lg2, axis=-1, keepdims=True)
    i2 = jnp.min(jnp.where(lg2 == v2, lane, float(width)), axis=-1, keepdims=True)
    e2 = jnp.exp(v2 - v1)
    den = 1.0 + e2
    sel1 = lane == i1
    sel2 = lane == i2
    picked = jnp.where(sel1 | sel2, 1.0, 0.0)
    r = lax.broadcasted_iota(jnp.int32, (tm, tm), 0)
    c = lax.broadcasted_iota(jnp.int32, (tm, tm), 1)
    before = jnp.where(c < r, 1.0, 0.0)
    seen = _dot(before, picked) + cnt_ref[...]
    rank1 = jnp.sum(jnp.where(sel1, seen, 0.0), axis=-1, keepdims=True)
    rank2 = jnp.sum(jnp.where(sel2, seen, 0.0), axis=-1, keepdims=True)
    cnt_ref[...] += jnp.sum(picked, axis=0, keepdims=True)
    cols = ((ROUTE_I1, i1), (ROUTE_I2, i2), (ROUTE_W1, 1.0 / den), (ROUTE_W2, e2 / den),
            (ROUTE_RANK1, rank1), (ROUTE_RANK2, rank2))
    route = jnp.zeros_like(logits)
    for j, v in cols:
        route = jnp.where(lane == j, v, route)
    route_ref[...] = route


def _router(x, mod, j_sh, g, w_router_pad, l, *, tm, tiles_per_block, n_experts, hn_rows, hn_row0, hn_buf):
    t, d = x.shape
    rows = mod.shape[2]
    width = w_router_pad.shape[-1]
    assert t % tm == 0 and hn_row0 % tm == 0 and (hn_rows - hn_row0) % tm == 0
    n_own = t // tm
    n_fill = (hn_rows - hn_row0) // tm - n_own if hn_buf is None else 0
    own = lambda m: jnp.minimum(m, n_own - 1)
    in_specs = [pl.BlockSpec((tm, d), lambda m: (own(m), 0)),
                _mod_spec(j_sh, rows, d, tiles_per_block, tm, tile_of=own),
                _mod_spec(j_sh + 1, rows, d, tiles_per_block, tm, tile_of=own),
                pl.BlockSpec((None, 1, d), lambda m: (l, 0, 0)),
                pl.BlockSpec((d, width), lambda m: (0, 0))]
    args = [x, mod, mod, g.reshape(g.shape[0], 1, d), w_router_pad]
    aliases = {}
    if hn_buf is not None:
        in_specs.append(pl.BlockSpec(memory_space=pl.ANY))
        args.append(hn_buf)
        aliases = {len(args) - 1: 0}
    return pl.pallas_call(
        functools.partial(_router_kernel, n_experts=n_experts, n_own=n_own),
        grid=(n_own + n_fill,),
        in_specs=in_specs,
        out_specs=[pl.BlockSpec((tm, d), lambda m: (m + hn_row0 // tm, 0)),
                   pl.BlockSpec((tm, width), lambda m: (own(m), 0)),
                   pl.BlockSpec((1, width), lambda m: (0, 0))],
        out_shape=[jax.ShapeDtypeStruct((hn_rows, d), F32), jax.ShapeDtypeStruct((t, width), F32),
                   jax.ShapeDtypeStruct((1, width), F32)],
        input_output_aliases=aliases,
        compiler_params=_params("arbitrary"),
        name="moe_router",
    )(*args)


def _dispatch_kernel(tok_ref, nu_ref, hn_ref, xs_ref, sem, *, tg):
    step_rows = xs_ref.shape[0]
    base = pl.program_id(0) * step_rows

    def row_copy(tok, i):
        return pltpu.make_async_copy(hn_ref.at[pl.ds(tok, 1)], xs_ref.at[pl.ds(i, 1)], sem.at[0])

    def used(j):
        return pl.program_id(0) * (step_rows // tg) + j < nu_ref[0]

    for j in range(step_rows // tg):
        @pl.when(used(j))
        def _(j=j):
            lax.fori_loop(j * tg, (j + 1) * tg, lambda i, c: (row_copy(tok_ref[base + i], i).start(), c)[1], 0,
                          unroll=DMA_UNROLL)

    for j in range(step_rows // tg):
        rows = pl.ds(j * tg, tg)

        @pl.when(used(j))
        def _(rows=rows):
            pltpu.make_async_copy(hn_ref.at[pl.ds(0, tg)], xs_ref.at[rows], sem.at[0]).wait()

        @pl.when(jnp.logical_not(used(j)))
        def _(rows=rows):
            xs_ref[rows, :] = jnp.zeros((tg, xs_ref.shape[1]), xs_ref.dtype)


def _dispatch(tok_of_row, n_used, hn, *, tg):
    rows = tok_of_row.shape[0]
    d = hn.shape[1]
    step_rows = tg * (DISPATCH_TILES_PER_STEP if (rows // tg) % DISPATCH_TILES_PER_STEP == 0 else 1)
    return pl.pallas_call(
        functools.partial(_dispatch_kernel, tg=tg),
        grid_spec=pltpu.PrefetchScalarGridSpec(
            num_scalar_prefetch=2, grid=(rows // step_rows,),
            in_specs=[pl.BlockSpec(memory_space=pl.ANY)],
            out_specs=pl.BlockSpec((step_rows, d), lambda m, tok, nu: (m, 0)),
            scratch_shapes=[pltpu.SemaphoreType.DMA((1,))]),
        out_shape=jax.ShapeDtypeStruct((rows, d), F32),
        compiler_params=_params("arbitrary"),
        name="moe_dispatch",
    )(tok_of_row, n_used, hn)


def _grouped_up_kernel(te_ref, nu_ref, xs_ref, w1_ref, w3_ref, o_ref):
    m = pl.program_id(1)

    @pl.when(m < nu_ref[0])
    def _():
        h = xs_ref[...]
        o_ref[...] = (_silu(_dot(h, w1_ref[...])) * _dot(h, w3_ref[...])).astype(o_ref.dtype)

    @pl.when(m >= nu_ref[0])
    def _():
        o_ref[...] = jnp.zeros_like(o_ref)


def _grouped_up(tile_expert, n_used, xs, w1, w3, lm, *, tg, tn):
    r, d = xs.shape
    f = w1.shape[-1]
    wspec = pl.BlockSpec((None, None, d, tn), lambda n, m, te, nu: (lm, te[m], 0, n))
    return pl.pallas_call(
        _grouped_up_kernel,
        grid_spec=pltpu.PrefetchScalarGridSpec(
            num_scalar_prefetch=2, grid=(f // tn, r // tg),
            in_specs=[pl.BlockSpec((tg, d), lambda n, m, te, nu: (m, 0)), wspec, wspec],
            out_specs=pl.BlockSpec((tg, tn), lambda n, m, te, nu: (m, n))),
        out_shape=jax.ShapeDtypeStruct((r, f), ACT_DTYPE),
        compiler_params=_params("arbitrary", "arbitrary"),
        name="moe_up",
    )(tile_expert, n_used, xs, w1, w3)


def _grouped_down_kernel(te_ref, nu_ref, a_ref, w_ref, o_ref, wcast_ref):
    m = pl.program_id(1)
    new_block = jnp.logical_or(m == 0, te_ref[m] != te_ref[jnp.maximum(m - 1, 0)])

    @pl.when(new_block)
    def _():
        wcast_ref[...] = w_ref[...].astype(wcast_ref.dtype)

    @pl.when(m < nu_ref[0])
    def _():
        o_ref[...] = _dot(a_ref[...], wcast_ref[...])

    @pl.when(m >= nu_ref[0])
    def _():
        o_ref[...] = jnp.zeros_like(o_ref)


def _grouped_down(tile_expert, n_used, act, w2, lm, *, tg, tn):
    r, f = act.shape
    d = w2.shape[-1]
    return pl.pallas_call(
        _grouped_down_kernel,
        grid_spec=pltpu.PrefetchScalarGridSpec(
            num_scalar_prefetch=2, grid=(d // tn, r // tg),
            in_specs=[pl.BlockSpec((tg, f), lambda n, m, te, nu: (m, 0)),
                      pl.BlockSpec((None, None, f, tn), lambda n, m, te, nu: (lm, te[m], 0, n))],
            out_specs=pl.BlockSpec((tg, tn), lambda n, m, te, nu: (m, n)),
            scratch_shapes=[pltpu.VMEM((f, tn), act.dtype)]),
        out_shape=jax.ShapeDtypeStruct((r, d), F32),
        compiler_params=_params("arbitrary", "arbitrary"),
        name="moe_down",
    )(tile_expert, n_used, act, w2)


def _combine_kernel(p1_ref, p2_ref, y_ref, route_ref, x_ref, g_ref, *rest, final_norm):
    if final_norm:
        gf_ref, o_ref, ya_ref, yb_ref, sem = rest
    else:
        o_ref, ya_ref, yb_ref, sem = rest
    tc = ya_ref.shape[0]
    base = pl.program_id(0) * tc

    def row_copy(row, dst_ref, i):
        return pltpu.make_async_copy(y_ref.at[pl.ds(row, 1)], dst_ref.at[pl.ds(i, 1)], sem.at[0])

    def body(i, c):
        row_copy(p1_ref[base + i], ya_ref, i).start()
        row_copy(p2_ref[base + i], yb_ref, i).start()
        return c

    lax.fori_loop(0, tc, body, 0, unroll=DMA_UNROLL)
    for dst_ref in (ya_ref, yb_ref):
        pltpu.make_async_copy(y_ref.at[pl.ds(0, tc)], dst_ref, sem.at[0]).wait()
    w1 = route_ref[:, ROUTE_W1:ROUTE_W1 + 1]
    w2 = route_ref[:, ROUTE_W2:ROUTE_W2 + 1]
    out = x_ref[...] + g_ref[...] * (w1 * ya_ref[...] + w2 * yb_ref[...])
    o_ref[...] = _rmsnorm(out, gf_ref[...]) if final_norm else out


def _combine(pos1, pos2, y, route, x, mod, j_g, *, tc, tiles_per_block, final_g=None):
    t, d = x.shape
    rows = mod.shape[2]
    width = route.shape[-1]
    in_specs = [pl.BlockSpec(memory_space=pl.ANY),
                pl.BlockSpec((tc, width), lambda m, p1, p2: (m, 0)),
                pl.BlockSpec((tc, d), lambda m, p1, p2: (m, 0)),
                _mod_spec(j_g, rows, d, tiles_per_block, tc)]
    args = [pos1, pos2, y, route, x, mod]
    if final_g is not None:
        in_specs.append(pl.BlockSpec((1, d), lambda m, p1, p2: (0, 0)))
        args.append(final_g.reshape(1, d))
    return pl.pallas_call(
        functools.partial(_combine_kernel, final_norm=final_g is not None),
        grid_spec=pltpu.PrefetchScalarGridSpec(
            num_scalar_prefetch=2, grid=(t // tc,),
            in_specs=in_specs,
            out_specs=pl.BlockSpec((tc, d), lambda m, p1, p2: (m, 0)),
            scratch_shapes=[pltpu.VMEM((tc, d), F32), pltpu.VMEM((tc, d), F32), pltpu.SemaphoreType.DMA((1,))]),
        out_shape=jax.ShapeDtypeStruct((t, d), F32),
        compiler_params=_params("arbitrary"),
        name="moe_combine",
    )(*args)


def _routed_moe(runs, l, g, w_router, w1, w3, w2, lm, *, tg, final_g=None):
    ne = w_router.shape[-1]
    w_r = jnp.pad(w_router[lm], ((0, 0), (0, V7X_LANES - ne)))
    t_all = sum(r["x"].shape[0] for r in runs)
    routed, hn_all, row0 = [], None, 0
    for r in runs:
        hn_all, route, cnt = _router(r["x"], r["mods"][l], 3, g, w_r, l, n_experts=ne, hn_rows=t_all, hn_row0=row0,
                                     hn_buf=hn_all, **r["tiled"](min(r["tm"], 512)))
        routed.append((hn_all, route, cnt))
        row0 += r["x"].shape[0]
    counts_each = [cnt[0, :ne].astype(jnp.int32) for _, _, cnt in routed]
    tiles_e = (sum(counts_each) + (tg - 1)) // tg
    tile_end = jnp.cumsum(tiles_e)
    row_start = (tile_end - tiles_e) * tg
    n_tiles = -(-TOP_K * t_all // tg) + ne
    tile_expert = jnp.minimum(jnp.sum(jnp.arange(n_tiles)[:, None] >= tile_end[None, :], axis=1), ne - 1)
    tile_expert = tile_expert.astype(jnp.int32)
    n_used = tile_end[-1:].astype(jnp.int32)
    pos, toks, first_row, tok0 = [], [], row_start, 0
    for (_, route, _), c_e, r in zip(routed, counts_each, runs):
        col = lambda j: route[:, j].astype(jnp.int32)
        pos.append((first_row[col(ROUTE_I1)] + col(ROUTE_RANK1), first_row[col(ROUTE_I2)] + col(ROUTE_RANK2)))
        toks.append(tok0 + jnp.arange(route.shape[0], dtype=jnp.int32))
        first_row, tok0 = first_row + c_e, tok0 + route.shape[0]
    tok_of_row = jnp.zeros((n_tiles * tg,), jnp.int32).at[jnp.concatenate([q for pq in pos for q in pq])].set(
        jnp.concatenate([tk for tk in toks for _ in range(TOP_K)]), unique_indices=True)
    xs = _dispatch(tok_of_row, n_used, hn_all, tg=tg)
    act = _grouped_up(tile_expert, n_used, xs, w1, w3, lm, tg=tg, tn=1024)
    y = _grouped_down(tile_expert, n_used, act, w2, lm, tg=tg, tn=1024)
    out = []
    for (p1, p2), (_, route, _), r in zip(pos, routed, runs):
        tc = min(r["tm"], COMBINE_TILE)
        out.append(_combine(p1, p2, y, route, r["x"], r["mods"][l], 5, tc=tc,
                            tiles_per_block=r["tiled"](tc)["tiles_per_block"], final_g=final_g))
    return out


def _final_norm_kernel(x_ref, g_ref, o_ref):
    o_ref[...] = _rmsnorm(x_ref[...], g_ref[...])


def _final_norm(x, g, *, tm):
    t, d = x.shape
    return pl.pallas_call(
        _final_norm_kernel,
        grid=(t // tm,),
        in_specs=[pl.BlockSpec((tm, d), lambda m: (m, 0)), pl.BlockSpec((1, d), lambda m: (0, 0))],
        out_specs=pl.BlockSpec((tm, d), lambda m: (m, 0)),
        out_shape=jax.ShapeDtypeStruct((t, d), F32),
        compiler_params=_params("arbitrary"),
        name="final_norm",
    )(x, g.reshape(1, d))


def _make_run(x, mods, state, *, bn, seq, time_major, tm):
    def tiled(tile):
        return dict(tm=tile, tiles_per_block=1 if time_major else seq // tile)

    assert seq >= CONV_W - 1
    cols = dict(gate_merge=256, glu=512) if tm > 512 else dict(gate_merge=512, glu=1024)
    return dict(x=x, mods=mods, state=state, bn=bn, seq=seq, time_major=time_major, tm=tm, tiled=tiled, cols=cols,
                s5_re=[], s5_im=[], rg=[], conv=[])


def _mixer(r, p, l):
    x, mod, state, bn, seq = r["x"], r["mods"][l], r["state"], r["bn"], r["seq"]
    depth = p["w_in"].shape[0]
    c = p["s5_w_glu"].shape[-1]
    kw = r["tiled"](r["tm"])
    nconv = CONV_W - 1
    tb = p["s5_tables"][l]
    wa, wi = p["rg_tables"][l]
    proj = _normed_linear(x, mod, 0, p["norm_mix"], l, [p["w_in"]], l, mode="in", tn=1024, plain_cols=2 * c, **kw)
    h0re = state["s5_re"][l].reshape(bn, -1)
    h0im = state["s5_im"][l].reshape(bn, -1)
    s5_args = (tb, p["s5_d"].reshape(depth, c), p["s5_w_glu"], p["s5_b_glu"], l)
    rg_args = (p["rg_conv_w"], p["rg_conv_b"], wa, wi, p["rg_b_a"].reshape(depth, c), p["rg_b_i"].reshape(depth, c),
               p["rg_lam"], l)
    if r["time_major"]:
        y_s5, hre, him = _s5_step(proj, h0re, h0im, *s5_args, bn=bn, seq=seq)
        y_rg, hrg = _rg_step(proj, state["conv"][l].transpose(1, 0, 2), state["rg"][l], *rg_args, bn=bn, seq=seq)
        r["conv"].append(proj[:, c:2 * c].reshape(seq, bn, c)[seq - nconv:].transpose(1, 0, 2))
    else:
        y_s5, hre, him = _s5_seq(proj, h0re, h0im, *s5_args, bn=bn, seq=seq)
        y_rg, hrg = _rg_seq(proj, state["conv"][l], state["rg"][l], *rg_args, bn=bn, seq=seq)
        tail_rows = [seq - SEG_TILE + (SEG_LEN - nconv + f) * V7X_SUBLANES + V7X_SUBLANES - 1 for f in range(nconv)]
        proj3 = proj.reshape(bn, seq, 3 * c)
        r["conv"].append(jnp.stack([proj3[:, row, c:2 * c] for row in tail_rows], axis=1))
    r["s5_re"].append(hre.reshape(state["s5_re"][l].shape))
    r["s5_im"].append(him.reshape(state["s5_im"][l].shape))
    r["rg"].append(hrg)
    merged = _gate_merge(x, mod, p["norm_mix"], p["w_gate"], p["b_gate"], y_s5, y_rg, p["w_br_s5"], p["w_br_rg"], l,
                         tn=r["cols"]["gate_merge"], **kw)
    r["x"] = _linear_residual(merged, p["w_out"], l, x, mod, 2, tn=1024, tk=2048, **kw)


def _dense_ffn(r, p, l):
    kw = r["tiled"](r["tm"])
    mod = r["mods"][l]
    act = _normed_linear(r["x"], mod, 3, p["norm_ffn"], l, [p["ffn_w1"], p["ffn_w3"]], l // 2, mode="glu",
                         tn=r["cols"]["glu"], out_dtype=ACT_DTYPE, **kw)
    r["x"] = _linear_residual(act, p["ffn_w2"], l // 2, r["x"], mod, 5, tn=1024, tk=2048, **kw)


def _forward(runs, p):
    depth = p["w_in"].shape[0]
    for l in range(depth):
        for r in runs:
            _mixer(r, p, l)
        if l % 2 == 0:
            for r in runs:
                _dense_ffn(r, p, l)
        else:
            final_g = p["norm_f"] if l == depth - 1 else None
            new_x = _routed_moe(runs, l, p["norm_ffn"], p["moe_router"], p["moe_w1"], p["moe_w3"], p["moe_w2"], l // 2,
                                tg=MOE_ROW_TILE, final_g=final_g)
            for r, x in zip(runs, new_x):
                r["x"] = x
    if depth % 2 == 0:
        return [r["x"] for r in runs]
    return [_final_norm(r["x"], p["norm_f"], tm=min(r["tm"], 256)) for r in runs]


def kernel(x_prompt, x_sample, c_prompt, c_sample, state_s5_re, state_s5_im, state_rglru, state_conv, norm_mix, norm_ffn, norm_f, w_ada, b_ada, w_in, s5_lam_re, s5_lam_im, s5_log_dt, s5_b_re, s5_b_im, s5_c_re, s5_c_im, s5_d, s5_w_glu, s5_b_glu, rg_conv_w, rg_conv_b, rg_w_a, rg_b_a, rg_w_i, rg_b_i, rg_lam, w_gate, b_gate, w_br_s5, w_br_rg, w_out, ffn_w1, ffn_w3, ffn_w2, moe_router, moe_w1, moe_w3, moe_w2):
    p = dict(norm_mix=norm_mix, norm_ffn=norm_ffn, norm_f=norm_f, w_in=w_in,
             s5_lam_re=s5_lam_re, s5_lam_im=s5_lam_im, s5_log_dt=s5_log_dt, s5_b_re=s5_b_re, s5_b_im=s5_b_im,
             s5_c_re=s5_c_re, s5_c_im=s5_c_im, s5_d=s5_d, s5_w_glu=s5_w_glu, s5_b_glu=s5_b_glu,
             rg_conv_w=rg_conv_w, rg_conv_b=rg_conv_b, rg_w_a=rg_w_a, rg_b_a=rg_b_a, rg_w_i=rg_w_i, rg_b_i=rg_b_i,
             rg_lam=rg_lam, w_gate=w_gate, b_gate=b_gate, w_br_s5=w_br_s5, w_br_rg=w_br_rg, w_out=w_out,
             ffn_w1=ffn_w1, ffn_w3=ffn_w3, ffn_w2=ffn_w2, moe_router=moe_router, moe_w1=moe_w1, moe_w3=moe_w3,
             moe_w2=moe_w2)
    for name in ("w_out", "ffn_w2"):
        p[name] = p[name].astype(ACT_DTYPE)
    depth = w_in.shape[0]
    p["s5_tables"] = [_s5_tables(s5_lam_re[l], s5_lam_im[l], s5_log_dt[l], s5_b_re[l], s5_b_im[l], s5_c_re[l],
                                 s5_c_im[l]) for l in range(depth)]
    p["rg_tables"] = [_rg_tables(rg_w_a[l], rg_w_i[l]) for l in range(depth)]
    bp, sp, d = x_prompt.shape
    bs, ss, _ = x_sample.shape
    g, pst = state_s5_re.shape[2:]
    c = state_rglru.shape[-1]

    n_c = bp + bs
    rows_c = -(-n_c // V7X_SUBLANES) * V7X_SUBLANES
    c_all = jnp.pad(jnp.concatenate([c_prompt, c_sample], axis=0), ((0, rows_c - n_c), (0, 0)))
    mod_all = _ada_mod(c_all, w_ada, b_ada)
    mods_p, mods_s = [], []
    for l in range(depth):
        mp = mod_all[l, :bp].reshape(bp, 6, 1, d).transpose(1, 0, 2, 3)
        ms = mod_all[l, bp:n_c].reshape(bs, 6, d).transpose(1, 0, 2)
        ms = jnp.tile(ms, (1, ss, 1)).reshape(6, 1, ss * bs, d)
        mods_p.append(mp)
        mods_s.append(ms)

    zero_state = dict(s5_re=jnp.zeros((depth, bp, g, pst), F32), s5_im=jnp.zeros((depth, bp, g, pst), F32),
                      rg=jnp.zeros((depth, bp, c), F32), conv=jnp.zeros((depth, bp, CONV_W - 1, c), x_prompt.dtype))
    seg_shape = (bp, sp // SEG_TILE, V7X_SUBLANES, SEG_LEN, d)
    x_p = x_prompt.reshape(seg_shape).transpose(0, 1, 3, 2, 4).reshape(bp * sp, d)
    run_p = _make_run(x_p, mods_p, zero_state, bn=bp, seq=sp, time_major=False, tm=min(PROMPT_TILE, sp))
    sample_state = dict(s5_re=state_s5_re, s5_im=state_s5_im, rg=state_rglru, conv=state_conv)
    x_s = x_sample.transpose(1, 0, 2).reshape(ss * bs, d)
    run_s = _make_run(x_s, mods_s, sample_state, bn=bs, seq=ss, time_major=True, tm=ss * bs)
    y_p, y_s = _forward([run_p, run_s], p)
    y_p = y_p.reshape(bp, sp // SEG_TILE, SEG_LEN, V7X_SUBLANES, d).transpose(0, 1, 3, 2, 4).reshape(bp, sp, d)
    y_s = y_s.reshape(ss, bs, d).transpose(1, 0, 2)
    new_state = lambda r: tuple(jnp.stack(r[k]) for k in ("s5_re", "s5_im", "rg", "conv"))
    return (y_p, y_s) + new_state(run_p) + new_state(run_s)
```
